```python
import jax
import jax.numpy as jnp
from jax import lax
import numpy as np

D_MODEL = 1024
BATCH = 4
SEQ = 4096
DEPTH = 1
DEC_BATCH = 32
DEC_SEQ = 16
PAST_LEN = 2048

CHUNK = 64
N_PAST_CHUNKS = 8
A_REACH = N_PAST_CHUNKS * CHUNK
BAND = A_REACH + CHUNK
H_A = 8
DH_A = 64
H_B = 8
DH_B = 64
H_M = 4
DH_M = 128
N_MEM = 256
REL_CLIP = 128
N_REL = 2 * REL_CLIP + 1
D_FF = 4 * D_MODEL
Q_BLOCK = 128
EPS = 1e-6
W_A = H_A * DH_A
W_B = H_B * DH_B
W_M = H_M * DH_M
N_BRANCH = 3
IN_SIZES = (W_A, W_A, W_A, W_B, W_B, W_B, H_B, W_M, N_BRANCH * D_MODEL)
IN_SPLITS = tuple(int(s) for s in np.cumsum(IN_SIZES)[:-1])
D_IN = int(sum(IN_SIZES))

kernel_name = 'chunk_streaming_hybrid_encoder_step'


def rmsnorm(x, g):
    xf = x.astype(jnp.float32)
    y = xf * lax.rsqrt(jnp.mean(xf * xf, axis=-1, keepdims=True) + EPS)
    return (y * g.astype(jnp.float32)).astype(x.dtype)


def project(xn, w_in, b_f, g_qa, g_ka, g_qb, g_kb, g_qm):
    b, t, _ = xn.shape
    z = xn @ w_in
    qa, ka, va, qb, kb, vb, fl, qm, gates = jnp.split(z, IN_SPLITS, axis=-1)
    qa = rmsnorm(qa.reshape(b, t, H_A, DH_A), g_qa)
    ka = rmsnorm(ka.reshape(b, t, H_A, DH_A), g_ka)
    va = va.reshape(b, t, H_A, DH_A)
    qb = rmsnorm(qb.reshape(b, t, H_B, DH_B), g_qb)
    kb = rmsnorm(kb.reshape(b, t, H_B, DH_B), g_kb)
    vb = vb.reshape(b, t, H_B, DH_B)
    logf = jax.nn.log_sigmoid(fl.astype(jnp.float32) + b_f.astype(jnp.float32))
    qm = rmsnorm(qm.reshape(b, t, H_M, DH_M), g_qm)
    return qa, ka, va, qb, kb, vb, logf, qm, gates


def rel_bias_lookup(rel_bias, dist):
    bias = rel_bias[jnp.clip(dist, -REL_CLIP, REL_CLIP) + REL_CLIP].astype(jnp.float32)
    return jnp.moveaxis(bias, -1, 0)


def band_attn_prompt(q, k, v, rel_bias):
    b, t, h, d = q.shape
    nc = t // CHUNK
    qc = q.reshape(b, nc, CHUNK, h, d)
    pad = ((0, 0), (A_REACH, 0), (0, 0), (0, 0))
    kp = jnp.pad(k, pad)
    vp = jnp.pad(v, pad)
    rows = jnp.arange(nc)[:, None] * CHUNK + jnp.arange(BAND)[None, :]
    kband = kp[:, rows]
    vband = vp[:, rows]
    s = jnp.einsum('bcqhd,bckhd->bchqk', qc, kband).astype(jnp.float32) * (d ** -0.5)
    dist = jnp.arange(CHUNK)[:, None] + A_REACH - jnp.arange(BAND)[None, :]
    s = s + rel_bias_lookup(rel_bias, dist)
    valid = rows >= A_REACH
    s = jnp.where(valid[None, :, None, None, :], s, -jnp.inf)
    p = jax.nn.softmax(s, axis=-1).astype(v.dtype)
    o = jnp.einsum('bchqk,bckhd->bcqhd', p, vband)
    return o.reshape(b, t, h * d)


def band_attn_sample(q, k_new, v_new, k_cache, v_cache, rel_bias, past_len):
    b, s_len, h, d = q.shape
    n_cache = k_cache.shape[1]
    kk = jnp.concatenate([k_cache, k_new], axis=1)
    vv = jnp.concatenate([v_cache, v_new], axis=1)
    q_pos = past_len + jnp.arange(s_len)
    k_pos = jnp.concatenate([past_len - n_cache + jnp.arange(n_cache), q_pos])
    dist = q_pos[:, None] - k_pos[None, :]
    s = jnp.einsum('bqhd,bkhd->bhqk', q, kk).astype(jnp.float32) * (d ** -0.5)
    s = s + rel_bias_lookup(rel_bias, dist)
    p = jax.nn.softmax(s, axis=-1).astype(vv.dtype)
    o = jnp.einsum('bhqk,bkhd->bqhd', p, vv)
    return o.reshape(b, s_len, h * d)


def forget_attn_prompt(q, k, v, logf):
    b, t, h, d = q.shape
    nb = t // Q_BLOCK
    c = jnp.cumsum(logf, axis=1).transpose(0, 2, 1)
    qblk = q.reshape(b, nb, Q_BLOCK, h, d).transpose(1, 0, 2, 3, 4)
    cblk = c.reshape(b, h, nb, Q_BLOCK).transpose(2, 0, 1, 3)
    k_pos = jnp.arange(t)
    scale = d ** -0.5

    def one_block(args):
        i, qi, ci = args
        s = jnp.einsum('bqhd,bkhd->bhqk', qi, k).astype(jnp.float32) * scale
        s = s + ci[..., :, None] - c[..., None, :]
        q_pos = i * Q_BLOCK + jnp.arange(Q_BLOCK)
        s = jnp.where(k_pos[None, :] <= q_pos[:, None], s, -jnp.inf)
        p = jax.nn.softmax(s, axis=-1).astype(v.dtype)
        return jnp.einsum('bhqk,bkhd->bqhd', p, v)

    o = lax.map(one_block, (jnp.arange(nb), qblk, cblk))
    return o.transpose(1, 0, 2, 3, 4).reshape(b, t, h * d)


def forget_attn_sample(q, k_new, v_new, logf_new, k_cache, v_cache, logf_cache):
    b, s_len, h, d = q.shape
    p_len = k_cache.shape[1]
    kk = jnp.concatenate([k_cache, k_new], axis=1)
    vv = jnp.concatenate([v_cache, v_new], axis=1)
    lf = jnp.concatenate([logf_cache.astype(jnp.float32), logf_new], axis=1)
    c = jnp.cumsum(lf, axis=1).transpose(0, 2, 1)
    cq = c[..., p_len:]
    s = jnp.einsum('bqhd,bkhd->bhqk', q, kk).astype(jnp.float32) * (d ** -0.5)
    s = s + cq[..., :, None] - c[..., None, :]
    q_pos = p_len + jnp.arange(s_len)
    k_pos = jnp.arange(p_len + s_len)
    s = jnp.where(k_pos[None, :] <= q_pos[:, None], s, -jnp.inf)
    p = jax.nn.softmax(s, axis=-1).astype(vv.dtype)
    o = jnp.einsum('bhqk,bkhd->bqhd', p, vv)
    return o.reshape(b, s_len, h * d)


def mem_kv(mem, g_mem, w_mkv, g_km):
    b, n, _ = mem.shape
    mk, mv = jnp.split(rmsnorm(mem, g_mem) @ w_mkv, 2, axis=-1)
    return rmsnorm(mk.reshape(b, n, H_M, DH_M), g_km), mv.reshape(b, n, H_M, DH_M)


def mem_attn(q, mk, mv):
    b, t, h, d = q.shape
    s = jnp.einsum('bqhd,bmhd->bhqm', q, mk).astype(jnp.float32) * (d ** -0.5)
    p = jax.nn.softmax(s, axis=-1).astype(mv.dtype)
    o = jnp.einsum('bhqm,bmhd->bqhd', p, mv)
    return o.reshape(b, t, h * d)


def gated_merge(o_a, o_b, o_m, gates, w_pa, w_pb, w_pm, w_o):
    g_a, g_b, g_m = jnp.split(jax.nn.sigmoid(gates), N_BRANCH, axis=-1)
    h = g_a * (o_a @ w_pa) + g_b * (o_b @ w_pb) + g_m * (o_m @ w_pm)
    return h @ w_o


def sq_relu_mlp(x, g_ffn, w_up, w_down):
    h = jax.nn.relu(rmsnorm(x, g_ffn) @ w_up)
    return (h * h) @ w_down


def setup_inputs(seed: int = 0) -> dict:
    key = jax.random.key(seed)
    ks = iter(jax.random.split(key, 40))
    f32 = jnp.float32
    L = DEPTH
    a_cache = min(A_REACH, PAST_LEN)

    def nrm(shape, scale=1.0):
        return scale * jax.random.normal(next(ks), shape, f32)

    def gain(n):
        return 1.0 + 0.05 * nrm((L, n))

    return {
        'x_prompt': nrm((BATCH, SEQ, D_MODEL)),
        'x_sample': nrm((DEC_BATCH, DEC_SEQ, D_MODEL)),
        'mem_prompt': nrm((BATCH, N_MEM, D_MODEL)),
        'cache_a_k': nrm((L, DEC_BATCH, a_cache, H_A, DH_A)),
        'cache_a_v': nrm((L, DEC_BATCH, a_cache, H_A, DH_A)),
        'cache_b_k': nrm((L, DEC_BATCH, PAST_LEN, H_B, DH_B)),
        'cache_b_v': nrm((L, DEC_BATCH, PAST_LEN, H_B, DH_B)),
        'cache_b_logf': jax.nn.log_sigmoid(3.0 + 1.5 * nrm((L, DEC_BATCH, PAST_LEN, H_B))),
        'cache_mem_k': nrm((L, DEC_BATCH, N_MEM, H_M, DH_M)),
        'cache_mem_v': nrm((L, DEC_BATCH, N_MEM, H_M, DH_M)),
        'g_mix': gain(D_MODEL),
        'w_in': nrm((L, D_MODEL, D_IN), D_MODEL ** -0.5),
        'b_f': 1.0 + 5.0 * jax.random.uniform(next(ks), (L, H_B), f32),
        'g_qa': gain(DH_A),
        'g_ka': gain(DH_A),
        'g_qb': gain(DH_B),
        'g_kb': gain(DH_B),
        'g_qm': gain(DH_M),
        'g_km': gain(DH_M),
        'rel_bias': nrm((L, N_REL, H_A), 0.5),
        'g_mem': gain(D_MODEL),
        'w_mkv': nrm((L, D_MODEL, 2 * W_M), D_MODEL ** -0.5),
        'w_pa': nrm((L, W_A, D_MODEL), W_A ** -0.5),
        'w_pb': nrm((L, W_B, D_MODEL), W_B ** -0.5),
        'w_pm': nrm((L, W_M, D_MODEL), W_M ** -0.5),
        'w_o': nrm((L, D_MODEL, D_MODEL), D_MODEL ** -0.5),
        'g_ffn': gain(D_MODEL),
        'w_up': nrm((L, D_MODEL, D_FF), D_MODEL ** -0.5),
        'w_down': nrm((L, D_FF, D_MODEL), D_FF ** -0.5),
    }


def reference(x_prompt, x_sample, mem_prompt, cache_a_k, cache_a_v, cache_b_k, cache_b_v, cache_b_logf,
              cache_mem_k, cache_mem_v, g_mix, w_in, b_f, g_qa, g_ka, g_qb, g_kb, g_qm, g_km, rel_bias,
              g_mem, w_mkv, w_pa, w_pb, w_pm, w_o, g_ffn, w_up, w_down):
    past_len = cache_b_k.shape[2]
    keep_p = min(A_REACH, x_prompt.shape[1])
    xp = x_prompt
    xs = x_sample
    ak_p, av_p, bk_p, bv_p, blf_p, mk_p, mv_p = [], [], [], [], [], [], []
    ak_s, av_s, bk_s, bv_s, blf_s = [], [], [], [], []
    for l in range(DEPTH):
        xn = rmsnorm(xp, g_mix[l])
        qa, ka, va, qb, kb, vb, lf, qm, gates = project(xn, w_in[l], b_f[l], g_qa[l], g_ka[l], g_qb[l], g_kb[l], g_qm[l])
        mk, mv = mem_kv(mem_prompt, g_mem[l], w_mkv[l], g_km[l])
        o_a = band_attn_prompt(qa, ka, va, rel_bias[l])
        o_b = forget_attn_prompt(qb, kb, vb, lf)
        o_m = mem_attn(qm, mk, mv)
        xp = xp + gated_merge(o_a, o_b, o_m, gates, w_pa[l], w_pb[l], w_pm[l], w_o[l])
        xp = xp + sq_relu_mlp(xp, g_ffn[l], w_up[l], w_down[l])
        ak_p.append(ka[:, -keep_p:])
        av_p.append(va[:, -keep_p:])
        bk_p.append(kb)
        bv_p.append(vb)
        blf_p.append(lf)
        mk_p.append(mk)
        mv_p.append(mv)
        xn = rmsnorm(xs, g_mix[l])
        qa, ka, va, qb, kb, vb, lf, qm, gates = project(xn, w_in[l], b_f[l], g_qa[l], g_ka[l], g_qb[l], g_kb[l], g_qm[l])
        o_a = band_attn_sample(qa, ka, va, cache_a_k[l], cache_a_v[l], rel_bias[l], past_len)
        o_b = forget_attn_sample(qb, kb, vb, lf, cache_b_k[l], cache_b_v[l], cache_b_logf[l])
        o_m = mem_attn(qm, cache_mem_k[l], cache_mem_v[l])
        xs = xs + gated_merge(o_a, o_b, o_m, gates, w_pa[l], w_pb[l], w_pm[l], w_o[l])
        xs = xs + sq_relu_mlp(xs, g_ffn[l], w_up[l], w_down[l])
        ak_s.append(ka)
        av_s.append(va)
        bk_s.append(kb)
        bv_s.append(vb)
        blf_s.append(lf)
    new_a_k_prompt = jnp.stack(ak_p)
    new_a_v_prompt = jnp.stack(av_p)
    new_b_k_prompt = jnp.stack(bk_p)
    new_b_v_prompt = jnp.stack(bv_p)
    new_b_logf_prompt = jnp.stack(blf_p)
    new_mem_k_prompt = jnp.stack(mk_p)
    new_mem_v_prompt = jnp.stack(mv_p)
    new_a_k_sample = jnp.stack(ak_s)
    new_a_v_sample = jnp.stack(av_s)
    new_b_k_sample = jnp.stack(bk_s)
    new_b_v_sample = jnp.stack(bv_s)
    new_b_logf_sample = jnp.stack(blf_s)
    return (xp, xs, new_a_k_prompt, new_a_v_prompt, new_b_k_prompt, new_b_v_prompt, new_b_logf_prompt,
            new_mem_k_prompt, new_mem_v_prompt, new_a_k_sample, new_a_v_sample, new_b_k_sample,
            new_b_v_sample, new_b_logf_sample)
```

```python
import functools

import jax
import jax.numpy as jnp
from jax import lax
from jax.experimental import pallas as pl
from jax.experimental.pallas import tpu as pltpu

BF16 = jnp.bfloat16
F32 = jnp.float32

EPS = 1e-6
CHUNK = 64
REL_CLIP = 128
NEG = -1e30
LANE_GROUP = 256
VMEM_LIMIT = 56 * 1024 * 1024


def _cparams(*sem):
    return pltpu.CompilerParams(dimension_semantics=sem, vmem_limit_bytes=VMEM_LIMIT)


def _const_spec(shape):
    nd = len(shape)
    return pl.BlockSpec(shape, lambda *_: (0,) * nd, pipeline_mode=pl.Buffered(1))


def _nt_dot(a, b):
    return lax.dot_general(a, b, (((1,), (1,)), ((), ())), preferred_element_type=F32)


def _dot(a, b):
    return jnp.dot(a, b, preferred_element_type=F32)


def _rms_rows(x, gain):
    ms = jnp.mean(x * x, axis=-1, keepdims=True)
    return x * lax.rsqrt(ms + EPS) * gain


def _head_norm(z, bd_ref, gain):
    z2 = (z * z).astype(BF16)
    ms = jnp.concatenate(
        [_dot(z2[:, h * 256:(h + 1) * 256], bd_ref[...]) for h in range(z.shape[1] // 256)], axis=-1)
    return z * lax.rsqrt(ms + EPS) * gain


def _log_sigmoid(x):
    return jnp.minimum(x, 0.0) - jnp.log1p(jnp.exp(-jnp.abs(x)))


def _lane_group(width, head):
    return lax.broadcasted_iota(jnp.int32, (1, width), 1) // head


def _block_diag_rows(q4, head):
    grp = _lane_group(q4.shape[1], head)
    zero = jnp.zeros_like(q4)
    return jnp.concatenate([jnp.where(grp == g, q4, zero) for g in range(q4.shape[1] // head)], axis=0)


def _pick_diag(o, rows, head):
    grp = _lane_group(o.shape[1], head)
    out = jnp.zeros((rows, o.shape[1]), o.dtype)
    for g in range(o.shape[1] // head):
        out = jnp.where(grp == g, o[g * rows:(g + 1) * rows], out)
    return out


def _proj_kernel(x_ref, gmix_ref, wqkv_ref, wft_ref, wqm_ref, wg_ref, gains_ref, gqm_ref, bf_ref,
                 bd64_ref, bd128_ref,
                 qa_o, ka_o, va_o, qb_o, kb_o, vb_o, qm_o, g_o, kaf_o, vaf_o, kbf_o, vbf_o, lft_o,
                 *, scale_a, scale_b, scale_m):
    xn = _rms_rows(x_ref[...], gmix_ref[...]).astype(BF16)

    def seg(j):
        return _dot(xn, wqkv_ref[:, j * 512:(j + 1) * 512])

    qa = _head_norm(seg(0), bd64_ref, gains_ref[0:1, :])
    qa_o[...] = (qa * scale_a).astype(BF16)
    ka = _head_norm(seg(1), bd64_ref, gains_ref[1:2, :])
    ka_o[...] = ka.astype(BF16)
    kaf_o[...] = ka
    va = seg(2)
    va_o[...] = va.astype(BF16)
    vaf_o[...] = va
    qb = _head_norm(seg(3), bd64_ref, gains_ref[2:3, :])
    qb_o[...] = (qb * scale_b).astype(BF16)
    kb = _head_norm(seg(4), bd64_ref, gains_ref[3:4, :])
    kb_o[...] = kb.astype(BF16)
    kbf_o[...] = kb
    vb = seg(5)
    vb_o[...] = vb.astype(BF16)
    vbf_o[...] = vb
    qm = _head_norm(_dot(xn, wqm_ref[...]), bd128_ref, gqm_ref[...])
    qm_o[...] = (qm * scale_m).astype(BF16)
    for j in range(wg_ref.shape[1] // 512):
        g_o[:, j * 512:(j + 1) * 512] = jax.nn.sigmoid(
            _dot(xn, wg_ref[:, j * 512:(j + 1) * 512])).astype(BF16)
    flt = _nt_dot(wft_ref[...], xn)
    lft_o[...] = _log_sigmoid(flt[0:8, :] + bf_ref[...])


def _proj(x, gmix, wqkv, wft, wqm, wg, gains, gqm, bf, bd64, bd128, *, tm, group_rows, keep_rows):
    rows, d = x.shape
    n_tiles = rows // tm
    tpg = group_rows // tm
    kpt = keep_rows // tm
    n_groups = rows // group_rows

    def keep_map(i):
        return ((i // tpg) * kpt + jnp.clip(i % tpg - (tpg - kpt), 0, kpt - 1), 0)

    row = lambda i: (i, 0)
    wide = lambda dt: jax.ShapeDtypeStruct((rows, 512), dt)
    out_shape = ([wide(BF16)] * 7 + [jax.ShapeDtypeStruct((rows, wg.shape[1]), BF16)]
                 + [jax.ShapeDtypeStruct((n_groups * keep_rows, 512), F32)] * 2 + [wide(F32)] * 2
                 + [jax.ShapeDtypeStruct((8, rows), F32)])
    out_specs = ([pl.BlockSpec((tm, 512), row)] * 7 + [pl.BlockSpec((tm, wg.shape[1]), row)]
                 + [pl.BlockSpec((tm, 512), keep_map)] * 2 + [pl.BlockSpec((tm, 512), row)] * 2
                 + [pl.BlockSpec((8, tm), lambda i: (0, i))])
    in_specs = [pl.BlockSpec((tm, d), row)] + [_const_spec(a.shape) for a in
                                               (gmix, wqkv, wft, wqm, wg, gains, gqm, bf, bd64, bd128)]
    kern = functools.partial(_proj_kernel, scale_a=64 ** -0.5, scale_b=64 ** -0.5, scale_m=128 ** -0.5)
    return pl.pallas_call(kern, out_shape=out_shape, grid=(n_tiles,), in_specs=in_specs,
                          out_specs=out_specs, compiler_params=_cparams("arbitrary"),
                          name="proj")(x, gmix, wqkv, wft, wqm, wg, gains, gqm, bf, bd64, bd128)


def _memkv_kernel(m_ref, gmem_ref, w_ref, gkm_ref, bd128_ref, mk_o, mv_o):
    xn = _rms_rows(m_ref[...], gmem_ref[...]).astype(BF16)
    half = w_ref.shape[1] // 2
    mk_o[...] = _head_norm(_dot(xn, w_ref[:, :half]), bd128_ref, gkm_ref[...])
    mv_o[...] = _dot(xn, w_ref[:, half:])


def _memkv(mem, gmem, w, gkm, bd128, *, tm):
    rows, d = mem.shape
    half = w.shape[1] // 2
    row = lambda i: (i, 0)
    return pl.pallas_call(
        _memkv_kernel, out_shape=[jax.ShapeDtypeStruct((rows, half), F32)] * 2, grid=(rows // tm,),
        in_specs=[pl.BlockSpec((tm, d), row)] + [_const_spec(a.shape) for a in (gmem, w, gkm, bd128)],
        out_specs=[pl.BlockSpec((tm, half), row)] * 2, compiler_params=_cparams("arbitrary"),
        name="memkv")(mem, gmem, w, gkm, bd128)


def _cumsum_kernel(x_ref, tri_ref, o_ref):
    rows, length = x_ref.shape
    carry = jnp.zeros((rows, 1), F32)
    tri = tri_ref[...]
    for c in range(length // 128):
        x = x_ref[:, c * 128:(c + 1) * 128]
        hi = x.astype(BF16)
        r1 = x - hi.astype(F32)
        mid = r1.astype(BF16)
        lo = (r1 - mid.astype(F32)).astype(BF16)
        cc = _dot(hi, tri) + _dot(mid, tri) + _dot(lo, tri) + carry
        o_ref[:, c * 128:(c + 1) * 128] = cc
        carry = cc[:, 127:128]


def _cumsum_lanes(x, tri, *, rb):
    rows, length = x.shape
    return pl.pallas_call(
        _cumsum_kernel, out_shape=jax.ShapeDtypeStruct((rows, length), F32), grid=(rows // rb,),
        in_specs=[pl.BlockSpec((rb, length), lambda i: (i, 0)), _const_spec(tri.shape)],
        out_specs=pl.BlockSpec((rb, length), lambda i: (i, 0)), compiler_params=_cparams("arbitrary"),
        name="cumsum")(x, tri)


def _band_kernel(q_ref, k0_ref, k1_ref, k2_ref, v0_ref, v1_ref, v2_ref, bias_ref, o_ref, k_sc, v_sc,
                 *, reach):
    tq = q_ref.shape[0]
    band = reach + CHUNK
    i = pl.program_id(1)
    for n, (kr, vr) in enumerate(((k0_ref, v0_ref), (k1_ref, v1_ref), (k2_ref, v2_ref))):
        k_sc[n * tq:(n + 1) * tq, :] = kr[...]
        v_sc[n * tq:(n + 1) * tq, :] = vr[...]
    col = lax.broadcasted_iota(jnp.int32, (1, band), 1)
    for j in range(tq // CHUNK):
        r0 = 2 * tq + j * CHUNK - reach
        valid = (i - 2) * tq + r0 + col >= 0
        for hg in range(q_ref.shape[1] // LANE_GROUP):
            lanes = slice(hg * LANE_GROUP, (hg + 1) * LANE_GROUP)
            qbd = _block_diag_rows(q_ref[j * CHUNK:(j + 1) * CHUNK, lanes], CHUNK)
            s = _nt_dot(qbd, k_sc[r0:r0 + band, lanes]) + bias_ref[hg * 4 * CHUNK:(hg + 1) * 4 * CHUNK, :]
            s = jnp.where(valid, s, NEG)
            m = jnp.max(s, axis=-1, keepdims=True)
            p = jnp.exp(s - m)
            l = jnp.sum(p, axis=-1, keepdims=True)
            o = _dot(p.astype(BF16), v_sc[r0:r0 + band, lanes]) / l
            o_ref[j * CHUNK:(j + 1) * CHUNK, lanes] = _pick_diag(o, CHUNK, CHUNK).astype(o_ref.dtype)


def _band_prompt(q, k, v, bias, *, reach, tq):
    b, t, w = q.shape
    assert reach == 2 * tq and tq % CHUNK == 0
    blk = lambda off: pl.BlockSpec((None, tq, w), lambda bi, i: (bi, jnp.maximum(i - off, 0), 0))
    return pl.pallas_call(
        functools.partial(_band_kernel, reach=reach),
        out_shape=jax.ShapeDtypeStruct((b, t, w), BF16), grid=(b, t // tq),
        in_specs=[blk(0), blk(2), blk(1), blk(0), blk(2), blk(1), blk(0), _const_spec(bias.shape)],
        out_specs=blk(0),
        scratch_shapes=[pltpu.VMEM((3 * tq, w), BF16), pltpu.VMEM((3 * tq, w), BF16)],
        compiler_params=_cparams("arbitrary", "arbitrary"), name="band_prompt")(q, k, k, k, v, v, v, bias)


def _forget_kernel(q_ref, k_ref, v_ref, c_ref, o_ref, qbd_sc, m_sc, l_sc, acc_sc, *, head):
    tq = q_ref.shape[0]
    n_heads = q_ref.shape[1] // head
    i = pl.program_id(2)
    q4 = q_ref[...]
    grp = _lane_group(q4.shape[1], head)
    for g in range(n_heads):
        qbd_sc[g] = jnp.where(grp == g, q4, jnp.zeros_like(q4))
    m_sc[...] = jnp.full(m_sc.shape, NEG, F32)
    l_sc[...] = jnp.zeros(l_sc.shape, F32)
    acc_sc[...] = jnp.zeros(acc_sc.shape, F32)
    c_first = [c_ref[g, i][:, 0:1] for g in range(n_heads)]

    def step(j, masked):
        k0 = pl.multiple_of(j * tq, tq)
        kt = k_ref[pl.ds(k0, tq), :]
        vt = v_ref[pl.ds(k0, tq), :]
        if masked:
            keep = (lax.broadcasted_iota(jnp.int32, (tq, tq), 1)
                    <= lax.broadcasted_iota(jnp.int32, (tq, tq), 0))
        for g in range(n_heads):
            s = _nt_dot(qbd_sc[g], kt) + (c_first[g] - c_ref[g, j])
            if masked:
                s = jnp.where(keep, s, NEG)
            m_prev = m_sc[g]
            m_new = jnp.maximum(m_prev, jnp.max(s, axis=-1, keepdims=True))
            alpha = jnp.exp(m_prev - m_new)
            p = jnp.exp(s - m_new)
            l_sc[g] = alpha * l_sc[g] + jnp.sum(p, axis=-1, keepdims=True)
            acc_sc[g] = alpha * acc_sc[g] + _dot(p.astype(BF16), vt)
            m_sc[g] = m_new

    def body(j, carry):
        step(j, False)
        return carry

    lax.fori_loop(0, i, body, 0)
    step(i, True)
    out = jnp.zeros((tq, q4.shape[1]), F32)
    for g in range(n_heads):
        out = jnp.where(grp == g, acc_sc[g] / l_sc[g], out)
    o_ref[...] = out.astype(o_ref.dtype)


def _forget_prompt(q, k, v, c, *, head, tq):
    b, t, w = q.shape
    n_heads = LANE_GROUP // head
    return pl.pallas_call(
        functools.partial(_forget_kernel, head=head),
        out_shape=jax.ShapeDtypeStruct((b, t, w), BF16), grid=(b, w // LANE_GROUP, t // tq),
        in_specs=[pl.BlockSpec((None, tq, LANE_GROUP), lambda bi, hg, i: (bi, i, hg)),
                  pl.BlockSpec((None, t, LANE_GROUP), lambda bi, hg, i: (bi, 0, hg)),
                  pl.BlockSpec((None, t, LANE_GROUP), lambda bi, hg, i: (bi, 0, hg)),
                  pl.BlockSpec((None, None, n_heads, t // tq, 1, tq),
                               lambda bi, hg, i: (bi, hg, 0, 0, 0, 0))],
        out_specs=pl.BlockSpec((None, tq, LANE_GROUP), lambda bi, hg, i: (bi, i, hg)),
        scratch_shapes=[pltpu.VMEM((n_heads, tq, LANE_GROUP), BF16), pltpu.VMEM((n_heads, tq, 1), F32),
                        pltpu.VMEM((n_heads, tq, 1), F32), pltpu.VMEM((n_heads, tq, LANE_GROUP), F32)],
        compiler_params=_cparams("arbitrary", "arbitrary", "arbitrary"), name="forget_prompt")(q, k, v, c)


def _mem_kernel(q_ref, mk_ref, mv_ref, o_ref, *, head):
    for h in range(q_ref.shape[1] // head):
        lanes = slice(h * head, (h + 1) * head)
        s = _nt_dot(q_ref[:, lanes], mk_ref[:, lanes].astype(BF16))
        m = jnp.max(s, axis=-1, keepdims=True)
        p = jnp.exp(s - m)
        l = jnp.sum(p, axis=-1, keepdims=True)
        o = _dot(p.astype(BF16), mv_ref[:, lanes].astype(BF16)) / l
        o_ref[:, lanes] = o.astype(o_ref.dtype)


def _mem_attn(q, mk, mv, *, head, tq):
    b, t, w = q.shape
    n_mem = mk.shape[1]
    kv = pl.BlockSpec((None, n_mem, w), lambda bi, i: (bi, 0, 0))
    qs = pl.BlockSpec((None, tq, w), lambda bi, i: (bi, i, 0))
    return pl.pallas_call(
        functools.partial(_mem_kernel, head=head), out_shape=jax.ShapeDtypeStruct((b, t, w), BF16),
        grid=(b, t // tq), in_specs=[qs, kv, kv], out_specs=qs,
        compiler_params=_cparams("arbitrary", "arbitrary"), name="mem_attn")(q, mk, mv)


def _merge_kernel(x_ref, oa_ref, ob_ref, om_ref, g_ref, wpa_ref, wpb_ref, wpm_ref, wo_ref, y_ref):
    d = x_ref.shape[1]
    h = (g_ref[:, 0:d].astype(F32) * _dot(oa_ref[...], wpa_ref[...])
         + g_ref[:, d:2 * d].astype(F32) * _dot(ob_ref[...], wpb_ref[...])
         + g_ref[:, 2 * d:3 * d].astype(F32) * _dot(om_ref[...], wpm_ref[...]))
    y_ref[...] = x_ref[...] + _dot(h.astype(BF16), wo_ref[...])


def _merge(x, oa, ob, om, g, wpa, wpb, wpm, wo, *, tm):
    rows, d = x.shape
    row = lambda i: (i, 0)
    acts = (x, oa, ob, om, g)
    return pl.pallas_call(
        _merge_kernel, out_shape=jax.ShapeDtypeStruct((rows, d), F32), grid=(rows // tm,),
        in_specs=[pl.BlockSpec((tm, a.shape[1]), row) for a in acts]
        + [_const_spec(a.shape) for a in (wpa, wpb, wpm, wo)],
        out_specs=pl.BlockSpec((tm, d), row), compiler_params=_cparams("arbitrary"),
        name="merge")(*acts, wpa, wpb, wpm, wo)


def _mlp_kernel(x_ref, g_ref, wup_ref, wdn_ref, y_ref, *, ff_chunk):
    x = x_ref[...]
    xn = _rms_rows(x, g_ref[...]).astype(BF16)
    y = x
    for c in range(wup_ref.shape[1] // ff_chunk):
        cols = slice(c * ff_chunk, (c + 1) * ff_chunk)
        u = jnp.maximum(_dot(xn, wup_ref[:, cols]), 0.0)
        y = y + _dot((u * u).astype(BF16), wdn_ref[cols, :])
    y_ref[...] = y


def _mlp(x, g, wup, wdn, *, tm, ff_chunk):
    rows, d = x.shape
    row = lambda i: (i, 0)
    return pl.pallas_call(
        functools.partial(_mlp_kernel, ff_chunk=ff_chunk), out_shape=jax.ShapeDtypeStruct((rows, d), F32),
        grid=(rows // tm,),
        in_specs=[pl.BlockSpec((tm, d), row)] + [_const_spec(a.shape) for a in (g, wup, wdn)],
        out_specs=pl.BlockSpec((tm, d), row), compiler_params=_cparams("arbitrary"),
        name="mlp")(x, g, wup, wdn)


def _pad_rows(x, rows):
    return jnp.concatenate([x, jnp.zeros((rows - x.shape[0], x.shape[1]), x.dtype)], axis=0)


def _band_sample_kernel(q_ref, kc_ref, vc_ref, kn_ref, vn_ref, bc_ref, bn_ref, o_ref, *, head):
    s_len = q_ref.shape[0]
    n_pad = bn_ref.shape[1]
    for hg in range(q_ref.shape[1] // LANE_GROUP):
        lanes = slice(hg * LANE_GROUP, (hg + 1) * LANE_GROUP)
        rows = slice(hg * 4 * s_len, (hg + 1) * 4 * s_len)
        qbd = _block_diag_rows(q_ref[:, lanes], head)
        sc = _nt_dot(qbd, kc_ref[:, lanes].astype(BF16)) + bc_ref[rows, :]
        sn = _nt_dot(qbd, _pad_rows(kn_ref[:, lanes], n_pad)) + bn_ref[rows, :]
        m = jnp.maximum(jnp.max(sc, axis=-1, keepdims=True), jnp.max(sn, axis=-1, keepdims=True))
        pc = jnp.exp(sc - m)
        pn = jnp.exp(sn - m)
        l = jnp.sum(pc, axis=-1, keepdims=True) + jnp.sum(pn, axis=-1, keepdims=True)
        o = (_dot(pc.astype(BF16), vc_ref[:, lanes].astype(BF16))
             + _dot(pn.astype(BF16), _pad_rows(vn_ref[:, lanes], n_pad))) / l
        o_ref[:, lanes] = _pick_diag(o, s_len, head).astype(o_ref.dtype)


def _band_sample(q, kc, vc, kn, vn, bias_c, bias_n, *, head):
    b, s_len, w = q.shape
    new = pl.BlockSpec((None, s_len, w), lambda bi: (bi, 0, 0))
    cache = pl.BlockSpec((None, kc.shape[1], w), lambda bi: (bi, 0, 0))
    return pl.pallas_call(
        functools.partial(_band_sample_kernel, head=head), out_shape=jax.ShapeDtypeStruct((b, s_len, w), BF16),
        grid=(b,), in_specs=[new, cache, cache, new, new, _const_spec(bias_c.shape), _const_spec(bias_n.shape)],
        out_specs=new, compiler_params=_cparams("arbitrary"), name="band_sample")(q, kc, vc, kn, vn, bias_c, bias_n)


def _forget_sample_kernel(q_ref, kc_ref, vc_ref, kn_ref, vn_ref, c_ref, o_ref, *, head, n_pad):
    s_len = q_ref.shape[0]
    p_len = kc_ref.shape[0]
    n_heads = LANE_GROUP // head
    col = lax.broadcasted_iota(jnp.int32, (s_len, n_pad), 1)
    causal = col <= lax.broadcasted_iota(jnp.int32, (s_len, n_pad), 0)
    for hg in range(q_ref.shape[1] // LANE_GROUP):
        lanes = slice(hg * LANE_GROUP, (hg + 1) * LANE_GROUP)
        qbd = _block_diag_rows(q_ref[:, lanes], head)
        bias_c, bias_n = [], []
        for g in range(n_heads):
            c_last = c_ref[hg, g:g + 1, p_len - 1:p_len]
            bias_c.append(jnp.broadcast_to(c_last - c_ref[hg, g:g + 1, 0:p_len], (s_len, p_len)))
            bias_n.append(jnp.where(causal, c_last - c_ref[hg, g:g + 1, p_len:p_len + n_pad], NEG))
        sc = _nt_dot(qbd, kc_ref[:, lanes].astype(BF16)) + jnp.concatenate(bias_c, axis=0)
        sn = _nt_dot(qbd, _pad_rows(kn_ref[:, lanes], n_pad)) + jnp.concatenate(bias_n, axis=0)
        m = jnp.maximum(jnp.max(sc, axis=-1, keepdims=True), jnp.max(sn, axis=-1, keepdims=True))
        pc = jnp.exp(sc - m)
        pn = jnp.exp(sn - m)
        l = jnp.sum(pc, axis=-1, keepdims=True) + jnp.sum(pn, axis=-1, keepdims=True)
        o = (_dot(pc.astype(BF16), vc_ref[:, lanes].astype(BF16))
             + _dot(pn.astype(BF16), _pad_rows(vn_ref[:, lanes], n_pad))) / l
        o_ref[:, lanes] = _pick_diag(o, s_len, head).astype(o_ref.dtype)


def _forget_sample(q, kc, vc, kn, vn, c, *, head, n_pad):
    b, s_len, w = q.shape
    new = pl.BlockSpec((None, s_len, w), lambda bi: (bi, 0, 0))
    cache = pl.BlockSpec((None, kc.shape[1], w), lambda bi: (bi, 0, 0))
    cs = pl.BlockSpec((None,) + c.shape[1:], lambda bi: (bi, 0, 0, 0))
    return pl.pallas_call(
        functools.partial(_forget_sample_kernel, head=head, n_pad=n_pad),
        out_shape=jax.ShapeDtypeStruct((b, s_len, w), BF16), grid=(b,),
        in_specs=[new, cache, cache, new, new, cs], out_specs=new,
        compiler_params=_cparams("arbitrary"), name="forget_sample")(q, kc, vc, kn, vn, c)


def _block_diag_mean(head, size=256):
    r = jnp.arange(size) // head
    return jnp.where(r[:, None] == r[None, :], 1.0 / head, 0.0).astype(BF16)


def _rel_bias_rows(rel_bias, dist):
    bias = rel_bias[jnp.clip(dist, -REL_CLIP, REL_CLIP) + REL_CLIP]
    return jnp.moveaxis(bias, -1, 0).reshape(-1, dist.shape[1]).astype(F32)


def kernel(x_prompt, x_sample, mem_prompt, cache_a_k, cache_a_v, cache_b_k, cache_b_v, cache_b_logf,
           cache_mem_k, cache_mem_v, g_mix, w_in, b_f, g_qa, g_ka, g_qb, g_kb, g_qm, g_km, rel_bias,
           g_mem, w_mkv, w_pa, w_pb, w_pm, w_o, g_ffn, w_up, w_down):
    depth = w_in.shape[0]
    assert depth == 1
    batch, seq, d = x_prompt.shape
    dec_b, dec_s, _ = x_sample.shape
    _, _, n_cache, h_a, dh_a = cache_a_k.shape
    _, _, past, h_b, dh_b = cache_b_k.shape
    _, _, n_mem, h_m, dh_m = cache_mem_k.shape
    w_a, w_b, w_m = h_a * dh_a, h_b * dh_b, h_m * dh_m
    assert w_a == w_b == w_m == 512 and dh_a == dh_b == 64 and dh_m == 128 and h_b == 8
    keep_p = min(n_cache, seq)
    n_pad = 128
    l = 0

    qkv_end = 3 * w_a + 3 * w_b
    f_end = qkv_end + h_b
    qm_end = f_end + w_m
    w = w_in[l]
    wqkv = w[:, :qkv_end].astype(BF16)
    wft = jnp.zeros((16, d), BF16).at[:h_b].set(w[:, qkv_end:f_end].T.astype(BF16))
    wqm = w[:, f_end:qm_end].astype(BF16)
    wg = w[:, qm_end:].astype(BF16)
    gains = jnp.stack([jnp.tile(g[l], h_a) for g in (g_qa, g_ka, g_qb, g_kb)]).astype(F32)
    gqm = jnp.tile(g_qm[l], h_m)[None, :]
    gkm = jnp.tile(g_km[l], h_m)[None, :]
    bf = b_f[l][:, None].astype(F32)
    bd64 = _block_diag_mean(dh_a)
    bd128 = _block_diag_mean(dh_m)
    tri = (jnp.arange(128)[:, None] <= jnp.arange(128)[None, :]).astype(BF16)
    gmix = g_mix[l][None, :]
    proj_w = (gmix, wqkv, wft, wqm, wg, gains, gqm, bf, bd64, bd128)
    wpa, wpb, wpm, wo = (a[l].astype(BF16) for a in (w_pa, w_pb, w_pm, w_o))
    wup, wdn = w_up[l].astype(BF16), w_down[l].astype(BF16)
    gffn = g_ffn[l][None, :]

    xp = x_prompt.reshape(batch * seq, d)
    (qa, ka, va, qb, kb, vb, qm, gates, ka_f, va_f, kb_f, vb_f, lft) = _proj(
        xp, *proj_w, tm=512, group_rows=seq, keep_rows=keep_p)
    mk_f, mv_f = _memkv(mem_prompt.reshape(batch * n_mem, d), g_mem[l][None, :], w_mkv[l].astype(BF16),
                        gkm, bd128, tm=256)
    tq_b = 512
    lf_rows = lft.reshape(h_b, batch, seq).transpose(1, 0, 2).reshape(batch * h_b, seq)
    c_p = _cumsum_lanes(lf_rows, tri, rb=batch * h_b)
    c_p = c_p.reshape(batch, h_b // 4, 4, seq // tq_b, 1, tq_b)
    three = lambda a: a.reshape(batch, seq, -1)
    dist_p = jnp.arange(CHUNK)[:, None] + n_cache - jnp.arange(n_cache + CHUNK)[None, :]
    o_a = _band_prompt(three(qa), three(ka), three(va), _rel_bias_rows(rel_bias[l], dist_p),
                       reach=n_cache, tq=256)
    o_b = _forget_prompt(three(qb), three(kb), three(vb), c_p, head=dh_b, tq=tq_b)
    o_m = _mem_attn(three(qm), mk_f.reshape(batch, n_mem, w_m), mv_f.reshape(batch, n_mem, w_m),
                    head=dh_m, tq=512)
    x1 = _merge(xp, o_a.reshape(-1, w_a), o_b.reshape(-1, w_b), o_m.reshape(-1, w_m), gates,
                wpa, wpb, wpm, wo, tm=512)
    y_prompt = _mlp(x1, gffn, wup, wdn, tm=512, ff_chunk=1024).reshape(batch, seq, d)

    xs = x_sample.reshape(dec_b * dec_s, d)
    (qa_s, ka_s, va_s, qb_s, kb_s, vb_s, qm_s, gates_s, ka_sf, va_sf, kb_sf, vb_sf, lft_s) = _proj(
        xs, *proj_w, tm=256, group_rows=dec_b * dec_s, keep_rows=dec_b * dec_s)
    sthree = lambda a: a.reshape(dec_b, dec_s, -1)
    q_pos = past + jnp.arange(dec_s)
    dist_c = q_pos[:, None] - (past - n_cache + jnp.arange(n_cache))[None, :]
    dist_n = q_pos[:, None] - q_pos[None, :]
    bias_c = _rel_bias_rows(rel_bias[l], dist_c)
    bias_n = jnp.full((h_a * dec_s, n_pad), NEG, F32).at[:, :dec_s].set(_rel_bias_rows(rel_bias[l], dist_n))
    o_a_s = _band_sample(sthree(qa_s), cache_a_k[l].reshape(dec_b, n_cache, w_a),
                         cache_a_v[l].reshape(dec_b, n_cache, w_a), sthree(ka_s), sthree(va_s),
                         bias_c, bias_n, head=dh_a)
    lf_new = lft_s.reshape(h_b, dec_b, dec_s).transpose(1, 0, 2)
    lf_all = jnp.concatenate([cache_b_logf[l].astype(F32).transpose(0, 2, 1), lf_new,
                              jnp.zeros((dec_b, h_b, n_pad - dec_s), F32)], axis=-1)
    c_s = _cumsum_lanes(lf_all.reshape(dec_b * h_b, past + n_pad), tri, rb=32)
    c_s = c_s.reshape(dec_b, h_b // 4, 4, past + n_pad)
    o_b_s = _forget_sample(sthree(qb_s), cache_b_k[l].reshape(dec_b, past, w_b),
                           cache_b_v[l].reshape(dec_b, past, w_b), sthree(kb_s), sthree(vb_s), c_s,
                           head=dh_b, n_pad=n_pad)
    o_m_s = _mem_attn(sthree(qm_s), cache_mem_k[l].reshape(dec_b, n_mem, w_m),
                      cache_mem_v[l].reshape(dec_b, n_mem, w_m), head=dh_m, tq=dec_s)
    x1_s = _merge(xs, o_a_s.reshape(-1, w_a), o_b_s.reshape(-1, w_b), o_m_s.reshape(-1, w_m), gates_s,
                  wpa, wpb, wpm, wo, tm=256)
    y_sample = _mlp(x1_s, gffn, wup, wdn, tm=256, ff_chunk=1024).reshape(dec_b, dec_s, d)

    lead = lambda a, *shape: a.reshape((depth,) + shape)
    return (y_prompt, y_sample,
            lead(ka_f, batch, keep_p, h_a, dh_a), lead(va_f, batch, keep_p, h_a, dh_a),
            lead(kb_f, batch, seq, h_b, dh_b), lead(vb_f, batch, seq, h_b, dh_b),
            lead(lft.T, batch, seq, h_b),
            lead(mk_f, batch, n_mem, h_m, dh_m), lead(mv_f, batch, n_mem, h_m, dh_m),
            lead(ka_sf, dec_b, dec_s, h_a, dh_a), lead(va_sf, dec_b, dec_s, h_a, dh_a),
            lead(kb_sf, dec_b, dec_s, h_b, dh_b), lead(vb_sf, dec_b, dec_s, h_b, dh_b),
            lead(lft_s.T, dec_b, dec_s, h_b))
```

```python
import functools

import jax
import jax.numpy as jnp
from jax import lax
from jax.experimental import pallas as pl
from jax.experimental.pallas import tpu as pltpu

BF16 = jnp.bfloat16
F32 = jnp.float32

EPS = 1e-6
CHUNK = 64
REL_CLIP = 128
NEG = -1e30
LANE_GROUP = 256
VMEM_LIMIT = 56 * 1024 * 1024


def _cparams(*sem):
    return pltpu.CompilerParams(dimension_semantics=sem, vmem_limit_bytes=VMEM_LIMIT)


def _const_spec(shape):
    nd = len(shape)
    return pl.BlockSpec(shape, lambda *_: (0,) * nd, pipeline_mode=pl.Buffered(1))


def _nt_dot(a, b):
    return lax.dot_general(a, b, (((1,), (1,)), ((), ())), preferred_element_type=F32)


def _dot(a, b):
    return jnp.dot(a, b, preferred_element_type=F32)


def _rms_rows(x, gain):
    ms = jnp.mean(x * x, axis=-1, keepdims=True)
    return x * lax.rsqrt(ms + EPS) * gain


def _head_norm(z, bd_ref, gain):
    z2 = (z * z).astype(BF16)
    ms = jnp.concatenate(
        [_dot(z2[:, h * 256:(h + 1) * 256], bd_ref[...]) for h in range(z.shape[1] // 256)], axis=-1)
    return z * lax.rsqrt(ms + EPS) * gain


def _log_sigmoid(x):
    return jnp.minimum(x, 0.0) - jnp.log1p(jnp.exp(-jnp.abs(x)))


def _lane_group(width, head):
    return lax.broadcasted_iota(jnp.int32, (1, width), 1) // head


def _block_diag_rows(q4, head):
    grp = _lane_group(q4.shape[1], head)
    zero = jnp.zeros_like(q4)
    return jnp.concatenate([jnp.where(grp == g, q4, zero) for g in range(q4.shape[1] // head)], axis=0)


def _pick_diag(o, rows, head):
    grp = _lane_group(o.shape[1], head)
    out = jnp.zeros((rows, o.shape[1]), o.dtype)
    for g in range(o.shape[1] // head):
        out = jnp.where(grp == g, o[g * rows:(g + 1) * rows], out)
    return out


def _proj_kernel(x_ref, gmix_ref, wqkv_ref, wft_ref, wqm_ref, wg_ref, gains_ref, gqm_ref, bf_ref,
                 bd64_ref, bd128_ref,
                 qa_o, ka_o, va_o, qb_o, kb_o, vbt_o, qm_o, g_o, kaf_o, vaf_o, kbf_o, vbf_o, lft_o,
                 *, scale_a, scale_b, scale_m):
    xn = _rms_rows(x_ref[...], gmix_ref[...]).astype(BF16)

    def seg(j):
        return _dot(xn, wqkv_ref[:, j * 512:(j + 1) * 512])

    qa = _head_norm(seg(0), bd64_ref, gains_ref[0:1, :])
    qa_o[...] = (qa * scale_a).astype(BF16)
    ka = _head_norm(seg(1), bd64_ref, gains_ref[1:2, :])
    ka_o[...] = ka.astype(BF16)
    kaf_o[...] = ka
    va = seg(2)
    va_o[...] = va.astype(BF16)
    vaf_o[...] = va
    qb = _head_norm(seg(3), bd64_ref, gains_ref[2:3, :])
    qb_o[...] = (qb * scale_b).astype(BF16)
    kb = _head_norm(seg(4), bd64_ref, gains_ref[3:4, :])
    kb_o[...] = kb.astype(BF16)
    kbf_o[...] = kb
    vb = seg(5)
    vbt_o[...] = vb.T.astype(BF16)
    vbf_o[...] = vb
    qm = _head_norm(_dot(xn, wqm_ref[...]), bd128_ref, gqm_ref[...])
    qm_o[...] = (qm * scale_m).astype(BF16)
    for j in range(wg_ref.shape[1] // 512):
        g_o[:, j * 512:(j + 1) * 512] = jax.nn.sigmoid(
            _dot(xn, wg_ref[:, j * 512:(j + 1) * 512])).astype(BF16)
    flt = _nt_dot(wft_ref[...], xn)
    lft_o[...] = _log_sigmoid(flt[0:8, :] + bf_ref[...])


def _proj(x, gmix, wqkv, wft, wqm, wg, gains, gqm, bf, bd64, bd128, *, tm, group_rows, keep_rows):
    rows, d = x.shape
    n_tiles = rows // tm
    tpg = group_rows // tm
    kpt = keep_rows // tm
    n_groups = rows // group_rows

    def keep_map(i):
        return ((i // tpg) * kpt + jnp.clip(i % tpg - (tpg - kpt), 0, kpt - 1), 0)

    row = lambda i: (i, 0)
    wide = lambda dt: (jax.ShapeDtypeStruct((rows, 512), dt), pl.BlockSpec((tm, 512), row))
    kept = (jax.ShapeDtypeStruct((n_groups * keep_rows, 512), F32), pl.BlockSpec((tm, 512), keep_map))
    outs = ([wide(BF16)] * 5
            + [(jax.ShapeDtypeStruct((n_tiles, 512, tm), BF16),
                pl.BlockSpec((None, 512, tm), lambda i: (i, 0, 0)))]
            + [wide(BF16)]
            + [(jax.ShapeDtypeStruct((rows, wg.shape[1]), BF16), pl.BlockSpec((tm, wg.shape[1]), row))]
            + [kept] * 2 + [wide(F32)] * 2
            + [(jax.ShapeDtypeStruct((8, rows), F32), pl.BlockSpec((8, tm), lambda i: (0, i)))])
    in_specs = [pl.BlockSpec((tm, d), row)] + [_const_spec(a.shape) for a in
                                               (gmix, wqkv, wft, wqm, wg, gains, gqm, bf, bd64, bd128)]
    kern = functools.partial(_proj_kernel, scale_a=64 ** -0.5, scale_b=64 ** -0.5, scale_m=128 ** -0.5)
    return pl.pallas_call(kern, out_shape=[o[0] for o in outs], grid=(n_tiles,), in_specs=in_specs,
                          out_specs=[o[1] for o in outs], compiler_params=_cparams("arbitrary"),
                          name="proj")(x, gmix, wqkv, wft, wqm, wg, gains, gqm, bf, bd64, bd128)


def _memkv_kernel(m_ref, gmem_ref, w_ref, gkm_ref, bd128_ref, mk_o, mv_o):
    xn = _rms_rows(m_ref[...], gmem_ref[...]).astype(BF16)
    half = w_ref.shape[1] // 2
    mk_o[...] = _head_norm(_dot(xn, w_ref[:, :half]), bd128_ref, gkm_ref[...])
    mv_o[...] = _dot(xn, w_ref[:, half:])


def _memkv(mem, gmem, w, gkm, bd128, *, tm):
    rows, d = mem.shape
    half = w.shape[1] // 2
    row = lambda i: (i, 0)
    return pl.pallas_call(
        _memkv_kernel, out_shape=[jax.ShapeDtypeStruct((rows, half), F32)] * 2, grid=(rows // tm,),
        in_specs=[pl.BlockSpec((tm, d), row)] + [_const_spec(a.shape) for a in (gmem, w, gkm, bd128)],
        out_specs=[pl.BlockSpec((tm, half), row)] * 2, compiler_params=_cparams("arbitrary"),
        name="memkv")(mem, gmem, w, gkm, bd128)


def _cumsum_kernel(x_ref, tri_ref, o_ref):
    rows, length = x_ref.shape
    carry = jnp.zeros((rows, 1), F32)
    tri = tri_ref[...]
    for c in range(length // 128):
        x = x_ref[:, c * 128:(c + 1) * 128]
        hi = x.astype(BF16)
        r1 = x - hi.astype(F32)
        mid = r1.astype(BF16)
        lo = (r1 - mid.astype(F32)).astype(BF16)
        cc = _dot(hi, tri) + _dot(mid, tri) + _dot(lo, tri) + carry
        o_ref[:, c * 128:(c + 1) * 128] = cc
        carry = cc[:, 127:128]


def _cumsum_lanes(x, tri, *, rb):
    rows, length = x.shape
    return pl.pallas_call(
        _cumsum_kernel, out_shape=jax.ShapeDtypeStruct((rows, length), F32), grid=(rows // rb,),
        in_specs=[pl.BlockSpec((rb, length), lambda i: (i, 0)), _const_spec(tri.shape)],
        out_specs=pl.BlockSpec((rb, length), lambda i: (i, 0)), compiler_params=_cparams("arbitrary"),
        name="cumsum")(x, tri)


def _band_kernel(q_ref, k0_ref, k1_ref, k2_ref, v0_ref, v1_ref, v2_ref, bias_ref, o_ref, k_sc, v_sc,
                 *, reach):
    tq = q_ref.shape[0]
    band = reach + CHUNK
    i = pl.program_id(1)
    for n, (kr, vr) in enumerate(((k0_ref, v0_ref), (k1_ref, v1_ref), (k2_ref, v2_ref))):
        k_sc[n * tq:(n + 1) * tq, :] = kr[...]
        v_sc[n * tq:(n + 1) * tq, :] = vr[...]
    col = lax.broadcasted_iota(jnp.int32, (1, band), 1)
    for j in range(tq // CHUNK):
        r0 = 2 * tq + j * CHUNK - reach
        valid = (i - 2) * tq + r0 + col >= 0
        for hg in range(q_ref.shape[1] // LANE_GROUP):
            lanes = slice(hg * LANE_GROUP, (hg + 1) * LANE_GROUP)
            qbd = _block_diag_rows(q_ref[j * CHUNK:(j + 1) * CHUNK, lanes], CHUNK)
            s = _nt_dot(qbd, k_sc[r0:r0 + band, lanes]) + bias_ref[hg * 4 * CHUNK:(hg + 1) * 4 * CHUNK, :]
            s = jnp.where(valid, s, NEG)
            m = jnp.max(s, axis=-1, keepdims=True)
            p = jnp.exp(s - m)
            l = jnp.sum(p, axis=-1, keepdims=True)
            o = _dot(p.astype(BF16), v_sc[r0:r0 + band, lanes]) / l
            o_ref[j * CHUNK:(j + 1) * CHUNK, lanes] = _pick_diag(o, CHUNK, CHUNK).astype(o_ref.dtype)


def _band_prompt(q, k, v, bias, *, reach, tq):
    b, t, w = q.shape
    assert reach == 2 * tq and tq % CHUNK == 0
    blk = lambda off: pl.BlockSpec((None, tq, w), lambda bi, i: (bi, jnp.maximum(i - off, 0), 0))
    return pl.pallas_call(
        functools.partial(_band_kernel, reach=reach),
        out_shape=jax.ShapeDtypeStruct((b, t, w), BF16), grid=(b, t // tq),
        in_specs=[blk(0), blk(2), blk(1), blk(0), blk(2), blk(1), blk(0), _const_spec(bias.shape)],
        out_specs=blk(0),
        scratch_shapes=[pltpu.VMEM((3 * tq, w), BF16), pltpu.VMEM((3 * tq, w), BF16)],
        compiler_params=_cparams("arbitrary", "arbitrary"), name="band_prompt")(q, k, k, k, v, v, v, bias)


def _forget_kernel(q_ref, k_ref, vt_ref, c_ref, o_ref, qt_sc, m_sc, l_sc, acc_sc, *, head):
    tq, width = q_ref.shape
    n_heads = width // head
    i = pl.program_id(2)
    q_t = q_ref[...].astype(F32).T
    row_grp = lax.broadcasted_iota(jnp.int32, (width, 1), 0) // head
    for g in range(n_heads):
        qt_sc[g] = jnp.where(row_grp == g, q_t, 0.0).astype(BF16)
    m_sc[...] = jnp.full(m_sc.shape, NEG, F32)
    l_sc[...] = jnp.zeros(l_sc.shape, F32)
    acc_sc[...] = jnp.zeros(acc_sc.shape, F32)
    c_head = c_ref[pl.ds(pl.multiple_of(i * tq, tq), 8), :]
    c_first = [c_head[0:1, g:g + 1] for g in range(n_heads)]

    def step(j, masked):
        k0 = pl.multiple_of(j * tq, tq)
        kt = k_ref[pl.ds(k0, tq), :]
        cb = c_ref[pl.ds(k0, tq), :]
        if masked:
            keep = (lax.broadcasted_iota(jnp.int32, (tq, tq), 0)
                    <= lax.broadcasted_iota(jnp.int32, (tq, tq), 1))
        for g in range(n_heads):
            s = _dot(kt, qt_sc[g]) + (c_first[g] - cb[:, g:g + 1])
            if masked:
                s = jnp.where(keep, s, NEG)
            m_prev = m_sc[g]
            m_new = jnp.maximum(m_prev, jnp.max(s, axis=0, keepdims=True))
            alpha = jnp.exp(m_prev - m_new)
            p = jnp.exp(s - m_new)
            l_sc[g] = alpha * l_sc[g] + jnp.sum(p, axis=0, keepdims=True)
            acc_sc[g] = alpha * acc_sc[g] + _dot(vt_ref[j, g], p.astype(BF16))
            m_sc[g] = m_new

    def body(j, carry):
        step(j, False)
        return carry

    lax.fori_loop(0, i, body, 0)
    step(i, True)
    o_t = jnp.concatenate([acc_sc[g] / l_sc[g] for g in range(n_heads)], axis=0)
    o_ref[...] = o_t.T.astype(o_ref.dtype)


def _forget_prompt(q, k, vt, c, *, head, tq):
    b, t, w = q.shape
    n_heads = LANE_GROUP // head
    return pl.pallas_call(
        functools.partial(_forget_kernel, head=head),
        out_shape=jax.ShapeDtypeStruct((b, t, w), BF16), grid=(b, w // LANE_GROUP, t // tq),
        in_specs=[pl.BlockSpec((None, tq, LANE_GROUP), lambda bi, hg, i: (bi, i, hg)),
                  pl.BlockSpec((None, t, LANE_GROUP), lambda bi, hg, i: (bi, 0, hg)),
                  pl.BlockSpec((None, t // tq, n_heads, head, tq), lambda bi, hg, i: (bi, 0, hg, 0, 0)),
                  pl.BlockSpec((None, None, t, n_heads), lambda bi, hg, i: (bi, hg, 0, 0))],
        out_specs=pl.BlockSpec((None, tq, LANE_GROUP), lambda bi, hg, i: (bi, i, hg)),
        scratch_shapes=[pltpu.VMEM((n_heads, LANE_GROUP, tq), BF16), pltpu.VMEM((n_heads, 1, tq), F32),
                        pltpu.VMEM((n_heads, 1, tq), F32), pltpu.VMEM((n_heads, head, tq), F32)],
        compiler_params=_cparams("arbitrary", "arbitrary", "arbitrary"), name="forget_prompt")(q, k, vt, c)


def _mem_kernel(q_ref, mk_ref, mv_ref, o_ref, *, head):
    for h in range(q_ref.shape[1] // head):
        lanes = slice(h * head, (h + 1) * head)
        s = _nt_dot(q_ref[:, lanes], mk_ref[:, lanes].astype(BF16))
        m = jnp.max(s, axis=-1, keepdims=True)
        p = jnp.exp(s - m)
        l = jnp.sum(p, axis=-1, keepdims=True)
        o = _dot(p.astype(BF16), mv_ref[:, lanes].astype(BF16)) / l
        o_ref[:, lanes] = o.astype(o_ref.dtype)


def _mem_attn(q, mk, mv, *, head, tq):
    b, t, w = q.shape
    n_mem = mk.shape[1]
    kv = pl.BlockSpec((None, n_mem, w), lambda bi, i: (bi, 0, 0))
    qs = pl.BlockSpec((None, tq, w), lambda bi, i: (bi, i, 0))
    return pl.pallas_call(
        functools.partial(_mem_kernel, head=head), out_shape=jax.ShapeDtypeStruct((b, t, w), BF16),
        grid=(b, t // tq), in_specs=[qs, kv, kv], out_specs=qs,
        compiler_params=_cparams("arbitrary", "arbitrary"), name="mem_attn")(q, mk, mv)


def _merge_kernel(x_ref, oa_ref, ob_ref, om_ref, g_ref, wpa_ref, wpb_ref, wpm_ref, wo_ref, y_ref):
    d = x_ref.shape[1]
    h = (g_ref[:, 0:d].astype(F32) * _dot(oa_ref[...], wpa_ref[...])
         + g_ref[:, d:2 * d].astype(F32) * _dot(ob_ref[...], wpb_ref[...])
         + g_ref[:, 2 * d:3 * d].astype(F32) * _dot(om_ref[...], wpm_ref[...]))
    y_ref[...] = x_ref[...] + _dot(h.astype(BF16), wo_ref[...])


def _merge(x, oa, ob, om, g, wpa, wpb, wpm, wo, *, tm):
    rows, d = x.shape
    row = lambda i: (i, 0)
    acts = (x, oa, ob, om, g)
    return pl.pallas_call(
        _merge_kernel, out_shape=jax.ShapeDtypeStruct((rows, d), F32), grid=(rows // tm,),
        in_specs=[pl.BlockSpec((tm, a.shape[1]), row) for a in acts]
        + [_const_spec(a.shape) for a in (wpa, wpb, wpm, wo)],
        out_specs=pl.BlockSpec((tm, d), row), compiler_params=_cparams("arbitrary"),
        name="merge")(*acts, wpa, wpb, wpm, wo)


def _mlp_kernel(x_ref, g_ref, wup_ref, wdn_ref, y_ref, *, ff_chunk):
    x = x_ref[...]
    xn = _rms_rows(x, g_ref[...]).astype(BF16)
    y = x
    for c in range(wup_ref.shape[1] // ff_chunk):
        cols = slice(c * ff_chunk, (c + 1) * ff_chunk)
        u = jnp.maximum(_dot(xn, wup_ref[:, cols]), 0.0)
        y = y + _dot((u * u).astype(BF16), wdn_ref[cols, :])
    y_ref[...] = y


def _mlp(x, g, wup, wdn, *, tm, ff_chunk):
    rows, d = x.shape
    row = lambda i: (i, 0)
    return pl.pallas_call(
        functools.partial(_mlp_kernel, ff_chunk=ff_chunk), out_shape=jax.ShapeDtypeStruct((rows, d), F32),
        grid=(rows // tm,),
        in_specs=[pl.BlockSpec((tm, d), row)] + [_const_spec(a.shape) for a in (g, wup, wdn)],
        out_specs=pl.BlockSpec((tm, d), row), compiler_params=_cparams("arbitrary"),
        name="mlp")(x, g, wup, wdn)


def _pad_rows(x, rows):
    return jnp.concatenate([x, jnp.zeros((rows - x.shape[0], x.shape[1]), x.dtype)], axis=0)


def _band_sample_kernel(q_ref, kc_ref, vc_ref, kn_ref, vn_ref, bc_ref, bn_ref, o_ref, *, head):
    s_len = q_ref.shape[0]
    n_pad = bn_ref.shape[1]
    for hg in range(q_ref.shape[1] // LANE_GROUP):
        lanes = slice(hg * LANE_GROUP, (hg + 1) * LANE_GROUP)
        rows = slice(hg * 4 * s_len, (hg + 1) * 4 * s_len)
        qbd = _block_diag_rows(q_ref[:, lanes], head)
        sc = _nt_dot(qbd, kc_ref[:, lanes].astype(BF16)) + bc_ref[rows, :]
        sn = _nt_dot(qbd, _pad_rows(kn_ref[:, lanes].astype(BF16), n_pad)) + bn_ref[rows, :]
        m = jnp.maximum(jnp.max(sc, axis=-1, keepdims=True), jnp.max(sn, axis=-1, keepdims=True))
        pc = jnp.exp(sc - m)
        pn = jnp.exp(sn - m)
        l = jnp.sum(pc, axis=-1, keepdims=True) + jnp.sum(pn, axis=-1, keepdims=True)
        o = (_dot(pc.astype(BF16), vc_ref[:, lanes].astype(BF16))
             + _dot(pn.astype(BF16), _pad_rows(vn_ref[:, lanes].astype(BF16), n_pad))) / l
        o_ref[:, lanes] = _pick_diag(o, s_len, head).astype(o_ref.dtype)


def _band_sample(q, kc, vc, kn, vn, bias_c, bias_n, *, head):
    b, s_len, w = q.shape
    new = pl.BlockSpec((None, s_len, w), lambda bi: (bi, 0, 0))
    cache = pl.BlockSpec((None, kc.shape[1], w), lambda bi: (bi, 0, 0))
    return pl.pallas_call(
        functools.partial(_band_sample_kernel, head=head), out_shape=jax.ShapeDtypeStruct((b, s_len, w), BF16),
        grid=(b,), in_specs=[new, cache, cache, new, new, _const_spec(bias_c.shape), _const_spec(bias_n.shape)],
        out_specs=new, compiler_params=_cparams("arbitrary"), name="band_sample")(q, kc, vc, kn, vn, bias_c, bias_n)


def _forget_sample_kernel(q_ref, kc_ref, vc_ref, kn_ref, vn_ref, c_ref, o_ref, *, head, n_pad):
    s_len = q_ref.shape[0]
    p_len = kc_ref.shape[0]
    n_heads = LANE_GROUP // head
    col = lax.broadcasted_iota(jnp.int32, (s_len, n_pad), 1)
    causal = col <= lax.broadcasted_iota(jnp.int32, (s_len, n_pad), 0)
    for hg in range(q_ref.shape[1] // LANE_GROUP):
        lanes = slice(hg * LANE_GROUP, (hg + 1) * LANE_GROUP)
        qbd = _block_diag_rows(q_ref[:, lanes], head)
        bias_c, bias_n = [], []
        for g in range(n_heads):
            c_last = c_ref[hg, g:g + 1, p_len - 1:p_len]
            bias_c.append(jnp.broadcast_to(c_last - c_ref[hg, g:g + 1, 0:p_len], (s_len, p_len)))
            bias_n.append(jnp.where(causal, c_last - c_ref[hg, g:g + 1, p_len:p_len + n_pad], NEG))
        sc = _nt_dot(qbd, kc_ref[:, lanes].astype(BF16)) + jnp.concatenate(bias_c, axis=0)
        sn = _nt_dot(qbd, _pad_rows(kn_ref[:, lanes].astype(BF16), n_pad)) + jnp.concatenate(bias_n, axis=0)
        m = jnp.maximum(jnp.max(sc, axis=-1, keepdims=True), jnp.max(sn, axis=-1, keepdims=True))
        pc = jnp.exp(sc - m)
        pn = jnp.exp(sn - m)
        l = jnp.sum(pc, axis=-1, keepdims=True) + jnp.sum(pn, axis=-1, keepdims=True)
        o = (_dot(pc.astype(BF16), vc_ref[:, lanes].astype(BF16))
             + _dot(pn.astype(BF16), _pad_rows(vn_ref[:, lanes].astype(BF16), n_pad))) / l
        o_ref[:, lanes] = _pick_diag(o, s_len, head).astype(o_ref.dtype)


def _forget_sample(q, kc, vc, kn, vn, c, *, head, n_pad):
    b, s_len, w = q.shape
    new = pl.BlockSpec((None, s_len, w), lambda bi: (bi, 0, 0))
    cache = pl.BlockSpec((None, kc.shape[1], w), lambda bi: (bi, 0, 0))
    cs = pl.BlockSpec((None,) + c.shape[1:], lambda bi: (bi, 0, 0, 0))
    return pl.pallas_call(
        functools.partial(_forget_sample_kernel, head=head, n_pad=n_pad),
        out_shape=jax.ShapeDtypeStruct((b, s_len, w), BF16), grid=(b,),
        in_specs=[new, cache, cache, new, new, cs], out_specs=new,
        compiler_params=_cparams("arbitrary"), name="forget_sample")(q, kc, vc, kn, vn, c)


def _block_diag_mean(head, size=256):
    r = jnp.arange(size) // head
    return jnp.where(r[:, None] == r[None, :], 1.0 / head, 0.0).astype(BF16)


def _rel_bias_rows(rel_bias, d0, n_q, n_k):
    dist = d0 + (n_q - 1) - jnp.arange(n_q + n_k - 1)
    e = rel_bias[jnp.clip(dist, -REL_CLIP, REL_CLIP) + REL_CLIP].T.astype(F32)
    rows = jnp.stack([e[:, n_q - 1 - q:n_q - 1 - q + n_k] for q in range(n_q)], axis=1)
    return rows.reshape(-1, n_k)


def kernel(x_prompt, x_sample, mem_prompt, cache_a_k, cache_a_v, cache_b_k, cache_b_v, cache_b_logf,
           cache_mem_k, cache_mem_v, g_mix, w_in, b_f, g_qa, g_ka, g_qb, g_kb, g_qm, g_km, rel_bias,
           g_mem, w_mkv, w_pa, w_pb, w_pm, w_o, g_ffn, w_up, w_down):
    depth = w_in.shape[0]
    assert depth == 1
    batch, seq, d = x_prompt.shape
    dec_b, dec_s, _ = x_sample.shape
    _, _, n_cache, h_a, dh_a = cache_a_k.shape
    _, _, past, h_b, dh_b = cache_b_k.shape
    _, _, n_mem, h_m, dh_m = cache_mem_k.shape
    w_a, w_b, w_m = h_a * dh_a, h_b * dh_b, h_m * dh_m
    assert w_a == w_b == w_m == 512 and dh_a == dh_b == 64 and dh_m == 128 and h_b == 8
    keep_p = min(n_cache, seq)
    n_pad = 128
    l = 0

    qkv_end = 3 * w_a + 3 * w_b
    f_end = qkv_end + h_b
    qm_end = f_end + w_m
    w = w_in[l]
    wqkv = w[:, :qkv_end].astype(BF16)
    wft = jnp.zeros((16, d), BF16).at[:h_b].set(w[:, qkv_end:f_end].T.astype(BF16))
    wqm = w[:, f_end:qm_end].astype(BF16)
    wg = w[:, qm_end:].astype(BF16)
    gains = jnp.stack([jnp.tile(g[l], h_a) for g in (g_qa, g_ka, g_qb, g_kb)]).astype(F32)
    gqm = jnp.tile(g_qm[l], h_m)[None, :]
    gkm = jnp.tile(g_km[l], h_m)[None, :]
    bf = b_f[l][:, None].astype(F32)
    bd64 = _block_diag_mean(dh_a)
    bd128 = _block_diag_mean(dh_m)
    tri = (jnp.arange(128)[:, None] <= jnp.arange(128)[None, :]).astype(BF16)
    gmix = g_mix[l][None, :]
    proj_w = (gmix, wqkv, wft, wqm, wg, gains, gqm, bf, bd64, bd128)
    wpa, wpb, wpm, wo = (a[l].astype(BF16) for a in (w_pa, w_pb, w_pm, w_o))
    wup, wdn = w_up[l].astype(BF16), w_down[l].astype(BF16)
    gffn = g_ffn[l][None, :]

    xp = x_prompt.reshape(batch * seq, d)
    (qa, ka, va, qb, kb, vbt, qm, gates, ka_f, va_f, kb_f, vb_f, lft) = _proj(
        xp, *proj_w, tm=512, group_rows=seq, keep_rows=keep_p)
    mk_f, mv_f = _memkv(mem_prompt.reshape(batch * n_mem, d), g_mem[l][None, :], w_mkv[l].astype(BF16),
                        gkm, bd128, tm=256)
    tq_b = 512
    lf_rows = lft.reshape(h_b, batch, seq).transpose(1, 0, 2).reshape(batch * h_b, seq)
    c_p = _cumsum_lanes(lf_rows, tri, rb=batch * h_b)
    c_p = c_p.reshape(batch, h_b // 4, 4, seq).transpose(0, 1, 3, 2)
    vt = vbt.reshape(batch, seq // tq_b, h_b, dh_b, tq_b)
    three = lambda a: a.reshape(batch, seq, -1)
    o_a = _band_prompt(three(qa), three(ka), three(va),
                       _rel_bias_rows(rel_bias[l], n_cache, CHUNK, n_cache + CHUNK), reach=n_cache, tq=256)
    o_b = _forget_prompt(three(qb), three(kb), vt, c_p, head=dh_b, tq=tq_b)
    o_m = _mem_attn(three(qm), mk_f.reshape(batch, n_mem, w_m), mv_f.reshape(batch, n_mem, w_m),
                    head=dh_m, tq=512)
    x1 = _merge(xp, o_a.reshape(-1, w_a), o_b.reshape(-1, w_b), o_m.reshape(-1, w_m), gates,
                wpa, wpb, wpm, wo, tm=512)
    y_prompt = _mlp(x1, gffn, wup, wdn, tm=512, ff_chunk=1024).reshape(batch, seq, d)

    xs = x_sample.reshape(dec_b * dec_s, d)
    (qa_s, ka_s, va_s, qb_s, kb_s, _, qm_s, gates_s, ka_sf, va_sf, kb_sf, vb_sf, lft_s) = _proj(
        xs, *proj_w, tm=256, group_rows=dec_b * dec_s, keep_rows=dec_b * dec_s)
    sthree = lambda a: a.reshape(dec_b, dec_s, -1)
    bias_c = _rel_bias_rows(rel_bias[l], n_cache, dec_s, n_cache)
    bias_n = jnp.full((h_a * dec_s, n_pad), NEG, F32).at[:, :dec_s].set(
        _rel_bias_rows(rel_bias[l], 0, dec_s, dec_s))
    o_a_s = _band_sample(sthree(qa_s), cache_a_k[l].reshape(dec_b, n_cache, w_a),
                         cache_a_v[l].reshape(dec_b, n_cache, w_a), sthree(ka_s), sthree(va_s),
                         bias_c, bias_n, head=dh_a)
    lf_new = lft_s.reshape(h_b, dec_b, dec_s).transpose(1, 0, 2)
    lf_all = jnp.concatenate([cache_b_logf[l].astype(F32).transpose(0, 2, 1), lf_new,
                              jnp.zeros((dec_b, h_b, n_pad - dec_s), F32)], axis=-1)
    c_s = _cumsum_lanes(lf_all.reshape(dec_b * h_b, past + n_pad), tri, rb=32)
    c_s = c_s.reshape(dec_b, h_b // 4, 4, past + n_pad)
    o_b_s = _forget_sample(sthree(qb_s), cache_b_k[l].reshape(dec_b, past, w_b),
                           cache_b_v[l].reshape(dec_b, past, w_b), sthree(kb_s), sthree(vb_sf), c_s,
                           head=dh_b, n_pad=n_pad)
    o_m_s = _mem_attn(sthree(qm_s), cache_mem_k[l].reshape(dec_b, n_mem, w_m),
                      cache_mem_v[l].reshape(dec_b, n_mem, w_m), head=dh_m, tq=dec_s)
    x1_s = _merge(xs, o_a_s.reshape(-1, w_a), o_b_s.reshape(-1, w_b), o_m_s.reshape(-1, w_m), gates_s,
                  wpa, wpb, wpm, wo, tm=256)
    y_sample = _mlp(x1_s, gffn, wup, wdn, tm=256, ff_chunk=1024).reshape(dec_b, dec_s, d)

    lead = lambda a, *shape: a.reshape((depth,) + shape)
    return (y_prompt, y_sample,
            lead(ka_f, batch, keep_p, h_a, dh_a), lead(va_f, batch, keep_p, h_a, dh_a),
            lead(kb_f, batch, seq, h_b, dh_b), lead(vb_f, batch, seq, h_b, dh_b),
            lead(lft.T, batch, seq, h_b),
            lead(mk_f, batch, n_mem, h_m, dh_m), lead(mv_f, batch, n_mem, h_m, dh_m),
            lead(ka_sf, dec_b, dec_s, h_a, dh_a), lead(va_sf, dec_b, dec_s, h_a, dh_a),
            lead(kb_sf, dec_b, dec_s, h_b, dh_b), lead(vb_sf, dec_b, dec_s, h_b, dh_b),
            lead(lft_s.T, dec_b, dec_s, h_b))
```

```python
import functools

import jax
import jax.numpy as jnp
from jax import lax
from jax.experimental import pallas as pl
from jax.experimental.pallas import tpu as pltpu

BF16 = jnp.bfloat16
F32 = jnp.float32

EPS = 1e-6
CHUNK = 64
REL_CLIP = 128
NEG = -1e30
LANE_GROUP = 256
LOG2E = 1.4426950408889634
V_ROWS = 64 + 16
VMEM_LIMIT = 56 * 1024 * 1024


def _cparams(*sem):
    return pltpu.CompilerParams(dimension_semantics=sem, vmem_limit_bytes=VMEM_LIMIT)


def _const_spec(shape):
    nd = len(shape)
    return pl.BlockSpec(shape, lambda *_: (0,) * nd, pipeline_mode=pl.Buffered(1))


def _nt_dot(a, b):
    return lax.dot_general(a, b, (((1,), (1,)), ((), ())), preferred_element_type=F32)


def _dot(a, b):
    return jnp.dot(a, b, preferred_element_type=F32)


def _rms_rows(x, gain):
    ms = jnp.mean(x * x, axis=-1, keepdims=True)
    return x * lax.rsqrt(ms + EPS) * gain


def _head_norm(z, bd_ref, gain):
    z2 = (z * z).astype(BF16)
    ms = jnp.concatenate(
        [_dot(z2[:, h * 256:(h + 1) * 256], bd_ref[...]) for h in range(z.shape[1] // 256)], axis=-1)
    return z * lax.rsqrt(ms + EPS) * gain


def _log_sigmoid(x):
    return jnp.minimum(x, 0.0) - jnp.log1p(jnp.exp(-jnp.abs(x)))


def _lane_group(width, head):
    return lax.broadcasted_iota(jnp.int32, (1, width), 1) // head


def _block_diag_rows(q4, head):
    grp = _lane_group(q4.shape[1], head)
    zero = jnp.zeros_like(q4)
    return jnp.concatenate([jnp.where(grp == g, q4, zero) for g in range(q4.shape[1] // head)], axis=0)


def _pick_diag(o, rows, head):
    grp = _lane_group(o.shape[1], head)
    out = jnp.zeros((rows, o.shape[1]), o.dtype)
    for g in range(o.shape[1] // head):
        out = jnp.where(grp == g, o[g * rows:(g + 1) * rows], out)
    return out


def _proj_kernel(x_ref, gmix_ref, wqkv_ref, wft_ref, wqm_ref, wg_ref, gains_ref, gqm_ref, bf_ref,
                 bd64_ref, bd128_ref,
                 qa_o, ka_o, va_o, qb_o, kb_o, vbt_o, qm_o, g_o, kaf_o, vaf_o, kbf_o, vbf_o, lft_o,
                 *, scale_a, scale_b, scale_m):
    xn = _rms_rows(x_ref[...], gmix_ref[...]).astype(BF16)

    def seg(j):
        return _dot(xn, wqkv_ref[:, j * 512:(j + 1) * 512])

    qa = _head_norm(seg(0), bd64_ref, gains_ref[0:1, :])
    qa_o[...] = (qa * scale_a).astype(BF16)
    ka = _head_norm(seg(1), bd64_ref, gains_ref[1:2, :])
    ka_o[...] = ka.astype(BF16)
    kaf_o[...] = ka
    va = seg(2)
    va_o[...] = va.astype(BF16)
    vaf_o[...] = va
    qb = _head_norm(seg(3), bd64_ref, gains_ref[2:3, :])
    qb_o[...] = (qb * scale_b).astype(BF16)
    kb = _head_norm(seg(4), bd64_ref, gains_ref[3:4, :])
    kb_o[...] = kb.astype(BF16)
    kbf_o[...] = kb
    vb = seg(5)
    vbf_o[...] = vb
    vb_t = vb.T.astype(BF16)
    head = vb_t.shape[0] // vbt_o.shape[0]
    pad = vbt_o.shape[1] - head
    ones_row = (lax.broadcasted_iota(jnp.int32, (pad, vb_t.shape[1]), 0) == 0).astype(BF16)
    for h in range(vbt_o.shape[0]):
        vbt_o[h, 0:head, :] = vb_t[h * head:(h + 1) * head, :]
        vbt_o[h, head:head + pad, :] = ones_row
    qm = _head_norm(_dot(xn, wqm_ref[...]), bd128_ref, gqm_ref[...])
    qm_o[...] = (qm * scale_m).astype(BF16)
    for j in range(wg_ref.shape[1] // 512):
        g_o[:, j * 512:(j + 1) * 512] = jax.nn.sigmoid(
            _dot(xn, wg_ref[:, j * 512:(j + 1) * 512])).astype(BF16)
    flt = _nt_dot(wft_ref[...], xn)
    lft_o[...] = _log_sigmoid(flt[0:8, :] + bf_ref[...])


def _proj(x, gmix, wqkv, wft, wqm, wg, gains, gqm, bf, bd64, bd128, *, tm, group_rows, keep_rows):
    rows, d = x.shape
    n_tiles = rows // tm
    tpg = group_rows // tm
    kpt = keep_rows // tm
    n_groups = rows // group_rows

    def keep_map(i):
        return ((i // tpg) * kpt + jnp.clip(i % tpg - (tpg - kpt), 0, kpt - 1), 0)

    row = lambda i: (i, 0)
    wide = lambda dt: (jax.ShapeDtypeStruct((rows, 512), dt), pl.BlockSpec((tm, 512), row))
    kept = (jax.ShapeDtypeStruct((n_groups * keep_rows, 512), F32), pl.BlockSpec((tm, 512), keep_map))
    outs = ([wide(BF16)] * 5
            + [(jax.ShapeDtypeStruct((n_tiles, 8, V_ROWS, tm), BF16),
                pl.BlockSpec((None, 8, V_ROWS, tm), lambda i: (i, 0, 0, 0)))]
            + [wide(BF16)]
            + [(jax.ShapeDtypeStruct((rows, wg.shape[1]), BF16), pl.BlockSpec((tm, wg.shape[1]), row))]
            + [kept] * 2 + [wide(F32)] * 2
            + [(jax.ShapeDtypeStruct((8, rows), F32), pl.BlockSpec((8, tm), lambda i: (0, i)))])
    in_specs = [pl.BlockSpec((tm, d), row)] + [_const_spec(a.shape) for a in
                                               (gmix, wqkv, wft, wqm, wg, gains, gqm, bf, bd64, bd128)]
    kern = functools.partial(_proj_kernel, scale_a=64 ** -0.5, scale_b=LOG2E * 64 ** -0.5,
                             scale_m=128 ** -0.5)
    return pl.pallas_call(kern, out_shape=[o[0] for o in outs], grid=(n_tiles,), in_specs=in_specs,
                          out_specs=[o[1] for o in outs], compiler_params=_cparams("arbitrary"),
                          name="proj")(x, gmix, wqkv, wft, wqm, wg, gains, gqm, bf, bd64, bd128)


def _memkv_kernel(m_ref, gmem_ref, w_ref, gkm_ref, bd128_ref, mk_o, mv_o):
    xn = _rms_rows(m_ref[...], gmem_ref[...]).astype(BF16)
    half = w_ref.shape[1] // 2
    mk_o[...] = _head_norm(_dot(xn, w_ref[:, :half]), bd128_ref, gkm_ref[...])
    mv_o[...] = _dot(xn, w_ref[:, half:])


def _memkv(mem, gmem, w, gkm, bd128, *, tm):
    rows, d = mem.shape
    half = w.shape[1] // 2
    row = lambda i: (i, 0)
    return pl.pallas_call(
        _memkv_kernel, out_shape=[jax.ShapeDtypeStruct((rows, half), F32)] * 2, grid=(rows // tm,),
        in_specs=[pl.BlockSpec((tm, d), row)] + [_const_spec(a.shape) for a in (gmem, w, gkm, bd128)],
        out_specs=[pl.BlockSpec((tm, half), row)] * 2, compiler_params=_cparams("arbitrary"),
        name="memkv")(mem, gmem, w, gkm, bd128)


def _cumsum_kernel(x_ref, tri_ref, o_ref):
    rows, length = x_ref.shape
    carry = jnp.zeros((rows, 1), F32)
    tri = tri_ref[...]
    for c in range(length // 128):
        x = x_ref[:, c * 128:(c + 1) * 128]
        hi = x.astype(BF16)
        r1 = x - hi.astype(F32)
        mid = r1.astype(BF16)
        lo = (r1 - mid.astype(F32)).astype(BF16)
        cc = _dot(hi, tri) + _dot(mid, tri) + _dot(lo, tri) + carry
        o_ref[:, c * 128:(c + 1) * 128] = cc
        carry = cc[:, 127:128]


def _cumsum_lanes(x, tri, *, rb):
    rows, length = x.shape
    return pl.pallas_call(
        _cumsum_kernel, out_shape=jax.ShapeDtypeStruct((rows, length), F32), grid=(rows // rb,),
        in_specs=[pl.BlockSpec((rb, length), lambda i: (i, 0)), _const_spec(tri.shape)],
        out_specs=pl.BlockSpec((rb, length), lambda i: (i, 0)), compiler_params=_cparams("arbitrary"),
        name="cumsum")(x, tri)


def _band_kernel(q_ref, k0_ref, k1_ref, k2_ref, v0_ref, v1_ref, v2_ref, bias_ref, o_ref, k_sc, v_sc,
                 *, reach):
    tq = q_ref.shape[0]
    band = reach + CHUNK
    i = pl.program_id(1)
    for n, (kr, vr) in enumerate(((k0_ref, v0_ref), (k1_ref, v1_ref), (k2_ref, v2_ref))):
        k_sc[n * tq:(n + 1) * tq, :] = kr[...]
        v_sc[n * tq:(n + 1) * tq, :] = vr[...]
    col = lax.broadcasted_iota(jnp.int32, (1, band), 1)
    for j in range(tq // CHUNK):
        r0 = 2 * tq + j * CHUNK - reach
        valid = (i - 2) * tq + r0 + col >= 0
        for hg in range(q_ref.shape[1] // LANE_GROUP):
            lanes = slice(hg * LANE_GROUP, (hg + 1) * LANE_GROUP)
            qbd = _block_diag_rows(q_ref[j * CHUNK:(j + 1) * CHUNK, lanes], CHUNK)
            s = _nt_dot(qbd, k_sc[r0:r0 + band, lanes]) + bias_ref[hg * 4 * CHUNK:(hg + 1) * 4 * CHUNK, :]
            s = jnp.where(valid, s, NEG)
            m = jnp.max(s, axis=-1, keepdims=True)
            p = jnp.exp(s - m)
            l = jnp.sum(p, axis=-1, keepdims=True)
            o = _dot(p.astype(BF16), v_sc[r0:r0 + band, lanes]) / l
            o_ref[j * CHUNK:(j + 1) * CHUNK, lanes] = _pick_diag(o, CHUNK, CHUNK).astype(o_ref.dtype)


def _band_prompt(q, k, v, bias, *, reach, tq):
    b, t, w = q.shape
    assert reach == 2 * tq and tq % CHUNK == 0
    blk = lambda off: pl.BlockSpec((None, tq, w), lambda bi, i: (bi, jnp.maximum(i - off, 0), 0))
    return pl.pallas_call(
        functools.partial(_band_kernel, reach=reach),
        out_shape=jax.ShapeDtypeStruct((b, t, w), BF16), grid=(b, t // tq),
        in_specs=[blk(0), blk(2), blk(1), blk(0), blk(2), blk(1), blk(0), _const_spec(bias.shape)],
        out_specs=blk(0),
        scratch_shapes=[pltpu.VMEM((3 * tq, w), BF16), pltpu.VMEM((3 * tq, w), BF16)],
        compiler_params=_cparams("arbitrary", "arbitrary"), name="band_prompt")(q, k, k, k, v, v, v, bias)


def _forget_kernel(q_ref, k_ref, vt_ref, c_ref, o_ref, qt_sc, m_sc, acc_sc, *, head, lookahead):
    tq, width = q_ref.shape
    n_heads = width // head
    i = pl.program_id(2)
    q_t = q_ref[...].astype(F32).T
    row_grp = lax.broadcasted_iota(jnp.int32, (width, 1), 0) // head
    for g in range(n_heads):
        qt_sc[g] = jnp.where(row_grp == g, q_t, 0.0).astype(BF16)
    m_sc[...] = jnp.full(m_sc.shape, NEG, F32)
    acc_sc[...] = jnp.zeros(acc_sc.shape, F32)
    c_first = c_ref[pl.ds(pl.multiple_of(i * tq, tq), 8), :][0:1, :]

    def step(j, masked):
        k0 = pl.multiple_of(j * tq, tq)
        kt = k_ref[pl.ds(k0, tq), :]
        bias = (c_first - c_ref[pl.ds(k0, tq), :]) * LOG2E
        if masked:
            keep = (lax.broadcasted_iota(jnp.int32, (tq, tq), 0)
                    <= lax.broadcasted_iota(jnp.int32, (tq, tq), 1))

        def scores(g):
            s = _dot(kt, qt_sc[g]) + bias[:, g:g + 1]
            return jnp.where(keep, s, NEG) if masked else s

        pending = [scores(g) for g in range(min(lookahead, n_heads))]
        for g in range(n_heads):
            s = pending.pop(0)
            if g + lookahead < n_heads:
                pending.append(scores(g + lookahead))
            m_prev = m_sc[g]
            m_new = jnp.maximum(m_prev, jnp.max(s, axis=0, keepdims=True))
            p = jnp.exp2(s - m_new).astype(BF16)
            acc_sc[g] = jnp.exp2(m_prev - m_new) * acc_sc[g] + _dot(vt_ref[j, g], p)
            m_sc[g] = m_new

    def body(j, carry):
        step(j, False)
        return carry

    lax.fori_loop(0, i, body, 0)
    step(i, True)
    o_t = jnp.concatenate([acc_sc[g, 0:head, :] / acc_sc[g, head:head + 1, :] for g in range(n_heads)],
                          axis=0)
    o_ref[...] = o_t.T.astype(o_ref.dtype)


def _forget_prompt(q, k, vt, c, *, head, tq):
    b, t, w = q.shape
    n_heads = LANE_GROUP // head
    v_rows = vt.shape[3]
    return pl.pallas_call(
        functools.partial(_forget_kernel, head=head, lookahead=2),
        out_shape=jax.ShapeDtypeStruct((b, t, w), BF16), grid=(b, w // LANE_GROUP, t // tq),
        in_specs=[pl.BlockSpec((None, tq, LANE_GROUP), lambda bi, hg, i: (bi, i, hg)),
                  pl.BlockSpec((None, t, LANE_GROUP), lambda bi, hg, i: (bi, 0, hg)),
                  pl.BlockSpec((None, t // tq, n_heads, v_rows, tq), lambda bi, hg, i: (bi, 0, hg, 0, 0)),
                  pl.BlockSpec((None, None, t, n_heads), lambda bi, hg, i: (bi, hg, 0, 0))],
        out_specs=pl.BlockSpec((None, tq, LANE_GROUP), lambda bi, hg, i: (bi, i, hg)),
        scratch_shapes=[pltpu.VMEM((n_heads, LANE_GROUP, tq), BF16), pltpu.VMEM((n_heads, 1, tq), F32),
                        pltpu.VMEM((n_heads, v_rows, tq), F32)],
        compiler_params=_cparams("arbitrary", "arbitrary", "arbitrary"), name="forget_prompt")(q, k, vt, c)


def _mem_kernel(q_ref, mk_ref, mv_ref, o_ref, *, head):
    for h in range(q_ref.shape[1] // head):
        lanes = slice(h * head, (h + 1) * head)
        s = _nt_dot(q_ref[:, lanes], mk_ref[:, lanes].astype(BF16))
        m = jnp.max(s, axis=-1, keepdims=True)
        p = jnp.exp(s - m)
        l = jnp.sum(p, axis=-1, keepdims=True)
        o = _dot(p.astype(BF16), mv_ref[:, lanes].astype(BF16)) / l
        o_ref[:, lanes] = o.astype(o_ref.dtype)


def _mem_attn(q, mk, mv, *, head, tq):
    b, t, w = q.shape
    n_mem = mk.shape[1]
    kv = pl.BlockSpec((None, n_mem, w), lambda bi, i: (bi, 0, 0))
    qs = pl.BlockSpec((None, tq, w), lambda bi, i: (bi, i, 0))
    return pl.pallas_call(
        functools.partial(_mem_kernel, head=head), out_shape=jax.ShapeDtypeStruct((b, t, w), BF16),
        grid=(b, t // tq), in_specs=[qs, kv, kv], out_specs=qs,
        compiler_params=_cparams("arbitrary", "arbitrary"), name="mem_attn")(q, mk, mv)


def _merge_kernel(x_ref, oa_ref, ob_ref, om_ref, g_ref, wpa_ref, wpb_ref, wpm_ref, wo_ref, y_ref):
    d = x_ref.shape[1]
    h = (g_ref[:, 0:d].astype(F32) * _dot(oa_ref[...], wpa_ref[...])
         + g_ref[:, d:2 * d].astype(F32) * _dot(ob_ref[...], wpb_ref[...])
         + g_ref[:, 2 * d:3 * d].astype(F32) * _dot(om_ref[...], wpm_ref[...]))
    y_ref[...] = x_ref[...] + _dot(h.astype(BF16), wo_ref[...])


def _merge(x, oa, ob, om, g, wpa, wpb, wpm, wo, *, tm):
    rows, d = x.shape
    row = lambda i: (i, 0)
    acts = (x, oa, ob, om, g)
    return pl.pallas_call(
        _merge_kernel, out_shape=jax.ShapeDtypeStruct((rows, d), F32), grid=(rows // tm,),
        in_specs=[pl.BlockSpec((tm, a.shape[1]), row) for a in acts]
        + [_const_spec(a.shape) for a in (wpa, wpb, wpm, wo)],
        out_specs=pl.BlockSpec((tm, d), row), compiler_params=_cparams("arbitrary"),
        name="merge")(*acts, wpa, wpb, wpm, wo)


def _mlp_kernel(x_ref, g_ref, wup_ref, wdn_ref, y_ref, *, ff_chunk):
    x = x_ref[...]
    xn = _rms_rows(x, g_ref[...]).astype(BF16)
    y = x
    for c in range(wup_ref.shape[1] // ff_chunk):
        cols = slice(c * ff_chunk, (c + 1) * ff_chunk)
        u = jnp.maximum(_dot(xn, wup_ref[:, cols]), 0.0)
        y = y + _dot((u * u).astype(BF16), wdn_ref[cols, :])
    y_ref[...] = y


def _mlp(x, g, wup, wdn, *, tm, ff_chunk):
    rows, d = x.shape
    row = lambda i: (i, 0)
    return pl.pallas_call(
        functools.partial(_mlp_kernel, ff_chunk=ff_chunk), out_shape=jax.ShapeDtypeStruct((rows, d), F32),
        grid=(rows // tm,),
        in_specs=[pl.BlockSpec((tm, d), row)] + [_const_spec(a.shape) for a in (g, wup, wdn)],
        out_specs=pl.BlockSpec((tm, d), row), compiler_params=_cparams("arbitrary"),
        name="mlp")(x, g, wup, wdn)


def _pad_rows(x, rows):
    return jnp.concatenate([x, jnp.zeros((rows - x.shape[0], x.shape[1]), x.dtype)], axis=0)


def _band_sample_kernel(q_ref, kc_ref, vc_ref, kn_ref, vn_ref, bc_ref, bn_ref, o_ref, *, head):
    s_len = q_ref.shape[0]
    n_pad = bn_ref.shape[1]
    for hg in range(q_ref.shape[1] // LANE_GROUP):
        lanes = slice(hg * LANE_GROUP, (hg + 1) * LANE_GROUP)
        rows = slice(hg * 4 * s_len, (hg + 1) * 4 * s_len)
        qbd = _block_diag_rows(q_ref[:, lanes], head)
        sc = _nt_dot(qbd, kc_ref[:, lanes].astype(BF16)) + bc_ref[rows, :]
        sn = _nt_dot(qbd, _pad_rows(kn_ref[:, lanes].astype(BF16), n_pad)) + bn_ref[rows, :]
        m = jnp.maximum(jnp.max(sc, axis=-1, keepdims=True), jnp.max(sn, axis=-1, keepdims=True))
        pc = jnp.exp(sc - m)
        pn = jnp.exp(sn - m)
        l = jnp.sum(pc, axis=-1, keepdims=True) + jnp.sum(pn, axis=-1, keepdims=True)
        o = (_dot(pc.astype(BF16), vc_ref[:, lanes].astype(BF16))
             + _dot(pn.astype(BF16), _pad_rows(vn_ref[:, lanes].astype(BF16), n_pad))) / l
        o_ref[:, lanes] = _pick_diag(o, s_len, head).astype(o_ref.dtype)


def _band_sample(q, kc, vc, kn, vn, bias_c, bias_n, *, head):
    b, s_len, w = q.shape
    new = pl.BlockSpec((None, s_len, w), lambda bi: (bi, 0, 0))
    cache = pl.BlockSpec((None, kc.shape[1], w), lambda bi: (bi, 0, 0))
    return pl.pallas_call(
        functools.partial(_band_sample_kernel, head=head), out_shape=jax.ShapeDtypeStruct((b, s_len, w), BF16),
        grid=(b,), in_specs=[new, cache, cache, new, new, _const_spec(bias_c.shape), _const_spec(bias_n.shape)],
        out_specs=new, compiler_params=_cparams("arbitrary"), name="band_sample")(q, kc, vc, kn, vn, bias_c, bias_n)


def _forget_sample_kernel(q_ref, kc_ref, vc_ref, kn_ref, vn_ref, c_ref, o_ref, *, head, n_pad):
    s_len = q_ref.shape[0]
    p_len = kc_ref.shape[0]
    n_heads = LANE_GROUP // head
    col = lax.broadcasted_iota(jnp.int32, (s_len, n_pad), 1)
    causal = col <= lax.broadcasted_iota(jnp.int32, (s_len, n_pad), 0)
    for hg in range(q_ref.shape[1] // LANE_GROUP):
        lanes = slice(hg * LANE_GROUP, (hg + 1) * LANE_GROUP)
        qbd = _block_diag_rows(q_ref[:, lanes], head)
        bias_c, bias_n = [], []
        for g in range(n_heads):
            c_last = c_ref[hg, g:g + 1, p_len - 1:p_len]
            cg = (c_last - c_ref[hg, g:g + 1, :]) * LOG2E
            bias_c.append(jnp.broadcast_to(cg[:, 0:p_len], (s_len, p_len)))
            bias_n.append(jnp.where(causal, cg[:, p_len:p_len + n_pad], NEG))
        sc = _nt_dot(qbd, kc_ref[:, lanes].astype(BF16)) + jnp.concatenate(bias_c, axis=0)
        sn = _nt_dot(qbd, _pad_rows(kn_ref[:, lanes].astype(BF16), n_pad)) + jnp.concatenate(bias_n, axis=0)
        m = jnp.maximum(jnp.max(sc, axis=-1, keepdims=True), jnp.max(sn, axis=-1, keepdims=True))
        pc = jnp.exp2(sc - m)
        pn = jnp.exp2(sn - m)
        l = jnp.sum(pc, axis=-1, keepdims=True) + jnp.sum(pn, axis=-1, keepdims=True)
        o = (_dot(pc.astype(BF16), vc_ref[:, lanes].astype(BF16))
             + _dot(pn.astype(BF16), _pad_rows(vn_ref[:, lanes].astype(BF16), n_pad))) / l
        o_ref[:, lanes] = _pick_diag(o, s_len, head).astype(o_ref.dtype)


def _forget_sample(q, kc, vc, kn, vn, c, *, head, n_pad):
    b, s_len, w = q.shape
    new = pl.BlockSpec((None, s_len, w), lambda bi: (bi, 0, 0))
    cache = pl.BlockSpec((None, kc.shape[1], w), lambda bi: (bi, 0, 0))
    cs = pl.BlockSpec((None,) + c.shape[1:], lambda bi: (bi, 0, 0, 0))
    return pl.pallas_call(
        functools.partial(_forget_sample_kernel, head=head, n_pad=n_pad),
        out_shape=jax.ShapeDtypeStruct((b, s_len, w), BF16), grid=(b,),
        in_specs=[new, cache, cache, new, new, cs], out_specs=new,
        compiler_params=_cparams("arbitrary"), name="forget_sample")(q, kc, vc, kn, vn, c)


def _block_diag_mean(head, size=256):
    r = jnp.arange(size) // head
    return jnp.where(r[:, None] == r[None, :], 1.0 / head, 0.0).astype(BF16)


def _rel_bias_rows(rel_bias, d0, n_q, n_k):
    n = n_q + n_k - 1
    dist = d0 + (n_q - 1) - jnp.arange(n)
    e = rel_bias[jnp.clip(dist, -REL_CLIP, REL_CLIP) + REL_CLIP].T.astype(F32)
    h = e.shape[0]
    skew = jnp.tile(jnp.pad(e, ((0, 0), (0, 1))), (1, n_q))[:, :n_q * n].reshape(h, n_q, n)
    return skew[:, :, n_q - 1:].reshape(h * n_q, n_k)


def kernel(x_prompt, x_sample, mem_prompt, cache_a_k, cache_a_v, cache_b_k, cache_b_v, cache_b_logf,
           cache_mem_k, cache_mem_v, g_mix, w_in, b_f, g_qa, g_ka, g_qb, g_kb, g_qm, g_km, rel_bias,
           g_mem, w_mkv, w_pa, w_pb, w_pm, w_o, g_ffn, w_up, w_down):
    depth = w_in.shape[0]
    assert depth == 1
    batch, seq, d = x_prompt.shape
    dec_b, dec_s, _ = x_sample.shape
    _, _, n_cache, h_a, dh_a = cache_a_k.shape
    _, _, past, h_b, dh_b = cache_b_k.shape
    _, _, n_mem, h_m, dh_m = cache_mem_k.shape
    w_a, w_b, w_m = h_a * dh_a, h_b * dh_b, h_m * dh_m
    assert w_a == w_b == w_m == 512 and dh_a == dh_b == 64 and dh_m == 128 and h_b == 8
    keep_p = min(n_cache, seq)
    n_pad = 128
    l = 0

    qkv_end = 3 * w_a + 3 * w_b
    f_end = qkv_end + h_b
    qm_end = f_end + w_m
    w = w_in[l]
    wqkv = w[:, :qkv_end].astype(BF16)
    wft = jnp.zeros((16, d), BF16).at[:h_b].set(w[:, qkv_end:f_end].T.astype(BF16))
    wqm = w[:, f_end:qm_end].astype(BF16)
    wg = w[:, qm_end:].astype(BF16)
    gains = jnp.stack([jnp.tile(g[l], h_a) for g in (g_qa, g_ka, g_qb, g_kb)]).astype(F32)
    gqm = jnp.tile(g_qm[l], h_m)[None, :]
    gkm = jnp.tile(g_km[l], h_m)[None, :]
    bf = b_f[l][:, None].astype(F32)
    bd64 = _block_diag_mean(dh_a)
    bd128 = _block_diag_mean(dh_m)
    tri = (jnp.arange(128)[:, None] <= jnp.arange(128)[None, :]).astype(BF16)
    gmix = g_mix[l][None, :]
    proj_w = (gmix, wqkv, wft, wqm, wg, gains, gqm, bf, bd64, bd128)
    wpa, wpb, wpm, wo = (a[l].astype(BF16) for a in (w_pa, w_pb, w_pm, w_o))
    wup, wdn = w_up[l].astype(BF16), w_down[l].astype(BF16)
    gffn = g_ffn[l][None, :]

    xp = x_prompt.reshape(batch * seq, d)
    (qa, ka, va, qb, kb, vbt, qm, gates, ka_f, va_f, kb_f, vb_f, lft) = _proj(
        xp, *proj_w, tm=512, group_rows=seq, keep_rows=keep_p)
    mk_f, mv_f = _memkv(mem_prompt.reshape(batch * n_mem, d), g_mem[l][None, :], w_mkv[l].astype(BF16),
                        gkm, bd128, tm=256)
    tq_b = 512
    lf_rows = lft.reshape(h_b, batch, seq).transpose(1, 0, 2).reshape(batch * h_b, seq)
    c_p = _cumsum_lanes(lf_rows, tri, rb=batch * h_b)
    c_p = c_p.reshape(batch, h_b // 4, 4, seq).transpose(0, 1, 3, 2)
    vt = vbt.reshape(batch, seq // tq_b, h_b, V_ROWS, tq_b)
    three = lambda a: a.reshape(batch, seq, -1)
    o_a = _band_prompt(three(qa), three(ka), three(va),
                       _rel_bias_rows(rel_bias[l], n_cache, CHUNK, n_cache + CHUNK), reach=n_cache, tq=256)
    o_b = _forget_prompt(three(qb), three(kb), vt, c_p, head=dh_b, tq=tq_b)
    o_m = _mem_attn(three(qm), mk_f.reshape(batch, n_mem, w_m), mv_f.reshape(batch, n_mem, w_m),
                    head=dh_m, tq=512)
    x1 = _merge(xp, o_a.reshape(-1, w_a), o_b.reshape(-1, w_b), o_m.reshape(-1, w_m), gates,
                wpa, wpb, wpm, wo, tm=512)
    y_prompt = _mlp(x1, gffn, wup, wdn, tm=512, ff_chunk=1024).reshape(batch, seq, d)

    xs = x_sample.reshape(dec_b * dec_s, d)
    (qa_s, ka_s, va_s, qb_s, kb_s, _, qm_s, gates_s, ka_sf, va_sf, kb_sf, vb_sf, lft_s) = _proj(
        xs, *proj_w, tm=256, group_rows=dec_b * dec_s, keep_rows=dec_b * dec_s)
    sthree = lambda a: a.reshape(dec_b, dec_s, -1)
    bias_c = _rel_bias_rows(rel_bias[l], n_cache, dec_s, n_cache)
    bias_n = jnp.full((h_a * dec_s, n_pad), NEG, F32).at[:, :dec_s].set(
        _rel_bias_rows(rel_bias[l], 0, dec_s, dec_s))
    o_a_s = _band_sample(sthree(qa_s), cache_a_k[l].reshape(dec_b, n_cache, w_a),
                         cache_a_v[l].reshape(dec_b, n_cache, w_a), sthree(ka_s), sthree(va_s),
                         bias_c, bias_n, head=dh_a)
    lf_new = lft_s.reshape(h_b, dec_b, dec_s).transpose(1, 0, 2)
    lf_all = jnp.concatenate([cache_b_logf[l].astype(F32).transpose(0, 2, 1), lf_new,
                              jnp.zeros((dec_b, h_b, n_pad - dec_s), F32)], axis=-1)
    c_s = _cumsum_lanes(lf_all.reshape(dec_b * h_b, past + n_pad), tri, rb=32)
    c_s = c_s.reshape(dec_b, h_b // 4, 4, past + n_pad)
    o_b_s = _forget_sample(sthree(qb_s), cache_b_k[l].reshape(dec_b, past, w_b),
                           cache_b_v[l].reshape(dec_b, past, w_b), sthree(kb_s), sthree(vb_sf), c_s,
                           head=dh_b, n_pad=n_pad)
    o_m_s = _mem_attn(sthree(qm_s), cache_mem_k[l].reshape(dec_b, n_mem, w_m),
                      cache_mem_v[l].reshape(dec_b, n_mem, w_m), head=dh_m, tq=dec_s)
    x1_s = _merge(xs, o_a_s.reshape(-1, w_a), o_b_s.reshape(-1, w_b), o_m_s.reshape(-1, w_m), gates_s,
                  wpa, wpb, wpm, wo, tm=256)
    y_sample = _mlp(x1_s, gffn, wup, wdn, tm=256, ff_chunk=1024).reshape(dec_b, dec_s, d)

    lead = lambda a, *shape: a.reshape((depth,) + shape)
    return (y_prompt, y_sample,
            lead(ka_f, batch, keep_p, h_a, dh_a), lead(va_f, batch, keep_p, h_a, dh_a),
            lead(kb_f, batch, seq, h_b, dh_b), lead(vb_f, batch, seq, h_b, dh_b),
            lead(lft.T, batch, seq, h_b),
            lead(mk_f, batch, n_mem, h_m, dh_m), lead(mv_f, batch, n_mem, h_m, dh_m),
            lead(ka_sf, dec_b, dec_s, h_a, dh_a), lead(va_sf, dec_b, dec_s, h_a, dh_a),
            lead(kb_sf, dec_b, dec_s, h_b, dh_b), lead(vb_sf, dec_b, dec_s, h_b, dh_b),
            lead(lft_s.T, dec_b, dec_s, h_b))
```

```python
import functools

import jax
import jax.numpy as jnp
from jax import lax
from jax.experimental import pallas as pl
from jax.experimental.pallas import tpu as pltpu

BF16 = jnp.bfloat16
F32 = jnp.float32

EPS = 1e-6
CHUNK = 64
REL_CLIP = 128
NEG = -1e30
LANE_GROUP = 256
LOG2E = 1.4426950408889634
V_ROWS = 64 + 16
VMEM_LIMIT = 56 * 1024 * 1024


def _cparams(*sem):
    return pltpu.CompilerParams(dimension_semantics=sem, vmem_limit_bytes=VMEM_LIMIT)


def _const_spec(shape):
    nd = len(shape)
    return pl.BlockSpec(shape, lambda *_: (0,) * nd, pipeline_mode=pl.Buffered(1))


def _nt_dot(a, b):
    return lax.dot_general(a, b, (((1,), (1,)), ((), ())), preferred_element_type=F32)


def _dot(a, b):
    return jnp.dot(a, b, preferred_element_type=F32)


def _rms_rows(x, gain):
    ms = jnp.mean(x * x, axis=-1, keepdims=True)
    return x * lax.rsqrt(ms + EPS) * gain


def _head_norm(z, bd_ref, gain):
    z2 = (z * z).astype(BF16)
    ms = jnp.concatenate(
        [_dot(z2[:, h * 256:(h + 1) * 256], bd_ref[...]) for h in range(z.shape[1] // 256)], axis=-1)
    return z * lax.rsqrt(ms + EPS) * gain


def _log_sigmoid(x):
    return jnp.minimum(x, 0.0) - jnp.log1p(jnp.exp(-jnp.abs(x)))


def _lane_group(width, head):
    return lax.broadcasted_iota(jnp.int32, (1, width), 1) // head


def _block_diag_rows(q4, head):
    grp = _lane_group(q4.shape[1], head)
    zero = jnp.zeros_like(q4)
    return jnp.concatenate([jnp.where(grp == g, q4, zero) for g in range(q4.shape[1] // head)], axis=0)


def _pick_diag(o, rows, head):
    grp = _lane_group(o.shape[1], head)
    out = jnp.zeros((rows, o.shape[1]), o.dtype)
    for g in range(o.shape[1] // head):
        out = jnp.where(grp == g, o[g * rows:(g + 1) * rows], out)
    return out


def _store_heads_t(ref, z_t):
    dh = ref.shape[1]
    for h in range(ref.shape[0]):
        ref[h] = z_t[h * dh:(h + 1) * dh, :]


def _proj_kernel(x_ref, gmix_ref, wqkv_ref, wft_ref, wqm_ref, wg_ref, gains_ref, gqm_ref, bf_ref,
                 bd64_ref, bd128_ref, *out_refs, out_names, scale_a, scale_b, scale_m, keep_from,
                 tiles_per_group):
    o = dict(zip(out_names, out_refs))
    head_major = "vbt" in o
    xn = _rms_rows(x_ref[...], gmix_ref[...]).astype(BF16)

    def seg(j):
        return _dot(xn, wqkv_ref[:, j * 512:(j + 1) * 512])

    qa = _head_norm(seg(0), bd64_ref, gains_ref[0:1, :])
    o["qa"][...] = (qa * scale_a).astype(BF16)
    ka = _head_norm(seg(1), bd64_ref, gains_ref[1:2, :])
    o["ka"][...] = ka.astype(BF16)
    va = seg(2)
    o["va"][...] = va.astype(BF16)
    qb = _head_norm(seg(3), bd64_ref, gains_ref[2:3, :])
    o["qb"][...] = (qb * scale_b).astype(BF16)
    kb = _head_norm(seg(4), bd64_ref, gains_ref[3:4, :])
    o["kb"][...] = kb.astype(BF16)
    vb = seg(5)
    if head_major:
        @pl.when(pl.program_id(0) % tiles_per_group >= keep_from)
        def _():
            _store_heads_t(o["ka_f"], ka.T)
            _store_heads_t(o["va_f"], va.T)
        _store_heads_t(o["kb_f"], kb.T)
        vb_t = vb.T
        _store_heads_t(o["vb_f"], vb_t)
        vbt = o["vbt"]
        head = vb_t.shape[0] // vbt.shape[0]
        pad = vbt.shape[1] - head
        ones_row = (lax.broadcasted_iota(jnp.int32, (pad, vb_t.shape[1]), 0) == 0).astype(BF16)
        for h in range(vbt.shape[0]):
            vbt[h, 0:head, :] = vb_t[h * head:(h + 1) * head, :].astype(BF16)
            vbt[h, head:head + pad, :] = ones_row
    else:
        o["ka_f"][...] = ka
        o["va_f"][...] = va
        o["kb_f"][...] = kb
        o["vb_f"][...] = vb
    qm = _head_norm(_dot(xn, wqm_ref[...]), bd128_ref, gqm_ref[...])
    o["qm"][...] = (qm * scale_m).astype(BF16)
    g_o = o["gates"]
    for j in range(wg_ref.shape[1] // 512):
        g_o[:, j * 512:(j + 1) * 512] = jax.nn.sigmoid(
            _dot(xn, wg_ref[:, j * 512:(j + 1) * 512])).astype(BF16)
    flt = _nt_dot(wft_ref[...], xn)
    o["lft"][...] = _log_sigmoid(flt[0:8, :] + bf_ref[...])


def _proj(x, gmix, wqkv, wft, wqm, wg, gains, gqm, bf, bd64, bd128, *, tm, group_rows=None, keep_rows=None):
    rows, d = x.shape
    n_tiles = rows // tm
    row = lambda i: (i, 0)
    wide = lambda dt: (jax.ShapeDtypeStruct((rows, 512), dt), pl.BlockSpec((tm, 512), row))
    outs = {n: wide(BF16) for n in ("qa", "ka", "va", "qb", "kb", "qm")}
    outs["gates"] = (jax.ShapeDtypeStruct((rows, wg.shape[1]), BF16), pl.BlockSpec((tm, wg.shape[1]), row))
    outs["lft"] = (jax.ShapeDtypeStruct((8, rows), F32), pl.BlockSpec((8, tm), lambda i: (0, i)))
    tpg = keep_from = 1
    if group_rows is None:
        for n in ("ka_f", "va_f", "kb_f", "vb_f"):
            outs[n] = wide(F32)
    else:
        assert keep_rows == tm
        tpg = group_rows // tm
        keep_from = tpg - 1
        groups = rows // group_rows
        kept = (jax.ShapeDtypeStruct((groups, 8, 64, keep_rows), F32),
                pl.BlockSpec((None, 8, 64, tm), lambda i: (i // tpg, 0, 0, 0)))
        full = (jax.ShapeDtypeStruct((groups, 8, 64, group_rows), F32),
                pl.BlockSpec((None, 8, 64, tm), lambda i: (i // tpg, 0, 0, i % tpg)))
        outs.update(ka_f=kept, va_f=kept, kb_f=full, vb_f=full)
        outs["vbt"] = (jax.ShapeDtypeStruct((n_tiles, 8, V_ROWS, tm), BF16),
                       pl.BlockSpec((None, 8, V_ROWS, tm), lambda i: (i, 0, 0, 0)))
    names = tuple(outs)
    in_specs = [pl.BlockSpec((tm, d), row)] + [_const_spec(a.shape) for a in
                                               (gmix, wqkv, wft, wqm, wg, gains, gqm, bf, bd64, bd128)]
    kern = functools.partial(_proj_kernel, out_names=names, scale_a=LOG2E * 64 ** -0.5,
                             scale_b=LOG2E * 64 ** -0.5, scale_m=LOG2E * 128 ** -0.5,
                             keep_from=keep_from, tiles_per_group=tpg)
    res = pl.pallas_call(kern, out_shape=[outs[n][0] for n in names], grid=(n_tiles,), in_specs=in_specs,
                         out_specs=[outs[n][1] for n in names], compiler_params=_cparams("arbitrary"),
                         name="proj")(x, gmix, wqkv, wft, wqm, wg, gains, gqm, bf, bd64, bd128)
    return dict(zip(names, res))


def _memkv_kernel(m_ref, gmem_ref, w_ref, gkm_ref, bd128_ref, mk_o, mv_o, *, heads):
    tm = m_ref.shape[0]
    xn = _rms_rows(m_ref[...], gmem_ref[...]).astype(BF16)
    half = w_ref.shape[1] // 2
    dh = half // heads
    mk = _head_norm(_dot(xn, w_ref[:, :half]), bd128_ref, gkm_ref[...])
    mv = _dot(xn, w_ref[:, half:])
    for h in range(heads):
        mk_o[pl.ds(h, tm, stride=heads), :] = mk[:, h * dh:(h + 1) * dh]
        mv_o[pl.ds(h, tm, stride=heads), :] = mv[:, h * dh:(h + 1) * dh]


def _memkv(mem, gmem, w, gkm, bd128, *, tm, heads):
    rows, d = mem.shape
    dh = w.shape[1] // 2 // heads
    row = lambda i: (i, 0)
    return pl.pallas_call(
        functools.partial(_memkv_kernel, heads=heads),
        out_shape=[jax.ShapeDtypeStruct((rows * heads, dh), F32)] * 2, grid=(rows // tm,),
        in_specs=[pl.BlockSpec((tm, d), row)] + [_const_spec(a.shape) for a in (gmem, w, gkm, bd128)],
        out_specs=[pl.BlockSpec((tm * heads, dh), row)] * 2, compiler_params=_cparams("arbitrary"),
        name="memkv")(mem, gmem, w, gkm, bd128)


def _cumsum_kernel(x_ref, tri_ref, o_ref):
    rows, length = x_ref.shape
    carry = jnp.zeros((rows, 1), F32)
    tri = tri_ref[...]
    for c in range(length // 128):
        x = x_ref[:, c * 128:(c + 1) * 128]
        hi = x.astype(BF16)
        r1 = x - hi.astype(F32)
        mid = r1.astype(BF16)
        lo = (r1 - mid.astype(F32)).astype(BF16)
        cc = _dot(hi, tri) + _dot(mid, tri) + _dot(lo, tri) + carry
        o_ref[:, c * 128:(c + 1) * 128] = cc
        carry = cc[:, 127:128]


def _cumsum_lanes(x, tri, *, rb):
    rows, length = x.shape
    return pl.pallas_call(
        _cumsum_kernel, out_shape=jax.ShapeDtypeStruct((rows, length), F32), grid=(rows // rb,),
        in_specs=[pl.BlockSpec((rb, length), lambda i: (i, 0)), _const_spec(tri.shape)],
        out_specs=pl.BlockSpec((rb, length), lambda i: (i, 0)), compiler_params=_cparams("arbitrary"),
        name="cumsum")(x, tri)


def _band_kernel(q_ref, k0_ref, k1_ref, k2_ref, v0_ref, v1_ref, v2_ref, bias_ref, o_ref, k_sc, v_sc,
                 *, reach):
    tq = q_ref.shape[0]
    band = reach + CHUNK
    i = pl.program_id(1)
    for n, (kr, vr) in enumerate(((k0_ref, v0_ref), (k1_ref, v1_ref), (k2_ref, v2_ref))):
        k_sc[n * tq:(n + 1) * tq, :] = kr[...]
        v_sc[n * tq:(n + 1) * tq, :] = vr[...]
    col = lax.broadcasted_iota(jnp.int32, (1, band), 1)
    for j in range(tq // CHUNK):
        r0 = 2 * tq + j * CHUNK - reach
        valid = (i - 2) * tq + r0 + col >= 0
        for hg in range(q_ref.shape[1] // LANE_GROUP):
            lanes = slice(hg * LANE_GROUP, (hg + 1) * LANE_GROUP)
            qbd = _block_diag_rows(q_ref[j * CHUNK:(j + 1) * CHUNK, lanes], CHUNK)
            s = _nt_dot(qbd, k_sc[r0:r0 + band, lanes]) + bias_ref[hg * 4 * CHUNK:(hg + 1) * 4 * CHUNK, :]
            s = jnp.where(valid, s, NEG)
            m = jnp.max(s, axis=-1, keepdims=True)
            p = jnp.exp2(s - m)
            l = jnp.sum(p, axis=-1, keepdims=True)
            o = _dot(p.astype(BF16), v_sc[r0:r0 + band, lanes]) / l
            o_ref[j * CHUNK:(j + 1) * CHUNK, lanes] = _pick_diag(o, CHUNK, CHUNK).astype(o_ref.dtype)


def _band_prompt(q, k, v, bias, *, reach, tq):
    b, t, w = q.shape
    assert reach == 2 * tq and tq % CHUNK == 0
    blk = lambda off: pl.BlockSpec((None, tq, w), lambda bi, i: (bi, jnp.maximum(i - off, 0), 0))
    return pl.pallas_call(
        functools.partial(_band_kernel, reach=reach),
        out_shape=jax.ShapeDtypeStruct((b, t, w), BF16), grid=(b, t // tq),
        in_specs=[blk(0), blk(2), blk(1), blk(0), blk(2), blk(1), blk(0), _const_spec(bias.shape)],
        out_specs=blk(0),
        scratch_shapes=[pltpu.VMEM((3 * tq, w), BF16), pltpu.VMEM((3 * tq, w), BF16)],
        compiler_params=_cparams("arbitrary", "arbitrary"), name="band_prompt")(q, k, k, k, v, v, v, bias)


def _forget_kernel(q_ref, k_ref, vt_ref, c_ref, o_ref, qt_sc, m_sc, acc_sc, *, head, lookahead):
    tq, width = q_ref.shape
    n_heads = width // head
    i = pl.program_id(2)
    q_t = q_ref[...].astype(F32).T
    row_grp = lax.broadcasted_iota(jnp.int32, (width, 1), 0) // head
    for g in range(n_heads):
        qt_sc[g] = jnp.where(row_grp == g, q_t, 0.0).astype(BF16)
    m_sc[...] = jnp.full(m_sc.shape, NEG, F32)
    acc_sc[...] = jnp.zeros(acc_sc.shape, F32)
    c_first = c_ref[pl.ds(pl.multiple_of(i * tq, tq), 8), :][0:1, :]

    def step(j, masked):
        k0 = pl.multiple_of(j * tq, tq)
        kt = k_ref[pl.ds(k0, tq), :]
        bias = (c_first - c_ref[pl.ds(k0, tq), :]) * LOG2E
        if masked:
            keep = (lax.broadcasted_iota(jnp.int32, (tq, tq), 0)
                    <= lax.broadcasted_iota(jnp.int32, (tq, tq), 1))

        def scores(g):
            s = _dot(kt, qt_sc[g]) + bias[:, g:g + 1]
            return jnp.where(keep, s, NEG) if masked else s

        pending = [scores(g) for g in range(min(lookahead, n_heads))]
        for g in range(n_heads):
            s = pending.pop(0)
            if g + lookahead < n_heads:
                pending.append(scores(g + lookahead))
            m_prev = m_sc[g]
            m_new = jnp.maximum(m_prev, jnp.max(s, axis=0, keepdims=True))
            p = jnp.exp2(s - m_new).astype(BF16)
            acc_sc[g] = jnp.exp2(m_prev - m_new) * acc_sc[g] + _dot(vt_ref[j, g], p)
            m_sc[g] = m_new

    def body(j, carry):
        step(j, False)
        return carry

    lax.fori_loop(0, i, body, 0)
    step(i, True)
    o_t = jnp.concatenate([acc_sc[g, 0:head, :] / acc_sc[g, head:head + 1, :] for g in range(n_heads)],
                          axis=0)
    o_ref[...] = o_t.T.astype(o_ref.dtype)


def _forget_prompt(q, k, vt, c, *, head, tq):
    b, t, w = q.shape
    n_heads = LANE_GROUP // head
    v_rows = vt.shape[3]
    return pl.pallas_call(
        functools.partial(_forget_kernel, head=head, lookahead=2),
        out_shape=jax.ShapeDtypeStruct((b, t, w), BF16), grid=(b, w // LANE_GROUP, t // tq),
        in_specs=[pl.BlockSpec((None, tq, LANE_GROUP), lambda bi, hg, i: (bi, i, hg)),
                  pl.BlockSpec((None, t, LANE_GROUP), lambda bi, hg, i: (bi, 0, hg)),
                  pl.BlockSpec((None, t // tq, n_heads, v_rows, tq), lambda bi, hg, i: (bi, 0, hg, 0, 0)),
                  pl.BlockSpec((None, None, t, n_heads), lambda bi, hg, i: (bi, hg, 0, 0))],
        out_specs=pl.BlockSpec((None, tq, LANE_GROUP), lambda bi, hg, i: (bi, i, hg)),
        scratch_shapes=[pltpu.VMEM((n_heads, LANE_GROUP, tq), BF16), pltpu.VMEM((n_heads, 1, tq), F32),
                        pltpu.VMEM((n_heads, v_rows, tq), F32)],
        compiler_params=_cparams("arbitrary", "arbitrary", "arbitrary"), name="forget_prompt")(q, k, vt, c)


def _mem_kernel(q_ref, mk_ref, mv_ref, o_ref, *, heads):
    head = q_ref.shape[1] // heads
    n_mem = mk_ref.shape[0] // heads
    for h in range(heads):
        lanes = slice(h * head, (h + 1) * head)
        kh = mk_ref[pl.ds(h, n_mem, stride=heads), :].astype(BF16)
        vh = mv_ref[pl.ds(h, n_mem, stride=heads), :].astype(BF16)
        s = _nt_dot(q_ref[:, lanes], kh)
        m = jnp.max(s, axis=-1, keepdims=True)
        p = jnp.exp2(s - m)
        l = jnp.sum(p, axis=-1, keepdims=True)
        o_ref[:, lanes] = (_dot(p.astype(BF16), vh) / l).astype(o_ref.dtype)


def _mem_attn(q, mk, mv, *, heads, tq):
    b, t, w = q.shape
    kv = pl.BlockSpec((None,) + mk.shape[1:], lambda bi, i: (bi, 0, 0))
    qs = pl.BlockSpec((None, tq, w), lambda bi, i: (bi, i, 0))
    return pl.pallas_call(
        functools.partial(_mem_kernel, heads=heads), out_shape=jax.ShapeDtypeStruct((b, t, w), BF16),
        grid=(b, t // tq), in_specs=[qs, kv, kv], out_specs=qs,
        compiler_params=_cparams("arbitrary", "arbitrary"), name="mem_attn")(q, mk, mv)


def _merge_kernel(x_ref, oa_ref, ob_ref, om_ref, g_ref, wpa_ref, wpb_ref, wpm_ref, wo_ref, y_ref):
    d = x_ref.shape[1]
    h = (g_ref[:, 0:d].astype(F32) * _dot(oa_ref[...], wpa_ref[...])
         + g_ref[:, d:2 * d].astype(F32) * _dot(ob_ref[...], wpb_ref[...])
         + g_ref[:, 2 * d:3 * d].astype(F32) * _dot(om_ref[...], wpm_ref[...]))
    y_ref[...] = x_ref[...] + _dot(h.astype(BF16), wo_ref[...])


def _merge(x, oa, ob, om, g, wpa, wpb, wpm, wo, *, tm):
    rows, d = x.shape
    row = lambda i: (i, 0)
    acts = (x, oa, ob, om, g)
    return pl.pallas_call(
        _merge_kernel, out_shape=jax.ShapeDtypeStruct((rows, d), F32), grid=(rows // tm,),
        in_specs=[pl.BlockSpec((tm, a.shape[1]), row) for a in acts]
        + [_const_spec(a.shape) for a in (wpa, wpb, wpm, wo)],
        out_specs=pl.BlockSpec((tm, d), row), compiler_params=_cparams("arbitrary"),
        name="merge")(*acts, wpa, wpb, wpm, wo)


def _mlp_kernel(x_ref, g_ref, wup_ref, wdn_ref, y_ref, *, ff_chunk):
    x = x_ref[...]
    xn = _rms_rows(x, g_ref[...]).astype(BF16)
    y = x
    for c in range(wup_ref.shape[1] // ff_chunk):
        cols = slice(c * ff_chunk, (c + 1) * ff_chunk)
        u = jnp.maximum(_dot(xn, wup_ref[:, cols]), 0.0)
        y = y + _dot((u * u).astype(BF16), wdn_ref[cols, :])
    y_ref[...] = y


def _mlp(x, g, wup, wdn, *, tm, ff_chunk):
    rows, d = x.shape
    row = lambda i: (i, 0)
    return pl.pallas_call(
        functools.partial(_mlp_kernel, ff_chunk=ff_chunk), out_shape=jax.ShapeDtypeStruct((rows, d), F32),
        grid=(rows // tm,),
        in_specs=[pl.BlockSpec((tm, d), row)] + [_const_spec(a.shape) for a in (g, wup, wdn)],
        out_specs=pl.BlockSpec((tm, d), row), compiler_params=_cparams("arbitrary"),
        name="mlp")(x, g, wup, wdn)


def _pad_rows(x, rows):
    return jnp.concatenate([x, jnp.zeros((rows - x.shape[0], x.shape[1]), x.dtype)], axis=0)


def _cached_head(qh, kct, vct, kn, vn, bias_c, bias_n):
    sc = _dot(qh, kct) + bias_c
    sn = _nt_dot(qh, kn) + bias_n
    m = jnp.maximum(jnp.max(sc, axis=-1, keepdims=True), jnp.max(sn, axis=-1, keepdims=True))
    pc = jnp.exp2(sc - m)
    pn = jnp.exp2(sn - m)
    l = jnp.sum(pc, axis=-1, keepdims=True) + jnp.sum(pn, axis=-1, keepdims=True)
    return (_nt_dot(pc.astype(BF16), vct) + _dot(pn.astype(BF16), vn)) / l


def _band_sample_kernel(q_ref, kc_ref, vc_ref, kn_ref, vn_ref, bc_ref, bn_ref, o_ref):
    s_len = q_ref.shape[0]
    heads, head, _ = kc_ref.shape
    n_pad = bn_ref.shape[1]
    for h in range(heads):
        lanes = slice(h * head, (h + 1) * head)
        rows = slice(h * s_len, (h + 1) * s_len)
        o = _cached_head(q_ref[:, lanes], kc_ref[h].astype(BF16), vc_ref[h].astype(BF16),
                         _pad_rows(kn_ref[:, lanes].astype(BF16), n_pad),
                         _pad_rows(vn_ref[:, lanes].astype(BF16), n_pad), bc_ref[rows, :], bn_ref[rows, :])
        o_ref[:, lanes] = o.astype(o_ref.dtype)


def _band_sample(q, kct, vct, kn, vn, bias_c, bias_n):
    b, s_len, w = q.shape
    new = pl.BlockSpec((None, s_len, w), lambda bi: (bi, 0, 0))
    cache = pl.BlockSpec((None,) + kct.shape[1:], lambda bi: (bi, 0, 0, 0))
    return pl.pallas_call(
        _band_sample_kernel, out_shape=jax.ShapeDtypeStruct((b, s_len, w), BF16), grid=(b,),
        in_specs=[new, cache, cache, new, new, _const_spec(bias_c.shape), _const_spec(bias_n.shape)],
        out_specs=new, compiler_params=_cparams("arbitrary"), name="band_sample")(
            q, kct, vct, kn, vn, bias_c, bias_n)


def _forget_sample_kernel(q_ref, kc_ref, vc_ref, kn_ref, vn_ref, c_ref, o_ref, *, n_pad):
    s_len = q_ref.shape[0]
    heads, head, p_len = kc_ref.shape
    causal = (lax.broadcasted_iota(jnp.int32, (s_len, n_pad), 1)
              <= lax.broadcasted_iota(jnp.int32, (s_len, n_pad), 0))
    for h in range(heads):
        lanes = slice(h * head, (h + 1) * head)
        ch = (c_ref[h:h + 1, p_len - 1:p_len] - c_ref[h:h + 1, :]) * LOG2E
        o = _cached_head(q_ref[:, lanes], kc_ref[h].astype(BF16), vc_ref[h].astype(BF16),
                         _pad_rows(kn_ref[:, lanes].astype(BF16), n_pad),
                         _pad_rows(vn_ref[:, lanes].astype(BF16), n_pad),
                         ch[:, 0:p_len], jnp.where(causal, ch[:, p_len:p_len + n_pad], NEG))
        o_ref[:, lanes] = o.astype(o_ref.dtype)


def _forget_sample(q, kct, vct, kn, vn, c, *, n_pad):
    b, s_len, w = q.shape
    new = pl.BlockSpec((None, s_len, w), lambda bi: (bi, 0, 0))
    cache = pl.BlockSpec((None,) + kct.shape[1:], lambda bi: (bi, 0, 0, 0))
    cs = pl.BlockSpec((None,) + c.shape[1:], lambda bi: (bi, 0, 0))
    return pl.pallas_call(
        functools.partial(_forget_sample_kernel, n_pad=n_pad),
        out_shape=jax.ShapeDtypeStruct((b, s_len, w), BF16), grid=(b,),
        in_specs=[new, cache, cache, new, new, cs], out_specs=new,
        compiler_params=_cparams("arbitrary"), name="forget_sample")(q, kct, vct, kn, vn, c)


def _block_diag_mean(head, size=256):
    r = jnp.arange(size) // head
    return jnp.where(r[:, None] == r[None, :], 1.0 / head, 0.0).astype(BF16)


def _rel_bias_rows(rel_bias, d0, n_q, n_k):
    n = n_q + n_k - 1
    dist = d0 + (n_q - 1) - jnp.arange(n)
    e = rel_bias[jnp.clip(dist, -REL_CLIP, REL_CLIP) + REL_CLIP].T.astype(F32)
    h = e.shape[0]
    skew = jnp.tile(jnp.pad(e, ((0, 0), (0, 1))), (1, n_q))[:, :n_q * n].reshape(h, n_q, n)
    return skew[:, :, n_q - 1:].reshape(h * n_q, n_k)


def kernel(x_prompt, x_sample, mem_prompt, cache_a_k, cache_a_v, cache_b_k, cache_b_v, cache_b_logf,
           cache_mem_k, cache_mem_v, g_mix, w_in, b_f, g_qa, g_ka, g_qb, g_kb, g_qm, g_km, rel_bias,
           g_mem, w_mkv, w_pa, w_pb, w_pm, w_o, g_ffn, w_up, w_down):
    depth = w_in.shape[0]
    assert depth == 1
    batch, seq, d = x_prompt.shape
    dec_b, dec_s, _ = x_sample.shape
    _, _, n_cache, h_a, dh_a = cache_a_k.shape
    _, _, past, h_b, dh_b = cache_b_k.shape
    _, _, n_mem, h_m, dh_m = cache_mem_k.shape
    w_a, w_b, w_m = h_a * dh_a, h_b * dh_b, h_m * dh_m
    assert w_a == w_b == w_m == 512 and dh_a == dh_b == 64 and dh_m == 128 and h_b == 8
    keep_p = min(n_cache, seq)
    n_pad = 128
    l = 0

    qkv_end = 3 * w_a + 3 * w_b
    f_end = qkv_end + h_b
    qm_end = f_end + w_m
    w = w_in[l]
    wqkv = w[:, :qkv_end].astype(BF16)
    wft = jnp.zeros((16, d), BF16).at[:h_b].set(w[:, qkv_end:f_end].T.astype(BF16))
    wqm = w[:, f_end:qm_end].astype(BF16)
    wg = w[:, qm_end:].astype(BF16)
    gains = jnp.stack([jnp.tile(g[l], h_a) for g in (g_qa, g_ka, g_qb, g_kb)]).astype(F32)
    gqm = jnp.tile(g_qm[l], h_m)[None, :]
    gkm = jnp.tile(g_km[l], h_m)[None, :]
    bf = b_f[l][:, None].astype(F32)
    bd64 = _block_diag_mean(dh_a)
    bd128 = _block_diag_mean(dh_m)
    tri = (jnp.arange(128)[:, None] <= jnp.arange(128)[None, :]).astype(BF16)
    gmix = g_mix[l][None, :]
    proj_w = (gmix, wqkv, wft, wqm, wg, gains, gqm, bf, bd64, bd128)
    wpa, wpb, wpm, wo = (a[l].astype(BF16) for a in (w_pa, w_pb, w_pm, w_o))
    wup, wdn = w_up[l].astype(BF16), w_down[l].astype(BF16)
    gffn = g_ffn[l][None, :]

    xp = x_prompt.reshape(batch * seq, d)
    pp = _proj(xp, *proj_w, tm=512, group_rows=seq, keep_rows=keep_p)
    mk_f, mv_f = _memkv(mem_prompt.reshape(batch * n_mem, d), g_mem[l][None, :], w_mkv[l].astype(BF16),
                        gkm, bd128, tm=256, heads=h_m)
    tq_b = 512
    lf_p = pp["lft"].reshape(h_b, batch, seq).transpose(1, 0, 2)
    c_p = _cumsum_lanes(lf_p.reshape(batch * h_b, seq), tri, rb=batch * h_b)
    c_p = c_p.reshape(batch, h_b // 4, 4, seq).transpose(0, 1, 3, 2)
    vt = pp["vbt"].reshape(batch, seq // tq_b, h_b, V_ROWS, tq_b)
    three = lambda a: a.reshape(batch, seq, -1)
    bias_p = _rel_bias_rows(rel_bias[l], n_cache, CHUNK, n_cache + CHUNK) * LOG2E
    o_a = _band_prompt(three(pp["qa"]), three(pp["ka"]), three(pp["va"]), bias_p, reach=n_cache, tq=256)
    o_b = _forget_prompt(three(pp["qb"]), three(pp["kb"]), vt, c_p, head=dh_b, tq=tq_b)
    o_m = _mem_attn(three(pp["qm"]), mk_f.reshape(batch, n_mem * h_m, dh_m),
                    mv_f.reshape(batch, n_mem * h_m, dh_m), heads=h_m, tq=512)
    x1 = _merge(xp, o_a.reshape(-1, w_a), o_b.reshape(-1, w_b), o_m.reshape(-1, w_m), pp["gates"],
                wpa, wpb, wpm, wo, tm=512)
    y_prompt = _mlp(x1, gffn, wup, wdn, tm=512, ff_chunk=1024).reshape(batch, seq, d)

    xs = x_sample.reshape(dec_b * dec_s, d)
    ps = _proj(xs, *proj_w, tm=256)
    sthree = lambda a: a.reshape(dec_b, dec_s, -1)
    heads_t = lambda a: jnp.transpose(a[l], (0, 2, 3, 1))
    bias_c = _rel_bias_rows(rel_bias[l], n_cache, dec_s, n_cache) * LOG2E
    bias_n = jnp.full((h_a * dec_s, n_pad), NEG, F32).at[:, :dec_s].set(
        _rel_bias_rows(rel_bias[l], 0, dec_s, dec_s) * LOG2E)
    o_a_s = _band_sample(sthree(ps["qa"]), heads_t(cache_a_k), heads_t(cache_a_v), sthree(ps["ka"]),
                         sthree(ps["va"]), bias_c, bias_n)
    lf_new = ps["lft"].reshape(h_b, dec_b, dec_s).transpose(1, 0, 2)
    lf_all = jnp.concatenate([cache_b_logf[l].astype(F32).transpose(0, 2, 1), lf_new,
                              jnp.zeros((dec_b, h_b, n_pad - dec_s), F32)], axis=-1)
    c_s = _cumsum_lanes(lf_all.reshape(dec_b * h_b, past + n_pad), tri, rb=32)
    o_b_s = _forget_sample(sthree(ps["qb"]), heads_t(cache_b_k), heads_t(cache_b_v), sthree(ps["kb"]),
                           sthree(ps["vb_f"]), c_s.reshape(dec_b, h_b, past + n_pad), n_pad=n_pad)
    o_m_s = _mem_attn(sthree(ps["qm"]), cache_mem_k[l].reshape(dec_b, n_mem * h_m, dh_m),
                      cache_mem_v[l].reshape(dec_b, n_mem * h_m, dh_m), heads=h_m, tq=dec_s)
    x1_s = _merge(xs, o_a_s.reshape(-1, w_a), o_b_s.reshape(-1, w_b), o_m_s.reshape(-1, w_m), ps["gates"],
                  wpa, wpb, wpm, wo, tm=256)
    y_sample = _mlp(x1_s, gffn, wup, wdn, tm=256, ff_chunk=1024).reshape(dec_b, dec_s, d)

    lead = lambda a, *shape: a.reshape((depth,) + shape)
    frames_major = lambda a: jnp.transpose(a, (0, 3, 1, 2))[None]
    return (y_prompt, y_sample,
            frames_major(pp["ka_f"]), frames_major(pp["va_f"]),
            frames_major(pp["kb_f"]), frames_major(pp["vb_f"]),
            lf_p.transpose(0, 2, 1)[None],
            lead(mk_f, batch, n_mem, h_m, dh_m), lead(mv_f, batch, n_mem, h_m, dh_m),
            lead(ps["ka_f"], dec_b, dec_s, h_a, dh_a), lead(ps["va_f"], dec_b, dec_s, h_a, dh_a),
            lead(ps["kb_f"], dec_b, dec_s, h_b, dh_b), lead(ps["vb_f"], dec_b, dec_s, h_b, dh_b),
            lead(ps["lft"].T, dec_b, dec_s, h_b))
```

```python
import functools

import jax
import jax.numpy as jnp
from jax import lax
from jax.experimental import pallas as pl
from jax.experimental.pallas import tpu as pltpu

BF16 = jnp.bfloat16
F32 = jnp.float32

EPS = 1e-6
CHUNK = 64
REL_CLIP = 128
NEG = -1e30
LANE_GROUP = 256
LOG2E = 1.4426950408889634
V_ROWS = 64 + 16
VMEM_LIMIT = 56 * 1024 * 1024


def _cparams(*sem):
    return pltpu.CompilerParams(dimension_semantics=sem, vmem_limit_bytes=VMEM_LIMIT)


def _const_spec(shape):
    nd = len(shape)
    return pl.BlockSpec(shape, lambda *_: (0,) * nd, pipeline_mode=pl.Buffered(1))


def _nt_dot(a, b):
    return lax.dot_general(a, b, (((1,), (1,)), ((), ())), preferred_element_type=F32)


def _dot(a, b):
    return jnp.dot(a, b, preferred_element_type=F32)


def _rms_rows(x, gain):
    ms = jnp.mean(x * x, axis=-1, keepdims=True)
    return x * lax.rsqrt(ms + EPS) * gain


def _head_norm(z, bd_ref, gain):
    z2 = (z * z).astype(BF16)
    ms = jnp.concatenate(
        [_dot(z2[:, h * 256:(h + 1) * 256], bd_ref[...]) for h in range(z.shape[1] // 256)], axis=-1)
    return z * lax.rsqrt(ms + EPS) * gain


def _log_sigmoid(x):
    return jnp.minimum(x, 0.0) - jnp.log1p(jnp.exp(-jnp.abs(x)))


def _lane_group(width, head):
    return lax.broadcasted_iota(jnp.int32, (1, width), 1) // head


def _block_diag_rows(q4, head):
    grp = _lane_group(q4.shape[1], head)
    zero = jnp.zeros_like(q4)
    return jnp.concatenate([jnp.where(grp == g, q4, zero) for g in range(q4.shape[1] // head)], axis=0)


def _pick_diag(o, rows, head):
    grp = _lane_group(o.shape[1], head)
    out = jnp.zeros((rows, o.shape[1]), o.dtype)
    for g in range(o.shape[1] // head):
        out = jnp.where(grp == g, o[g * rows:(g + 1) * rows], out)
    return out


def _store_heads_t(ref, z_t):
    dh = ref.shape[1]
    for h in range(ref.shape[0]):
        ref[h] = z_t[h * dh:(h + 1) * dh, :]


def _proj_kernel(x_ref, gmix_ref, wqkv_ref, wft_ref, wqm_ref, wg_ref, gains_ref, gqm_ref, bf_ref,
                 bd64_ref, bd128_ref, *out_refs, out_names, scale_a, scale_b, scale_m, keep_from,
                 tiles_per_group):
    o = dict(zip(out_names, out_refs))
    head_major = "vbt" in o
    xn = _rms_rows(x_ref[...], gmix_ref[...]).astype(BF16)

    def seg(j):
        return _dot(xn, wqkv_ref[:, j * 512:(j + 1) * 512])

    qa = _head_norm(seg(0), bd64_ref, gains_ref[0:1, :])
    o["qa"][...] = (qa * scale_a).astype(BF16)
    ka = _head_norm(seg(1), bd64_ref, gains_ref[1:2, :])
    o["ka"][...] = ka.astype(BF16)
    va = seg(2)
    o["va"][...] = va.astype(BF16)
    qb = _head_norm(seg(3), bd64_ref, gains_ref[2:3, :])
    o["qb"][...] = (qb * scale_b).astype(BF16)
    kb = _head_norm(seg(4), bd64_ref, gains_ref[3:4, :])
    o["kb"][...] = kb.astype(BF16)
    vb = seg(5)
    if head_major:
        @pl.when(pl.program_id(0) % tiles_per_group >= keep_from)
        def _():
            _store_heads_t(o["ka_f"], ka.T)
            _store_heads_t(o["va_f"], va.T)
        _store_heads_t(o["kb_f"], kb.T)
        vb_t = vb.T
        _store_heads_t(o["vb_f"], vb_t)
        vbt = o["vbt"]
        head = vb_t.shape[0] // vbt.shape[0]
        pad = vbt.shape[1] - head
        ones_row = (lax.broadcasted_iota(jnp.int32, (pad, vb_t.shape[1]), 0) == 0).astype(BF16)
        for h in range(vbt.shape[0]):
            vbt[h, 0:head, :] = vb_t[h * head:(h + 1) * head, :].astype(BF16)
            vbt[h, head:head + pad, :] = ones_row
    else:
        o["ka_f"][...] = ka
        o["va_f"][...] = va
        o["kb_f"][...] = kb
        o["vb_f"][...] = vb
    qm = _head_norm(_dot(xn, wqm_ref[...]), bd128_ref, gqm_ref[...])
    o["qm"][...] = (qm * scale_m).astype(BF16)
    g_o = o["gates"]
    for j in range(wg_ref.shape[1] // 512):
        g_o[:, j * 512:(j + 1) * 512] = jax.nn.sigmoid(
            _dot(xn, wg_ref[:, j * 512:(j + 1) * 512])).astype(BF16)
    flt = _nt_dot(wft_ref[...], xn)
    o["lft"][...] = _log_sigmoid(flt[0:8, :] + bf_ref[...])


def _proj(x, gmix, wqkv, wft, wqm, wg, gains, gqm, bf, bd64, bd128, *, tm, group_rows=None, keep_rows=None):
    rows, d = x.shape
    n_tiles = rows // tm
    row = lambda i: (i, 0)
    wide = lambda dt: (jax.ShapeDtypeStruct((rows, 512), dt), pl.BlockSpec((tm, 512), row))
    outs = {n: wide(BF16) for n in ("qa", "ka", "va", "qb", "kb", "qm")}
    outs["gates"] = (jax.ShapeDtypeStruct((rows, wg.shape[1]), BF16), pl.BlockSpec((tm, wg.shape[1]), row))
    outs["lft"] = (jax.ShapeDtypeStruct((8, rows), F32), pl.BlockSpec((8, tm), lambda i: (0, i)))
    tpg = keep_from = 1
    if group_rows is None:
        for n in ("ka_f", "va_f", "kb_f", "vb_f"):
            outs[n] = wide(F32)
    else:
        assert keep_rows == tm
        tpg = group_rows // tm
        keep_from = tpg - 1
        groups = rows // group_rows
        kept = (jax.ShapeDtypeStruct((groups, 8, 64, keep_rows), F32),
                pl.BlockSpec((None, 8, 64, tm), lambda i: (i // tpg, 0, 0, 0)))
        full = (jax.ShapeDtypeStruct((groups, 8, 64, group_rows), F32),
                pl.BlockSpec((None, 8, 64, tm), lambda i: (i // tpg, 0, 0, i % tpg)))
        outs.update(ka_f=kept, va_f=kept, kb_f=full, vb_f=full)
        outs["vbt"] = (jax.ShapeDtypeStruct((n_tiles, 8, V_ROWS, tm), BF16),
                       pl.BlockSpec((None, 8, V_ROWS, tm), lambda i: (i, 0, 0, 0)))
    names = tuple(outs)
    in_specs = [pl.BlockSpec((tm, d), row)] + [_const_spec(a.shape) for a in
                                               (gmix, wqkv, wft, wqm, wg, gains, gqm, bf, bd64, bd128)]
    kern = functools.partial(_proj_kernel, out_names=names, scale_a=LOG2E * 64 ** -0.5,
                             scale_b=LOG2E * 64 ** -0.5, scale_m=LOG2E * 128 ** -0.5,
                             keep_from=keep_from, tiles_per_group=tpg)
    res = pl.pallas_call(kern, out_shape=[outs[n][0] for n in names], grid=(n_tiles,), in_specs=in_specs,
                         out_specs=[outs[n][1] for n in names], compiler_params=_cparams("arbitrary"),
                         name="proj")(x, gmix, wqkv, wft, wqm, wg, gains, gqm, bf, bd64, bd128)
    return dict(zip(names, res))


def _memkv_kernel(m_ref, gmem_ref, w_ref, gkm_ref, bd128_ref, mk_o, mv_o, *, heads):
    tm = m_ref.shape[0]
    xn = _rms_rows(m_ref[...], gmem_ref[...]).astype(BF16)
    half = w_ref.shape[1] // 2
    dh = half // heads
    mk = _head_norm(_dot(xn, w_ref[:, :half]), bd128_ref, gkm_ref[...])
    mv = _dot(xn, w_ref[:, half:])
    for h in range(heads):
        mk_o[pl.ds(h, tm, stride=heads), :] = mk[:, h * dh:(h + 1) * dh]
        mv_o[pl.ds(h, tm, stride=heads), :] = mv[:, h * dh:(h + 1) * dh]


def _memkv(mem, gmem, w, gkm, bd128, *, tm, heads):
    rows, d = mem.shape
    dh = w.shape[1] // 2 // heads
    row = lambda i: (i, 0)
    return pl.pallas_call(
        functools.partial(_memkv_kernel, heads=heads),
        out_shape=[jax.ShapeDtypeStruct((rows * heads, dh), F32)] * 2, grid=(rows // tm,),
        in_specs=[pl.BlockSpec((tm, d), row)] + [_const_spec(a.shape) for a in (gmem, w, gkm, bd128)],
        out_specs=[pl.BlockSpec((tm * heads, dh), row)] * 2, compiler_params=_cparams("arbitrary"),
        name="memkv")(mem, gmem, w, gkm, bd128)


def _cumsum_kernel(x_ref, tri_ref, o_ref):
    rows, length = x_ref.shape
    carry = jnp.zeros((rows, 1), F32)
    tri = tri_ref[...]
    for c in range(length // 128):
        x = x_ref[:, c * 128:(c + 1) * 128]
        hi = x.astype(BF16)
        r1 = x - hi.astype(F32)
        mid = r1.astype(BF16)
        lo = (r1 - mid.astype(F32)).astype(BF16)
        cc = _dot(hi, tri) + _dot(mid, tri) + _dot(lo, tri) + carry
        o_ref[:, c * 128:(c + 1) * 128] = cc
        carry = cc[:, 127:128]


def _cumsum_lanes(x, tri, *, rb):
    rows, length = x.shape
    return pl.pallas_call(
        _cumsum_kernel, out_shape=jax.ShapeDtypeStruct((rows, length), F32), grid=(rows // rb,),
        in_specs=[pl.BlockSpec((rb, length), lambda i: (i, 0)), _const_spec(tri.shape)],
        out_specs=pl.BlockSpec((rb, length), lambda i: (i, 0)), compiler_params=_cparams("arbitrary"),
        name="cumsum")(x, tri)


def _band_kernel(q_ref, k0_ref, k1_ref, k2_ref, v0_ref, v1_ref, v2_ref, bias_ref, o_ref, k_sc, v_sc,
                 *, reach, lookahead):
    tq = q_ref.shape[0]
    band = reach + CHUNK
    i = pl.program_id(1)
    for n, (kr, vr) in enumerate(((k0_ref, v0_ref), (k1_ref, v1_ref), (k2_ref, v2_ref))):
        k_sc[n * tq:(n + 1) * tq, :] = kr[...]
        v_sc[n * tq:(n + 1) * tq, :] = vr[...]
    col = lax.broadcasted_iota(jnp.int32, (1, band), 1)
    units = [(j, hg) for j in range(tq // CHUNK) for hg in range(q_ref.shape[1] // LANE_GROUP)]

    def scores(j, hg):
        r0 = 2 * tq + j * CHUNK - reach
        lanes = slice(hg * LANE_GROUP, (hg + 1) * LANE_GROUP)
        qbd = _block_diag_rows(q_ref[j * CHUNK:(j + 1) * CHUNK, lanes], CHUNK)
        s = _nt_dot(qbd, k_sc[r0:r0 + band, lanes]) + bias_ref[hg * 4 * CHUNK:(hg + 1) * 4 * CHUNK, :]
        return jnp.where((i - 2) * tq + r0 + col >= 0, s, NEG)

    pending = [scores(*u) for u in units[:lookahead]]
    for n, (j, hg) in enumerate(units):
        s = pending.pop(0)
        if n + lookahead < len(units):
            pending.append(scores(*units[n + lookahead]))
        r0 = 2 * tq + j * CHUNK - reach
        lanes = slice(hg * LANE_GROUP, (hg + 1) * LANE_GROUP)
        m = jnp.max(s, axis=-1, keepdims=True)
        p = jnp.exp2(s - m)
        l = jnp.sum(p, axis=-1, keepdims=True)
        o = _dot(p.astype(BF16), v_sc[r0:r0 + band, lanes]) / l
        o_ref[j * CHUNK:(j + 1) * CHUNK, lanes] = _pick_diag(o, CHUNK, CHUNK).astype(o_ref.dtype)


def _band_prompt(q, k, v, bias, *, reach, tq):
    b, t, w = q.shape
    assert reach == 2 * tq and tq % CHUNK == 0
    blk = lambda off: pl.BlockSpec((None, tq, w), lambda bi, i: (bi, jnp.maximum(i - off, 0), 0))
    return pl.pallas_call(
        functools.partial(_band_kernel, reach=reach, lookahead=2),
        out_shape=jax.ShapeDtypeStruct((b, t, w), BF16), grid=(b, t // tq),
        in_specs=[blk(0), blk(2), blk(1), blk(0), blk(2), blk(1), blk(0), _const_spec(bias.shape)],
        out_specs=blk(0),
        scratch_shapes=[pltpu.VMEM((3 * tq, w), BF16), pltpu.VMEM((3 * tq, w), BF16)],
        compiler_params=_cparams("arbitrary", "arbitrary"), name="band_prompt")(q, k, k, k, v, v, v, bias)


def _forget_kernel(q_ref, k_ref, vt_ref, c_ref, o_ref, qt_sc, m_sc, acc_sc, *, head, lookahead):
    tq, width = q_ref.shape
    n_heads = width // head
    i = pl.program_id(2)
    q_t = q_ref[...].astype(F32).T
    row_grp = lax.broadcasted_iota(jnp.int32, (width, 1), 0) // head
    for g in range(n_heads):
        qt_sc[g] = jnp.where(row_grp == g, q_t, 0.0).astype(BF16)
    m_sc[...] = jnp.full(m_sc.shape, NEG, F32)
    acc_sc[...] = jnp.zeros(acc_sc.shape, F32)
    c_first = c_ref[pl.ds(pl.multiple_of(i * tq, tq), 8), :][0:1, :]

    def step(j, masked):
        k0 = pl.multiple_of(j * tq, tq)
        kt = k_ref[pl.ds(k0, tq), :]
        bias = (c_first - c_ref[pl.ds(k0, tq), :]) * LOG2E
        if masked:
            keep = (lax.broadcasted_iota(jnp.int32, (tq, tq), 0)
                    <= lax.broadcasted_iota(jnp.int32, (tq, tq), 1))

        def scores(g):
            s = _dot(kt, qt_sc[g]) + bias[:, g:g + 1]
            return jnp.where(keep, s, NEG) if masked else s

        pending = [scores(g) for g in range(min(lookahead, n_heads))]
        for g in range(n_heads):
            s = pending.pop(0)
            if g + lookahead < n_heads:
                pending.append(scores(g + lookahead))
            m_prev = m_sc[g]
            m_new = jnp.maximum(m_prev, jnp.max(s, axis=0, keepdims=True))
            p = jnp.exp2(s - m_new).astype(BF16)
            acc_sc[g] = jnp.exp2(m_prev - m_new) * acc_sc[g] + _dot(vt_ref[j, g], p)
            m_sc[g] = m_new

    def body(j, carry):
        step(j, False)
        return carry

    lax.fori_loop(0, i, body, 0)
    step(i, True)
    o_t = jnp.concatenate([acc_sc[g, 0:head, :] / acc_sc[g, head:head + 1, :] for g in range(n_heads)],
                          axis=0)
    o_ref[...] = o_t.T.astype(o_ref.dtype)


def _forget_prompt(q, k, vt, c, *, head, tq):
    b, t, w = q.shape
    n_heads = LANE_GROUP // head
    v_rows = vt.shape[3]
    return pl.pallas_call(
        functools.partial(_forget_kernel, head=head, lookahead=2),
        out_shape=jax.ShapeDtypeStruct((b, t, w), BF16), grid=(b, w // LANE_GROUP, t // tq),
        in_specs=[pl.BlockSpec((None, tq, LANE_GROUP), lambda bi, hg, i: (bi, i, hg)),
                  pl.BlockSpec((None, t, LANE_GROUP), lambda bi, hg, i: (bi, 0, hg)),
                  pl.BlockSpec((None, t // tq, n_heads, v_rows, tq), lambda bi, hg, i: (bi, 0, hg, 0, 0)),
                  pl.BlockSpec((None, None, t, n_heads), lambda bi, hg, i: (bi, hg, 0, 0))],
        out_specs=pl.BlockSpec((None, tq, LANE_GROUP), lambda bi, hg, i: (bi, i, hg)),
        scratch_shapes=[pltpu.VMEM((n_heads, LANE_GROUP, tq), BF16), pltpu.VMEM((n_heads, 1, tq), F32),
                        pltpu.VMEM((n_heads, v_rows, tq), F32)],
        compiler_params=_cparams("arbitrary", "arbitrary", "arbitrary"), name="forget_prompt")(q, k, vt, c)


def _mem_kernel(q_ref, mk_ref, mv_ref, o_ref, *, heads):
    head = q_ref.shape[1] // heads
    n_mem = mk_ref.shape[0] // heads
    lanes = [slice(h * head, (h + 1) * head) for h in range(heads)]
    s = [_nt_dot(q_ref[:, lanes[h]], mk_ref[pl.ds(h, n_mem, stride=heads), :].astype(BF16))
         for h in range(heads)]
    p, inv_l = [], []
    for h in range(heads):
        e = jnp.exp2(s[h] - jnp.max(s[h], axis=-1, keepdims=True))
        inv_l.append(1.0 / jnp.sum(e, axis=-1, keepdims=True))
        p.append(e.astype(BF16))
    for h in range(heads):
        o = _dot(p[h], mv_ref[pl.ds(h, n_mem, stride=heads), :].astype(BF16))
        o_ref[:, lanes[h]] = (o * inv_l[h]).astype(o_ref.dtype)


def _mem_attn(q, mk, mv, *, heads, tq):
    b, t, w = q.shape
    kv = pl.BlockSpec((None,) + mk.shape[1:], lambda bi, i: (bi, 0, 0))
    qs = pl.BlockSpec((None, tq, w), lambda bi, i: (bi, i, 0))
    return pl.pallas_call(
        functools.partial(_mem_kernel, heads=heads), out_shape=jax.ShapeDtypeStruct((b, t, w), BF16),
        grid=(b, t // tq), in_specs=[qs, kv, kv], out_specs=qs,
        compiler_params=_cparams("arbitrary", "arbitrary"), name="mem_attn")(q, mk, mv)


def _merge_kernel(x_ref, oa_ref, ob_ref, om_ref, g_ref, wpa_ref, wpb_ref, wpm_ref, wo_ref, y_ref):
    d = x_ref.shape[1]
    h = (g_ref[:, 0:d].astype(F32) * _dot(oa_ref[...], wpa_ref[...])
         + g_ref[:, d:2 * d].astype(F32) * _dot(ob_ref[...], wpb_ref[...])
         + g_ref[:, 2 * d:3 * d].astype(F32) * _dot(om_ref[...], wpm_ref[...]))
    y_ref[...] = x_ref[...] + _dot(h.astype(BF16), wo_ref[...])


def _merge(x, oa, ob, om, g, wpa, wpb, wpm, wo, *, tm):
    rows, d = x.shape
    row = lambda i: (i, 0)
    acts = (x, oa, ob, om, g)
    return pl.pallas_call(
        _merge_kernel, out_shape=jax.ShapeDtypeStruct((rows, d), F32), grid=(rows // tm,),
        in_specs=[pl.BlockSpec((tm, a.shape[1]), row) for a in acts]
        + [_const_spec(a.shape) for a in (wpa, wpb, wpm, wo)],
        out_specs=pl.BlockSpec((tm, d), row), compiler_params=_cparams("arbitrary"),
        name="merge")(*acts, wpa, wpb, wpm, wo)


def _mlp_kernel(x_ref, g_ref, wup_ref, wdn_ref, y_ref, *, ff_chunk):
    x = x_ref[...]
    xn = _rms_rows(x, g_ref[...]).astype(BF16)
    y = x
    for c in range(wup_ref.shape[1] // ff_chunk):
        cols = slice(c * ff_chunk, (c + 1) * ff_chunk)
        u = jnp.maximum(_dot(xn, wup_ref[:, cols]), 0.0)
        y = y + _dot((u * u).astype(BF16), wdn_ref[cols, :])
    y_ref[...] = y


def _mlp(x, g, wup, wdn, *, tm, ff_chunk):
    rows, d = x.shape
    row = lambda i: (i, 0)
    return pl.pallas_call(
        functools.partial(_mlp_kernel, ff_chunk=ff_chunk), out_shape=jax.ShapeDtypeStruct((rows, d), F32),
        grid=(rows // tm,),
        in_specs=[pl.BlockSpec((tm, d), row)] + [_const_spec(a.shape) for a in (g, wup, wdn)],
        out_specs=pl.BlockSpec((tm, d), row), compiler_params=_cparams("arbitrary"),
        name="mlp")(x, g, wup, wdn)


def _pad_rows(x, rows):
    return jnp.concatenate([x, jnp.zeros((rows - x.shape[0], x.shape[1]), x.dtype)], axis=0)


def _cached_attention(q_ref, kc_ref, vc_ref, kn_ref, vn_ref, o_ref, bias_c, bias_n, n_pad):
    heads, head, _ = kc_ref.shape
    lanes = [slice(h * head, (h + 1) * head) for h in range(heads)]
    sc = [_dot(q_ref[:, lanes[h]], kc_ref[h].astype(BF16)) + bias_c(h) for h in range(heads)]
    sn = [_nt_dot(q_ref[:, lanes[h]], _pad_rows(kn_ref[:, lanes[h]].astype(BF16), n_pad)) + bias_n(h)
          for h in range(heads)]
    pc, pn, inv_l = [], [], []
    for h in range(heads):
        m = jnp.maximum(jnp.max(sc[h], axis=-1, keepdims=True), jnp.max(sn[h], axis=-1, keepdims=True))
        pc.append(jnp.exp2(sc[h] - m))
        pn.append(jnp.exp2(sn[h] - m))
        inv_l.append(1.0 / (jnp.sum(pc[h], axis=-1, keepdims=True) + jnp.sum(pn[h], axis=-1, keepdims=True)))
    for h in range(heads):
        o = (_nt_dot(pc[h].astype(BF16), vc_ref[h].astype(BF16))
             + _dot(pn[h].astype(BF16), _pad_rows(vn_ref[:, lanes[h]].astype(BF16), n_pad)))
        o_ref[:, lanes[h]] = (o * inv_l[h]).astype(o_ref.dtype)


def _band_sample_kernel(q_ref, kc_ref, vc_ref, kn_ref, vn_ref, bc_ref, bn_ref, o_ref):
    s_len = q_ref.shape[0]
    rows = lambda h: slice(h * s_len, (h + 1) * s_len)
    _cached_attention(q_ref, kc_ref, vc_ref, kn_ref, vn_ref, o_ref,
                      lambda h: bc_ref[rows(h), :], lambda h: bn_ref[rows(h), :], bn_ref.shape[1])


def _band_sample(q, kct, vct, kn, vn, bias_c, bias_n):
    b, s_len, w = q.shape
    new = pl.BlockSpec((None, s_len, w), lambda bi: (bi, 0, 0))
    cache = pl.BlockSpec((None,) + kct.shape[1:], lambda bi: (bi, 0, 0, 0))
    return pl.pallas_call(
        _band_sample_kernel, out_shape=jax.ShapeDtypeStruct((b, s_len, w), BF16), grid=(b,),
        in_specs=[new, cache, cache, new, new, _const_spec(bias_c.shape), _const_spec(bias_n.shape)],
        out_specs=new, compiler_params=_cparams("arbitrary"), name="band_sample")(
            q, kct, vct, kn, vn, bias_c, bias_n)


def _forget_sample_kernel(q_ref, kc_ref, vc_ref, kn_ref, vn_ref, c_ref, o_ref, *, n_pad):
    s_len = q_ref.shape[0]
    p_len = kc_ref.shape[2]
    causal = (lax.broadcasted_iota(jnp.int32, (s_len, n_pad), 1)
              <= lax.broadcasted_iota(jnp.int32, (s_len, n_pad), 0))
    ch = (c_ref[:, p_len - 1:p_len] - c_ref[...]) * LOG2E
    _cached_attention(q_ref, kc_ref, vc_ref, kn_ref, vn_ref, o_ref,
                      lambda h: ch[h:h + 1, 0:p_len],
                      lambda h: jnp.where(causal, ch[h:h + 1, p_len:p_len + n_pad], NEG), n_pad)


def _forget_sample(q, kct, vct, kn, vn, c, *, n_pad):
    b, s_len, w = q.shape
    new = pl.BlockSpec((None, s_len, w), lambda bi: (bi, 0, 0))
    cache = pl.BlockSpec((None,) + kct.shape[1:], lambda bi: (bi, 0, 0, 0))
    cs = pl.BlockSpec((None,) + c.shape[1:], lambda bi: (bi, 0, 0))
    return pl.pallas_call(
        functools.partial(_forget_sample_kernel, n_pad=n_pad),
        out_shape=jax.ShapeDtypeStruct((b, s_len, w), BF16), grid=(b,),
        in_specs=[new, cache, cache, new, new, cs], out_specs=new,
        compiler_params=_cparams("arbitrary"), name="forget_sample")(q, kct, vct, kn, vn, c)


def _block_diag_mean(head, size=256):
    r = jnp.arange(size) // head
    return jnp.where(r[:, None] == r[None, :], 1.0 / head, 0.0).astype(BF16)


def _rel_bias_rows(rel_bias, d0, n_q, n_k):
    n = n_q + n_k - 1
    dist = d0 + (n_q - 1) - jnp.arange(n)
    e = rel_bias[jnp.clip(dist, -REL_CLIP, REL_CLIP) + REL_CLIP].T.astype(F32)
    h = e.shape[0]
    skew = jnp.tile(jnp.pad(e, ((0, 0), (0, 1))), (1, n_q))[:, :n_q * n].reshape(h, n_q, n)
    return skew[:, :, n_q - 1:].reshape(h * n_q, n_k)


def kernel(x_prompt, x_sample, mem_prompt, cache_a_k, cache_a_v, cache_b_k, cache_b_v, cache_b_logf,
           cache_mem_k, cache_mem_v, g_mix, w_in, b_f, g_qa, g_ka, g_qb, g_kb, g_qm, g_km, rel_bias,
           g_mem, w_mkv, w_pa, w_pb, w_pm, w_o, g_ffn, w_up, w_down):
    depth = w_in.shape[0]
    assert depth == 1
    batch, seq, d = x_prompt.shape
    dec_b, dec_s, _ = x_sample.shape
    _, _, n_cache, h_a, dh_a = cache_a_k.shape
    _, _, past, h_b, dh_b = cache_b_k.shape
    _, _, n_mem, h_m, dh_m = cache_mem_k.shape
    w_a, w_b, w_m = h_a * dh_a, h_b * dh_b, h_m * dh_m
    assert w_a == w_b == w_m == 512 and dh_a == dh_b == 64 and dh_m == 128 and h_b == 8
    keep_p = min(n_cache, seq)
    n_pad = 128
    l = 0

    qkv_end = 3 * w_a + 3 * w_b
    f_end = qkv_end + h_b
    qm_end = f_end + w_m
    w = w_in[l]
    wqkv = w[:, :qkv_end].astype(BF16)
    wft = jnp.zeros((16, d), BF16).at[:h_b].set(w[:, qkv_end:f_end].T.astype(BF16))
    wqm = w[:, f_end:qm_end].astype(BF16)
    wg = w[:, qm_end:].astype(BF16)
    gains = jnp.stack([jnp.tile(g[l], h_a) for g in (g_qa, g_ka, g_qb, g_kb)]).astype(F32)
    gqm = jnp.tile(g_qm[l], h_m)[None, :]
    gkm = jnp.tile(g_km[l], h_m)[None, :]
    bf = b_f[l][:, None].astype(F32)
    bd64 = _block_diag_mean(dh_a)
    bd128 = _block_diag_mean(dh_m)
    tri = (jnp.arange(128)[:, None] <= jnp.arange(128)[None, :]).astype(BF16)
    gmix = g_mix[l][None, :]
    proj_w = (gmix, wqkv, wft, wqm, wg, gains, gqm, bf, bd64, bd128)
    wpa, wpb, wpm, wo = (a[l].astype(BF16) for a in (w_pa, w_pb, w_pm, w_o))
    wup, wdn = w_up[l].astype(BF16), w_down[l].astype(BF16)
    gffn = g_ffn[l][None, :]

    xp = x_prompt.reshape(batch * seq, d)
    pp = _proj(xp, *proj_w, tm=512, group_rows=seq, keep_rows=keep_p)
    mk_f, mv_f = _memkv(mem_prompt.reshape(batch * n_mem, d), g_mem[l][None, :], w_mkv[l].astype(BF16),
                        gkm, bd128, tm=256, heads=h_m)
    tq_b = 512
    lf_p = pp["lft"].reshape(h_b, batch, seq).transpose(1, 0, 2)
    c_p = _cumsum_lanes(lf_p.reshape(batch * h_b, seq), tri, rb=batch * h_b)
    c_p = c_p.reshape(batch, h_b // 4, 4, seq).transpose(0, 1, 3, 2)
    vt = pp["vbt"].reshape(batch, seq // tq_b, h_b, V_ROWS, tq_b)
    three = lambda a: a.reshape(batch, seq, -1)
    bias_p = _rel_bias_rows(rel_bias[l], n_cache, CHUNK, n_cache + CHUNK) * LOG2E
    o_a = _band_prompt(three(pp["qa"]), three(pp["ka"]), three(pp["va"]), bias_p, reach=n_cache, tq=256)
    o_b = _forget_prompt(three(pp["qb"]), three(pp["kb"]), vt, c_p, head=dh_b, tq=tq_b)
    o_m = _mem_attn(three(pp["qm"]), mk_f.reshape(batch, n_mem * h_m, dh_m),
                    mv_f.reshape(batch, n_mem * h_m, dh_m), heads=h_m, tq=512)
    x1 = _merge(xp, o_a.reshape(-1, w_a), o_b.reshape(-1, w_b), o_m.reshape(-1, w_m), pp["gates"],
                wpa, wpb, wpm, wo, tm=512)
    y_prompt = _mlp(x1, gffn, wup, wdn, tm=512, ff_chunk=1024).reshape(batch, seq, d)

    xs = x_sample.reshape(dec_b * dec_s, d)
    ps = _proj(xs, *proj_w, tm=256)
    sthree = lambda a: a.reshape(dec_b, dec_s, -1)
    heads_t = lambda a: jnp.transpose(a[l], (0, 2, 3, 1))
    bias_c = _rel_bias_rows(rel_bias[l], n_cache, dec_s, n_cache) * LOG2E
    bias_n = jnp.full((h_a * dec_s, n_pad), NEG, F32).at[:, :dec_s].set(
        _rel_bias_rows(rel_bias[l], 0, dec_s, dec_s) * LOG2E)
    o_a_s = _band_sample(sthree(ps["qa"]), heads_t(cache_a_k), heads_t(cache_a_v), sthree(ps["ka"]),
                         sthree(ps["va"]), bias_c, bias_n)
    lf_new = ps["lft"].reshape(h_b, dec_b, dec_s).transpose(1, 0, 2)
    lf_all = jnp.concatenate([cache_b_logf[l].astype(F32).transpose(0, 2, 1), lf_new,
                              jnp.zeros((dec_b, h_b, n_pad - dec_s), F32)], axis=-1)
    c_s = _cumsum_lanes(lf_all.reshape(dec_b * h_b, past + n_pad), tri, rb=32)
    o_b_s = _forget_sample(sthree(ps["qb"]), heads_t(cache_b_k), heads_t(cache_b_v), sthree(ps["kb"]),
                           sthree(ps["vb_f"]), c_s.reshape(dec_b, h_b, past + n_pad), n_pad=n_pad)
    o_m_s = _mem_attn(sthree(ps["qm"]), cache_mem_k[l].reshape(dec_b, n_mem * h_m, dh_m),
                      cache_mem_v[l].reshape(dec_b, n_mem * h_m, dh_m), heads=h_m, tq=dec_s)
    x1_s = _merge(xs, o_a_s.reshape(-1, w_a), o_b_s.reshape(-1, w_b), o_m_s.reshape(-1, w_m), ps["gates"],
                  wpa, wpb, wpm, wo, tm=256)
    y_sample = _mlp(x1_s, gffn, wup, wdn, tm=256, ff_chunk=1024).reshape(dec_b, dec_s, d)

    lead = lambda a, *shape: a.reshape((depth,) + shape)
    frames_major = lambda a: jnp.transpose(a, (0, 3, 1, 2))[None]
    return (y_prompt, y_sample,
            frames_major(pp["ka_f"]), frames_major(pp["va_f"]),
            frames_major(pp["kb_f"]), frames_major(pp["vb_f"]),
            lf_p.transpose(0, 2, 1)[None],
            lead(mk_f, batch, n_mem, h_m, dh_m), lead(mv_f, batch, n_mem, h_m, dh_m),
            lead(ps["ka_f"], dec_b, dec_s, h_a, dh_a), lead(ps["va_f"], dec_b, dec_s, h_a, dh_a),
            lead(ps["kb_f"], dec_b, dec_s, h_b, dh_b), lead(ps["vb_f"], dec_b, dec_s, h_b, dh_b),
            lead(ps["lft"].T, dec_b, dec_s, h_b))
```

```python
import functools

import jax
import jax.numpy as jnp
from jax import lax
from jax.experimental import pallas as pl
from jax.experimental.pallas import tpu as pltpu

BF16 = jnp.bfloat16
F32 = jnp.float32

EPS = 1e-6
CHUNK = 64
REL_CLIP = 128
NEG = -1e30
LANE_GROUP = 256
LOG2E = 1.4426950408889634
V_ROWS = 64 + 16
VMEM_LIMIT = 56 * 1024 * 1024


def _cparams(*sem):
    return pltpu.CompilerParams(dimension_semantics=sem, vmem_limit_bytes=VMEM_LIMIT)


def _const_spec(shape):
    nd = len(shape)
    return pl.BlockSpec(shape, lambda *_: (0,) * nd, pipeline_mode=pl.Buffered(1))


def _nt_dot(a, b):
    return lax.dot_general(a, b, (((1,), (1,)), ((), ())), preferred_element_type=F32)


def _dot(a, b):
    return jnp.dot(a, b, preferred_element_type=F32)


def _rms_rows(x, gain):
    ms = jnp.mean(x * x, axis=-1, keepdims=True)
    return x * lax.rsqrt(ms + EPS) * gain


def _head_norm(z, bd_ref, gain):
    z2 = (z * z).astype(BF16)
    ms = jnp.concatenate(
        [_dot(z2[:, h * 256:(h + 1) * 256], bd_ref[...]) for h in range(z.shape[1] // 256)], axis=-1)
    return z * lax.rsqrt(ms + EPS) * gain


def _log_sigmoid(x):
    return jnp.minimum(x, 0.0) - jnp.log1p(jnp.exp(-jnp.abs(x)))


def _lane_group(width, head):
    return lax.broadcasted_iota(jnp.int32, (1, width), 1) // head


def _block_diag_rows(q4, head):
    grp = _lane_group(q4.shape[1], head)
    zero = jnp.zeros_like(q4)
    return jnp.concatenate([jnp.where(grp == g, q4, zero) for g in range(q4.shape[1] // head)], axis=0)


def _pick_diag(o, rows, head):
    grp = _lane_group(o.shape[1], head)
    out = jnp.zeros((rows, o.shape[1]), o.dtype)
    for g in range(o.shape[1] // head):
        out = jnp.where(grp == g, o[g * rows:(g + 1) * rows], out)
    return out


def _store_heads_t(ref, z_t):
    dh = ref.shape[1]
    for h in range(ref.shape[0]):
        ref[h] = z_t[h * dh:(h + 1) * dh, :]


def _proj_kernel(x_ref, gmix_ref, wqkv_ref, wft_ref, wqm_ref, wg_ref, gains_ref, gqm_ref, bf_ref,
                 bd64_ref, bd128_ref, *out_refs, out_names, scale_a, scale_b, scale_m, keep_from,
                 tiles_per_group):
    o = dict(zip(out_names, out_refs))
    head_major = "vbt" in o
    xn = _rms_rows(x_ref[...], gmix_ref[...]).astype(BF16)

    def seg(j):
        return _dot(xn, wqkv_ref[:, j * 512:(j + 1) * 512])

    kb = _head_norm(seg(4), bd64_ref, gains_ref[3:4, :])
    o["kb"][...] = kb.astype(BF16)
    vb = seg(5)
    if head_major:
        _store_heads_t(o["kb_f"], kb.T)
        vb_t = vb.T
        _store_heads_t(o["vb_f"], vb_t)
        vbt = o["vbt"]
        head = vb_t.shape[0] // vbt.shape[0]
        pad = vbt.shape[1] - head
        ones_row = (lax.broadcasted_iota(jnp.int32, (pad, vb_t.shape[1]), 0) == 0).astype(BF16)
        for h in range(vbt.shape[0]):
            vbt[h, 0:head, :] = vb_t[h * head:(h + 1) * head, :].astype(BF16)
            vbt[h, head:head + pad, :] = ones_row
    else:
        o["kb_f"][...] = kb
        o["vb_f"][...] = vb
    qb = _head_norm(seg(3), bd64_ref, gains_ref[2:3, :])
    o["qb"][...] = (qb * scale_b).astype(BF16)
    qa = _head_norm(seg(0), bd64_ref, gains_ref[0:1, :])
    o["qa"][...] = (qa * scale_a).astype(BF16)
    ka = _head_norm(seg(1), bd64_ref, gains_ref[1:2, :])
    o["ka"][...] = ka.astype(BF16)
    qm = _head_norm(_dot(xn, wqm_ref[...]), bd128_ref, gqm_ref[...])
    o["qm"][...] = (qm * scale_m).astype(BF16)
    flt = _nt_dot(wft_ref[...], xn)
    o["lft"][...] = _log_sigmoid(flt[0:8, :] + bf_ref[...])
    g_o = o["gates"]
    for j in range(wg_ref.shape[1] // 512):
        g_o[:, j * 512:(j + 1) * 512] = jax.nn.sigmoid(
            _dot(xn, wg_ref[:, j * 512:(j + 1) * 512])).astype(BF16)
    va = seg(2)
    o["va"][...] = va.astype(BF16)
    if head_major:
        @pl.when(pl.program_id(0) % tiles_per_group >= keep_from)
        def _():
            _store_heads_t(o["ka_f"], ka.T)
            _store_heads_t(o["va_f"], va.T)
    else:
        o["ka_f"][...] = ka
        o["va_f"][...] = va


def _proj(x, gmix, wqkv, wft, wqm, wg, gains, gqm, bf, bd64, bd128, *, tm, group_rows=None, keep_rows=None):
    rows, d = x.shape
    n_tiles = rows // tm
    row = lambda i: (i, 0)
    wide = lambda dt: (jax.ShapeDtypeStruct((rows, 512), dt), pl.BlockSpec((tm, 512), row))
    outs = {n: wide(BF16) for n in ("qa", "ka", "va", "qb", "kb", "qm")}
    outs["gates"] = (jax.ShapeDtypeStruct((rows, wg.shape[1]), BF16), pl.BlockSpec((tm, wg.shape[1]), row))
    outs["lft"] = (jax.ShapeDtypeStruct((8, rows), F32), pl.BlockSpec((8, tm), lambda i: (0, i)))
    tpg = keep_from = 1
    if group_rows is None:
        for n in ("ka_f", "va_f", "kb_f", "vb_f"):
            outs[n] = wide(F32)
    else:
        assert keep_rows == tm
        tpg = group_rows // tm
        keep_from = tpg - 1
        groups = rows // group_rows
        kept = (jax.ShapeDtypeStruct((groups, 8, 64, keep_rows), F32),
                pl.BlockSpec((None, 8, 64, tm), lambda i: (i // tpg, 0, 0, 0)))
        full = (jax.ShapeDtypeStruct((groups, 8, 64, group_rows), F32),
                pl.BlockSpec((None, 8, 64, tm), lambda i: (i // tpg, 0, 0, i % tpg)))
        outs.update(ka_f=kept, va_f=kept, kb_f=full, vb_f=full)
        outs["vbt"] = (jax.ShapeDtypeStruct((n_tiles, 8, V_ROWS, tm), BF16),
                       pl.BlockSpec((None, 8, V_ROWS, tm), lambda i: (i, 0, 0, 0)))
    names = tuple(outs)
    in_specs = [pl.BlockSpec((tm, d), row)] + [_const_spec(a.shape) for a in
                                               (gmix, wqkv, wft, wqm, wg, gains, gqm, bf, bd64, bd128)]
    kern = functools.partial(_proj_kernel, out_names=names, scale_a=LOG2E * 64 ** -0.5,
                             scale_b=LOG2E * 64 ** -0.5, scale_m=LOG2E * 128 ** -0.5,
                             keep_from=keep_from, tiles_per_group=tpg)
    res = pl.pallas_call(kern, out_shape=[outs[n][0] for n in names], grid=(n_tiles,), in_specs=in_specs,
                         out_specs=[outs[n][1] for n in names], compiler_params=_cparams("arbitrary"),
                         name="proj")(x, gmix, wqkv, wft, wqm, wg, gains, gqm, bf, bd64, bd128)
    return dict(zip(names, res))


def _memkv_kernel(m_ref, gmem_ref, w_ref, gkm_ref, bd128_ref, mk_o, mv_o, *, heads):
    tm = m_ref.shape[0]
    xn = _rms_rows(m_ref[...], gmem_ref[...]).astype(BF16)
    half = w_ref.shape[1] // 2
    dh = half // heads
    mk = _head_norm(_dot(xn, w_ref[:, :half]), bd128_ref, gkm_ref[...])
    mv = _dot(xn, w_ref[:, half:])
    for h in range(heads):
        mk_o[pl.ds(h, tm, stride=heads), :] = mk[:, h * dh:(h + 1) * dh]
        mv_o[pl.ds(h, tm, stride=heads), :] = mv[:, h * dh:(h + 1) * dh]


def _memkv(mem, gmem, w, gkm, bd128, *, tm, heads):
    rows, d = mem.shape
    dh = w.shape[1] // 2 // heads
    row = lambda i: (i, 0)
    return pl.pallas_call(
        functools.partial(_memkv_kernel, heads=heads),
        out_shape=[jax.ShapeDtypeStruct((rows * heads, dh), F32)] * 2, grid=(rows // tm,),
        in_specs=[pl.BlockSpec((tm, d), row)] + [_const_spec(a.shape) for a in (gmem, w, gkm, bd128)],
        out_specs=[pl.BlockSpec((tm * heads, dh), row)] * 2, compiler_params=_cparams("arbitrary"),
        name="memkv")(mem, gmem, w, gkm, bd128)


def _cumsum_kernel(x_ref, tri_ref, o_ref):
    rows, length = x_ref.shape
    carry = jnp.zeros((rows, 1), F32)
    tri = tri_ref[...]
    for c in range(length // 128):
        x = x_ref[:, c * 128:(c + 1) * 128]
        hi = x.astype(BF16)
        r1 = x - hi.astype(F32)
        mid = r1.astype(BF16)
        lo = (r1 - mid.astype(F32)).astype(BF16)
        cc = _dot(hi, tri) + _dot(mid, tri) + _dot(lo, tri) + carry
        o_ref[:, c * 128:(c + 1) * 128] = cc
        carry = cc[:, 127:128]


def _cumsum_lanes(x, tri, *, rb):
    rows, length = x.shape
    return pl.pallas_call(
        _cumsum_kernel, out_shape=jax.ShapeDtypeStruct((rows, length), F32), grid=(rows // rb,),
        in_specs=[pl.BlockSpec((rb, length), lambda i: (i, 0)), _const_spec(tri.shape)],
        out_specs=pl.BlockSpec((rb, length), lambda i: (i, 0)), compiler_params=_cparams("arbitrary"),
        name="cumsum")(x, tri)


def _band_kernel(q_ref, k0_ref, k1_ref, k2_ref, v0_ref, v1_ref, v2_ref, bias_ref, o_ref, k_sc, v_sc,
                 *, reach, lookahead):
    tq = q_ref.shape[0]
    band = reach + CHUNK
    i = pl.program_id(1)
    for n, (kr, vr) in enumerate(((k0_ref, v0_ref), (k1_ref, v1_ref), (k2_ref, v2_ref))):
        k_sc[n * tq:(n + 1) * tq, :] = kr[...]
        v_sc[n * tq:(n + 1) * tq, :] = vr[...]
    col = lax.broadcasted_iota(jnp.int32, (1, band), 1)
    units = [(j, hg) for j in range(tq // CHUNK) for hg in range(q_ref.shape[1] // LANE_GROUP)]

    def scores(j, hg):
        r0 = 2 * tq + j * CHUNK - reach
        lanes = slice(hg * LANE_GROUP, (hg + 1) * LANE_GROUP)
        qbd = _block_diag_rows(q_ref[j * CHUNK:(j + 1) * CHUNK, lanes], CHUNK)
        s = _nt_dot(qbd, k_sc[r0:r0 + band, lanes]) + bias_ref[hg * 4 * CHUNK:(hg + 1) * 4 * CHUNK, :]
        return jnp.where((i - 2) * tq + r0 + col >= 0, s, NEG)

    pending = [scores(*u) for u in units[:lookahead]]
    for n, (j, hg) in enumerate(units):
        s = pending.pop(0)
        if n + lookahead < len(units):
            pending.append(scores(*units[n + lookahead]))
        r0 = 2 * tq + j * CHUNK - reach
        lanes = slice(hg * LANE_GROUP, (hg + 1) * LANE_GROUP)
        m = jnp.max(s, axis=-1, keepdims=True)
        p = jnp.exp2(s - m)
        l = jnp.sum(p, axis=-1, keepdims=True)
        o = _dot(p.astype(BF16), v_sc[r0:r0 + band, lanes]) / l
        o_ref[j * CHUNK:(j + 1) * CHUNK, lanes] = _pick_diag(o, CHUNK, CHUNK).astype(o_ref.dtype)


def _band_prompt(q, k, v, bias, *, reach, tq):
    b, t, w = q.shape
    assert reach == 2 * tq and tq % CHUNK == 0
    blk = lambda off: pl.BlockSpec((None, tq, w), lambda bi, i: (bi, jnp.maximum(i - off, 0), 0))
    return pl.pallas_call(
        functools.partial(_band_kernel, reach=reach, lookahead=2),
        out_shape=jax.ShapeDtypeStruct((b, t, w), BF16), grid=(b, t // tq),
        in_specs=[blk(0), blk(2), blk(1), blk(0), blk(2), blk(1), blk(0), _const_spec(bias.shape)],
        out_specs=blk(0),
        scratch_shapes=[pltpu.VMEM((3 * tq, w), BF16), pltpu.VMEM((3 * tq, w), BF16)],
        compiler_params=_cparams("arbitrary", "arbitrary"), name="band_prompt")(q, k, k, k, v, v, v, bias)


def _forget_kernel(q_ref, k_ref, vt_ref, c_ref, o_ref, qt_sc, m_sc, acc_sc, s_sc, *, head, lookahead):
    tq, width = q_ref.shape
    n_heads = width // head
    i = pl.program_id(2)
    q_t = q_ref[...].astype(F32).T
    row_grp = lax.broadcasted_iota(jnp.int32, (width, 1), 0) // head
    for g in range(n_heads):
        qt_sc[g] = jnp.where(row_grp == g, q_t, 0.0).astype(BF16)
    m_sc[...] = jnp.full(m_sc.shape, NEG, F32)
    acc_sc[...] = jnp.zeros(acc_sc.shape, F32)
    c_first = c_ref[pl.ds(pl.multiple_of(i * tq, tq), 8), :][0:1, :]

    def scores(j, g):
        k0 = pl.multiple_of(j * tq, tq)
        bias = (c_first[:, g:g + 1] - c_ref[pl.ds(k0, tq), g:g + 1]) * LOG2E
        return _dot(k_ref[pl.ds(k0, tq), :], qt_sc[g]) + bias

    def step(j, masked):
        if masked:
            keep = (lax.broadcasted_iota(jnp.int32, (tq, tq), 0)
                    <= lax.broadcasted_iota(jnp.int32, (tq, tq), 1))
        pending = [s_sc[n] for n in range(lookahead)]
        for g in range(n_heads):
            s = pending.pop(0)
            ahead = g + lookahead
            if ahead < n_heads:
                pending.append(scores(j, ahead))
            elif not masked:
                s_sc[ahead - n_heads] = scores(j + 1, ahead - n_heads)
            if masked:
                s = jnp.where(keep, s, NEG)
            m_prev = m_sc[g]
            m_new = jnp.maximum(m_prev, jnp.max(s, axis=0, keepdims=True))
            p = jnp.exp2(s - m_new).astype(BF16)
            acc_sc[g] = jnp.exp2(m_prev - m_new) * acc_sc[g] + _dot(vt_ref[j, g], p)
            m_sc[g] = m_new

    def body(j, carry):
        step(j, False)
        return carry

    for n in range(lookahead):
        s_sc[n] = scores(0, n)
    lax.fori_loop(0, i, body, 0)
    step(i, True)
    o_t = jnp.concatenate([acc_sc[g, 0:head, :] / acc_sc[g, head:head + 1, :] for g in range(n_heads)],
                          axis=0)
    o_ref[...] = o_t.T.astype(o_ref.dtype)


def _forget_prompt(q, k, vt, c, *, head, tq):
    b, t, w = q.shape
    n_heads = LANE_GROUP // head
    v_rows = vt.shape[3]
    lookahead = 2
    assert lookahead <= n_heads
    return pl.pallas_call(
        functools.partial(_forget_kernel, head=head, lookahead=lookahead),
        out_shape=jax.ShapeDtypeStruct((b, t, w), BF16), grid=(b, w // LANE_GROUP, t // tq),
        in_specs=[pl.BlockSpec((None, tq, LANE_GROUP), lambda bi, hg, i: (bi, i, hg)),
                  pl.BlockSpec((None, t, LANE_GROUP), lambda bi, hg, i: (bi, 0, hg)),
                  pl.BlockSpec((None, t // tq, n_heads, v_rows, tq), lambda bi, hg, i: (bi, 0, hg, 0, 0)),
                  pl.BlockSpec((None, None, t, n_heads), lambda bi, hg, i: (bi, hg, 0, 0))],
        out_specs=pl.BlockSpec((None, tq, LANE_GROUP), lambda bi, hg, i: (bi, i, hg)),
        scratch_shapes=[pltpu.VMEM((n_heads, LANE_GROUP, tq), BF16), pltpu.VMEM((n_heads, 1, tq), F32),
                        pltpu.VMEM((n_heads, v_rows, tq), F32), pltpu.VMEM((lookahead, tq, tq), F32)],
        compiler_params=_cparams("arbitrary", "arbitrary", "arbitrary"), name="forget_prompt")(q, k, vt, c)


def _mem_kernel(q_ref, mk_ref, mv_ref, o_ref, *, heads):
    bb = q_ref.shape[0]
    head = q_ref.shape[2] // heads
    n_mem = mk_ref.shape[1] // heads
    units = [(e, h, slice(h * head, (h + 1) * head)) for e in range(bb) for h in range(heads)]
    s = [_nt_dot(q_ref[e, :, ln], mk_ref[e, pl.ds(h, n_mem, stride=heads), :].astype(BF16))
         for e, h, ln in units]
    p, inv_l = [], []
    for u in range(len(units)):
        ex = jnp.exp2(s[u] - jnp.max(s[u], axis=-1, keepdims=True))
        inv_l.append(1.0 / jnp.sum(ex, axis=-1, keepdims=True))
        p.append(ex.astype(BF16))
    for u, (e, h, ln) in enumerate(units):
        o = _dot(p[u], mv_ref[e, pl.ds(h, n_mem, stride=heads), :].astype(BF16))
        o_ref[e, :, ln] = (o * inv_l[u]).astype(o_ref.dtype)


def _mem_attn(q, mk, mv, *, heads, tq, bb):
    b, t, w = q.shape
    kv = pl.BlockSpec((bb,) + mk.shape[1:], lambda bi, i: (bi, 0, 0))
    qs = pl.BlockSpec((bb, tq, w), lambda bi, i: (bi, i, 0))
    return pl.pallas_call(
        functools.partial(_mem_kernel, heads=heads), out_shape=jax.ShapeDtypeStruct((b, t, w), BF16),
        grid=(b // bb, t // tq), in_specs=[qs, kv, kv], out_specs=qs,
        compiler_params=_cparams("arbitrary", "arbitrary"), name="mem_attn")(q, mk, mv)


def _merge_kernel(x_ref, oa_ref, ob_ref, om_ref, g_ref, wpa_ref, wpb_ref, wpm_ref, wo_ref, y_ref):
    d = x_ref.shape[1]
    h = (g_ref[:, 0:d].astype(F32) * _dot(oa_ref[...], wpa_ref[...])
         + g_ref[:, d:2 * d].astype(F32) * _dot(ob_ref[...], wpb_ref[...])
         + g_ref[:, 2 * d:3 * d].astype(F32) * _dot(om_ref[...], wpm_ref[...]))
    y_ref[...] = x_ref[...] + _dot(h.astype(BF16), wo_ref[...])


def _merge(x, oa, ob, om, g, wpa, wpb, wpm, wo, *, tm):
    rows, d = x.shape
    row = lambda i: (i, 0)
    acts = (x, oa, ob, om, g)
    return pl.pallas_call(
        _merge_kernel, out_shape=jax.ShapeDtypeStruct((rows, d), F32), grid=(rows // tm,),
        in_specs=[pl.BlockSpec((tm, a.shape[1]), row) for a in acts]
        + [_const_spec(a.shape) for a in (wpa, wpb, wpm, wo)],
        out_specs=pl.BlockSpec((tm, d), row), compiler_params=_cparams("arbitrary"),
        name="merge")(*acts, wpa, wpb, wpm, wo)


def _mlp_kernel(x_ref, g_ref, wup_ref, wdn_ref, y_ref, *, ff_chunk):
    x = x_ref[...]
    xn = _rms_rows(x, g_ref[...]).astype(BF16)
    y = x
    for c in range(wup_ref.shape[1] // ff_chunk):
        cols = slice(c * ff_chunk, (c + 1) * ff_chunk)
        u = jnp.maximum(_dot(xn, wup_ref[:, cols]), 0.0)
        y = y + _dot((u * u).astype(BF16), wdn_ref[cols, :])
    y_ref[...] = y


def _mlp(x, g, wup, wdn, *, tm, ff_chunk):
    rows, d = x.shape
    row = lambda i: (i, 0)
    return pl.pallas_call(
        functools.partial(_mlp_kernel, ff_chunk=ff_chunk), out_shape=jax.ShapeDtypeStruct((rows, d), F32),
        grid=(rows // tm,),
        in_specs=[pl.BlockSpec((tm, d), row)] + [_const_spec(a.shape) for a in (g, wup, wdn)],
        out_specs=pl.BlockSpec((tm, d), row), compiler_params=_cparams("arbitrary"),
        name="mlp")(x, g, wup, wdn)


def _pad_rows(x, rows):
    return jnp.concatenate([x, jnp.zeros((rows - x.shape[0], x.shape[1]), x.dtype)], axis=0)


def _cached_attention(q_ref, kc_ref, vc_ref, kn_ref, vn_ref, o_ref, bias_c, bias_n, n_pad):
    bb, heads, head, _ = kc_ref.shape
    units = [(e, h, slice(h * head, (h + 1) * head)) for e in range(bb) for h in range(heads)]
    sc = [_dot(q_ref[e, :, ln], kc_ref[e, h].astype(BF16)) + bias_c(e, h) for e, h, ln in units]
    sn = [_nt_dot(q_ref[e, :, ln], _pad_rows(kn_ref[e, :, ln].astype(BF16), n_pad)) + bias_n(e, h)
          for e, h, ln in units]
    pc, pn, inv_l = [], [], []
    for u in range(len(units)):
        m = jnp.maximum(jnp.max(sc[u], axis=-1, keepdims=True), jnp.max(sn[u], axis=-1, keepdims=True))
        pc.append(jnp.exp2(sc[u] - m))
        pn.append(jnp.exp2(sn[u] - m))
        inv_l.append(1.0 / (jnp.sum(pc[u], axis=-1, keepdims=True) + jnp.sum(pn[u], axis=-1, keepdims=True)))
    for u, (e, h, ln) in enumerate(units):
        o = (_nt_dot(pc[u].astype(BF16), vc_ref[e, h].astype(BF16))
             + _dot(pn[u].astype(BF16), _pad_rows(vn_ref[e, :, ln].astype(BF16), n_pad)))
        o_ref[e, :, ln] = (o * inv_l[u]).astype(o_ref.dtype)


def _band_sample_kernel(q_ref, kc_ref, vc_ref, kn_ref, vn_ref, bc_ref, bn_ref, o_ref):
    s_len = q_ref.shape[1]
    rows = lambda h: slice(h * s_len, (h + 1) * s_len)
    _cached_attention(q_ref, kc_ref, vc_ref, kn_ref, vn_ref, o_ref,
                      lambda e, h: bc_ref[rows(h), :], lambda e, h: bn_ref[rows(h), :], bn_ref.shape[1])


def _band_sample(q, kct, vct, kn, vn, bias_c, bias_n, *, bb):
    b, s_len, w = q.shape
    new = pl.BlockSpec((bb, s_len, w), lambda bi: (bi, 0, 0))
    cache = pl.BlockSpec((bb,) + kct.shape[1:], lambda bi: (bi, 0, 0, 0))
    return pl.pallas_call(
        _band_sample_kernel, out_shape=jax.ShapeDtypeStruct((b, s_len, w), BF16), grid=(b // bb,),
        in_specs=[new, cache, cache, new, new, _const_spec(bias_c.shape), _const_spec(bias_n.shape)],
        out_specs=new, compiler_params=_cparams("arbitrary"), name="band_sample")(
            q, kct, vct, kn, vn, bias_c, bias_n)


def _forget_sample_kernel(q_ref, kc_ref, vc_ref, kn_ref, vn_ref, c_ref, o_ref, *, n_pad):
    s_len = q_ref.shape[1]
    p_len = kc_ref.shape[3]
    causal = (lax.broadcasted_iota(jnp.int32, (s_len, n_pad), 1)
              <= lax.broadcasted_iota(jnp.int32, (s_len, n_pad), 0))
    ch = [(c_ref[e, :, p_len - 1:p_len] - c_ref[e]) * LOG2E for e in range(c_ref.shape[0])]
    _cached_attention(q_ref, kc_ref, vc_ref, kn_ref, vn_ref, o_ref,
                      lambda e, h: ch[e][h:h + 1, 0:p_len],
                      lambda e, h: jnp.where(causal, ch[e][h:h + 1, p_len:p_len + n_pad], NEG), n_pad)


def _forget_sample(q, kct, vct, kn, vn, c, *, n_pad):
    b, s_len, w = q.shape
    new = pl.BlockSpec((1, s_len, w), lambda bi: (bi, 0, 0))
    cache = pl.BlockSpec((1,) + kct.shape[1:], lambda bi: (bi, 0, 0, 0))
    cs = pl.BlockSpec((1,) + c.shape[1:], lambda bi: (bi, 0, 0))
    return pl.pallas_call(
        functools.partial(_forget_sample_kernel, n_pad=n_pad),
        out_shape=jax.ShapeDtypeStruct((b, s_len, w), BF16), grid=(b,),
        in_specs=[new, cache, cache, new, new, cs], out_specs=new,
        compiler_params=_cparams("arbitrary"), name="forget_sample")(q, kct, vct, kn, vn, c)


def _block_diag_mean(head, size=256):
    r = jnp.arange(size) // head
    return jnp.where(r[:, None] == r[None, :], 1.0 / head, 0.0).astype(BF16)


def _rel_bias_rows(rel_bias, d0, n_q, n_k):
    n = n_q + n_k - 1
    dist = d0 + (n_q - 1) - jnp.arange(n)
    e = rel_bias[jnp.clip(dist, -REL_CLIP, REL_CLIP) + REL_CLIP].T.astype(F32)
    h = e.shape[0]
    skew = jnp.tile(jnp.pad(e, ((0, 0), (0, 1))), (1, n_q))[:, :n_q * n].reshape(h, n_q, n)
    return skew[:, :, n_q - 1:].reshape(h * n_q, n_k)


def kernel(x_prompt, x_sample, mem_prompt, cache_a_k, cache_a_v, cache_b_k, cache_b_v, cache_b_logf,
           cache_mem_k, cache_mem_v, g_mix, w_in, b_f, g_qa, g_ka, g_qb, g_kb, g_qm, g_km, rel_bias,
           g_mem, w_mkv, w_pa, w_pb, w_pm, w_o, g_ffn, w_up, w_down):
    depth = w_in.shape[0]
    assert depth == 1
    batch, seq, d = x_prompt.shape
    dec_b, dec_s, _ = x_sample.shape
    _, _, n_cache, h_a, dh_a = cache_a_k.shape
    _, _, past, h_b, dh_b = cache_b_k.shape
    _, _, n_mem, h_m, dh_m = cache_mem_k.shape
    w_a, w_b, w_m = h_a * dh_a, h_b * dh_b, h_m * dh_m
    assert w_a == w_b == w_m == 512 and dh_a == dh_b == 64 and dh_m == 128 and h_b == 8
    keep_p = min(n_cache, seq)
    n_pad = 128
    l = 0

    qkv_end = 3 * w_a + 3 * w_b
    f_end = qkv_end + h_b
    qm_end = f_end + w_m
    w = w_in[l]
    wqkv = w[:, :qkv_end].astype(BF16)
    wft = jnp.zeros((16, d), BF16).at[:h_b].set(w[:, qkv_end:f_end].T.astype(BF16))
    wqm = w[:, f_end:qm_end].astype(BF16)
    wg = w[:, qm_end:].astype(BF16)
    gains = jnp.stack([jnp.tile(g[l], h_a) for g in (g_qa, g_ka, g_qb, g_kb)]).astype(F32)
    gqm = jnp.tile(g_qm[l], h_m)[None, :]
    gkm = jnp.tile(g_km[l], h_m)[None, :]
    bf = b_f[l][:, None].astype(F32)
    bd64 = _block_diag_mean(dh_a)
    bd128 = _block_diag_mean(dh_m)
    tri = (jnp.arange(128)[:, None] <= jnp.arange(128)[None, :]).astype(BF16)
    gmix = g_mix[l][None, :]
    proj_w = (gmix, wqkv, wft, wqm, wg, gains, gqm, bf, bd64, bd128)
    wpa, wpb, wpm, wo = (a[l].astype(BF16) for a in (w_pa, w_pb, w_pm, w_o))
    wup, wdn = w_up[l].astype(BF16), w_down[l].astype(BF16)
    gffn = g_ffn[l][None, :]

    xp = x_prompt.reshape(batch * seq, d)
    pp = _proj(xp, *proj_w, tm=512, group_rows=seq, keep_rows=keep_p)
    mk_f, mv_f = _memkv(mem_prompt.reshape(batch * n_mem, d), g_mem[l][None, :], w_mkv[l].astype(BF16),
                        gkm, bd128, tm=256, heads=h_m)
    tq_b = 512
    lf_p = pp["lft"].reshape(h_b, batch, seq).transpose(1, 0, 2)
    c_p = _cumsum_lanes(lf_p.reshape(batch * h_b, seq), tri, rb=batch * h_b)
    c_p = c_p.reshape(batch, h_b // 4, 4, seq).transpose(0, 1, 3, 2)
    vt = pp["vbt"].reshape(batch, seq // tq_b, h_b, V_ROWS, tq_b)
    three = lambda a: a.reshape(batch, seq, -1)
    bias_p = _rel_bias_rows(rel_bias[l], n_cache, CHUNK, n_cache + CHUNK) * LOG2E
    o_a = _band_prompt(three(pp["qa"]), three(pp["ka"]), three(pp["va"]), bias_p, reach=n_cache, tq=256)
    o_b = _forget_prompt(three(pp["qb"]), three(pp["kb"]), vt, c_p, head=dh_b, tq=tq_b)
    o_m = _mem_attn(three(pp["qm"]), mk_f.reshape(batch, n_mem * h_m, dh_m),
                    mv_f.reshape(batch, n_mem * h_m, dh_m), heads=h_m, tq=512, bb=1)
    x1 = _merge(xp, o_a.reshape(-1, w_a), o_b.reshape(-1, w_b), o_m.reshape(-1, w_m), pp["gates"],
                wpa, wpb, wpm, wo, tm=512)
    y_prompt = _mlp(x1, gffn, wup, wdn, tm=512, ff_chunk=1024).reshape(batch, seq, d)

    xs = x_sample.reshape(dec_b * dec_s, d)
    ps = _proj(xs, *proj_w, tm=256)
    sthree = lambda a: a.reshape(dec_b, dec_s, -1)
    heads_t = lambda a: jnp.transpose(a[l], (0, 2, 3, 1))
    bias_c = _rel_bias_rows(rel_bias[l], n_cache, dec_s, n_cache) * LOG2E
    bias_n = jnp.full((h_a * dec_s, n_pad), NEG, F32).at[:, :dec_s].set(
        _rel_bias_rows(rel_bias[l], 0, dec_s, dec_s) * LOG2E)
    o_a_s = _band_sample(sthree(ps["qa"]), heads_t(cache_a_k), heads_t(cache_a_v), sthree(ps["ka"]),
                         sthree(ps["va"]), bias_c, bias_n, bb=4)
    lf_new = ps["lft"].reshape(h_b, dec_b, dec_s).transpose(1, 0, 2)
    lf_all = jnp.concatenate([cache_b_logf[l].astype(F32).transpose(0, 2, 1), lf_new,
                              jnp.zeros((dec_b, h_b, n_pad - dec_s), F32)], axis=-1)
    c_s = _cumsum_lanes(lf_all.reshape(dec_b * h_b, past + n_pad), tri, rb=32)
    o_b_s = _forget_sample(sthree(ps["qb"]), heads_t(cache_b_k), heads_t(cache_b_v), sthree(ps["kb"]),
                           sthree(ps["vb_f"]), c_s.reshape(dec_b, h_b, past + n_pad), n_pad=n_pad)
    o_m_s = _mem_attn(sthree(ps["qm"]), cache_mem_k[l].reshape(dec_b, n_mem * h_m, dh_m),
                      cache_mem_v[l].reshape(dec_b, n_mem * h_m, dh_m), heads=h_m, tq=dec_s, bb=8)
    x1_s = _merge(xs, o_a_s.reshape(-1, w_a), o_b_s.reshape(-1, w_b), o_m_s.reshape(-1, w_m), ps["gates"],
                  wpa, wpb, wpm, wo, tm=256)
    y_sample = _mlp(x1_s, gffn, wup, wdn, tm=256, ff_chunk=1024).reshape(dec_b, dec_s, d)

    lead = lambda a, *shape: a.reshape((depth,) + shape)
    frames_major = lambda a: jnp.transpose(a, (0, 3, 1, 2))[None]
    return (y_prompt, y_sample,
            frames_major(pp["ka_f"]), frames_major(pp["va_f"]),
            frames_major(pp["kb_f"]), frames_major(pp["vb_f"]),
            lf_p.transpose(0, 2, 1)[None],
            lead(mk_f, batch, n_mem, h_m, dh_m), lead(mv_f, batch, n_mem, h_m, dh_m),
            lead(ps["ka_f"], dec_b, dec_s, h_a, dh_a), lead(ps["va_f"], dec_b, dec_s, h_a, dh_a),
            lead(ps["kb_f"], dec_b, dec_s, h_b, dh_b), lead(ps["vb_f"], dec_b, dec_s, h_b, dh_b),
            lead(ps["lft"].T, dec_b, dec_s, h_b))
```

```python
import functools

import jax
import jax.numpy as jnp
from jax import lax
from jax.experimental import pallas as pl
from jax.experimental.pallas import tpu as pltpu

BF16 = jnp.bfloat16
F32 = jnp.float32

EPS = 1e-6
CHUNK = 64
REL_CLIP = 128
NEG = -1e30
LANE_GROUP = 256
LOG2E = 1.4426950408889634
V_ROWS = 64 + 16
VMEM_LIMIT = 56 * 1024 * 1024


def _cparams(*sem):
    return pltpu.CompilerParams(dimension_semantics=sem, vmem_limit_bytes=VMEM_LIMIT)


def _const_spec(shape):
    nd = len(shape)
    return pl.BlockSpec(shape, lambda *_: (0,) * nd, pipeline_mode=pl.Buffered(1))


def _nt_dot(a, b):
    return lax.dot_general(a, b, (((1,), (1,)), ((), ())), preferred_element_type=F32)


def _dot(a, b):
    return jnp.dot(a, b, preferred_element_type=F32)


def _rms_rows(x, gain):
    ms = jnp.mean(x * x, axis=-1, keepdims=True)
    return x * lax.rsqrt(ms + EPS) * gain


def _head_norm(z, bd_ref, gain):
    z2 = (z * z).astype(BF16)
    ms = jnp.concatenate(
        [_dot(z2[:, h * 256:(h + 1) * 256], bd_ref[...]) for h in range(z.shape[1] // 256)], axis=-1)
    return z * lax.rsqrt(ms + EPS) * gain


def _log_sigmoid(x):
    return jnp.minimum(x, 0.0) - jnp.log1p(jnp.exp(-jnp.abs(x)))


def _lane_group(width, head):
    return lax.broadcasted_iota(jnp.int32, (1, width), 1) // head


def _block_diag_rows(q4, head):
    grp = _lane_group(q4.shape[1], head)
    zero = jnp.zeros_like(q4)
    return jnp.concatenate([jnp.where(grp == g, q4, zero) for g in range(q4.shape[1] // head)], axis=0)


def _pick_diag(o, rows, head):
    grp = _lane_group(o.shape[1], head)
    out = jnp.zeros((rows, o.shape[1]), o.dtype)
    for g in range(o.shape[1] // head):
        out = jnp.where(grp == g, o[g * rows:(g + 1) * rows], out)
    return out


def _store_heads_t(ref, z_t):
    dh = ref.shape[1]
    for h in range(ref.shape[0]):
        ref[h] = z_t[h * dh:(h + 1) * dh, :]


def _proj_kernel(x_ref, gmix_ref, wqkv_ref, wft_ref, wqm_ref, gains_ref, gqm_ref, bf_ref,
                 bd64_ref, bd128_ref, *out_refs, out_names, scale_a, scale_b, scale_m, keep_from,
                 tiles_per_group):
    o = dict(zip(out_names, out_refs))
    head_major = "vbt" in o
    xn = _rms_rows(x_ref[...], gmix_ref[...]).astype(BF16)

    def seg(j):
        return _dot(xn, wqkv_ref[:, j * 512:(j + 1) * 512])

    kb = _head_norm(seg(4), bd64_ref, gains_ref[3:4, :])
    o["kb"][...] = kb.astype(BF16)
    vb = seg(5)
    if head_major:
        _store_heads_t(o["kb_f"], kb.T)
        vb_t = vb.T
        _store_heads_t(o["vb_f"], vb_t)
        vbt = o["vbt"]
        head = vb_t.shape[0] // vbt.shape[0]
        pad = vbt.shape[1] - head
        ones_row = (lax.broadcasted_iota(jnp.int32, (pad, vb_t.shape[1]), 0) == 0).astype(BF16)
        for h in range(vbt.shape[0]):
            vbt[h, 0:head, :] = vb_t[h * head:(h + 1) * head, :].astype(BF16)
            vbt[h, head:head + pad, :] = ones_row
    else:
        o["kb_f"][...] = kb
        o["vb_f"][...] = vb
    qb = _head_norm(seg(3), bd64_ref, gains_ref[2:3, :])
    o["qb"][...] = (qb * scale_b).astype(BF16)
    qa = _head_norm(seg(0), bd64_ref, gains_ref[0:1, :])
    o["qa"][...] = (qa * scale_a).astype(BF16)
    ka = _head_norm(seg(1), bd64_ref, gains_ref[1:2, :])
    o["ka"][...] = ka.astype(BF16)
    qm = _head_norm(_dot(xn, wqm_ref[...]), bd128_ref, gqm_ref[...])
    o["qm"][...] = (qm * scale_m).astype(BF16)
    flt = _nt_dot(wft_ref[...], xn)
    o["lft"][...] = _log_sigmoid(flt[0:8, :] + bf_ref[...])
    va = seg(2)
    o["va"][...] = va.astype(BF16)
    if head_major:
        @pl.when(pl.program_id(0) % tiles_per_group >= keep_from)
        def _():
            _store_heads_t(o["ka_f"], ka.T)
            _store_heads_t(o["va_f"], va.T)
    else:
        o["ka_f"][...] = ka
        o["va_f"][...] = va


def _proj(x, gmix, wqkv, wft, wqm, gains, gqm, bf, bd64, bd128, *, tm, group_rows=None, keep_rows=None):
    rows, d = x.shape
    n_tiles = rows // tm
    row = lambda i: (i, 0)
    wide = lambda dt: (jax.ShapeDtypeStruct((rows, 512), dt), pl.BlockSpec((tm, 512), row))
    outs = {n: wide(BF16) for n in ("qa", "ka", "va", "qb", "kb", "qm")}
    outs["lft"] = (jax.ShapeDtypeStruct((8, rows), F32), pl.BlockSpec((8, tm), lambda i: (0, i)))
    tpg = keep_from = 1
    if group_rows is None:
        for n in ("ka_f", "va_f", "kb_f", "vb_f"):
            outs[n] = wide(F32)
    else:
        assert keep_rows == tm
        tpg = group_rows // tm
        keep_from = tpg - 1
        groups = rows // group_rows
        kept = (jax.ShapeDtypeStruct((groups, 8, 64, keep_rows), F32),
                pl.BlockSpec((None, 8, 64, tm), lambda i: (i // tpg, 0, 0, 0)))
        full = (jax.ShapeDtypeStruct((groups, 8, 64, group_rows), F32),
                pl.BlockSpec((None, 8, 64, tm), lambda i: (i // tpg, 0, 0, i % tpg)))
        outs.update(ka_f=kept, va_f=kept, kb_f=full, vb_f=full)
        outs["vbt"] = (jax.ShapeDtypeStruct((n_tiles, 8, V_ROWS, tm), BF16),
                       pl.BlockSpec((None, 8, V_ROWS, tm), lambda i: (i, 0, 0, 0)))
    names = tuple(outs)
    in_specs = [pl.BlockSpec((tm, d), row)] + [_const_spec(a.shape) for a in
                                               (gmix, wqkv, wft, wqm, gains, gqm, bf, bd64, bd128)]
    kern = functools.partial(_proj_kernel, out_names=names, scale_a=LOG2E * 64 ** -0.5,
                             scale_b=LOG2E * 64 ** -0.5, scale_m=LOG2E * 128 ** -0.5,
                             keep_from=keep_from, tiles_per_group=tpg)
    res = pl.pallas_call(kern, out_shape=[outs[n][0] for n in names], grid=(n_tiles,), in_specs=in_specs,
                         out_specs=[outs[n][1] for n in names], compiler_params=_cparams("arbitrary"),
                         name="proj")(x, gmix, wqkv, wft, wqm, gains, gqm, bf, bd64, bd128)
    return dict(zip(names, res))


def _memkv_kernel(m_ref, gmem_ref, w_ref, gkm_ref, bd128_ref, mk_o, mv_o, *, heads):
    tm = m_ref.shape[0]
    xn = _rms_rows(m_ref[...], gmem_ref[...]).astype(BF16)
    half = w_ref.shape[1] // 2
    dh = half // heads
    mk = _head_norm(_dot(xn, w_ref[:, :half]), bd128_ref, gkm_ref[...])
    mv = _dot(xn, w_ref[:, half:])
    for h in range(heads):
        mk_o[pl.ds(h, tm, stride=heads), :] = mk[:, h * dh:(h + 1) * dh]
        mv_o[pl.ds(h, tm, stride=heads), :] = mv[:, h * dh:(h + 1) * dh]


def _memkv(mem, gmem, w, gkm, bd128, *, tm, heads):
    rows, d = mem.shape
    dh = w.shape[1] // 2 // heads
    row = lambda i: (i, 0)
    return pl.pallas_call(
        functools.partial(_memkv_kernel, heads=heads),
        out_shape=[jax.ShapeDtypeStruct((rows * heads, dh), F32)] * 2, grid=(rows // tm,),
        in_specs=[pl.BlockSpec((tm, d), row)] + [_const_spec(a.shape) for a in (gmem, w, gkm, bd128)],
        out_specs=[pl.BlockSpec((tm * heads, dh), row)] * 2, compiler_params=_cparams("arbitrary"),
        name="memkv")(mem, gmem, w, gkm, bd128)


def _cumsum_kernel(x_ref, tri_ref, o_ref):
    rows, length = x_ref.shape
    tri = tri_ref[...]

    def local(c):
        x = x_ref[:, c * 128:(c + 1) * 128]
        hi = x.astype(BF16)
        r1 = x - hi.astype(F32)
        mid = r1.astype(BF16)
        lo = (r1 - mid.astype(F32)).astype(BF16)
        return _dot(hi, tri) + _dot(mid, tri) + _dot(lo, tri)

    chunks = [local(c) for c in range(length // 128)]
    carry = jnp.zeros((rows, 1), F32)
    for c, cc in enumerate(chunks):
        o_ref[:, c * 128:(c + 1) * 128] = cc + carry
        carry = carry + cc[:, 127:128]


def _cumsum_lanes(x, tri, *, rb):
    rows, length = x.shape
    return pl.pallas_call(
        _cumsum_kernel, out_shape=jax.ShapeDtypeStruct((rows, length), F32), grid=(rows // rb,),
        in_specs=[pl.BlockSpec((rb, length), lambda i: (i, 0)), _const_spec(tri.shape)],
        out_specs=pl.BlockSpec((rb, length), lambda i: (i, 0)), compiler_params=_cparams("arbitrary"),
        name="cumsum")(x, tri)


def _band_kernel(q_ref, k0_ref, k1_ref, k2_ref, v0_ref, v1_ref, v2_ref, bias_ref, o_ref, k_sc, v_sc,
                 *, reach, lookahead):
    tq = q_ref.shape[0]
    band = reach + CHUNK
    i = pl.program_id(1)
    for n, (kr, vr) in enumerate(((k0_ref, v0_ref), (k1_ref, v1_ref), (k2_ref, v2_ref))):
        k_sc[n * tq:(n + 1) * tq, :] = kr[...]
        v_sc[n * tq:(n + 1) * tq, :] = vr[...]
    col = lax.broadcasted_iota(jnp.int32, (1, band), 1)
    units = [(j, hg) for j in range(tq // CHUNK) for hg in range(q_ref.shape[1] // LANE_GROUP)]

    def scores(j, hg):
        r0 = 2 * tq + j * CHUNK - reach
        lanes = slice(hg * LANE_GROUP, (hg + 1) * LANE_GROUP)
        qbd = _block_diag_rows(q_ref[j * CHUNK:(j + 1) * CHUNK, lanes], CHUNK)
        s = _nt_dot(qbd, k_sc[r0:r0 + band, lanes]) + bias_ref[hg * 4 * CHUNK:(hg + 1) * 4 * CHUNK, :]
        return jnp.where((i - 2) * tq + r0 + col >= 0, s, NEG)

    pending = [scores(*u) for u in units[:lookahead]]
    for n, (j, hg) in enumerate(units):
        s = pending.pop(0)
        if n + lookahead < len(units):
            pending.append(scores(*units[n + lookahead]))
        r0 = 2 * tq + j * CHUNK - reach
        lanes = slice(hg * LANE_GROUP, (hg + 1) * LANE_GROUP)
        m = jnp.max(s, axis=-1, keepdims=True)
        p = jnp.exp2(s - m)
        l = jnp.sum(p, axis=-1, keepdims=True)
        o = _dot(p.astype(BF16), v_sc[r0:r0 + band, lanes]) / l
        o_ref[j * CHUNK:(j + 1) * CHUNK, lanes] = _pick_diag(o, CHUNK, CHUNK).astype(o_ref.dtype)


def _band_prompt(q, k, v, bias, *, reach, tq):
    b, t, w = q.shape
    assert reach == 2 * tq and tq % CHUNK == 0
    blk = lambda off: pl.BlockSpec((None, tq, w), lambda bi, i: (bi, jnp.maximum(i - off, 0), 0))
    return pl.pallas_call(
        functools.partial(_band_kernel, reach=reach, lookahead=2),
        out_shape=jax.ShapeDtypeStruct((b, t, w), BF16), grid=(b, t // tq),
        in_specs=[blk(0), blk(2), blk(1), blk(0), blk(2), blk(1), blk(0), _const_spec(bias.shape)],
        out_specs=blk(0),
        scratch_shapes=[pltpu.VMEM((3 * tq, w), BF16), pltpu.VMEM((3 * tq, w), BF16)],
        compiler_params=_cparams("arbitrary", "arbitrary"), name="band_prompt")(q, k, k, k, v, v, v, bias)


def _forget_kernel(q_ref, k_ref, vt_ref, c_ref, o_ref, qt_sc, m_sc, acc_sc, *, head, lookahead):
    tq, width = q_ref.shape
    n_heads = width // head
    i = pl.program_id(2)
    q_t = q_ref[...].astype(F32).T
    row_grp = lax.broadcasted_iota(jnp.int32, (width, 1), 0) // head
    for g in range(n_heads):
        qt_sc[g] = jnp.where(row_grp == g, q_t, 0.0).astype(BF16)
    m_sc[...] = jnp.full(m_sc.shape, NEG, F32)
    acc_sc[...] = jnp.zeros(acc_sc.shape, F32)
    c_first = c_ref[pl.ds(pl.multiple_of(i * tq, tq), 8), :][0:1, :]

    def step(j, masked):
        k0 = pl.multiple_of(j * tq, tq)
        kt = k_ref[pl.ds(k0, tq), :]
        bias = (c_first - c_ref[pl.ds(k0, tq), :]) * LOG2E
        if masked:
            keep = (lax.broadcasted_iota(jnp.int32, (tq, tq), 0)
                    <= lax.broadcasted_iota(jnp.int32, (tq, tq), 1))

        def scores(g):
            s = _dot(kt, qt_sc[g]) + bias[:, g:g + 1]
            return jnp.where(keep, s, NEG) if masked else s

        pending = [scores(g) for g in range(min(lookahead, n_heads))]
        for g in range(n_heads):
            s = pending.pop(0)
            if g + lookahead < n_heads:
                pending.append(scores(g + lookahead))
            m_prev = m_sc[g]
            m_new = jnp.maximum(m_prev, jnp.max(s, axis=0, keepdims=True))
            p = jnp.exp2(s - m_new).astype(BF16)
            acc_sc[g] = jnp.exp2(m_prev - m_new) * acc_sc[g] + _dot(vt_ref[j, g], p)
            m_sc[g] = m_new

    def body(j, carry):
        step(j, False)
        return carry

    lax.fori_loop(0, i, body, 0)
    step(i, True)
    o_t = jnp.concatenate([acc_sc[g, 0:head, :] / acc_sc[g, head:head + 1, :] for g in range(n_heads)],
                          axis=0)
    o_ref[...] = o_t.T.astype(o_ref.dtype)


def _forget_prompt(q, k, vt, c, *, head, tq):
    b, t, w = q.shape
    n_heads = LANE_GROUP // head
    v_rows = vt.shape[3]
    lookahead = 2
    assert lookahead <= n_heads
    return pl.pallas_call(
        functools.partial(_forget_kernel, head=head, lookahead=lookahead),
        out_shape=jax.ShapeDtypeStruct((b, t, w), BF16), grid=(b, w // LANE_GROUP, t // tq),
        in_specs=[pl.BlockSpec((None, tq, LANE_GROUP), lambda bi, hg, i: (bi, i, hg)),
                  pl.BlockSpec((None, t, LANE_GROUP), lambda bi, hg, i: (bi, 0, hg)),
                  pl.BlockSpec((None, t // tq, n_heads, v_rows, tq), lambda bi, hg, i: (bi, 0, hg, 0, 0)),
                  pl.BlockSpec((None, None, t, n_heads), lambda bi, hg, i: (bi, hg, 0, 0))],
        out_specs=pl.BlockSpec((None, tq, LANE_GROUP), lambda bi, hg, i: (bi, i, hg)),
        scratch_shapes=[pltpu.VMEM((n_heads, LANE_GROUP, tq), BF16), pltpu.VMEM((n_heads, 1, tq), F32),
                        pltpu.VMEM((n_heads, v_rows, tq), F32)],
        compiler_params=_cparams("arbitrary", "arbitrary", "arbitrary"), name="forget_prompt")(q, k, vt, c)


def _mem_kernel(q_ref, mk_ref, mv_ref, o_ref, *, heads):
    bb = q_ref.shape[0]
    head = q_ref.shape[2] // heads
    n_mem = mk_ref.shape[1] // heads
    units = [(e, h, slice(h * head, (h + 1) * head)) for e in range(bb) for h in range(heads)]
    s = [_nt_dot(q_ref[e, :, ln], mk_ref[e, pl.ds(h, n_mem, stride=heads), :].astype(BF16))
         for e, h, ln in units]
    p, inv_l = [], []
    for u in range(len(units)):
        ex = jnp.exp2(s[u] - jnp.max(s[u], axis=-1, keepdims=True))
        inv_l.append(1.0 / jnp.sum(ex, axis=-1, keepdims=True))
        p.append(ex.astype(BF16))
    for u, (e, h, ln) in enumerate(units):
        o = _dot(p[u], mv_ref[e, pl.ds(h, n_mem, stride=heads), :].astype(BF16))
        o_ref[e, :, ln] = (o * inv_l[u]).astype(o_ref.dtype)


def _mem_attn(q, mk, mv, *, heads, tq, bb):
    b, t, w = q.shape
    kv = pl.BlockSpec((bb,) + mk.shape[1:], lambda bi, i: (bi, 0, 0))
    qs = pl.BlockSpec((bb, tq, w), lambda bi, i: (bi, i, 0))
    return pl.pallas_call(
        functools.partial(_mem_kernel, heads=heads), out_shape=jax.ShapeDtypeStruct((b, t, w), BF16),
        grid=(b // bb, t // tq), in_specs=[qs, kv, kv], out_specs=qs,
        compiler_params=_cparams("arbitrary", "arbitrary"), name="mem_attn")(q, mk, mv)


def _post_kernel(x_ref, oa_ref, ob_ref, om_ref, gmix_ref, wg_ref, wpa_ref, wpb_ref, wpm_ref, wo_ref,
                 gffn_ref, wup_ref, wdn_ref, y_ref, *, col_chunk, ff_chunk):
    x = x_ref[...]
    d = x.shape[1]
    xn = _rms_rows(x, gmix_ref[...]).astype(BF16)
    branches = ((oa_ref[...], wpa_ref), (ob_ref[...], wpb_ref), (om_ref[...], wpm_ref))
    h = []
    for c in range(d // col_chunk):
        cols = slice(c * col_chunk, (c + 1) * col_chunk)
        hc = 0.0
        for b, (o_b, wp_ref) in enumerate(branches):
            gate = jax.nn.sigmoid(_dot(xn, wg_ref[:, b * d + c * col_chunk:b * d + (c + 1) * col_chunk]))
            hc = hc + gate * _dot(o_b, wp_ref[:, cols])
        h.append(hc.astype(BF16))
    x1 = x + _dot(jnp.concatenate(h, axis=-1), wo_ref[...])
    xn1 = _rms_rows(x1, gffn_ref[...]).astype(BF16)
    y = x1
    for c in range(wup_ref.shape[1] // ff_chunk):
        cols = slice(c * ff_chunk, (c + 1) * ff_chunk)
        u = jnp.maximum(_dot(xn1, wup_ref[:, cols]), 0.0)
        y = y + _dot((u * u).astype(BF16), wdn_ref[cols, :])
    y_ref[...] = y


def _post(x, oa, ob, om, gmix, wg, wpa, wpb, wpm, wo, gffn, wup, wdn, *, tm):
    rows, d = x.shape
    row = lambda i: (i, 0)
    acts = (x, oa, ob, om)
    weights = (gmix, wg, wpa, wpb, wpm, wo, gffn, wup, wdn)
    return pl.pallas_call(
        functools.partial(_post_kernel, col_chunk=512, ff_chunk=1024),
        out_shape=jax.ShapeDtypeStruct((rows, d), F32), grid=(rows // tm,),
        in_specs=[pl.BlockSpec((tm, a.shape[1]), row) for a in acts] + [_const_spec(a.shape) for a in weights],
        out_specs=pl.BlockSpec((tm, d), row), compiler_params=_cparams("arbitrary"),
        name="post")(*acts, *weights)


def _pad_rows(x, rows):
    return jnp.concatenate([x, jnp.zeros((rows - x.shape[0], x.shape[1]), x.dtype)], axis=0)


def _cached_attention(q_ref, kc_ref, vc_ref, kn_ref, vn_ref, o_ref, bias_c, bias_n, n_pad):
    bb, heads, head, _ = kc_ref.shape
    units = [(e, h, slice(h * head, (h + 1) * head)) for e in range(bb) for h in range(heads)]
    sc = [_dot(q_ref[e, :, ln], kc_ref[e, h].astype(BF16)) + bias_c(e, h) for e, h, ln in units]
    sn = [_nt_dot(q_ref[e, :, ln], _pad_rows(kn_ref[e, :, ln].astype(BF16), n_pad)) + bias_n(e, h)
          for e, h, ln in units]
    pc, pn, inv_l = [], [], []
    for u in range(len(units)):
        m = jnp.maximum(jnp.max(sc[u], axis=-1, keepdims=True), jnp.max(sn[u], axis=-1, keepdims=True))
        pc.append(jnp.exp2(sc[u] - m))
        pn.append(jnp.exp2(sn[u] - m))
        inv_l.append(1.0 / (jnp.sum(pc[u], axis=-1, keepdims=True) + jnp.sum(pn[u], axis=-1, keepdims=True)))
    for u, (e, h, ln) in enumerate(units):
        o = (_nt_dot(pc[u].astype(BF16), vc_ref[e, h].astype(BF16))
             + _dot(pn[u].astype(BF16), _pad_rows(vn_ref[e, :, ln].astype(BF16), n_pad)))
        o_ref[e, :, ln] = (o * inv_l[u]).astype(o_ref.dtype)


def _band_sample_kernel(q_ref, kc_ref, vc_ref, kn_ref, vn_ref, bc_ref, bn_ref, o_ref):
    s_len = q_ref.shape[1]
    rows = lambda h: slice(h * s_len, (h + 1) * s_len)
    _cached_attention(q_ref, kc_ref, vc_ref, kn_ref, vn_ref, o_ref,
                      lambda e, h: bc_ref[rows(h), :], lambda e, h: bn_ref[rows(h), :], bn_ref.shape[1])


def _band_sample(q, kct, vct, kn, vn, bias_c, bias_n, *, bb):
    b, s_len, w = q.shape
    new = pl.BlockSpec((bb, s_len, w), lambda bi: (bi, 0, 0))
    cache = pl.BlockSpec((bb,) + kct.shape[1:], lambda bi: (bi, 0, 0, 0))
    return pl.pallas_call(
        _band_sample_kernel, out_shape=jax.ShapeDtypeStruct((b, s_len, w), BF16), grid=(b // bb,),
        in_specs=[new, cache, cache, new, new, _const_spec(bias_c.shape), _const_spec(bias_n.shape)],
        out_specs=new, compiler_params=_cparams("arbitrary"), name="band_sample")(
            q, kct, vct, kn, vn, bias_c, bias_n)


def _forget_sample_kernel(q_ref, kc_ref, vc_ref, kn_ref, vn_ref, c_ref, o_ref, *, n_pad):
    s_len = q_ref.shape[1]
    p_len = kc_ref.shape[3]
    causal = (lax.broadcasted_iota(jnp.int32, (s_len, n_pad), 1)
              <= lax.broadcasted_iota(jnp.int32, (s_len, n_pad), 0))
    ch = [(c_ref[e, :, p_len - 1:p_len] - c_ref[e]) * LOG2E for e in range(c_ref.shape[0])]
    _cached_attention(q_ref, kc_ref, vc_ref, kn_ref, vn_ref, o_ref,
                      lambda e, h: ch[e][h:h + 1, 0:p_len],
                      lambda e, h: jnp.where(causal, ch[e][h:h + 1, p_len:p_len + n_pad], NEG), n_pad)


def _forget_sample(q, kct, vct, kn, vn, c, *, n_pad):
    b, s_len, w = q.shape
    new = pl.BlockSpec((1, s_len, w), lambda bi: (bi, 0, 0))
    cache = pl.BlockSpec((1,) + kct.shape[1:], lambda bi: (bi, 0, 0, 0))
    cs = pl.BlockSpec((1,) + c.shape[1:], lambda bi: (bi, 0, 0))
    return pl.pallas_call(
        functools.partial(_forget_sample_kernel, n_pad=n_pad),
        out_shape=jax.ShapeDtypeStruct((b, s_len, w), BF16), grid=(b,),
        in_specs=[new, cache, cache, new, new, cs], out_specs=new,
        compiler_params=_cparams("arbitrary"), name="forget_sample")(q, kct, vct, kn, vn, c)


def _block_diag_mean(head, size=256):
    r = jnp.arange(size) // head
    return jnp.where(r[:, None] == r[None, :], 1.0 / head, 0.0).astype(BF16)


def _rel_bias_rows(rel_bias, d0, n_q, n_k):
    n = n_q + n_k - 1
    dist = d0 + (n_q - 1) - jnp.arange(n)
    e = rel_bias[jnp.clip(dist, -REL_CLIP, REL_CLIP) + REL_CLIP].T.astype(F32)
    h = e.shape[0]
    skew = jnp.tile(jnp.pad(e, ((0, 0), (0, 1))), (1, n_q))[:, :n_q * n].reshape(h, n_q, n)
    return skew[:, :, n_q - 1:].reshape(h * n_q, n_k)


def kernel(x_prompt, x_sample, mem_prompt, cache_a_k, cache_a_v, cache_b_k, cache_b_v, cache_b_logf,
           cache_mem_k, cache_mem_v, g_mix, w_in, b_f, g_qa, g_ka, g_qb, g_kb, g_qm, g_km, rel_bias,
           g_mem, w_mkv, w_pa, w_pb, w_pm, w_o, g_ffn, w_up, w_down):
    depth = w_in.shape[0]
    assert depth == 1
    batch, seq, d = x_prompt.shape
    dec_b, dec_s, _ = x_sample.shape
    _, _, n_cache, h_a, dh_a = cache_a_k.shape
    _, _, past, h_b, dh_b = cache_b_k.shape
    _, _, n_mem, h_m, dh_m = cache_mem_k.shape
    w_a, w_b, w_m = h_a * dh_a, h_b * dh_b, h_m * dh_m
    assert w_a == w_b == w_m == 512 and dh_a == dh_b == 64 and dh_m == 128 and h_b == 8
    keep_p = min(n_cache, seq)
    n_pad = 128
    l = 0

    qkv_end = 3 * w_a + 3 * w_b
    f_end = qkv_end + h_b
    qm_end = f_end + w_m
    w = w_in[l]
    wqkv = w[:, :qkv_end].astype(BF16)
    wft = jnp.zeros((16, d), BF16).at[:h_b].set(w[:, qkv_end:f_end].T.astype(BF16))
    wqm = w[:, f_end:qm_end].astype(BF16)
    wg = w[:, qm_end:].astype(BF16)
    gains = jnp.stack([jnp.tile(g[l], h_a) for g in (g_qa, g_ka, g_qb, g_kb)]).astype(F32)
    gqm = jnp.tile(g_qm[l], h_m)[None, :]
    gkm = jnp.tile(g_km[l], h_m)[None, :]
    bf = b_f[l][:, None].astype(F32)
    bd64 = _block_diag_mean(dh_a)
    bd128 = _block_diag_mean(dh_m)
    tri = (jnp.arange(128)[:, None] <= jnp.arange(128)[None, :]).astype(BF16)
    gmix = g_mix[l][None, :]
    proj_w = (gmix, wqkv, wft, wqm, gains, gqm, bf, bd64, bd128)
    wpa, wpb, wpm, wo = (a[l].astype(BF16) for a in (w_pa, w_pb, w_pm, w_o))
    wup, wdn = w_up[l].astype(BF16), w_down[l].astype(BF16)
    gffn = g_ffn[l][None, :]

    xp = x_prompt.reshape(batch * seq, d)
    pp = _proj(xp, *proj_w, tm=512, group_rows=seq, keep_rows=keep_p)
    mk_f, mv_f = _memkv(mem_prompt.reshape(batch * n_mem, d), g_mem[l][None, :], w_mkv[l].astype(BF16),
                        gkm, bd128, tm=256, heads=h_m)
    tq_b = 512
    lf_p = pp["lft"].reshape(h_b, batch, seq).transpose(1, 0, 2)
    c_p = _cumsum_lanes(lf_p.reshape(batch * h_b, seq), tri, rb=batch * h_b)
    c_p = c_p.reshape(batch, h_b // 4, 4, seq).transpose(0, 1, 3, 2)
    vt = pp["vbt"].reshape(batch, seq // tq_b, h_b, V_ROWS, tq_b)
    three = lambda a: a.reshape(batch, seq, -1)
    bias_p = _rel_bias_rows(rel_bias[l], n_cache, CHUNK, n_cache + CHUNK) * LOG2E
    o_a = _band_prompt(three(pp["qa"]), three(pp["ka"]), three(pp["va"]), bias_p, reach=n_cache, tq=256)
    o_b = _forget_prompt(three(pp["qb"]), three(pp["kb"]), vt, c_p, head=dh_b, tq=tq_b)
    o_m = _mem_attn(three(pp["qm"]), mk_f.reshape(batch, n_mem * h_m, dh_m),
                    mv_f.reshape(batch, n_mem * h_m, dh_m), heads=h_m, tq=512, bb=1)
    post_w = (gmix, wg, wpa, wpb, wpm, wo, gffn, wup, wdn)
    y_prompt = _post(xp, o_a.reshape(-1, w_a), o_b.reshape(-1, w_b), o_m.reshape(-1, w_m), *post_w,
                     tm=512).reshape(batch, seq, d)

    xs = x_sample.reshape(dec_b * dec_s, d)
    ps = _proj(xs, *proj_w, tm=256)
    sthree = lambda a: a.reshape(dec_b, dec_s, -1)
    heads_t = lambda a: jnp.transpose(a[l], (0, 2, 3, 1))
    bias_c = _rel_bias_rows(rel_bias[l], n_cache, dec_s, n_cache) * LOG2E
    bias_n = jnp.full((h_a * dec_s, n_pad), NEG, F32).at[:, :dec_s].set(
        _rel_bias_rows(rel_bias[l], 0, dec_s, dec_s) * LOG2E)
    o_a_s = _band_sample(sthree(ps["qa"]), heads_t(cache_a_k), heads_t(cache_a_v), sthree(ps["ka"]),
                         sthree(ps["va"]), bias_c, bias_n, bb=4)
    lf_new = ps["lft"].reshape(h_b, dec_b, dec_s).transpose(1, 0, 2)
    lf_all = jnp.concatenate([cache_b_logf[l].astype(F32).transpose(0, 2, 1), lf_new,
                              jnp.zeros((dec_b, h_b, n_pad - dec_s), F32)], axis=-1)
    c_s = _cumsum_lanes(lf_all.reshape(dec_b * h_b, past + n_pad), tri, rb=32)
    o_b_s = _forget_sample(sthree(ps["qb"]), heads_t(cache_b_k), heads_t(cache_b_v), sthree(ps["kb"]),
                           sthree(ps["vb_f"]), c_s.reshape(dec_b, h_b, past + n_pad), n_pad=n_pad)
    o_m_s = _mem_attn(sthree(ps["qm"]), cache_mem_k[l].reshape(dec_b, n_mem * h_m, dh_m),
                      cache_mem_v[l].reshape(dec_b, n_mem * h_m, dh_m), heads=h_m, tq=dec_s, bb=8)
    y_sample = _post(xs, o_a_s.reshape(-1, w_a), o_b_s.reshape(-1, w_b), o_m_s.reshape(-1, w_m), *post_w,
                     tm=256).reshape(dec_b, dec_s, d)

    lead = lambda a, *shape: a.reshape((depth,) + shape)
    frames_major = lambda a: jnp.transpose(a, (0, 3, 1, 2))[None]
    return (y_prompt, y_sample,
            frames_major(pp["ka_f"]), frames_major(pp["va_f"]),
            frames_major(pp["kb_f"]), frames_major(pp["vb_f"]),
            lf_p.transpose(0, 2, 1)[None],
            lead(mk_f, batch, n_mem, h_m, dh_m), lead(mv_f, batch, n_mem, h_m, dh_m),
            lead(ps["ka_f"], dec_b, dec_s, h_a, dh_a), lead(ps["va_f"], dec_b, dec_s, h_a, dh_a),
            lead(ps["kb_f"], dec_b, dec_s, h_b, dh_b), lead(ps["vb_f"], dec_b, dec_s, h_b, dh_b),
            lead(ps["lft"].T, dec_b, dec_s, h_b))
```

```python
import functools

import jax
import jax.numpy as jnp
from jax import lax
from jax.experimental import pallas as pl
from jax.experimental.pallas import tpu as pltpu

BF16 = jnp.bfloat16
F32 = jnp.float32

EPS = 1e-6
CHUNK = 64
REL_CLIP = 128
NEG = -1e30
LANE_GROUP = 256
LOG2E = 1.4426950408889634
V_ROWS = 64 + 16
VMEM_LIMIT = 56 * 1024 * 1024


def _cparams(*sem):
    return pltpu.CompilerParams(dimension_semantics=sem, vmem_limit_bytes=VMEM_LIMIT)


def _const_spec(shape):
    nd = len(shape)
    return pl.BlockSpec(shape, lambda *_: (0,) * nd, pipeline_mode=pl.Buffered(1))


def _nt_dot(a, b):
    return lax.dot_general(a, b, (((1,), (1,)), ((), ())), preferred_element_type=F32)


def _dot(a, b):
    return jnp.dot(a, b, preferred_element_type=F32)


def _rms_rows(x, gain):
    ms = jnp.mean(x * x, axis=-1, keepdims=True)
    return x * lax.rsqrt(ms + EPS) * gain


def _head_norm(z, bd_ref, gain):
    z2 = (z * z).astype(BF16)
    ms = jnp.concatenate(
        [_dot(z2[:, h * 256:(h + 1) * 256], bd_ref[...]) for h in range(z.shape[1] // 256)], axis=-1)
    return z * lax.rsqrt(ms + EPS) * gain


def _log_sigmoid(x):
    return jnp.minimum(x, 0.0) - jnp.log1p(jnp.exp(-jnp.abs(x)))


def _lane_group(width, head):
    return lax.broadcasted_iota(jnp.int32, (1, width), 1) // head


def _block_diag_rows(q4, head):
    grp = _lane_group(q4.shape[1], head)
    zero = jnp.zeros_like(q4)
    return jnp.concatenate([jnp.where(grp == g, q4, zero) for g in range(q4.shape[1] // head)], axis=0)


def _pick_diag(o, rows, head):
    grp = _lane_group(o.shape[1], head)
    out = jnp.zeros((rows, o.shape[1]), o.dtype)
    for g in range(o.shape[1] // head):
        out = jnp.where(grp == g, o[g * rows:(g + 1) * rows], out)
    return out


def _store_heads_t(ref, z_t):
    dh = ref.shape[1]
    for h in range(ref.shape[0]):
        ref[h] = z_t[h * dh:(h + 1) * dh, :]


def _proj_kernel(x_ref, gmix_ref, wqkv_ref, wft_ref, wqm_ref, gains_ref, gqm_ref, bf_ref,
                 bd64_ref, bd128_ref, *out_refs, out_names, scale_a, scale_b, scale_m, keep_from,
                 tiles_per_group):
    o = dict(zip(out_names, out_refs))
    head_major = "vbt" in o
    xn = _rms_rows(x_ref[...], gmix_ref[...]).astype(BF16)

    def seg(j):
        return _dot(xn, wqkv_ref[:, j * 512:(j + 1) * 512])

    kb = _head_norm(seg(4), bd64_ref, gains_ref[3:4, :])
    o["kb"][...] = kb.astype(BF16)
    vb = seg(5)
    if head_major:
        _store_heads_t(o["kb_f"], kb.T)
        vb_t = vb.T
        _store_heads_t(o["vb_f"], vb_t)
        vbt = o["vbt"]
        head = vb_t.shape[0] // vbt.shape[0]
        pad = vbt.shape[1] - head
        ones_row = (lax.broadcasted_iota(jnp.int32, (pad, vb_t.shape[1]), 0) == 0).astype(BF16)
        for h in range(vbt.shape[0]):
            vbt[h, 0:head, :] = vb_t[h * head:(h + 1) * head, :].astype(BF16)
            vbt[h, head:head + pad, :] = ones_row
    else:
        o["kb_f"][...] = kb
        o["vb_f"][...] = vb
    qb = _head_norm(seg(3), bd64_ref, gains_ref[2:3, :])
    o["qb"][...] = (qb * scale_b).astype(BF16)
    qa = _head_norm(seg(0), bd64_ref, gains_ref[0:1, :])
    o["qa"][...] = (qa * scale_a).astype(BF16)
    ka = _head_norm(seg(1), bd64_ref, gains_ref[1:2, :])
    o["ka"][...] = ka.astype(BF16)
    qm = _head_norm(_dot(xn, wqm_ref[...]), bd128_ref, gqm_ref[...])
    o["qm"][...] = (qm * scale_m).astype(BF16)
    flt = _nt_dot(wft_ref[...], xn)
    o["lft"][...] = _log_sigmoid(flt[0:8, :] + bf_ref[...])
    va = seg(2)
    o["va"][...] = va.astype(BF16)
    if head_major:
        @pl.when(pl.program_id(0) % tiles_per_group >= keep_from)
        def _():
            _store_heads_t(o["ka_f"], ka.T)
            _store_heads_t(o["va_f"], va.T)
    else:
        o["ka_f"][...] = ka
        o["va_f"][...] = va


def _proj(x, gmix, wqkv, wft, wqm, gains, gqm, bf, bd64, bd128, *, tm, group_rows=None, keep_rows=None):
    rows, d = x.shape
    n_tiles = rows // tm
    row = lambda i: (i, 0)
    wide = lambda dt: (jax.ShapeDtypeStruct((rows, 512), dt), pl.BlockSpec((tm, 512), row))
    outs = {n: wide(BF16) for n in ("qa", "ka", "va", "qb", "kb", "qm")}
    outs["lft"] = (jax.ShapeDtypeStruct((8, rows), F32), pl.BlockSpec((8, tm), lambda i: (0, i)))
    tpg = keep_from = 1
    if group_rows is None:
        for n in ("ka_f", "va_f", "kb_f", "vb_f"):
            outs[n] = wide(F32)
    else:
        assert keep_rows == tm
        tpg = group_rows // tm
        keep_from = tpg - 1
        groups = rows // group_rows
        kept = (jax.ShapeDtypeStruct((groups, 8, 64, keep_rows), F32),
                pl.BlockSpec((None, 8, 64, tm), lambda i: (i // tpg, 0, 0, 0)))
        full = (jax.ShapeDtypeStruct((groups, 8, 64, group_rows), F32),
                pl.BlockSpec((None, 8, 64, tm), lambda i: (i // tpg, 0, 0, i % tpg)))
        outs.update(ka_f=kept, va_f=kept, kb_f=full, vb_f=full)
        outs["vbt"] = (jax.ShapeDtypeStruct((n_tiles, 8, V_ROWS, tm), BF16),
                       pl.BlockSpec((None, 8, V_ROWS, tm), lambda i: (i, 0, 0, 0)))
    names = tuple(outs)
    in_specs = [pl.BlockSpec((tm, d), row)] + [_const_spec(a.shape) for a in
                                               (gmix, wqkv, wft, wqm, gains, gqm, bf, bd64, bd128)]
    kern = functools.partial(_proj_kernel, out_names=names, scale_a=LOG2E * 64 ** -0.5,
                             scale_b=LOG2E * 64 ** -0.5, scale_m=LOG2E * 128 ** -0.5,
                             keep_from=keep_from, tiles_per_group=tpg)
    res = pl.pallas_call(kern, out_shape=[outs[n][0] for n in names], grid=(n_tiles,), in_specs=in_specs,
                         out_specs=[outs[n][1] for n in names], compiler_params=_cparams("arbitrary"),
                         name="proj")(x, gmix, wqkv, wft, wqm, gains, gqm, bf, bd64, bd128)
    return dict(zip(names, res))


def _memkv_kernel(m_ref, gmem_ref, w_ref, gkm_ref, bd128_ref, mk_o, mv_o, *, heads):
    tm = m_ref.shape[0]
    xn = _rms_rows(m_ref[...], gmem_ref[...]).astype(BF16)
    half = w_ref.shape[1] // 2
    dh = half // heads
    mk = _head_norm(_dot(xn, w_ref[:, :half]), bd128_ref, gkm_ref[...])
    mv = _dot(xn, w_ref[:, half:])
    for h in range(heads):
        mk_o[pl.ds(h, tm, stride=heads), :] = mk[:, h * dh:(h + 1) * dh]
        mv_o[pl.ds(h, tm, stride=heads), :] = mv[:, h * dh:(h + 1) * dh]


def _memkv(mem, gmem, w, gkm, bd128, *, tm, heads):
    rows, d = mem.shape
    dh = w.shape[1] // 2 // heads
    row = lambda i: (i, 0)
    return pl.pallas_call(
        functools.partial(_memkv_kernel, heads=heads),
        out_shape=[jax.ShapeDtypeStruct((rows * heads, dh), F32)] * 2, grid=(rows // tm,),
        in_specs=[pl.BlockSpec((tm, d), row)] + [_const_spec(a.shape) for a in (gmem, w, gkm, bd128)],
        out_specs=[pl.BlockSpec((tm * heads, dh), row)] * 2, compiler_params=_cparams("arbitrary"),
        name="memkv")(mem, gmem, w, gkm, bd128)


def _cumsum_kernel(x_ref, tri_ref, o_ref):
    rows, length = x_ref.shape
    tri = tri_ref[...]

    def local(c):
        x = x_ref[:, c * 128:(c + 1) * 128]
        hi = x.astype(BF16)
        r1 = x - hi.astype(F32)
        mid = r1.astype(BF16)
        lo = (r1 - mid.astype(F32)).astype(BF16)
        return _dot(hi, tri) + _dot(mid, tri) + _dot(lo, tri)

    chunks = [local(c) for c in range(length // 128)]
    carry = jnp.zeros((rows, 1), F32)
    for c, cc in enumerate(chunks):
        o_ref[:, c * 128:(c + 1) * 128] = cc + carry
        carry = carry + cc[:, 127:128]


def _cumsum_lanes(x, tri, *, rb):
    rows, length = x.shape
    return pl.pallas_call(
        _cumsum_kernel, out_shape=jax.ShapeDtypeStruct((rows, length), F32), grid=(rows // rb,),
        in_specs=[pl.BlockSpec((rb, length), lambda i: (i, 0)), _const_spec(tri.shape)],
        out_specs=pl.BlockSpec((rb, length), lambda i: (i, 0)), compiler_params=_cparams("arbitrary"),
        name="cumsum")(x, tri)


def _band_kernel(q_ref, k0_ref, k1_ref, k2_ref, v0_ref, v1_ref, v2_ref, bias_ref, o_ref, k_sc, v_sc,
                 *, reach, lookahead):
    tq = q_ref.shape[0]
    band = reach + CHUNK
    i = pl.program_id(1)
    for n, (kr, vr) in enumerate(((k0_ref, v0_ref), (k1_ref, v1_ref), (k2_ref, v2_ref))):
        k_sc[n * tq:(n + 1) * tq, :] = kr[...]
        v_sc[n * tq:(n + 1) * tq, :] = vr[...]
    col = lax.broadcasted_iota(jnp.int32, (1, band), 1)
    units = [(j, hg) for j in range(tq // CHUNK) for hg in range(q_ref.shape[1] // LANE_GROUP)]

    def scores(j, hg):
        r0 = 2 * tq + j * CHUNK - reach
        lanes = slice(hg * LANE_GROUP, (hg + 1) * LANE_GROUP)
        qbd = _block_diag_rows(q_ref[j * CHUNK:(j + 1) * CHUNK, lanes], CHUNK)
        s = _nt_dot(qbd, k_sc[r0:r0 + band, lanes]) + bias_ref[hg * 4 * CHUNK:(hg + 1) * 4 * CHUNK, :]
        return jnp.where((i - 2) * tq + r0 + col >= 0, s, NEG)

    pending = [scores(*u) for u in units[:lookahead]]
    for n, (j, hg) in enumerate(units):
        s = pending.pop(0)
        if n + lookahead < len(units):
            pending.append(scores(*units[n + lookahead]))
        r0 = 2 * tq + j * CHUNK - reach
        lanes = slice(hg * LANE_GROUP, (hg + 1) * LANE_GROUP)
        m = jnp.max(s, axis=-1, keepdims=True)
        p = jnp.exp2(s - m)
        l = jnp.sum(p, axis=-1, keepdims=True)
        o = _dot(p.astype(BF16), v_sc[r0:r0 + band, lanes]) / l
        o_ref[j * CHUNK:(j + 1) * CHUNK, lanes] = _pick_diag(o, CHUNK, CHUNK).astype(o_ref.dtype)


def _band_prompt(q, k, v, bias, *, reach, tq):
    b, t, w = q.shape
    assert reach == 2 * tq and tq % CHUNK == 0
    blk = lambda off: pl.BlockSpec((None, tq, w), lambda bi, i: (bi, jnp.maximum(i - off, 0), 0))
    return pl.pallas_call(
        functools.partial(_band_kernel, reach=reach, lookahead=3),
        out_shape=jax.ShapeDtypeStruct((b, t, w), BF16), grid=(b, t // tq),
        in_specs=[blk(0), blk(2), blk(1), blk(0), blk(2), blk(1), blk(0), _const_spec(bias.shape)],
        out_specs=blk(0),
        scratch_shapes=[pltpu.VMEM((3 * tq, w), BF16), pltpu.VMEM((3 * tq, w), BF16)],
        compiler_params=_cparams("arbitrary", "arbitrary"), name="band_prompt")(q, k, k, k, v, v, v, bias)


def _forget_kernel(q_ref, k_ref, vt_ref, c_ref, o_ref, qt_sc, m_sc, acc_sc, *, head, lookahead):
    tq, width = q_ref.shape
    n_heads = width // head
    i = pl.program_id(2)
    q_t = q_ref[...].astype(F32).T
    row_grp = lax.broadcasted_iota(jnp.int32, (width, 1), 0) // head
    for g in range(n_heads):
        qt_sc[g] = jnp.where(row_grp == g, q_t, 0.0).astype(BF16)
    m_sc[...] = jnp.full(m_sc.shape, NEG, F32)
    acc_sc[...] = jnp.zeros(acc_sc.shape, F32)
    c_first = c_ref[pl.ds(pl.multiple_of(i * tq, tq), 8), :][0:1, :]

    def step(j, masked):
        k0 = pl.multiple_of(j * tq, tq)
        kt = k_ref[pl.ds(k0, tq), :]
        bias = (c_first - c_ref[pl.ds(k0, tq), :]) * LOG2E
        if masked:
            keep = (lax.broadcasted_iota(jnp.int32, (tq, tq), 0)
                    <= lax.broadcasted_iota(jnp.int32, (tq, tq), 1))

        def scores(g):
            s = _dot(kt, qt_sc[g]) + bias[:, g:g + 1]
            return jnp.where(keep, s, NEG) if masked else s

        pending = [scores(g) for g in range(min(lookahead, n_heads))]
        for g in range(n_heads):
            s = pending.pop(0)
            if g + lookahead < n_heads:
                pending.append(scores(g + lookahead))
            m_prev = m_sc[g]
            m_new = jnp.maximum(m_prev, jnp.max(s, axis=0, keepdims=True))
            p = jnp.exp2(s - m_new).astype(BF16)
            acc_sc[g] = jnp.exp2(m_prev - m_new) * acc_sc[g] + _dot(vt_ref[j, g], p)
            m_sc[g] = m_new

    def body(j, carry):
        step(j, False)
        return carry

    lax.fori_loop(0, i, body, 0)
    step(i, True)
    o_t = jnp.concatenate([acc_sc[g, 0:head, :] / acc_sc[g, head:head + 1, :] for g in range(n_heads)],
                          axis=0)
    o_ref[...] = o_t.T.astype(o_ref.dtype)


def _forget_prompt(q, k, vt, c, *, head, tq):
    b, t, w = q.shape
    n_heads = LANE_GROUP // head
    v_rows = vt.shape[3]
    lookahead = 3
    assert lookahead <= n_heads
    return pl.pallas_call(
        functools.partial(_forget_kernel, head=head, lookahead=lookahead),
        out_shape=jax.ShapeDtypeStruct((b, t, w), BF16), grid=(b, w // LANE_GROUP, t // tq),
        in_specs=[pl.BlockSpec((None, tq, LANE_GROUP), lambda bi, hg, i: (bi, i, hg)),
                  pl.BlockSpec((None, t, LANE_GROUP), lambda bi, hg, i: (bi, 0, hg)),
                  pl.BlockSpec((None, t // tq, n_heads, v_rows, tq), lambda bi, hg, i: (bi, 0, hg, 0, 0)),
                  pl.BlockSpec((None, None, t, n_heads), lambda bi, hg, i: (bi, hg, 0, 0))],
        out_specs=pl.BlockSpec((None, tq, LANE_GROUP), lambda bi, hg, i: (bi, i, hg)),
        scratch_shapes=[pltpu.VMEM((n_heads, LANE_GROUP, tq), BF16), pltpu.VMEM((n_heads, 1, tq), F32),
                        pltpu.VMEM((n_heads, v_rows, tq), F32)],
        compiler_params=_cparams("arbitrary", "arbitrary", "arbitrary"), name="forget_prompt")(q, k, vt, c)


def _mem_attention(qs, mk_refs, mv_refs, heads):
    n_mem = mk_refs[0].shape[0] // heads
    dh = mk_refs[0].shape[1]
    units = [(e, h) for e in range(len(qs)) for h in range(heads)]
    rows = lambda ref, h: ref[pl.ds(h, n_mem, stride=heads), :].astype(BF16)
    s = [_nt_dot(qs[e][:, h * dh:(h + 1) * dh], rows(mk_refs[e], h)) for e, h in units]
    p, inv_l = [], []
    for u in range(len(units)):
        ex = jnp.exp2(s[u] - jnp.max(s[u], axis=-1, keepdims=True))
        inv_l.append(1.0 / jnp.sum(ex, axis=-1, keepdims=True))
        p.append(ex.astype(BF16))
    o = [(_dot(p[u], rows(mv_refs[e], h)) * inv_l[u]).astype(BF16) for u, (e, h) in enumerate(units)]
    return [jnp.concatenate(o[e * heads:(e + 1) * heads], axis=-1) for e in range(len(qs))]


def _mem_kernel(q_ref, mk_ref, mv_ref, o_ref, *, heads):
    bb = q_ref.shape[0]
    outs = _mem_attention([q_ref[e] for e in range(bb)], [mk_ref.at[e] for e in range(bb)],
                          [mv_ref.at[e] for e in range(bb)], heads)
    for e in range(bb):
        o_ref[e] = outs[e]


def _mem_attn(q, mk, mv, *, heads, tq, bb):
    b, t, w = q.shape
    kv = pl.BlockSpec((bb,) + mk.shape[1:], lambda bi, i: (bi, 0, 0))
    qs = pl.BlockSpec((bb, tq, w), lambda bi, i: (bi, i, 0))
    return pl.pallas_call(
        functools.partial(_mem_kernel, heads=heads), out_shape=jax.ShapeDtypeStruct((b, t, w), BF16),
        grid=(b // bb, t // tq), in_specs=[qs, kv, kv], out_specs=qs,
        compiler_params=_cparams("arbitrary", "arbitrary"), name="mem_attn")(q, mk, mv)


def _post_kernel(x_ref, oa_ref, ob_ref, *refs, mem_heads, col_chunk, ff_chunk):
    if mem_heads:
        qm_ref, mk_ref, mv_ref, *refs = refs
        o_m, = _mem_attention([qm_ref[...]], [mk_ref], [mv_ref], mem_heads)
    else:
        om_ref, *refs = refs
        o_m = om_ref[...]
    gmix_ref, wg_ref, wpa_ref, wpb_ref, wpm_ref, wo_ref, gffn_ref, wup_ref, wdn_ref, y_ref = refs
    x = x_ref[...]
    d = x.shape[1]
    xn = _rms_rows(x, gmix_ref[...]).astype(BF16)
    branches = ((oa_ref[...], wpa_ref), (ob_ref[...], wpb_ref), (o_m, wpm_ref))
    h = []
    for c in range(d // col_chunk):
        cols = slice(c * col_chunk, (c + 1) * col_chunk)
        hc = 0.0
        for b, (o_b, wp_ref) in enumerate(branches):
            gate = jax.nn.sigmoid(_dot(xn, wg_ref[:, b * d + c * col_chunk:b * d + (c + 1) * col_chunk]))
            hc = hc + gate * _dot(o_b, wp_ref[:, cols])
        h.append(hc.astype(BF16))
    x1 = x + _dot(jnp.concatenate(h, axis=-1), wo_ref[...])
    xn1 = _rms_rows(x1, gffn_ref[...]).astype(BF16)
    y = x1
    for c in range(wup_ref.shape[1] // ff_chunk):
        cols = slice(c * ff_chunk, (c + 1) * ff_chunk)
        u = jnp.maximum(_dot(xn1, wup_ref[:, cols]), 0.0)
        y = y + _dot((u * u).astype(BF16), wdn_ref[cols, :])
    y_ref[...] = y


def _post(x, oa, ob, om, gmix, wg, wpa, wpb, wpm, wo, gffn, wup, wdn, *, tm, mem=None):
    rows, d = x.shape
    row = lambda i: (i, 0)
    acts = [x, oa, ob]
    specs = [pl.BlockSpec((tm, a.shape[1]), row) for a in acts]
    heads = 0
    if om is None:
        qm, mk, mv, heads = mem
        tiles_per_batch = rows // mk.shape[0] // tm
        kv = pl.BlockSpec((None,) + mk.shape[1:], lambda i: (i // tiles_per_batch, 0, 0))
        acts += [qm, mk, mv]
        specs += [pl.BlockSpec((tm, qm.shape[1]), row), kv, kv]
    else:
        acts.append(om)
        specs.append(pl.BlockSpec((tm, om.shape[1]), row))
    weights = (gmix, wg, wpa, wpb, wpm, wo, gffn, wup, wdn)
    return pl.pallas_call(
        functools.partial(_post_kernel, mem_heads=heads, col_chunk=512, ff_chunk=1024),
        out_shape=jax.ShapeDtypeStruct((rows, d), F32), grid=(rows // tm,),
        in_specs=specs + [_const_spec(a.shape) for a in weights],
        out_specs=pl.BlockSpec((tm, d), row), compiler_params=_cparams("arbitrary"),
        name="post")(*acts, *weights)


def _pad_rows(x, rows):
    return jnp.concatenate([x, jnp.zeros((rows - x.shape[0], x.shape[1]), x.dtype)], axis=0)


def _cached_attention(q_ref, kc_ref, vc_ref, kn_ref, vn_ref, o_ref, bias_c, bias_n, n_pad):
    bb, heads, head, _ = kc_ref.shape
    units = [(e, h, slice(h * head, (h + 1) * head)) for e in range(bb) for h in range(heads)]
    sc = [_dot(q_ref[e, :, ln], kc_ref[e, h].astype(BF16)) + bias_c(e, h) for e, h, ln in units]
    sn = [_nt_dot(q_ref[e, :, ln], _pad_rows(kn_ref[e, :, ln].astype(BF16), n_pad)) + bias_n(e, h)
          for e, h, ln in units]
    pc, pn, inv_l = [], [], []
    for u in range(len(units)):
        m = jnp.maximum(jnp.max(sc[u], axis=-1, keepdims=True), jnp.max(sn[u], axis=-1, keepdims=True))
        pc.append(jnp.exp2(sc[u] - m))
        pn.append(jnp.exp2(sn[u] - m))
        inv_l.append(1.0 / (jnp.sum(pc[u], axis=-1, keepdims=True) + jnp.sum(pn[u], axis=-1, keepdims=True)))
    for u, (e, h, ln) in enumerate(units):
        o = (_nt_dot(pc[u].astype(BF16), vc_ref[e, h].astype(BF16))
             + _dot(pn[u].astype(BF16), _pad_rows(vn_ref[e, :, ln].astype(BF16), n_pad)))
        o_ref[e, :, ln] = (o * inv_l[u]).astype(o_ref.dtype)


def _band_sample_kernel(q_ref, kc_ref, vc_ref, kn_ref, vn_ref, bc_ref, bn_ref, o_ref):
    s_len = q_ref.shape[1]
    rows = lambda h: slice(h * s_len, (h + 1) * s_len)
    _cached_attention(q_ref, kc_ref, vc_ref, kn_ref, vn_ref, o_ref,
                      lambda e, h: bc_ref[rows(h), :], lambda e, h: bn_ref[rows(h), :], bn_ref.shape[1])


def _band_sample(q, kct, vct, kn, vn, bias_c, bias_n, *, bb):
    b, s_len, w = q.shape
    new = pl.BlockSpec((bb, s_len, w), lambda bi: (bi, 0, 0))
    cache = pl.BlockSpec((bb,) + kct.shape[1:], lambda bi: (bi, 0, 0, 0))
    return pl.pallas_call(
        _band_sample_kernel, out_shape=jax.ShapeDtypeStruct((b, s_len, w), BF16), grid=(b // bb,),
        in_specs=[new, cache, cache, new, new, _const_spec(bias_c.shape), _const_spec(bias_n.shape)],
        out_specs=new, compiler_params=_cparams("arbitrary"), name="band_sample")(
            q, kct, vct, kn, vn, bias_c, bias_n)


def _forget_sample_kernel(q_ref, kc_ref, vc_ref, kn_ref, vn_ref, c_ref, o_ref, *, n_pad):
    s_len = q_ref.shape[1]
    p_len = kc_ref.shape[3]
    causal = (lax.broadcasted_iota(jnp.int32, (s_len, n_pad), 1)
              <= lax.broadcasted_iota(jnp.int32, (s_len, n_pad), 0))
    ch = [(c_ref[e, :, p_len - 1:p_len] - c_ref[e]) * LOG2E for e in range(c_ref.shape[0])]
    _cached_attention(q_ref, kc_ref, vc_ref, kn_ref, vn_ref, o_ref,
                      lambda e, h: ch[e][h:h + 1, 0:p_len],
                      lambda e, h: jnp.where(causal, ch[e][h:h + 1, p_len:p_len + n_pad], NEG), n_pad)


def _forget_sample(q, kct, vct, kn, vn, c, *, n_pad):
    b, s_len, w = q.shape
    new = pl.BlockSpec((1, s_len, w), lambda bi: (bi, 0, 0))
    cache = pl.BlockSpec((1,) + kct.shape[1:], lambda bi: (bi, 0, 0, 0))
    cs = pl.BlockSpec((1,) + c.shape[1:], lambda bi: (bi, 0, 0))
    return pl.pallas_call(
        functools.partial(_forget_sample_kernel, n_pad=n_pad),
        out_shape=jax.ShapeDtypeStruct((b, s_len, w), BF16), grid=(b,),
        in_specs=[new, cache, cache, new, new, cs], out_specs=new,
        compiler_params=_cparams("arbitrary"), name="forget_sample")(q, kct, vct, kn, vn, c)


def _block_diag_mean(head, size=256):
    r = jnp.arange(size) // head
    return jnp.where(r[:, None] == r[None, :], 1.0 / head, 0.0).astype(BF16)


def _rel_bias_rows(rel_bias, d0, n_q, n_k):
    n = n_q + n_k - 1
    dist = d0 + (n_q - 1) - jnp.arange(n)
    e = rel_bias[jnp.clip(dist, -REL_CLIP, REL_CLIP) + REL_CLIP].T.astype(F32)
    h = e.shape[0]
    skew = jnp.tile(jnp.pad(e, ((0, 0), (0, 1))), (1, n_q))[:, :n_q * n].reshape(h, n_q, n)
    return skew[:, :, n_q - 1:].reshape(h * n_q, n_k)


def kernel(x_prompt, x_sample, mem_prompt, cache_a_k, cache_a_v, cache_b_k, cache_b_v, cache_b_logf,
           cache_mem_k, cache_mem_v, g_mix, w_in, b_f, g_qa, g_ka, g_qb, g_kb, g_qm, g_km, rel_bias,
           g_mem, w_mkv, w_pa, w_pb, w_pm, w_o, g_ffn, w_up, w_down):
    depth = w_in.shape[0]
    assert depth == 1
    batch, seq, d = x_prompt.shape
    dec_b, dec_s, _ = x_sample.shape
    _, _, n_cache, h_a, dh_a = cache_a_k.shape
    _, _, past, h_b, dh_b = cache_b_k.shape
    _, _, n_mem, h_m, dh_m = cache_mem_k.shape
    w_a, w_b, w_m = h_a * dh_a, h_b * dh_b, h_m * dh_m
    assert w_a == w_b == w_m == 512 and dh_a == dh_b == 64 and dh_m == 128 and h_b == 8
    keep_p = min(n_cache, seq)
    n_pad = 128
    l = 0

    qkv_end = 3 * w_a + 3 * w_b
    f_end = qkv_end + h_b
    qm_end = f_end + w_m
    w = w_in[l]
    wqkv = w[:, :qkv_end].astype(BF16)
    wft = jnp.zeros((16, d), BF16).at[:h_b].set(w[:, qkv_end:f_end].T.astype(BF16))
    wqm = w[:, f_end:qm_end].astype(BF16)
    wg = w[:, qm_end:].astype(BF16)
    gains = jnp.stack([jnp.tile(g[l], h_a) for g in (g_qa, g_ka, g_qb, g_kb)]).astype(F32)
    gqm = jnp.tile(g_qm[l], h_m)[None, :]
    gkm = jnp.tile(g_km[l], h_m)[None, :]
    bf = b_f[l][:, None].astype(F32)
    bd64 = _block_diag_mean(dh_a)
    bd128 = _block_diag_mean(dh_m)
    tri = (jnp.arange(128)[:, None] <= jnp.arange(128)[None, :]).astype(BF16)
    gmix = g_mix[l][None, :]
    proj_w = (gmix, wqkv, wft, wqm, gains, gqm, bf, bd64, bd128)
    wpa, wpb, wpm, wo = (a[l].astype(BF16) for a in (w_pa, w_pb, w_pm, w_o))
    wup, wdn = w_up[l].astype(BF16), w_down[l].astype(BF16)
    gffn = g_ffn[l][None, :]

    xp = x_prompt.reshape(batch * seq, d)
    pp = _proj(xp, *proj_w, tm=512, group_rows=seq, keep_rows=keep_p)
    mk_f, mv_f = _memkv(mem_prompt.reshape(batch * n_mem, d), g_mem[l][None, :], w_mkv[l].astype(BF16),
                        gkm, bd128, tm=256, heads=h_m)
    tq_b = 512
    lf_p = pp["lft"].reshape(h_b, batch, seq).transpose(1, 0, 2)
    c_p = _cumsum_lanes(lf_p.reshape(batch * h_b, seq), tri, rb=batch * h_b)
    c_p = c_p.reshape(batch, h_b // 4, 4, seq).transpose(0, 1, 3, 2)
    vt = pp["vbt"].reshape(batch, seq // tq_b, h_b, V_ROWS, tq_b)
    three = lambda a: a.reshape(batch, seq, -1)
    bias_p = _rel_bias_rows(rel_bias[l], n_cache, CHUNK, n_cache + CHUNK) * LOG2E
    o_a = _band_prompt(three(pp["qa"]), three(pp["ka"]), three(pp["va"]), bias_p, reach=n_cache, tq=256)
    o_b = _forget_prompt(three(pp["qb"]), three(pp["kb"]), vt, c_p, head=dh_b, tq=tq_b)
    post_w = (gmix, wg, wpa, wpb, wpm, wo, gffn, wup, wdn)
    mem_p = (pp["qm"], mk_f.reshape(batch, n_mem * h_m, dh_m), mv_f.reshape(batch, n_mem * h_m, dh_m), h_m)
    y_prompt = _post(xp, o_a.reshape(-1, w_a), o_b.reshape(-1, w_b), None, *post_w, tm=512,
                     mem=mem_p).reshape(batch, seq, d)

    xs = x_sample.reshape(dec_b * dec_s, d)
    ps = _proj(xs, *proj_w, tm=512)
    sthree = lambda a: a.reshape(dec_b, dec_s, -1)
    heads_t = lambda a: jnp.transpose(a[l], (0, 2, 3, 1))
    bias_c = _rel_bias_rows(rel_bias[l], n_cache, dec_s, n_cache) * LOG2E
    bias_n = jnp.full((h_a * dec_s, n_pad), NEG, F32).at[:, :dec_s].set(
        _rel_bias_rows(rel_bias[l], 0, dec_s, dec_s) * LOG2E)
    o_a_s = _band_sample(sthree(ps["qa"]), heads_t(cache_a_k), heads_t(cache_a_v), sthree(ps["ka"]),
                         sthree(ps["va"]), bias_c, bias_n, bb=4)
    lf_new = ps["lft"].reshape(h_b, dec_b, dec_s).transpose(1, 0, 2)
    lf_all = jnp.concatenate([cache_b_logf[l].astype(F32).transpose(0, 2, 1), lf_new,
                              jnp.zeros((dec_b, h_b, n_pad - dec_s), F32)], axis=-1)
    c_s = _cumsum_lanes(lf_all.reshape(dec_b * h_b, past + n_pad), tri, rb=32)
    o_b_s = _forget_sample(sthree(ps["qb"]), heads_t(cache_b_k), heads_t(cache_b_v), sthree(ps["kb"]),
                           sthree(ps["vb_f"]), c_s.reshape(dec_b, h_b, past + n_pad), n_pad=n_pad)
    o_m_s = _mem_attn(sthree(ps["qm"]), cache_mem_k[l].reshape(dec_b, n_mem * h_m, dh_m),
                      cache_mem_v[l].reshape(dec_b, n_mem * h_m, dh_m), heads=h_m, tq=dec_s, bb=8)
    y_sample = _post(xs, o_a_s.reshape(-1, w_a), o_b_s.reshape(-1, w_b), o_m_s.reshape(-1, w_m), *post_w,
                     tm=512).reshape(dec_b, dec_s, d)

    lead = lambda a, *shape: a.reshape((depth,) + shape)
    frames_major = lambda a: jnp.transpose(a, (0, 3, 1, 2))[None]
    return (y_prompt, y_sample,
            frames_major(pp["ka_f"]), frames_major(pp["va_f"]),
            frames_major(pp["kb_f"]), frames_major(pp["vb_f"]),
            lf_p.transpose(0, 2, 1)[None],
            lead(mk_f, batch, n_mem, h_m, dh_m), lead(mv_f, batch, n_mem, h_m, dh_m),
            lead(ps["ka_f"], dec_b, dec_s, h_a, dh_a), lead(ps["va_f"], dec_b, dec_s, h_a, dh_a),
            lead(ps["kb_f"], dec_b, dec_s, h_b, dh_b), lead(ps["vb_f"], dec_b, dec_s, h_b, dh_b),
            lead(ps["lft"].T, dec_b, dec_s, h_b))
```

```python
import functools

import jax
import jax.numpy as jnp
from jax import lax
from jax.experimental import pallas as pl
from jax.experimental.pallas import tpu as pltpu

BF16 = jnp.bfloat16
F32 = jnp.float32

EPS = 1e-6
CHUNK = 64
REL_CLIP = 128
NEG = -1e30
LANE_GROUP = 256
LOG2E = 1.4426950408889634
V_ROWS = 64 + 16
MAX_ONE_PASS_BOUND = 40.0
VMEM_LIMIT = 56 * 1024 * 1024


def _cparams(*sem):
    return pltpu.CompilerParams(dimension_semantics=sem, vmem_limit_bytes=VMEM_LIMIT)


def _const_spec(shape):
    nd = len(shape)
    return pl.BlockSpec(shape, lambda *_: (0,) * nd, pipeline_mode=pl.Buffered(1))


def _nt_dot(a, b):
    return lax.dot_general(a, b, (((1,), (1,)), ((), ())), preferred_element_type=F32)


def _dot(a, b):
    return jnp.dot(a, b, preferred_element_type=F32)


def _rms_rows(x, gain):
    ms = jnp.mean(x * x, axis=-1, keepdims=True)
    return x * lax.rsqrt(ms + EPS) * gain


def _head_norm(z, bd_ref, gain):
    z2 = (z * z).astype(BF16)
    ms = jnp.concatenate(
        [_dot(z2[:, h * 256:(h + 1) * 256], bd_ref[...]) for h in range(z.shape[1] // 256)], axis=-1)
    return z * lax.rsqrt(ms + EPS) * gain


def _log_sigmoid(x):
    return jnp.minimum(x, 0.0) - jnp.log1p(jnp.exp(-jnp.abs(x)))


def _lane_group(width, head):
    return lax.broadcasted_iota(jnp.int32, (1, width), 1) // head


def _block_diag_rows(q4, head):
    grp = _lane_group(q4.shape[1], head)
    zero = jnp.zeros_like(q4)
    return jnp.concatenate([jnp.where(grp == g, q4, zero) for g in range(q4.shape[1] // head)], axis=0)


def _pick_diag(o, rows, head):
    grp = _lane_group(o.shape[1], head)
    out = jnp.zeros((rows, o.shape[1]), o.dtype)
    for g in range(o.shape[1] // head):
        out = jnp.where(grp == g, o[g * rows:(g + 1) * rows], out)
    return out


def _store_heads_t(ref, z_t):
    dh = ref.shape[1]
    for h in range(ref.shape[0]):
        ref[h] = z_t[h * dh:(h + 1) * dh, :]


def _proj_kernel(x_ref, gmix_ref, wqkv_ref, wft_ref, wqm_ref, gains_ref, gqm_ref, bf_ref,
                 bd64_ref, bd128_ref, *out_refs, out_names, scale_a, scale_b, scale_m, keep_from,
                 tiles_per_group):
    o = dict(zip(out_names, out_refs))
    head_major = "vbt" in o
    xn = _rms_rows(x_ref[...], gmix_ref[...]).astype(BF16)

    def seg(j):
        return _dot(xn, wqkv_ref[:, j * 512:(j + 1) * 512])

    kb = _head_norm(seg(4), bd64_ref, gains_ref[3:4, :])
    o["kb"][...] = kb.astype(BF16)
    vb = seg(5)
    if head_major:
        _store_heads_t(o["kb_f"], kb.T)
        vb_t = vb.T
        _store_heads_t(o["vb_f"], vb_t)
        vbt = o["vbt"]
        head = vb_t.shape[0] // vbt.shape[0]
        pad = vbt.shape[1] - head
        ones_row = (lax.broadcasted_iota(jnp.int32, (pad, vb_t.shape[1]), 0) == 0).astype(BF16)
        for h in range(vbt.shape[0]):
            vbt[h, 0:head, :] = vb_t[h * head:(h + 1) * head, :].astype(BF16)
            vbt[h, head:head + pad, :] = ones_row
    else:
        o["kb_f"][...] = kb
        o["vb_f"][...] = vb
    qb = _head_norm(seg(3), bd64_ref, gains_ref[2:3, :])
    o["qb"][...] = (qb * scale_b).astype(BF16)
    qa = _head_norm(seg(0), bd64_ref, gains_ref[0:1, :])
    o["qa"][...] = (qa * scale_a).astype(BF16)
    ka = _head_norm(seg(1), bd64_ref, gains_ref[1:2, :])
    o["ka"][...] = ka.astype(BF16)
    qm = _head_norm(_dot(xn, wqm_ref[...]), bd128_ref, gqm_ref[...])
    o["qm"][...] = (qm * scale_m).astype(BF16)
    flt = _nt_dot(wft_ref[...], xn)
    o["lft"][...] = _log_sigmoid(flt[0:8, :] + bf_ref[...])
    va = seg(2)
    o["va"][...] = va.astype(BF16)
    if head_major:
        @pl.when(pl.program_id(0) % tiles_per_group >= keep_from)
        def _():
            _store_heads_t(o["ka_f"], ka.T)
            _store_heads_t(o["va_f"], va.T)
    else:
        o["ka_f"][...] = ka
        o["va_f"][...] = va


def _proj(x, gmix, wqkv, wft, wqm, gains, gqm, bf, bd64, bd128, *, tm, group_rows=None, keep_rows=None):
    rows, d = x.shape
    n_tiles = rows // tm
    row = lambda i: (i, 0)
    wide = lambda dt: (jax.ShapeDtypeStruct((rows, 512), dt), pl.BlockSpec((tm, 512), row))
    outs = {n: wide(BF16) for n in ("qa", "ka", "va", "qb", "kb", "qm")}
    outs["lft"] = (jax.ShapeDtypeStruct((8, rows), F32), pl.BlockSpec((8, tm), lambda i: (0, i)))
    tpg = keep_from = 1
    if group_rows is None:
        for n in ("ka_f", "va_f", "kb_f", "vb_f"):
            outs[n] = wide(F32)
    else:
        assert keep_rows == tm
        tpg = group_rows // tm
        keep_from = tpg - 1
        groups = rows // group_rows
        kept = (jax.ShapeDtypeStruct((groups, 8, 64, keep_rows), F32),
                pl.BlockSpec((None, 8, 64, tm), lambda i: (i // tpg, 0, 0, 0)))
        full = (jax.ShapeDtypeStruct((groups, 8, 64, group_rows), F32),
                pl.BlockSpec((None, 8, 64, tm), lambda i: (i // tpg, 0, 0, i % tpg)))
        outs.update(ka_f=kept, va_f=kept, kb_f=full, vb_f=full)
        outs["vbt"] = (jax.ShapeDtypeStruct((n_tiles, 8, V_ROWS, tm), BF16),
                       pl.BlockSpec((None, 8, V_ROWS, tm), lambda i: (i, 0, 0, 0)))
    names = tuple(outs)
    in_specs = [pl.BlockSpec((tm, d), row)] + [_const_spec(a.shape) for a in
                                               (gmix, wqkv, wft, wqm, gains, gqm, bf, bd64, bd128)]
    kern = functools.partial(_proj_kernel, out_names=names, scale_a=LOG2E * 64 ** -0.5,
                             scale_b=LOG2E * 64 ** -0.5, scale_m=LOG2E * 128 ** -0.5,
                             keep_from=keep_from, tiles_per_group=tpg)
    res = pl.pallas_call(kern, out_shape=[outs[n][0] for n in names], grid=(n_tiles,), in_specs=in_specs,
                         out_specs=[outs[n][1] for n in names], compiler_params=_cparams("arbitrary"),
                         name="proj")(x, gmix, wqkv, wft, wqm, gains, gqm, bf, bd64, bd128)
    return dict(zip(names, res))


def _memkv_kernel(m_ref, gmem_ref, w_ref, gkm_ref, bd128_ref, mk_o, mv_o, *, heads):
    tm = m_ref.shape[0]
    xn = _rms_rows(m_ref[...], gmem_ref[...]).astype(BF16)
    half = w_ref.shape[1] // 2
    dh = half // heads
    mk = _head_norm(_dot(xn, w_ref[:, :half]), bd128_ref, gkm_ref[...])
    mv = _dot(xn, w_ref[:, half:])
    for h in range(heads):
        mk_o[pl.ds(h, tm, stride=heads), :] = mk[:, h * dh:(h + 1) * dh]
        mv_o[pl.ds(h, tm, stride=heads), :] = mv[:, h * dh:(h + 1) * dh]


def _memkv(mem, gmem, w, gkm, bd128, *, tm, heads):
    rows, d = mem.shape
    dh = w.shape[1] // 2 // heads
    row = lambda i: (i, 0)
    return pl.pallas_call(
        functools.partial(_memkv_kernel, heads=heads),
        out_shape=[jax.ShapeDtypeStruct((rows * heads, dh), F32)] * 2, grid=(rows // tm,),
        in_specs=[pl.BlockSpec((tm, d), row)] + [_const_spec(a.shape) for a in (gmem, w, gkm, bd128)],
        out_specs=[pl.BlockSpec((tm * heads, dh), row)] * 2, compiler_params=_cparams("arbitrary"),
        name="memkv")(mem, gmem, w, gkm, bd128)


def _cumsum_kernel(x_ref, tri_ref, o_ref):
    rows, length = x_ref.shape
    tri = tri_ref[...]

    def local(c):
        x = x_ref[:, c * 128:(c + 1) * 128]
        hi = x.astype(BF16)
        r1 = x - hi.astype(F32)
        mid = r1.astype(BF16)
        lo = (r1 - mid.astype(F32)).astype(BF16)
        return _dot(hi, tri) + _dot(mid, tri) + _dot(lo, tri)

    chunks = [local(c) for c in range(length // 128)]
    carry = jnp.zeros((rows, 1), F32)
    for c, cc in enumerate(chunks):
        o_ref[:, c * 128:(c + 1) * 128] = cc + carry
        carry = carry + cc[:, 127:128]


def _cumsum_lanes(x, tri, *, rb):
    rows, length = x.shape
    return pl.pallas_call(
        _cumsum_kernel, out_shape=jax.ShapeDtypeStruct((rows, length), F32), grid=(rows // rb,),
        in_specs=[pl.BlockSpec((rb, length), lambda i: (i, 0)), _const_spec(tri.shape)],
        out_specs=pl.BlockSpec((rb, length), lambda i: (i, 0)), compiler_params=_cparams("arbitrary"),
        name="cumsum")(x, tri)


def _band_kernel(q_ref, k0_ref, k1_ref, k2_ref, v0_ref, v1_ref, v2_ref, bias_ref, o_ref, k_sc, v_sc,
                 *, reach, lookahead):
    tq = q_ref.shape[0]
    band = reach + CHUNK
    i = pl.program_id(1)
    for n, (kr, vr) in enumerate(((k0_ref, v0_ref), (k1_ref, v1_ref), (k2_ref, v2_ref))):
        k_sc[n * tq:(n + 1) * tq, :] = kr[...]
        v_sc[n * tq:(n + 1) * tq, :] = vr[...]
    col = lax.broadcasted_iota(jnp.int32, (1, band), 1)
    units = [(j, hg) for j in range(tq // CHUNK) for hg in range(q_ref.shape[1] // LANE_GROUP)]

    def scores(j, hg):
        r0 = 2 * tq + j * CHUNK - reach
        lanes = slice(hg * LANE_GROUP, (hg + 1) * LANE_GROUP)
        qbd = _block_diag_rows(q_ref[j * CHUNK:(j + 1) * CHUNK, lanes], CHUNK)
        s = _nt_dot(qbd, k_sc[r0:r0 + band, lanes]) + bias_ref[hg * 4 * CHUNK:(hg + 1) * 4 * CHUNK, :]
        return jnp.where((i - 2) * tq + r0 + col >= 0, s, NEG)

    pending = [scores(*u) for u in units[:lookahead]]
    for n, (j, hg) in enumerate(units):
        s = pending.pop(0)
        if n + lookahead < len(units):
            pending.append(scores(*units[n + lookahead]))
        r0 = 2 * tq + j * CHUNK - reach
        lanes = slice(hg * LANE_GROUP, (hg + 1) * LANE_GROUP)
        m = jnp.max(s, axis=-1, keepdims=True)
        p = jnp.exp2(s - m)
        l = jnp.sum(p, axis=-1, keepdims=True)
        o = _dot(p.astype(BF16), v_sc[r0:r0 + band, lanes]) / l
        o_ref[j * CHUNK:(j + 1) * CHUNK, lanes] = _pick_diag(o, CHUNK, CHUNK).astype(o_ref.dtype)


def _band_prompt(q, k, v, bias, *, reach, tq):
    b, t, w = q.shape
    assert reach == 2 * tq and tq % CHUNK == 0
    blk = lambda off: pl.BlockSpec((None, tq, w), lambda bi, i: (bi, jnp.maximum(i - off, 0), 0))
    return pl.pallas_call(
        functools.partial(_band_kernel, reach=reach, lookahead=2),
        out_shape=jax.ShapeDtypeStruct((b, t, w), BF16), grid=(b, t // tq),
        in_specs=[blk(0), blk(2), blk(1), blk(0), blk(2), blk(1), blk(0), _const_spec(bias.shape)],
        out_specs=blk(0),
        scratch_shapes=[pltpu.VMEM((3 * tq, w), BF16), pltpu.VMEM((3 * tq, w), BF16)],
        compiler_params=_cparams("arbitrary", "arbitrary"), name="band_prompt")(q, k, k, k, v, v, v, bias)


def _forget_kernel(bound_ref, q_ref, k_ref, vt_ref, c_ref, o_ref, qt_sc, m_sc, acc_sc, *, head, lookahead):
    tq, width = q_ref.shape
    n_heads = width // head
    i = pl.program_id(2)
    logit_bound = bound_ref[0]
    q_t = q_ref[...].astype(F32).T
    row_grp = lax.broadcasted_iota(jnp.int32, (width, 1), 0) // head
    for g in range(n_heads):
        qt_sc[g] = jnp.where(row_grp == g, q_t, 0.0).astype(BF16)
    m_sc[...] = jnp.full(m_sc.shape, NEG, F32)
    acc_sc[...] = jnp.zeros(acc_sc.shape, F32)
    c_first = c_ref[pl.ds(pl.multiple_of(i * tq, tq), 8), :][0:1, :]

    def step(j, masked):
        k0 = pl.multiple_of(j * tq, tq)
        kt = k_ref[pl.ds(k0, tq), :]
        bias = (c_first - c_ref[pl.ds(k0, tq), :]) * LOG2E
        if masked:
            keep = (lax.broadcasted_iota(jnp.int32, (tq, tq), 0)
                    <= lax.broadcasted_iota(jnp.int32, (tq, tq), 1))

        def scores(g):
            s = _dot(kt, qt_sc[g]) + bias[:, g:g + 1]
            return jnp.where(keep, s, NEG) if masked else s

        pending = [scores(g) for g in range(min(lookahead, n_heads))]
        for g in range(n_heads):
            s = pending.pop(0)
            if g + lookahead < n_heads:
                pending.append(scores(g + lookahead))
            m_prev = m_sc[g]
            m_new = jnp.maximum(m_prev, jnp.max(s, axis=0, keepdims=True))
            p = jnp.exp2(s - m_new).astype(BF16)
            acc_sc[g] = jnp.exp2(m_prev - m_new) * acc_sc[g] + _dot(vt_ref[j, g], p)
            m_sc[g] = m_new

    def bounded_step(j):
        k0 = pl.multiple_of(j * tq, tq)
        kt = k_ref[pl.ds(k0, tq), :]
        c_tile = c_ref[pl.ds(k0, tq), :]
        c_last = c_tile[tq - 1:tq, :]
        m_tile = logit_bound + (c_first - c_last) * LOG2E
        shift = (c_last - c_tile) * LOG2E - logit_bound
        pending = [_dot(kt, qt_sc[g]) + shift[:, g:g + 1] for g in range(min(lookahead, n_heads))]
        for g in range(n_heads):
            s = pending.pop(0)
            if g + lookahead < n_heads:
                pending.append(_dot(kt, qt_sc[g + lookahead]) + shift[:, g + lookahead:g + lookahead + 1])
            m_new = jnp.broadcast_to(m_tile[:, g:g + 1], (1, tq))
            acc_sc[g] = jnp.exp2(m_sc[g] - m_new) * acc_sc[g] + _dot(vt_ref[j, g], jnp.exp2(s).astype(BF16))
            m_sc[g] = m_new

    def loop(one_step):
        def body(j, carry):
            one_step(j)
            return carry
        lax.fori_loop(0, i, body, 0)

    lax.cond(logit_bound <= MAX_ONE_PASS_BOUND, lambda: loop(bounded_step),
             lambda: loop(lambda j: step(j, False)))
    step(i, True)
    o_t = jnp.concatenate([acc_sc[g, 0:head, :] / acc_sc[g, head:head + 1, :] for g in range(n_heads)],
                          axis=0)
    o_ref[...] = o_t.T.astype(o_ref.dtype)


def _forget_prompt(logit_bound, q, k, vt, c, *, head, tq):
    b, t, w = q.shape
    n_heads = LANE_GROUP // head
    v_rows = vt.shape[3]
    lookahead = 3
    assert lookahead <= n_heads
    return pl.pallas_call(
        functools.partial(_forget_kernel, head=head, lookahead=lookahead),
        out_shape=jax.ShapeDtypeStruct((b, t, w), BF16), grid=(b, w // LANE_GROUP, t // tq),
        in_specs=[pl.BlockSpec(memory_space=pltpu.SMEM),
                  pl.BlockSpec((None, tq, LANE_GROUP), lambda bi, hg, i: (bi, i, hg)),
                  pl.BlockSpec((None, t, LANE_GROUP), lambda bi, hg, i: (bi, 0, hg)),
                  pl.BlockSpec((None, t // tq, n_heads, v_rows, tq), lambda bi, hg, i: (bi, 0, hg, 0, 0)),
                  pl.BlockSpec((None, None, t, n_heads), lambda bi, hg, i: (bi, hg, 0, 0))],
        out_specs=pl.BlockSpec((None, tq, LANE_GROUP), lambda bi, hg, i: (bi, i, hg)),
        scratch_shapes=[pltpu.VMEM((n_heads, LANE_GROUP, tq), BF16), pltpu.VMEM((n_heads, 1, tq), F32),
                        pltpu.VMEM((n_heads, v_rows, tq), F32)],
        compiler_params=_cparams("arbitrary", "arbitrary", "arbitrary"), name="forget_prompt")(
            logit_bound, q, k, vt, c)


def _mem_attention(qs, mk_refs, mv_refs, heads):
    n_mem = mk_refs[0].shape[0] // heads
    dh = mk_refs[0].shape[1]
    units = [(e, h) for e in range(len(qs)) for h in range(heads)]
    rows = lambda ref, h: ref[pl.ds(h, n_mem, stride=heads), :].astype(BF16)
    s = [_nt_dot(qs[e][:, h * dh:(h + 1) * dh], rows(mk_refs[e], h)) for e, h in units]
    p, inv_l = [], []
    for u in range(len(units)):
        ex = jnp.exp2(s[u] - jnp.max(s[u], axis=-1, keepdims=True))
        inv_l.append(1.0 / jnp.sum(ex, axis=-1, keepdims=True))
        p.append(ex.astype(BF16))
    o = [(_dot(p[u], rows(mv_refs[e], h)) * inv_l[u]).astype(BF16) for u, (e, h) in enumerate(units)]
    return [jnp.concatenate(o[e * heads:(e + 1) * heads], axis=-1) for e in range(len(qs))]


def _mem_kernel(q_ref, mk_ref, mv_ref, o_ref, *, heads):
    bb = q_ref.shape[0]
    outs = _mem_attention([q_ref[e] for e in range(bb)], [mk_ref.at[e] for e in range(bb)],
                          [mv_ref.at[e] for e in range(bb)], heads)
    for e in range(bb):
        o_ref[e] = outs[e]


def _mem_attn(q, mk, mv, *, heads, tq, bb):
    b, t, w = q.shape
    kv = pl.BlockSpec((bb,) + mk.shape[1:], lambda bi, i: (bi, 0, 0))
    qs = pl.BlockSpec((bb, tq, w), lambda bi, i: (bi, i, 0))
    return pl.pallas_call(
        functools.partial(_mem_kernel, heads=heads), out_shape=jax.ShapeDtypeStruct((b, t, w), BF16),
        grid=(b // bb, t // tq), in_specs=[qs, kv, kv], out_specs=qs,
        compiler_params=_cparams("arbitrary", "arbitrary"), name="mem_attn")(q, mk, mv)


def _post_kernel(x_ref, oa_ref, ob_ref, *refs, mem_heads, col_chunk, ff_chunk):
    if mem_heads:
        qm_ref, mk_ref, mv_ref, *refs = refs
        o_m, = _mem_attention([qm_ref[...]], [mk_ref], [mv_ref], mem_heads)
    else:
        om_ref, *refs = refs
        o_m = om_ref[...]
    gmix_ref, wg_ref, wpa_ref, wpb_ref, wpm_ref, wo_ref, gffn_ref, wup_ref, wdn_ref, y_ref = refs
    x = x_ref[...]
    d = x.shape[1]
    xn = _rms_rows(x, gmix_ref[...]).astype(BF16)
    branches = ((oa_ref[...], wpa_ref), (ob_ref[...], wpb_ref), (o_m, wpm_ref))
    h = []
    for c in range(d // col_chunk):
        cols = slice(c * col_chunk, (c + 1) * col_chunk)
        hc = 0.0
        for b, (o_b, wp_ref) in enumerate(branches):
            gate = jax.nn.sigmoid(_dot(xn, wg_ref[:, b * d + c * col_chunk:b * d + (c + 1) * col_chunk]))
            hc = hc + gate * _dot(o_b, wp_ref[:, cols])
        h.append(hc.astype(BF16))
    x1 = x + _dot(jnp.concatenate(h, axis=-1), wo_ref[...])
    xn1 = _rms_rows(x1, gffn_ref[...]).astype(BF16)
    y = x1
    for c in range(wup_ref.shape[1] // ff_chunk):
        cols = slice(c * ff_chunk, (c + 1) * ff_chunk)
        u = jnp.maximum(_dot(xn1, wup_ref[:, cols]), 0.0)
        y = y + _dot((u * u).astype(BF16), wdn_ref[cols, :])
    y_ref[...] = y


def _post(x, oa, ob, om, gmix, wg, wpa, wpb, wpm, wo, gffn, wup, wdn, *, tm, mem=None):
    rows, d = x.shape
    row = lambda i: (i, 0)
    acts = [x, oa, ob]
    specs = [pl.BlockSpec((tm, a.shape[1]), row) for a in acts]
    heads = 0
    if om is None:
        qm, mk, mv, heads = mem
        tiles_per_batch = rows // mk.shape[0] // tm
        kv = pl.BlockSpec((None,) + mk.shape[1:], lambda i: (i // tiles_per_batch, 0, 0))
        acts += [qm, mk, mv]
        specs += [pl.BlockSpec((tm, qm.shape[1]), row), kv, kv]
    else:
        acts.append(om)
        specs.append(pl.BlockSpec((tm, om.shape[1]), row))
    weights = (gmix, wg, wpa, wpb, wpm, wo, gffn, wup, wdn)
    return pl.pallas_call(
        functools.partial(_post_kernel, mem_heads=heads, col_chunk=512, ff_chunk=1024),
        out_shape=jax.ShapeDtypeStruct((rows, d), F32), grid=(rows // tm,),
        in_specs=specs + [_const_spec(a.shape) for a in weights],
        out_specs=pl.BlockSpec((tm, d), row), compiler_params=_cparams("arbitrary"),
        name="post")(*acts, *weights)


def _pad_rows(x, rows):
    return jnp.concatenate([x, jnp.zeros((rows - x.shape[0], x.shape[1]), x.dtype)], axis=0)


def _cached_attention(q_ref, kc_ref, vc_ref, kn_ref, vn_ref, o_ref, bias_c, bias_n, n_pad):
    bb, heads, head, _ = kc_ref.shape
    units = [(e, h, slice(h * head, (h + 1) * head)) for e in range(bb) for h in range(heads)]
    sc = [_dot(q_ref[e, :, ln], kc_ref[e, h].astype(BF16)) + bias_c(e, h) for e, h, ln in units]
    sn = [_nt_dot(q_ref[e, :, ln], _pad_rows(kn_ref[e, :, ln].astype(BF16), n_pad)) + bias_n(e, h)
          for e, h, ln in units]
    pc, pn, inv_l = [], [], []
    for u in range(len(units)):
        m = jnp.maximum(jnp.max(sc[u], axis=-1, keepdims=True), jnp.max(sn[u], axis=-1, keepdims=True))
        pc.append(jnp.exp2(sc[u] - m))
        pn.append(jnp.exp2(sn[u] - m))
        inv_l.append(1.0 / (jnp.sum(pc[u], axis=-1, keepdims=True) + jnp.sum(pn[u], axis=-1, keepdims=True)))
    for u, (e, h, ln) in enumerate(units):
        o = (_nt_dot(pc[u].astype(BF16), vc_ref[e, h].astype(BF16))
             + _dot(pn[u].astype(BF16), _pad_rows(vn_ref[e, :, ln].astype(BF16), n_pad)))
        o_ref[e, :, ln] = (o * inv_l[u]).astype(o_ref.dtype)


def _band_sample_kernel(q_ref, kc_ref, vc_ref, kn_ref, vn_ref, bc_ref, bn_ref, o_ref):
    s_len = q_ref.shape[1]
    rows = lambda h: slice(h * s_len, (h + 1) * s_len)
    _cached_attention(q_ref, kc_ref, vc_ref, kn_ref, vn_ref, o_ref,
                      lambda e, h: bc_ref[rows(h), :], lambda e, h: bn_ref[rows(h), :], bn_ref.shape[1])


def _band_sample(q, kct, vct, kn, vn, bias_c, bias_n, *, bb):
    b, s_len, w = q.shape
    new = pl.BlockSpec((bb, s_len, w), lambda bi: (bi, 0, 0))
    cache = pl.BlockSpec((bb,) + kct.shape[1:], lambda bi: (bi, 0, 0, 0))
    return pl.pallas_call(
        _band_sample_kernel, out_shape=jax.ShapeDtypeStruct((b, s_len, w), BF16), grid=(b // bb,),
        in_specs=[new, cache, cache, new, new, _const_spec(bias_c.shape), _const_spec(bias_n.shape)],
        out_specs=new, compiler_params=_cparams("arbitrary"), name="band_sample")(
            q, kct, vct, kn, vn, bias_c, bias_n)


def _forget_sample_kernel(q_ref, kc_ref, vc_ref, kn_ref, vn_ref, c_ref, o_ref, *, n_pad):
    s_len = q_ref.shape[1]
    p_len = kc_ref.shape[3]
    causal = (lax.broadcasted_iota(jnp.int32, (s_len, n_pad), 1)
              <= lax.broadcasted_iota(jnp.int32, (s_len, n_pad), 0))
    ch = [(c_ref[e, :, p_len - 1:p_len] - c_ref[e]) * LOG2E for e in range(c_ref.shape[0])]
    _cached_attention(q_ref, kc_ref, vc_ref, kn_ref, vn_ref, o_ref,
                      lambda e, h: ch[e][h:h + 1, 0:p_len],
                      lambda e, h: jnp.where(causal, ch[e][h:h + 1, p_len:p_len + n_pad], NEG), n_pad)


def _forget_sample(q, kct, vct, kn, vn, c, *, n_pad):
    b, s_len, w = q.shape
    new = pl.BlockSpec((1, s_len, w), lambda bi: (bi, 0, 0))
    cache = pl.BlockSpec((1,) + kct.shape[1:], lambda bi: (bi, 0, 0, 0))
    cs = pl.BlockSpec((1,) + c.shape[1:], lambda bi: (bi, 0, 0))
    return pl.pallas_call(
        functools.partial(_forget_sample_kernel, n_pad=n_pad),
        out_shape=jax.ShapeDtypeStruct((b, s_len, w), BF16), grid=(b,),
        in_specs=[new, cache, cache, new, new, cs], out_specs=new,
        compiler_params=_cparams("arbitrary"), name="forget_sample")(q, kct, vct, kn, vn, c)


def _block_diag_mean(head, size=256):
    r = jnp.arange(size) // head
    return jnp.where(r[:, None] == r[None, :], 1.0 / head, 0.0).astype(BF16)


def _rel_bias_rows(rel_bias, d0, n_q, n_k):
    n = n_q + n_k - 1
    dist = d0 + (n_q - 1) - jnp.arange(n)
    e = rel_bias[jnp.clip(dist, -REL_CLIP, REL_CLIP) + REL_CLIP].T.astype(F32)
    h = e.shape[0]
    skew = jnp.tile(jnp.pad(e, ((0, 0), (0, 1))), (1, n_q))[:, :n_q * n].reshape(h, n_q, n)
    return skew[:, :, n_q - 1:].reshape(h * n_q, n_k)


def kernel(x_prompt, x_sample, mem_prompt, cache_a_k, cache_a_v, cache_b_k, cache_b_v, cache_b_logf,
           cache_mem_k, cache_mem_v, g_mix, w_in, b_f, g_qa, g_ka, g_qb, g_kb, g_qm, g_km, rel_bias,
           g_mem, w_mkv, w_pa, w_pb, w_pm, w_o, g_ffn, w_up, w_down):
    depth = w_in.shape[0]
    assert depth == 1
    batch, seq, d = x_prompt.shape
    dec_b, dec_s, _ = x_sample.shape
    _, _, n_cache, h_a, dh_a = cache_a_k.shape
    _, _, past, h_b, dh_b = cache_b_k.shape
    _, _, n_mem, h_m, dh_m = cache_mem_k.shape
    w_a, w_b, w_m = h_a * dh_a, h_b * dh_b, h_m * dh_m
    assert w_a == w_b == w_m == 512 and dh_a == dh_b == 64 and dh_m == 128 and h_b == 8
    keep_p = min(n_cache, seq)
    n_pad = 128
    l = 0

    qkv_end = 3 * w_a + 3 * w_b
    f_end = qkv_end + h_b
    qm_end = f_end + w_m
    w = w_in[l]
    wqkv = w[:, :qkv_end].astype(BF16)
    wft = jnp.zeros((16, d), BF16).at[:h_b].set(w[:, qkv_end:f_end].T.astype(BF16))
    wqm = w[:, f_end:qm_end].astype(BF16)
    wg = w[:, qm_end:].astype(BF16)
    gains = jnp.stack([jnp.tile(g[l], h_a) for g in (g_qa, g_ka, g_qb, g_kb)]).astype(F32)
    gqm = jnp.tile(g_qm[l], h_m)[None, :]
    gkm = jnp.tile(g_km[l], h_m)[None, :]
    bf = b_f[l][:, None].astype(F32)
    bd64 = _block_diag_mean(dh_a)
    bd128 = _block_diag_mean(dh_m)
    tri = (jnp.arange(128)[:, None] <= jnp.arange(128)[None, :]).astype(BF16)
    gmix = g_mix[l][None, :]
    proj_w = (gmix, wqkv, wft, wqm, gains, gqm, bf, bd64, bd128)
    wpa, wpb, wpm, wo = (a[l].astype(BF16) for a in (w_pa, w_pb, w_pm, w_o))
    wup, wdn = w_up[l].astype(BF16), w_down[l].astype(BF16)
    gffn = g_ffn[l][None, :]

    xp = x_prompt.reshape(batch * seq, d)
    pp = _proj(xp, *proj_w, tm=512, group_rows=seq, keep_rows=keep_p)
    mk_f, mv_f = _memkv(mem_prompt.reshape(batch * n_mem, d), g_mem[l][None, :], w_mkv[l].astype(BF16),
                        gkm, bd128, tm=256, heads=h_m)
    tq_b = 512
    lf_p = pp["lft"].reshape(h_b, batch, seq).transpose(1, 0, 2)
    c_p = _cumsum_lanes(lf_p.reshape(batch * h_b, seq), tri, rb=batch * h_b)
    c_p = c_p.reshape(batch, h_b // 4, 4, seq).transpose(0, 1, 3, 2)
    vt = pp["vbt"].reshape(batch, seq // tq_b, h_b, V_ROWS, tq_b)
    three = lambda a: a.reshape(batch, seq, -1)
    bias_p = _rel_bias_rows(rel_bias[l], n_cache, CHUNK, n_cache + CHUNK) * LOG2E
    o_a = _band_prompt(three(pp["qa"]), three(pp["ka"]), three(pp["va"]), bias_p, reach=n_cache, tq=256)
    logit_bound = (1.02 * LOG2E * dh_b ** 0.5 * jnp.max(jnp.abs(g_qb[l])) * jnp.max(jnp.abs(g_kb[l])))
    o_b = _forget_prompt(logit_bound.reshape(1).astype(F32), three(pp["qb"]), three(pp["kb"]), vt, c_p,
                         head=dh_b, tq=tq_b)
    post_w = (gmix, wg, wpa, wpb, wpm, wo, gffn, wup, wdn)
    mem_p = (pp["qm"], mk_f.reshape(batch, n_mem * h_m, dh_m), mv_f.reshape(batch, n_mem * h_m, dh_m), h_m)
    y_prompt = _post(xp, o_a.reshape(-1, w_a), o_b.reshape(-1, w_b), None, *post_w, tm=512,
                     mem=mem_p).reshape(batch, seq, d)

    xs = x_sample.reshape(dec_b * dec_s, d)
    ps = _proj(xs, *proj_w, tm=512)
    sthree = lambda a: a.reshape(dec_b, dec_s, -1)
    heads_t = lambda a: jnp.transpose(a[l], (0, 2, 3, 1))
    bias_c = _rel_bias_rows(rel_bias[l], n_cache, dec_s, n_cache) * LOG2E
    bias_n = jnp.full((h_a * dec_s, n_pad), NEG, F32).at[:, :dec_s].set(
        _rel_bias_rows(rel_bias[l], 0, dec_s, dec_s) * LOG2E)
    o_a_s = _band_sample(sthree(ps["qa"]), heads_t(cache_a_k), heads_t(cache_a_v), sthree(ps["ka"]),
                         sthree(ps["va"]), bias_c, bias_n, bb=4)
    lf_new = ps["lft"].reshape(h_b, dec_b, dec_s).transpose(1, 0, 2)
    lf_all = jnp.concatenate([cache_b_logf[l].astype(F32).transpose(0, 2, 1), lf_new,
                              jnp.zeros((dec_b, h_b, n_pad - dec_s), F32)], axis=-1)
    c_s = _cumsum_lanes(lf_all.reshape(dec_b * h_b, past + n_pad), tri, rb=32)
    o_b_s = _forget_sample(sthree(ps["qb"]), heads_t(cache_b_k), heads_t(cache_b_v), sthree(ps["kb"]),
                           sthree(ps["vb_f"]), c_s.reshape(dec_b, h_b, past + n_pad), n_pad=n_pad)
    o_m_s = _mem_attn(sthree(ps["qm"]), cache_mem_k[l].reshape(dec_b, n_mem * h_m, dh_m),
                      cache_mem_v[l].reshape(dec_b, n_mem * h_m, dh_m), heads=h_m, tq=dec_s, bb=8)
    y_sample = _post(xs, o_a_s.reshape(-1, w_a), o_b_s.reshape(-1, w_b), o_m_s.reshape(-1, w_m), *post_w,
                     tm=512).reshape(dec_b, dec_s, d)

    lead = lambda a, *shape: a.reshape((depth,) + shape)
    frames_major = lambda a: jnp.transpose(a, (0, 3, 1, 2))[None]
    return (y_prompt, y_sample,
            frames_major(pp["ka_f"]), frames_major(pp["va_f"]),
            frames_major(pp["kb_f"]), frames_major(pp["vb_f"]),
            lf_p.transpose(0, 2, 1)[None],
            lead(mk_f, batch, n_mem, h_m, dh_m), lead(mv_f, batch, n_mem, h_m, dh_m),
            lead(ps["ka_f"], dec_b, dec_s, h_a, dh_a), lead(ps["va_f"], dec_b, dec_s, h_a, dh_a),
            lead(ps["kb_f"], dec_b, dec_s, h_b, dh_b), lead(ps["vb_f"], dec_b, dec_s, h_b, dh_b),
            lead(ps["lft"].T, dec_b, dec_s, h_b))
```

```python
import functools

import jax
import jax.numpy as jnp
from jax import lax
from jax.experimental import pallas as pl
from jax.experimental.pallas import tpu as pltpu

BF16 = jnp.bfloat16
F32 = jnp.float32

EPS = 1e-6
CHUNK = 64
REL_CLIP = 128
NEG = -1e30
LANE_GROUP = 256
LOG2E = 1.4426950408889634
V_ROWS = 64 + 16
MAX_ONE_PASS_BOUND = 40.0
VMEM_LIMIT = 56 * 1024 * 1024


def _cparams(*sem):
    return pltpu.CompilerParams(dimension_semantics=sem, vmem_limit_bytes=VMEM_LIMIT)


def _const_spec(shape):
    nd = len(shape)
    return pl.BlockSpec(shape, lambda *_: (0,) * nd, pipeline_mode=pl.Buffered(1))


def _nt_dot(a, b):
    return lax.dot_general(a, b, (((1,), (1,)), ((), ())), preferred_element_type=F32)


def _dot(a, b):
    return jnp.dot(a, b, preferred_element_type=F32)


def _rms_rows(x, gain):
    ms = jnp.mean(x * x, axis=-1, keepdims=True)
    return x * lax.rsqrt(ms + EPS) * gain


def _head_norm(z, bd_ref, gain):
    z2 = (z * z).astype(BF16)
    ms = jnp.concatenate(
        [_dot(z2[:, h * 256:(h + 1) * 256], bd_ref[...]) for h in range(z.shape[1] // 256)], axis=-1)
    return z * lax.rsqrt(ms + EPS) * gain


def _log_sigmoid(x):
    return jnp.minimum(x, 0.0) - jnp.log1p(jnp.exp(-jnp.abs(x)))


def _lane_group(width, head):
    return lax.broadcasted_iota(jnp.int32, (1, width), 1) // head


def _block_diag_rows(q4, head):
    grp = _lane_group(q4.shape[1], head)
    zero = jnp.zeros_like(q4)
    return jnp.concatenate([jnp.where(grp == g, q4, zero) for g in range(q4.shape[1] // head)], axis=0)


def _pick_diag(o, rows, head):
    grp = _lane_group(o.shape[1], head)
    out = jnp.zeros((rows, o.shape[1]), o.dtype)
    for g in range(o.shape[1] // head):
        out = jnp.where(grp == g, o[g * rows:(g + 1) * rows], out)
    return out


def _store_heads_t(ref, z_t):
    dh = ref.shape[1]
    for h in range(ref.shape[0]):
        ref[h] = z_t[h * dh:(h + 1) * dh, :]


def _proj_kernel(x_ref, gmix_ref, wqkv_ref, wft_ref, wqm_ref, gains_ref, gqm_ref, bf_ref,
                 bd64_ref, bd128_ref, *out_refs, out_names, scale_a, scale_b, scale_m, keep_from,
                 tiles_per_group):
    o = dict(zip(out_names, out_refs))
    head_major = "vbt" in o
    xn = _rms_rows(x_ref[...], gmix_ref[...]).astype(BF16)

    def seg(j):
        return _dot(xn, wqkv_ref[:, j * 512:(j + 1) * 512])

    kb = _head_norm(seg(4), bd64_ref, gains_ref[3:4, :])
    o["kb"][...] = kb.astype(BF16)
    vb = seg(5)
    if head_major:
        _store_heads_t(o["kb_f"], kb.T)
        vb_t = vb.T
        _store_heads_t(o["vb_f"], vb_t)
        vbt = o["vbt"]
        head = vb_t.shape[0] // vbt.shape[0]
        pad = vbt.shape[1] - head
        ones_row = (lax.broadcasted_iota(jnp.int32, (pad, vb_t.shape[1]), 0) == 0).astype(BF16)
        for h in range(vbt.shape[0]):
            vbt[h, 0:head, :] = vb_t[h * head:(h + 1) * head, :].astype(BF16)
            vbt[h, head:head + pad, :] = ones_row
    else:
        o["kb_f"][...] = kb
        o["vb_f"][...] = vb
    qb = _head_norm(seg(3), bd64_ref, gains_ref[2:3, :])
    o["qb"][...] = (qb * scale_b).astype(BF16)
    qa = _head_norm(seg(0), bd64_ref, gains_ref[0:1, :])
    o["qa"][...] = (qa * scale_a).astype(BF16)
    ka = _head_norm(seg(1), bd64_ref, gains_ref[1:2, :])
    o["ka"][...] = ka.astype(BF16)
    qm = _head_norm(_dot(xn, wqm_ref[...]), bd128_ref, gqm_ref[...])
    o["qm"][...] = (qm * scale_m).astype(BF16)
    flt = _nt_dot(wft_ref[...], xn)
    o["lft"][...] = _log_sigmoid(flt[0:8, :] + bf_ref[...])
    va = seg(2)
    o["va"][...] = va.astype(BF16)
    if head_major:
        @pl.when(pl.program_id(0) % tiles_per_group >= keep_from)
        def _():
            _store_heads_t(o["ka_f"], ka.T)
            _store_heads_t(o["va_f"], va.T)
    else:
        o["ka_f"][...] = ka
        o["va_f"][...] = va


def _proj(x, gmix, wqkv, wft, wqm, gains, gqm, bf, bd64, bd128, *, tm, group_rows=None, keep_rows=None):
    rows, d = x.shape
    n_tiles = rows // tm
    row = lambda i: (i, 0)
    wide = lambda dt: (jax.ShapeDtypeStruct((rows, 512), dt), pl.BlockSpec((tm, 512), row))
    outs = {n: wide(BF16) for n in ("qa", "ka", "va", "qb", "kb", "qm")}
    outs["lft"] = (jax.ShapeDtypeStruct((8, rows), F32), pl.BlockSpec((8, tm), lambda i: (0, i)))
    tpg = keep_from = 1
    if group_rows is None:
        for n in ("ka_f", "va_f", "kb_f", "vb_f"):
            outs[n] = wide(F32)
    else:
        assert keep_rows == tm
        tpg = group_rows // tm
        keep_from = tpg - 1
        groups = rows // group_rows
        kept = (jax.ShapeDtypeStruct((groups, 8, 64, keep_rows), F32),
                pl.BlockSpec((None, 8, 64, tm), lambda i: (i // tpg, 0, 0, 0)))
        full = (jax.ShapeDtypeStruct((groups, 8, 64, group_rows), F32),
                pl.BlockSpec((None, 8, 64, tm), lambda i: (i // tpg, 0, 0, i % tpg)))
        outs.update(ka_f=kept, va_f=kept, kb_f=full, vb_f=full)
        outs["vbt"] = (jax.ShapeDtypeStruct((n_tiles, 8, V_ROWS, tm), BF16),
                       pl.BlockSpec((None, 8, V_ROWS, tm), lambda i: (i, 0, 0, 0)))
    names = tuple(outs)
    in_specs = [pl.BlockSpec((tm, d), row)] + [_const_spec(a.shape) for a in
                                               (gmix, wqkv, wft, wqm, gains, gqm, bf, bd64, bd128)]
    kern = functools.partial(_proj_kernel, out_names=names, scale_a=LOG2E * 64 ** -0.5,
                             scale_b=LOG2E * 64 ** -0.5, scale_m=LOG2E * 128 ** -0.5,
                             keep_from=keep_from, tiles_per_group=tpg)
    res = pl.pallas_call(kern, out_shape=[outs[n][0] for n in names], grid=(n_tiles,), in_specs=in_specs,
                         out_specs=[outs[n][1] for n in names], compiler_params=_cparams("arbitrary"),
                         name="proj")(x, gmix, wqkv, wft, wqm, gains, gqm, bf, bd64, bd128)
    return dict(zip(names, res))


def _memkv_kernel(m_ref, gmem_ref, w_ref, gkm_ref, bd128_ref, mk_o, mv_o, *, heads):
    tm = m_ref.shape[0]
    xn = _rms_rows(m_ref[...], gmem_ref[...]).astype(BF16)
    half = w_ref.shape[1] // 2
    dh = half // heads
    mk = _head_norm(_dot(xn, w_ref[:, :half]), bd128_ref, gkm_ref[...])
    mv = _dot(xn, w_ref[:, half:])
    for h in range(heads):
        mk_o[pl.ds(h, tm, stride=heads), :] = mk[:, h * dh:(h + 1) * dh]
        mv_o[pl.ds(h, tm, stride=heads), :] = mv[:, h * dh:(h + 1) * dh]


def _memkv(mem, gmem, w, gkm, bd128, *, tm, heads):
    rows, d = mem.shape
    dh = w.shape[1] // 2 // heads
    row = lambda i: (i, 0)
    return pl.pallas_call(
        functools.partial(_memkv_kernel, heads=heads),
        out_shape=[jax.ShapeDtypeStruct((rows * heads, dh), F32)] * 2, grid=(rows // tm,),
        in_specs=[pl.BlockSpec((tm, d), row)] + [_const_spec(a.shape) for a in (gmem, w, gkm, bd128)],
        out_specs=[pl.BlockSpec((tm * heads, dh), row)] * 2, compiler_params=_cparams("arbitrary"),
        name="memkv")(mem, gmem, w, gkm, bd128)


def _cumsum_kernel(x_ref, tri_ref, o_ref):
    rows, length = x_ref.shape
    tri = tri_ref[...]

    def local(c):
        x = x_ref[:, c * 128:(c + 1) * 128]
        hi = x.astype(BF16)
        r1 = x - hi.astype(F32)
        mid = r1.astype(BF16)
        lo = (r1 - mid.astype(F32)).astype(BF16)
        return _dot(hi, tri) + _dot(mid, tri) + _dot(lo, tri)

    chunks = [local(c) for c in range(length // 128)]
    carry = jnp.zeros((rows, 1), F32)
    for c, cc in enumerate(chunks):
        o_ref[:, c * 128:(c + 1) * 128] = cc + carry
        carry = carry + cc[:, 127:128]


def _cumsum_lanes(x, tri, *, rb):
    rows, length = x.shape
    return pl.pallas_call(
        _cumsum_kernel, out_shape=jax.ShapeDtypeStruct((rows, length), F32), grid=(rows // rb,),
        in_specs=[pl.BlockSpec((rb, length), lambda i: (i, 0)), _const_spec(tri.shape)],
        out_specs=pl.BlockSpec((rb, length), lambda i: (i, 0)), compiler_params=_cparams("arbitrary"),
        name="cumsum")(x, tri)


def _band_kernel(bound_ref, q_ref, k0_ref, k1_ref, k2_ref, v0_ref, v1_ref, v2_ref, bias_ref, o_ref,
                 k_sc, v_sc, *, reach, lookahead):
    tq = q_ref.shape[0]
    band = reach + CHUNK
    i = pl.program_id(1)
    for n, (kr, vr) in enumerate(((k0_ref, v0_ref), (k1_ref, v1_ref), (k2_ref, v2_ref))):
        k_sc[n * tq:(n + 1) * tq, :] = kr[...]
        v_sc[n * tq:(n + 1) * tq, :] = vr[...]
    col = lax.broadcasted_iota(jnp.int32, (1, band), 1)
    units = [(j, hg) for j in range(tq // CHUNK) for hg in range(q_ref.shape[1] // LANE_GROUP)]

    def scores(j, hg):
        r0 = 2 * tq + j * CHUNK - reach
        lanes = slice(hg * LANE_GROUP, (hg + 1) * LANE_GROUP)
        qbd = _block_diag_rows(q_ref[j * CHUNK:(j + 1) * CHUNK, lanes], CHUNK)
        s = _nt_dot(qbd, k_sc[r0:r0 + band, lanes]) + bias_ref[hg * 4 * CHUNK:(hg + 1) * 4 * CHUNK, :]
        return jnp.where((i - 2) * tq + r0 + col >= 0, s, NEG)

    def attend(one_pass):
        pending = [scores(*u) for u in units[:lookahead]]
        for n, (j, hg) in enumerate(units):
            s = pending.pop(0)
            if n + lookahead < len(units):
                pending.append(scores(*units[n + lookahead]))
            r0 = 2 * tq + j * CHUNK - reach
            lanes = slice(hg * LANE_GROUP, (hg + 1) * LANE_GROUP)
            p = jnp.exp2(s if one_pass else s - jnp.max(s, axis=-1, keepdims=True))
            l = jnp.sum(p, axis=-1, keepdims=True)
            o = _dot(p.astype(BF16), v_sc[r0:r0 + band, lanes]) / l
            o_ref[j * CHUNK:(j + 1) * CHUNK, lanes] = _pick_diag(o, CHUNK, CHUNK).astype(o_ref.dtype)

    lax.cond(bound_ref[0] <= MAX_ONE_PASS_BOUND, lambda: attend(True), lambda: attend(False))


def _band_prompt(logit_bound, q, k, v, bias, *, reach, tq):
    b, t, w = q.shape
    assert reach == 2 * tq and tq % CHUNK == 0
    blk = lambda off: pl.BlockSpec((None, tq, w), lambda bi, i: (bi, jnp.maximum(i - off, 0), 0))
    return pl.pallas_call(
        functools.partial(_band_kernel, reach=reach, lookahead=2),
        out_shape=jax.ShapeDtypeStruct((b, t, w), BF16), grid=(b, t // tq),
        in_specs=[pl.BlockSpec(memory_space=pltpu.SMEM), blk(0), blk(2), blk(1), blk(0), blk(2), blk(1), blk(0),
                  _const_spec(bias.shape)],
        out_specs=blk(0),
        scratch_shapes=[pltpu.VMEM((3 * tq, w), BF16), pltpu.VMEM((3 * tq, w), BF16)],
        compiler_params=_cparams("arbitrary", "arbitrary"), name="band_prompt")(
            logit_bound, q, k, k, k, v, v, v, bias)


def _forget_kernel(bound_ref, q_ref, k_ref, vt_ref, c_ref, o_ref, qt_sc, m_sc, acc_sc, *, head, lookahead):
    tq, width = q_ref.shape
    n_heads = width // head
    i = pl.program_id(2)
    logit_bound = bound_ref[0]
    q_t = q_ref[...].astype(F32).T
    row_grp = lax.broadcasted_iota(jnp.int32, (width, 1), 0) // head
    for g in range(n_heads):
        qt_sc[g] = jnp.where(row_grp == g, q_t, 0.0).astype(BF16)
    m_sc[...] = jnp.full(m_sc.shape, NEG, F32)
    acc_sc[...] = jnp.zeros(acc_sc.shape, F32)
    c_first = c_ref[pl.ds(pl.multiple_of(i * tq, tq), 8), :][0:1, :]

    def step(j, masked):
        k0 = pl.multiple_of(j * tq, tq)
        kt = k_ref[pl.ds(k0, tq), :]
        bias = (c_first - c_ref[pl.ds(k0, tq), :]) * LOG2E
        if masked:
            keep = (lax.broadcasted_iota(jnp.int32, (tq, tq), 0)
                    <= lax.broadcasted_iota(jnp.int32, (tq, tq), 1))

        def scores(g):
            s = _dot(kt, qt_sc[g]) + bias[:, g:g + 1]
            return jnp.where(keep, s, NEG) if masked else s

        pending = [scores(g) for g in range(min(lookahead, n_heads))]
        for g in range(n_heads):
            s = pending.pop(0)
            if g + lookahead < n_heads:
                pending.append(scores(g + lookahead))
            m_prev = m_sc[g]
            m_new = jnp.maximum(m_prev, jnp.max(s, axis=0, keepdims=True))
            p = jnp.exp2(s - m_new).astype(BF16)
            acc_sc[g] = jnp.exp2(m_prev - m_new) * acc_sc[g] + _dot(vt_ref[j, g], p)
            m_sc[g] = m_new

    def bounded_step(j):
        k0 = pl.multiple_of(j * tq, tq)
        kt = k_ref[pl.ds(k0, tq), :]
        c_tile = c_ref[pl.ds(k0, tq), :]
        c_last = c_tile[tq - 1:tq, :]
        m_tile = (c_first - c_last) * LOG2E
        shift = (c_last - c_tile) * LOG2E
        pending = [_dot(kt, qt_sc[g]) + shift[:, g:g + 1] for g in range(min(lookahead, n_heads))]
        for g in range(n_heads):
            s = pending.pop(0)
            if g + lookahead < n_heads:
                pending.append(_dot(kt, qt_sc[g + lookahead]) + shift[:, g + lookahead:g + lookahead + 1])
            m_new = jnp.broadcast_to(m_tile[:, g:g + 1], (1, tq))
            p = jnp.exp2(s).astype(BF16)
            acc_sc[g] = jnp.exp2(m_sc[g] - m_new) * acc_sc[g] + _dot(vt_ref[j, g], p)
            m_sc[g] = m_new

    def loop(one_step):
        def body(j, carry):
            one_step(j)
            return carry
        lax.fori_loop(0, i, body, 0)

    lax.cond(logit_bound <= MAX_ONE_PASS_BOUND, lambda: loop(bounded_step),
             lambda: loop(lambda j: step(j, False)))
    step(i, True)
    o_t = jnp.concatenate([acc_sc[g, 0:head, :] / acc_sc[g, head:head + 1, :] for g in range(n_heads)],
                          axis=0)
    o_ref[...] = o_t.T.astype(o_ref.dtype)


def _forget_prompt(logit_bound, q, k, vt, c, *, head, tq):
    b, t, w = q.shape
    n_heads = LANE_GROUP // head
    v_rows = vt.shape[3]
    lookahead = 3
    assert lookahead <= n_heads
    return pl.pallas_call(
        functools.partial(_forget_kernel, head=head, lookahead=lookahead),
        out_shape=jax.ShapeDtypeStruct((b, t, w), BF16), grid=(b, w // LANE_GROUP, t // tq),
        in_specs=[pl.BlockSpec(memory_space=pltpu.SMEM),
                  pl.BlockSpec((None, tq, LANE_GROUP), lambda bi, hg, i: (bi, i, hg)),
                  pl.BlockSpec((None, t, LANE_GROUP), lambda bi, hg, i: (bi, 0, hg)),
                  pl.BlockSpec((None, t // tq, n_heads, v_rows, tq), lambda bi, hg, i: (bi, 0, hg, 0, 0)),
                  pl.BlockSpec((None, None, t, n_heads), lambda bi, hg, i: (bi, hg, 0, 0))],
        out_specs=pl.BlockSpec((None, tq, LANE_GROUP), lambda bi, hg, i: (bi, i, hg)),
        scratch_shapes=[pltpu.VMEM((n_heads, LANE_GROUP, tq), BF16), pltpu.VMEM((n_heads, 1, tq), F32),
                        pltpu.VMEM((n_heads, v_rows, tq), F32)],
        compiler_params=_cparams("arbitrary", "arbitrary", "arbitrary"), name="forget_prompt")(
            logit_bound, q, k, vt, c)


def _mem_attention(qs, mk_refs, mv_refs, heads):
    n_mem = mk_refs[0].shape[0] // heads
    dh = mk_refs[0].shape[1]
    units = [(e, h) for e in range(len(qs)) for h in range(heads)]
    rows = lambda ref, h: ref[pl.ds(h, n_mem, stride=heads), :].astype(BF16)
    s = [_nt_dot(qs[e][:, h * dh:(h + 1) * dh], rows(mk_refs[e], h)) for e, h in units]
    p, inv_l = [], []
    for u in range(len(units)):
        ex = jnp.exp2(s[u] - jnp.max(s[u], axis=-1, keepdims=True))
        inv_l.append(1.0 / jnp.sum(ex, axis=-1, keepdims=True))
        p.append(ex.astype(BF16))
    o = [(_dot(p[u], rows(mv_refs[e], h)) * inv_l[u]).astype(BF16) for u, (e, h) in enumerate(units)]
    return [jnp.concatenate(o[e * heads:(e + 1) * heads], axis=-1) for e in range(len(qs))]


def _mem_kernel(q_ref, mk_ref, mv_ref, o_ref, *, heads):
    bb = q_ref.shape[0]
    outs = _mem_attention([q_ref[e] for e in range(bb)], [mk_ref.at[e] for e in range(bb)],
                          [mv_ref.at[e] for e in range(bb)], heads)
    for e in range(bb):
        o_ref[e] = outs[e]


def _mem_attn(q, mk, mv, *, heads, tq, bb):
    b, t, w = q.shape
    kv = pl.BlockSpec((bb,) + mk.shape[1:], lambda bi, i: (bi, 0, 0))
    qs = pl.BlockSpec((bb, tq, w), lambda bi, i: (bi, i, 0))
    return pl.pallas_call(
        functools.partial(_mem_kernel, heads=heads), out_shape=jax.ShapeDtypeStruct((b, t, w), BF16),
        grid=(b // bb, t // tq), in_specs=[qs, kv, kv], out_specs=qs,
        compiler_params=_cparams("arbitrary", "arbitrary"), name="mem_attn")(q, mk, mv)


def _post_kernel(x_ref, oa_ref, ob_ref, *refs, mem_heads, col_chunk, ff_chunk):
    if mem_heads:
        qm_ref, mk_ref, mv_ref, *refs = refs
        o_m, = _mem_attention([qm_ref[...]], [mk_ref], [mv_ref], mem_heads)
    else:
        om_ref, *refs = refs
        o_m = om_ref[...]
    gmix_ref, wg_ref, wpa_ref, wpb_ref, wpm_ref, wo_ref, gffn_ref, wup_ref, wdn_ref, y_ref = refs
    x = x_ref[...]
    d = x.shape[1]
    xn = _rms_rows(x, gmix_ref[...]).astype(BF16)
    branches = ((oa_ref[...], wpa_ref), (ob_ref[...], wpb_ref), (o_m, wpm_ref))
    h = []
    for c in range(d // col_chunk):
        cols = slice(c * col_chunk, (c + 1) * col_chunk)
        hc = 0.0
        for b, (o_b, wp_ref) in enumerate(branches):
            gate = jax.nn.sigmoid(_dot(xn, wg_ref[:, b * d + c * col_chunk:b * d + (c + 1) * col_chunk]))
            hc = hc + gate * _dot(o_b, wp_ref[:, cols])
        h.append(hc.astype(BF16))
    x1 = x + _dot(jnp.concatenate(h, axis=-1), wo_ref[...])
    xn1 = _rms_rows(x1, gffn_ref[...]).astype(BF16)
    y = x1
    for c in range(wup_ref.shape[1] // ff_chunk):
        cols = slice(c * ff_chunk, (c + 1) * ff_chunk)
        u = jnp.maximum(_dot(xn1, wup_ref[:, cols]), 0.0)
        y = y + _dot((u * u).astype(BF16), wdn_ref[cols, :])
    y_ref[...] = y


def _post(x, oa, ob, om, gmix, wg, wpa, wpb, wpm, wo, gffn, wup, wdn, *, tm, mem=None):
    rows, d = x.shape
    row = lambda i: (i, 0)
    acts = [x, oa, ob]
    specs = [pl.BlockSpec((tm, a.shape[1]), row) for a in acts]
    heads = 0
    if om is None:
        qm, mk, mv, heads = mem
        tiles_per_batch = rows // mk.shape[0] // tm
        kv = pl.BlockSpec((None,) + mk.shape[1:], lambda i: (i // tiles_per_batch, 0, 0))
        acts += [qm, mk, mv]
        specs += [pl.BlockSpec((tm, qm.shape[1]), row), kv, kv]
    else:
        acts.append(om)
        specs.append(pl.BlockSpec((tm, om.shape[1]), row))
    weights = (gmix, wg, wpa, wpb, wpm, wo, gffn, wup, wdn)
    return pl.pallas_call(
        functools.partial(_post_kernel, mem_heads=heads, col_chunk=512, ff_chunk=1024),
        out_shape=jax.ShapeDtypeStruct((rows, d), F32), grid=(rows // tm,),
        in_specs=specs + [_const_spec(a.shape) for a in weights],
        out_specs=pl.BlockSpec((tm, d), row), compiler_params=_cparams("arbitrary"),
        name="post")(*acts, *weights)


def _pad_rows(x, rows):
    return jnp.concatenate([x, jnp.zeros((rows - x.shape[0], x.shape[1]), x.dtype)], axis=0)


def _cached_attention(q_ref, kc_ref, vc_ref, kn_ref, vn_ref, o_ref, bias_c, bias_n, n_pad):
    bb, heads, head, _ = kc_ref.shape
    units = [(e, h, slice(h * head, (h + 1) * head)) for e in range(bb) for h in range(heads)]
    sc = [_dot(q_ref[e, :, ln], kc_ref[e, h].astype(BF16)) + bias_c(e, h) for e, h, ln in units]
    sn = [_nt_dot(q_ref[e, :, ln], _pad_rows(kn_ref[e, :, ln].astype(BF16), n_pad)) + bias_n(e, h)
          for e, h, ln in units]
    pc, pn, inv_l = [], [], []
    for u in range(len(units)):
        m = jnp.maximum(jnp.max(sc[u], axis=-1, keepdims=True), jnp.max(sn[u], axis=-1, keepdims=True))
        pc.append(jnp.exp2(sc[u] - m))
        pn.append(jnp.exp2(sn[u] - m))
        inv_l.append(1.0 / (jnp.sum(pc[u], axis=-1, keepdims=True) + jnp.sum(pn[u], axis=-1, keepdims=True)))
    for u, (e, h, ln) in enumerate(units):
        o = (_nt_dot(pc[u].astype(BF16), vc_ref[e, h].astype(BF16))
             + _dot(pn[u].astype(BF16), _pad_rows(vn_ref[e, :, ln].astype(BF16), n_pad)))
        o_ref[e, :, ln] = (o * inv_l[u]).astype(o_ref.dtype)


def _band_sample_kernel(q_ref, kc_ref, vc_ref, kn_ref, vn_ref, bc_ref, bn_ref, o_ref):
    s_len = q_ref.shape[1]
    rows = lambda h: slice(h * s_len, (h + 1) * s_len)
    _cached_attention(q_ref, kc_ref, vc_ref, kn_ref, vn_ref, o_ref,
                      lambda e, h: bc_ref[rows(h), :], lambda e, h: bn_ref[rows(h), :], bn_ref.shape[1])


def _band_sample(q, kct, vct, kn, vn, bias_c, bias_n, *, bb):
    b, s_len, w = q.shape
    new = pl.BlockSpec((bb, s_len, w), lambda bi: (bi, 0, 0))
    cache = pl.BlockSpec((bb,) + kct.shape[1:], lambda bi: (bi, 0, 0, 0))
    return pl.pallas_call(
        _band_sample_kernel, out_shape=jax.ShapeDtypeStruct((b, s_len, w), BF16), grid=(b // bb,),
        in_specs=[new, cache, cache, new, new, _const_spec(bias_c.shape), _const_spec(bias_n.shape)],
        out_specs=new, compiler_params=_cparams("arbitrary"), name="band_sample")(
            q, kct, vct, kn, vn, bias_c, bias_n)


def _forget_sample_kernel(q_ref, kc_ref, vc_ref, kn_ref, vn_ref, c_ref, o_ref, *, n_pad):
    s_len = q_ref.shape[1]
    p_len = kc_ref.shape[3]
    causal = (lax.broadcasted_iota(jnp.int32, (s_len, n_pad), 1)
              <= lax.broadcasted_iota(jnp.int32, (s_len, n_pad), 0))
    ch = [(c_ref[e, :, p_len - 1:p_len] - c_ref[e]) * LOG2E for e in range(c_ref.shape[0])]
    _cached_attention(q_ref, kc_ref, vc_ref, kn_ref, vn_ref, o_ref,
                      lambda e, h: ch[e][h:h + 1, 0:p_len],
                      lambda e, h: jnp.where(causal, ch[e][h:h + 1, p_len:p_len + n_pad], NEG), n_pad)


def _forget_sample(q, kct, vct, kn, vn, c, *, n_pad):
    b, s_len, w = q.shape
    new = pl.BlockSpec((1, s_len, w), lambda bi: (bi, 0, 0))
    cache = pl.BlockSpec((1,) + kct.shape[1:], lambda bi: (bi, 0, 0, 0))
    cs = pl.BlockSpec((1,) + c.shape[1:], lambda bi: (bi, 0, 0))
    return pl.pallas_call(
        functools.partial(_forget_sample_kernel, n_pad=n_pad),
        out_shape=jax.ShapeDtypeStruct((b, s_len, w), BF16), grid=(b,),
        in_specs=[new, cache, cache, new, new, cs], out_specs=new,
        compiler_params=_cparams("arbitrary"), name="forget_sample")(q, kct, vct, kn, vn, c)


def _qk_bound(g_q, g_k, dh):
    return 1.02 * LOG2E * dh ** 0.5 * jnp.max(jnp.abs(g_q)) * jnp.max(jnp.abs(g_k))


def _block_diag_mean(head, size=256):
    r = jnp.arange(size) // head
    return jnp.where(r[:, None] == r[None, :], 1.0 / head, 0.0).astype(BF16)


def _rel_bias_rows(rel_bias, d0, n_q, n_k):
    n = n_q + n_k - 1
    dist = d0 + (n_q - 1) - jnp.arange(n)
    e = rel_bias[jnp.clip(dist, -REL_CLIP, REL_CLIP) + REL_CLIP].T.astype(F32)
    h = e.shape[0]
    skew = jnp.tile(jnp.pad(e, ((0, 0), (0, 1))), (1, n_q))[:, :n_q * n].reshape(h, n_q, n)
    return skew[:, :, n_q - 1:].reshape(h * n_q, n_k)


def kernel(x_prompt, x_sample, mem_prompt, cache_a_k, cache_a_v, cache_b_k, cache_b_v, cache_b_logf,
           cache_mem_k, cache_mem_v, g_mix, w_in, b_f, g_qa, g_ka, g_qb, g_kb, g_qm, g_km, rel_bias,
           g_mem, w_mkv, w_pa, w_pb, w_pm, w_o, g_ffn, w_up, w_down):
    depth = w_in.shape[0]
    assert depth == 1
    batch, seq, d = x_prompt.shape
    dec_b, dec_s, _ = x_sample.shape
    _, _, n_cache, h_a, dh_a = cache_a_k.shape
    _, _, past, h_b, dh_b = cache_b_k.shape
    _, _, n_mem, h_m, dh_m = cache_mem_k.shape
    w_a, w_b, w_m = h_a * dh_a, h_b * dh_b, h_m * dh_m
    assert w_a == w_b == w_m == 512 and dh_a == dh_b == 64 and dh_m == 128 and h_b == 8
    keep_p = min(n_cache, seq)
    n_pad = 128
    l = 0

    qkv_end = 3 * w_a + 3 * w_b
    f_end = qkv_end + h_b
    qm_end = f_end + w_m
    w = w_in[l]
    wqkv = w[:, :qkv_end].astype(BF16)
    wft = jnp.zeros((16, d), BF16).at[:h_b].set(w[:, qkv_end:f_end].T.astype(BF16))
    wqm = w[:, f_end:qm_end].astype(BF16)
    wg = w[:, qm_end:].astype(BF16)
    gains = jnp.stack([jnp.tile(g[l], h_a) for g in (g_qa, g_ka, g_qb, g_kb)]).astype(F32)
    gqm = jnp.tile(g_qm[l], h_m)[None, :]
    gkm = jnp.tile(g_km[l], h_m)[None, :]
    bf = b_f[l][:, None].astype(F32)
    bd64 = _block_diag_mean(dh_a)
    bd128 = _block_diag_mean(dh_m)
    tri = (jnp.arange(128)[:, None] <= jnp.arange(128)[None, :]).astype(BF16)
    gmix = g_mix[l][None, :]
    proj_w = (gmix, wqkv, wft, wqm, gains, gqm, bf, bd64, bd128)
    wpa, wpb, wpm, wo = (a[l].astype(BF16) for a in (w_pa, w_pb, w_pm, w_o))
    wup, wdn = w_up[l].astype(BF16), w_down[l].astype(BF16)
    gffn = g_ffn[l][None, :]

    xp = x_prompt.reshape(batch * seq, d)
    pp = _proj(xp, *proj_w, tm=512, group_rows=seq, keep_rows=keep_p)
    mk_f, mv_f = _memkv(mem_prompt.reshape(batch * n_mem, d), g_mem[l][None, :], w_mkv[l].astype(BF16),
                        gkm, bd128, tm=256, heads=h_m)
    tq_b = 512
    lf_p = pp["lft"].reshape(h_b, batch, seq).transpose(1, 0, 2)
    c_p = _cumsum_lanes(lf_p.reshape(batch * h_b, seq), tri, rb=batch * h_b)
    c_p = c_p.reshape(batch, h_b // 4, 4, seq).transpose(0, 1, 3, 2)
    vt = pp["vbt"].reshape(batch, seq // tq_b, h_b, V_ROWS, tq_b)
    three = lambda a: a.reshape(batch, seq, -1)
    bias_p = _rel_bias_rows(rel_bias[l], n_cache, CHUNK, n_cache + CHUNK) * LOG2E
    bound_a = _qk_bound(g_qa[l], g_ka[l], dh_a) + LOG2E * jnp.max(jnp.abs(rel_bias[l]))
    o_a = _band_prompt(bound_a.reshape(1).astype(F32), three(pp["qa"]), three(pp["ka"]), three(pp["va"]),
                       bias_p, reach=n_cache, tq=256)
    o_b = _forget_prompt(_qk_bound(g_qb[l], g_kb[l], dh_b).reshape(1).astype(F32), three(pp["qb"]),
                         three(pp["kb"]), vt, c_p, head=dh_b, tq=tq_b)
    post_w = (gmix, wg, wpa, wpb, wpm, wo, gffn, wup, wdn)
    mem_p = (pp["qm"], mk_f.reshape(batch, n_mem * h_m, dh_m), mv_f.reshape(batch, n_mem * h_m, dh_m), h_m)
    y_prompt = _post(xp, o_a.reshape(-1, w_a), o_b.reshape(-1, w_b), None, *post_w, tm=512,
                     mem=mem_p).reshape(batch, seq, d)

    xs = x_sample.reshape(dec_b * dec_s, d)
    ps = _proj(xs, *proj_w, tm=512)
    sthree = lambda a: a.reshape(dec_b, dec_s, -1)
    heads_t = lambda a: jnp.transpose(a[l], (0, 2, 3, 1))
    bias_c = _rel_bias_rows(rel_bias[l], n_cache, dec_s, n_cache) * LOG2E
    bias_n = jnp.full((h_a * dec_s, n_pad), NEG, F32).at[:, :dec_s].set(
        _rel_bias_rows(rel_bias[l], 0, dec_s, dec_s) * LOG2E)
    o_a_s = _band_sample(sthree(ps["qa"]), heads_t(cache_a_k), heads_t(cache_a_v), sthree(ps["ka"]),
                         sthree(ps["va"]), bias_c, bias_n, bb=4)
    lf_new = ps["lft"].reshape(h_b, dec_b, dec_s).transpose(1, 0, 2)
    lf_all = jnp.concatenate([cache_b_logf[l].astype(F32).transpose(0, 2, 1), lf_new,
                              jnp.zeros((dec_b, h_b, n_pad - dec_s), F32)], axis=-1)
    c_s = _cumsum_lanes(lf_all.reshape(dec_b * h_b, past + n_pad), tri, rb=32)
    o_b_s = _forget_sample(sthree(ps["qb"]), heads_t(cache_b_k), heads_t(cache_b_v), sthree(ps["kb"]),
                           sthree(ps["vb_f"]), c_s.reshape(dec_b, h_b, past + n_pad), n_pad=n_pad)
    o_m_s = _mem_attn(sthree(ps["qm"]), cache_mem_k[l].reshape(dec_b, n_mem * h_m, dh_m),
                      cache_mem_v[l].reshape(dec_b, n_mem * h_m, dh_m), heads=h_m, tq=dec_s, bb=8)
    y_sample = _post(xs, o_a_s.reshape(-1, w_a), o_b_s.reshape(-1, w_b), o_m_s.reshape(-1, w_m), *post_w,
                     tm=512).reshape(dec_b, dec_s, d)

    lead = lambda a, *shape: a.reshape((depth,) + shape)
    frames_major = lambda a: jnp.transpose(a, (0, 3, 1, 2))[None]
    return (y_prompt, y_sample,
            frames_major(pp["ka_f"]), frames_major(pp["va_f"]),
            frames_major(pp["kb_f"]), frames_major(pp["vb_f"]),
            lf_p.transpose(0, 2, 1)[None],
            lead(mk_f, batch, n_mem, h_m, dh_m), lead(mv_f, batch, n_mem, h_m, dh_m),
            lead(ps["ka_f"], dec_b, dec_s, h_a, dh_a), lead(ps["va_f"], dec_b, dec_s, h_a, dh_a),
            lead(ps["kb_f"], dec_b, dec_s, h_b, dh_b), lead(ps["vb_f"], dec_b, dec_s, h_b, dh_b),
            lead(ps["lft"].T, dec_b, dec_s, h_b))
```

```python
import functools

import jax
import jax.numpy as jnp
from jax import lax
from jax.experimental import pallas as pl
from jax.experimental.pallas import tpu as pltpu

BF16 = jnp.bfloat16
F32 = jnp.float32

EPS = 1e-6
CHUNK = 64
REL_CLIP = 128
NEG = -1e30
LANE_GROUP = 256
LOG2E = 1.4426950408889634
V_ROWS = 64 + 16
MAX_ONE_PASS_BOUND = 40.0
VMEM_LIMIT = 56 * 1024 * 1024


def _cparams(*sem):
    return pltpu.CompilerParams(dimension_semantics=sem, vmem_limit_bytes=VMEM_LIMIT)


def _const_spec(shape):
    nd = len(shape)
    return pl.BlockSpec(shape, lambda *_: (0,) * nd, pipeline_mode=pl.Buffered(1))


def _nt_dot(a, b):
    return lax.dot_general(a, b, (((1,), (1,)), ((), ())), preferred_element_type=F32)


def _dot(a, b):
    return jnp.dot(a, b, preferred_element_type=F32)


def _rms_rows(x, gain):
    ms = jnp.mean(x * x, axis=-1, keepdims=True)
    return x * lax.rsqrt(ms + EPS) * gain


def _head_norm(z, bd_ref, gain):
    z2 = (z * z).astype(BF16)
    ms = jnp.concatenate(
        [_dot(z2[:, h * 256:(h + 1) * 256], bd_ref[...]) for h in range(z.shape[1] // 256)], axis=-1)
    return z * lax.rsqrt(ms + EPS) * gain


def _log_sigmoid(x):
    return jnp.minimum(x, 0.0) - jnp.log1p(jnp.exp(-jnp.abs(x)))


def _lane_group(width, head):
    return lax.broadcasted_iota(jnp.int32, (1, width), 1) // head


def _block_diag_rows(q4, head):
    grp = _lane_group(q4.shape[1], head)
    zero = jnp.zeros_like(q4)
    return jnp.concatenate([jnp.where(grp == g, q4, zero) for g in range(q4.shape[1] // head)], axis=0)


def _pick_diag(o, rows, head):
    grp = _lane_group(o.shape[1], head)
    out = jnp.zeros((rows, o.shape[1]), o.dtype)
    for g in range(o.shape[1] // head):
        out = jnp.where(grp == g, o[g * rows:(g + 1) * rows], out)
    return out


def _store_heads_t(ref, z_t):
    dh = ref.shape[1]
    for h in range(ref.shape[0]):
        ref[h] = z_t[h * dh:(h + 1) * dh, :]


def _proj_kernel(x_ref, gmix_ref, wqkv_ref, wft_ref, wqm_ref, gains_ref, gqm_ref, bf_ref,
                 bd64_ref, bd128_ref, *out_refs, out_names, scale_a, scale_b, scale_m, keep_from,
                 tiles_per_group):
    o = dict(zip(out_names, out_refs))
    head_major = "vbt" in o
    xn = _rms_rows(x_ref[...], gmix_ref[...]).astype(BF16)

    def seg(j):
        return _dot(xn, wqkv_ref[:, j * 512:(j + 1) * 512])

    kb = _head_norm(seg(4), bd64_ref, gains_ref[3:4, :])
    o["kb"][...] = kb.astype(BF16)
    vb = seg(5)
    if head_major:
        _store_heads_t(o["kb_f"], kb.T)
        vb_t = vb.T
        _store_heads_t(o["vb_f"], vb_t)
        vbt = o["vbt"]
        head = vb_t.shape[0] // vbt.shape[0]
        pad = vbt.shape[1] - head
        ones_row = (lax.broadcasted_iota(jnp.int32, (pad, vb_t.shape[1]), 0) == 0).astype(BF16)
        for h in range(vbt.shape[0]):
            vbt[h, 0:head, :] = vb_t[h * head:(h + 1) * head, :].astype(BF16)
            vbt[h, head:head + pad, :] = ones_row
    else:
        o["kb_f"][...] = kb
        o["vb_f"][...] = vb
    qb = _head_norm(seg(3), bd64_ref, gains_ref[2:3, :])
    o["qb"][...] = (qb * scale_b).astype(BF16)
    qa = _head_norm(seg(0), bd64_ref, gains_ref[0:1, :])
    o["qa"][...] = (qa * scale_a).astype(BF16)
    ka = _head_norm(seg(1), bd64_ref, gains_ref[1:2, :])
    o["ka"][...] = ka.astype(BF16)
    qm = _head_norm(_dot(xn, wqm_ref[...]), bd128_ref, gqm_ref[...])
    o["qm"][...] = (qm * scale_m).astype(BF16)
    flt = _nt_dot(wft_ref[...], xn)
    o["lft"][...] = _log_sigmoid(flt[0:8, :] + bf_ref[...])
    va = seg(2)
    o["va"][...] = va.astype(BF16)
    if head_major:
        @pl.when(pl.program_id(0) % tiles_per_group >= keep_from)
        def _():
            _store_heads_t(o["ka_f"], ka.T)
            _store_heads_t(o["va_f"], va.T)
    else:
        o["ka_f"][...] = ka
        o["va_f"][...] = va


def _proj(x, gmix, wqkv, wft, wqm, gains, gqm, bf, bd64, bd128, *, tm, group_rows=None, keep_rows=None):
    rows, d = x.shape
    n_tiles = rows // tm
    row = lambda i: (i, 0)
    wide = lambda dt: (jax.ShapeDtypeStruct((rows, 512), dt), pl.BlockSpec((tm, 512), row))
    outs = {n: wide(BF16) for n in ("qa", "ka", "va", "qb", "kb", "qm")}
    outs["lft"] = (jax.ShapeDtypeStruct((8, rows), F32), pl.BlockSpec((8, tm), lambda i: (0, i)))
    tpg = keep_from = 1
    if group_rows is None:
        for n in ("ka_f", "va_f", "kb_f", "vb_f"):
            outs[n] = wide(F32)
    else:
        assert keep_rows == tm
        tpg = group_rows // tm
        keep_from = tpg - 1
        groups = rows // group_rows
        kept = (jax.ShapeDtypeStruct((groups, 8, 64, keep_rows), F32),
                pl.BlockSpec((None, 8, 64, tm), lambda i: (i // tpg, 0, 0, 0)))
        full = (jax.ShapeDtypeStruct((groups, 8, 64, group_rows), F32),
                pl.BlockSpec((None, 8, 64, tm), lambda i: (i // tpg, 0, 0, i % tpg)))
        outs.update(ka_f=kept, va_f=kept, kb_f=full, vb_f=full)
        outs["vbt"] = (jax.ShapeDtypeStruct((n_tiles, 8, V_ROWS, tm), BF16),
                       pl.BlockSpec((None, 8, V_ROWS, tm), lambda i: (i, 0, 0, 0)))
    names = tuple(outs)
    in_specs = [pl.BlockSpec((tm, d), row)] + [_const_spec(a.shape) for a in
                                               (gmix, wqkv, wft, wqm, gains, gqm, bf, bd64, bd128)]
    kern = functools.partial(_proj_kernel, out_names=names, scale_a=LOG2E * 64 ** -0.5,
                             scale_b=LOG2E * 64 ** -0.5, scale_m=LOG2E * 128 ** -0.5,
                             keep_from=keep_from, tiles_per_group=tpg)
    res = pl.pallas_call(kern, out_shape=[outs[n][0] for n in names], grid=(n_tiles,), in_specs=in_specs,
                         out_specs=[outs[n][1] for n in names], compiler_params=_cparams("arbitrary"),
                         name="proj")(x, gmix, wqkv, wft, wqm, gains, gqm, bf, bd64, bd128)
    return dict(zip(names, res))


def _memkv_kernel(m_ref, gmem_ref, w_ref, gkm_ref, bd128_ref, mk_o, mv_o, *, heads):
    tm = m_ref.shape[0]
    xn = _rms_rows(m_ref[...], gmem_ref[...]).astype(BF16)
    half = w_ref.shape[1] // 2
    dh = half // heads
    mk = _head_norm(_dot(xn, w_ref[:, :half]), bd128_ref, gkm_ref[...])
    mv = _dot(xn, w_ref[:, half:])
    for h in range(heads):
        mk_o[pl.ds(h, tm, stride=heads), :] = mk[:, h * dh:(h + 1) * dh]
        mv_o[pl.ds(h, tm, stride=heads), :] = mv[:, h * dh:(h + 1) * dh]


def _memkv(mem, gmem, w, gkm, bd128, *, tm, heads):
    rows, d = mem.shape
    dh = w.shape[1] // 2 // heads
    row = lambda i: (i, 0)
    return pl.pallas_call(
        functools.partial(_memkv_kernel, heads=heads),
        out_shape=[jax.ShapeDtypeStruct((rows * heads, dh), F32)] * 2, grid=(rows // tm,),
        in_specs=[pl.BlockSpec((tm, d), row)] + [_const_spec(a.shape) for a in (gmem, w, gkm, bd128)],
        out_specs=[pl.BlockSpec((tm * heads, dh), row)] * 2, compiler_params=_cparams("arbitrary"),
        name="memkv")(mem, gmem, w, gkm, bd128)


def _cumsum_kernel(x_ref, tri_ref, o_ref):
    rows, length = x_ref.shape
    tri = tri_ref[...]

    def local(c):
        x = x_ref[:, c * 128:(c + 1) * 128]
        hi = x.astype(BF16)
        r1 = x - hi.astype(F32)
        mid = r1.astype(BF16)
        lo = (r1 - mid.astype(F32)).astype(BF16)
        return _dot(hi, tri) + _dot(mid, tri) + _dot(lo, tri)

    chunks = [local(c) for c in range(length // 128)]
    carry = jnp.zeros((rows, 1), F32)
    for c, cc in enumerate(chunks):
        o_ref[:, c * 128:(c + 1) * 128] = cc + carry
        carry = carry + cc[:, 127:128]


def _cumsum_lanes(x, tri, *, rb):
    rows, length = x.shape
    return pl.pallas_call(
        _cumsum_kernel, out_shape=jax.ShapeDtypeStruct((rows, length), F32), grid=(rows // rb,),
        in_specs=[pl.BlockSpec((rb, length), lambda i: (i, 0)), _const_spec(tri.shape)],
        out_specs=pl.BlockSpec((rb, length), lambda i: (i, 0)), compiler_params=_cparams("arbitrary"),
        name="cumsum")(x, tri)


def _band_kernel(bound_ref, q_ref, k0_ref, k1_ref, k2_ref, v0_ref, v1_ref, v2_ref, bias_ref, o_ref,
                 k_sc, v_sc, *, reach, lookahead):
    tq = q_ref.shape[0]
    band = reach + CHUNK
    i = pl.program_id(1)
    for n, (kr, vr) in enumerate(((k0_ref, v0_ref), (k1_ref, v1_ref), (k2_ref, v2_ref))):
        k_sc[n * tq:(n + 1) * tq, :] = kr[...]
        v_sc[n * tq:(n + 1) * tq, :] = vr[...]
    col = lax.broadcasted_iota(jnp.int32, (1, band), 1)
    units = [(j, hg) for j in range(tq // CHUNK) for hg in range(q_ref.shape[1] // LANE_GROUP)]

    def scores(j, hg):
        r0 = 2 * tq + j * CHUNK - reach
        lanes = slice(hg * LANE_GROUP, (hg + 1) * LANE_GROUP)
        qbd = _block_diag_rows(q_ref[j * CHUNK:(j + 1) * CHUNK, lanes], CHUNK)
        s = _nt_dot(qbd, k_sc[r0:r0 + band, lanes]) + bias_ref[hg * 4 * CHUNK:(hg + 1) * 4 * CHUNK, :]
        return jnp.where((i - 2) * tq + r0 + col >= 0, s, NEG)

    def attend(one_pass):
        pending = [scores(*u) for u in units[:lookahead]]
        for n, (j, hg) in enumerate(units):
            s = pending.pop(0)
            if n + lookahead < len(units):
                pending.append(scores(*units[n + lookahead]))
            r0 = 2 * tq + j * CHUNK - reach
            lanes = slice(hg * LANE_GROUP, (hg + 1) * LANE_GROUP)
            p = jnp.exp2(s if one_pass else s - jnp.max(s, axis=-1, keepdims=True))
            l = jnp.sum(p, axis=-1, keepdims=True)
            o = _dot(p.astype(BF16), v_sc[r0:r0 + band, lanes]) / l
            o_ref[j * CHUNK:(j + 1) * CHUNK, lanes] = _pick_diag(o, CHUNK, CHUNK).astype(o_ref.dtype)

    lax.cond(bound_ref[0] <= MAX_ONE_PASS_BOUND, lambda: attend(True), lambda: attend(False))


def _band_prompt(logit_bound, q, k, v, bias, *, reach, tq):
    b, t, w = q.shape
    assert reach == 2 * tq and tq % CHUNK == 0
    blk = lambda off: pl.BlockSpec((None, tq, w), lambda bi, i: (bi, jnp.maximum(i - off, 0), 0))
    return pl.pallas_call(
        functools.partial(_band_kernel, reach=reach, lookahead=2),
        out_shape=jax.ShapeDtypeStruct((b, t, w), BF16), grid=(b, t // tq),
        in_specs=[pl.BlockSpec(memory_space=pltpu.SMEM), blk(0), blk(2), blk(1), blk(0), blk(2), blk(1), blk(0),
                  _const_spec(bias.shape)],
        out_specs=blk(0),
        scratch_shapes=[pltpu.VMEM((3 * tq, w), BF16), pltpu.VMEM((3 * tq, w), BF16)],
        compiler_params=_cparams("arbitrary", "arbitrary"), name="band_prompt")(
            logit_bound, q, k, k, k, v, v, v, bias)


def _forget_kernel(bound_ref, q_ref, k_ref, vt_ref, c_ref, o_ref, qt_sc, m_sc, acc_sc, *, head, lookahead):
    tq, width = q_ref.shape
    n_heads = width // head
    i = pl.program_id(2)
    logit_bound = bound_ref[0]
    q_t = q_ref[...].astype(F32).T
    row_grp = lax.broadcasted_iota(jnp.int32, (width, 1), 0) // head
    for g in range(n_heads):
        qt_sc[g] = jnp.where(row_grp == g, q_t, 0.0).astype(BF16)
    m_sc[...] = jnp.full(m_sc.shape, NEG, F32)
    acc_sc[...] = jnp.zeros(acc_sc.shape, F32)
    c_first = c_ref[pl.ds(pl.multiple_of(i * tq, tq), 8), :][0:1, :]

    def step(j, masked):
        k0 = pl.multiple_of(j * tq, tq)
        kt = k_ref[pl.ds(k0, tq), :]
        bias = (c_first - c_ref[pl.ds(k0, tq), :]) * LOG2E
        if masked:
            keep = (lax.broadcasted_iota(jnp.int32, (tq, tq), 0)
                    <= lax.broadcasted_iota(jnp.int32, (tq, tq), 1))

        def scores(g):
            s = _dot(kt, qt_sc[g]) + bias[:, g:g + 1]
            return jnp.where(keep, s, NEG) if masked else s

        pending = [scores(g) for g in range(min(lookahead, n_heads))]
        for g in range(n_heads):
            s = pending.pop(0)
            if g + lookahead < n_heads:
                pending.append(scores(g + lookahead))
            m_prev = m_sc[g]
            m_new = jnp.maximum(m_prev, jnp.max(s, axis=0, keepdims=True))
            p = jnp.exp2(s - m_new).astype(BF16)
            acc_sc[g] = jnp.exp2(m_prev - m_new) * acc_sc[g] + _dot(vt_ref[j, g], p)
            m_sc[g] = m_new

    def bounded_tiles(tiles):
        m_tile, logits = [], []
        for j in tiles:
            k0 = pl.multiple_of(j * tq, tq)
            c_tile = c_ref[pl.ds(k0, tq), :]
            c_last = c_tile[tq - 1:tq, :]
            m_tile.append((c_first - c_last) * LOG2E)
            shift = (c_last - c_tile) * LOG2E
            logits.append(functools.partial(
                lambda g, k0, shift: _dot(k_ref[pl.ds(k0, tq), :], qt_sc[g]) + shift[:, g:g + 1],
                k0=k0, shift=shift))
        units = [(t, g) for t in range(len(tiles)) for g in range(n_heads)]
        pending = [logits[t](g) for t, g in units[:lookahead]]
        for n, (t, g) in enumerate(units):
            s = pending.pop(0)
            if n + lookahead < len(units):
                tn, gn = units[n + lookahead]
                pending.append(logits[tn](gn))
            m_new = jnp.broadcast_to(m_tile[t][:, g:g + 1], (1, tq))
            p = jnp.exp2(s).astype(BF16)
            acc_sc[g] = jnp.exp2(m_sc[g] - m_new) * acc_sc[g] + _dot(vt_ref[tiles[t], g], p)
            m_sc[g] = m_new

    def bounded_loop():
        def pair(jj, carry):
            bounded_tiles([2 * jj, 2 * jj + 1])
            return carry
        lax.fori_loop(0, i // 2, pair, 0)

        @pl.when(i % 2 == 1)
        def _():
            bounded_tiles([i - 1])

    def two_pass_loop():
        def body(j, carry):
            step(j, False)
            return carry
        lax.fori_loop(0, i, body, 0)

    lax.cond(logit_bound <= MAX_ONE_PASS_BOUND, bounded_loop, two_pass_loop)
    step(i, True)
    o_t = jnp.concatenate([acc_sc[g, 0:head, :] / acc_sc[g, head:head + 1, :] for g in range(n_heads)],
                          axis=0)
    o_ref[...] = o_t.T.astype(o_ref.dtype)


def _forget_prompt(logit_bound, q, k, vt, c, *, head, tq):
    b, t, w = q.shape
    n_heads = LANE_GROUP // head
    v_rows = vt.shape[3]
    lookahead = 3
    assert lookahead <= n_heads
    return pl.pallas_call(
        functools.partial(_forget_kernel, head=head, lookahead=lookahead),
        out_shape=jax.ShapeDtypeStruct((b, t, w), BF16), grid=(b, w // LANE_GROUP, t // tq),
        in_specs=[pl.BlockSpec(memory_space=pltpu.SMEM),
                  pl.BlockSpec((None, tq, LANE_GROUP), lambda bi, hg, i: (bi, i, hg)),
                  pl.BlockSpec((None, t, LANE_GROUP), lambda bi, hg, i: (bi, 0, hg)),
                  pl.BlockSpec((None, t // tq, n_heads, v_rows, tq), lambda bi, hg, i: (bi, 0, hg, 0, 0)),
                  pl.BlockSpec((None, None, t, n_heads), lambda bi, hg, i: (bi, hg, 0, 0))],
        out_specs=pl.BlockSpec((None, tq, LANE_GROUP), lambda bi, hg, i: (bi, i, hg)),
        scratch_shapes=[pltpu.VMEM((n_heads, LANE_GROUP, tq), BF16), pltpu.VMEM((n_heads, 1, tq), F32),
                        pltpu.VMEM((n_heads, v_rows, tq), F32)],
        compiler_params=_cparams("arbitrary", "arbitrary", "arbitrary"), name="forget_prompt")(
            logit_bound, q, k, vt, c)


def _mem_attention(qs, mk_refs, mv_refs, heads):
    n_mem = mk_refs[0].shape[0] // heads
    dh = mk_refs[0].shape[1]
    units = [(e, h) for e in range(len(qs)) for h in range(heads)]
    rows = lambda ref, h: ref[pl.ds(h, n_mem, stride=heads), :].astype(BF16)
    s = [_nt_dot(qs[e][:, h * dh:(h + 1) * dh], rows(mk_refs[e], h)) for e, h in units]
    p, inv_l = [], []
    for u in range(len(units)):
        ex = jnp.exp2(s[u] - jnp.max(s[u], axis=-1, keepdims=True))
        inv_l.append(1.0 / jnp.sum(ex, axis=-1, keepdims=True))
        p.append(ex.astype(BF16))
    o = [(_dot(p[u], rows(mv_refs[e], h)) * inv_l[u]).astype(BF16) for u, (e, h) in enumerate(units)]
    return [jnp.concatenate(o[e * heads:(e + 1) * heads], axis=-1) for e in range(len(qs))]


def _mem_kernel(q_ref, mk_ref, mv_ref, o_ref, *, heads):
    bb = q_ref.shape[0]
    outs = _mem_attention([q_ref[e] for e in range(bb)], [mk_ref.at[e] for e in range(bb)],
                          [mv_ref.at[e] for e in range(bb)], heads)
    for e in range(bb):
        o_ref[e] = outs[e]


def _mem_attn(q, mk, mv, *, heads, tq, bb):
    b, t, w = q.shape
    kv = pl.BlockSpec((bb,) + mk.shape[1:], lambda bi, i: (bi, 0, 0))
    qs = pl.BlockSpec((bb, tq, w), lambda bi, i: (bi, i, 0))
    return pl.pallas_call(
        functools.partial(_mem_kernel, heads=heads), out_shape=jax.ShapeDtypeStruct((b, t, w), BF16),
        grid=(b // bb, t // tq), in_specs=[qs, kv, kv], out_specs=qs,
        compiler_params=_cparams("arbitrary", "arbitrary"), name="mem_attn")(q, mk, mv)


def _post_kernel(x_ref, oa_ref, ob_ref, *refs, mem_heads, col_chunk, ff_chunk):
    if mem_heads:
        qm_ref, mk_ref, mv_ref, *refs = refs
        o_m, = _mem_attention([qm_ref[...]], [mk_ref], [mv_ref], mem_heads)
    else:
        om_ref, *refs = refs
        o_m = om_ref[...]
    gmix_ref, wg_ref, wpa_ref, wpb_ref, wpm_ref, wo_ref, gffn_ref, wup_ref, wdn_ref, y_ref = refs
    x = x_ref[...]
    d = x.shape[1]
    xn = _rms_rows(x, gmix_ref[...]).astype(BF16)
    branches = ((oa_ref[...], wpa_ref), (ob_ref[...], wpb_ref), (o_m, wpm_ref))
    h = []
    for c in range(d // col_chunk):
        cols = slice(c * col_chunk, (c + 1) * col_chunk)
        hc = 0.0
        for b, (o_b, wp_ref) in enumerate(branches):
            gate = jax.nn.sigmoid(_dot(xn, wg_ref[:, b * d + c * col_chunk:b * d + (c + 1) * col_chunk]))
            hc = hc + gate * _dot(o_b, wp_ref[:, cols])
        h.append(hc.astype(BF16))
    x1 = x + _dot(jnp.concatenate(h, axis=-1), wo_ref[...])
    xn1 = _rms_rows(x1, gffn_ref[...]).astype(BF16)
    y = x1
    for c in range(wup_ref.shape[1] // ff_chunk):
        cols = slice(c * ff_chunk, (c + 1) * ff_chunk)
        u = jnp.maximum(_dot(xn1, wup_ref[:, cols]), 0.0)
        y = y + _dot((u * u).astype(BF16), wdn_ref[cols, :])
    y_ref[...] = y


def _post(x, oa, ob, om, gmix, wg, wpa, wpb, wpm, wo, gffn, wup, wdn, *, tm, mem=None):
    rows, d = x.shape
    row = lambda i: (i, 0)
    acts = [x, oa, ob]
    specs = [pl.BlockSpec((tm, a.shape[1]), row) for a in acts]
    heads = 0
    if om is None:
        qm, mk, mv, heads = mem
        tiles_per_batch = rows // mk.shape[0] // tm
        kv = pl.BlockSpec((None,) + mk.shape[1:], lambda i: (i // tiles_per_batch, 0, 0))
        acts += [qm, mk, mv]
        specs += [pl.BlockSpec((tm, qm.shape[1]), row), kv, kv]
    else:
        acts.append(om)
        specs.append(pl.BlockSpec((tm, om.shape[1]), row))
    weights = (gmix, wg, wpa, wpb, wpm, wo, gffn, wup, wdn)
    return pl.pallas_call(
        functools.partial(_post_kernel, mem_heads=heads, col_chunk=512, ff_chunk=1024),
        out_shape=jax.ShapeDtypeStruct((rows, d), F32), grid=(rows // tm,),
        in_specs=specs + [_const_spec(a.shape) for a in weights],
        out_specs=pl.BlockSpec((tm, d), row), compiler_params=_cparams("arbitrary"),
        name="post")(*acts, *weights)


def _pad_rows(x, rows):
    return jnp.concatenate([x, jnp.zeros((rows - x.shape[0], x.shape[1]), x.dtype)], axis=0)


def _cached_attention(q_ref, kc_ref, vc_ref, kn_ref, vn_ref, o_ref, bias_c, bias_n, n_pad):
    bb, heads, head, _ = kc_ref.shape
    units = [(e, h, slice(h * head, (h + 1) * head)) for e in range(bb) for h in range(heads)]
    sc = [_dot(q_ref[e, :, ln], kc_ref[e, h].astype(BF16)) + bias_c(e, h) for e, h, ln in units]
    sn = [_nt_dot(q_ref[e, :, ln], _pad_rows(kn_ref[e, :, ln].astype(BF16), n_pad)) + bias_n(e, h)
          for e, h, ln in units]
    pc, pn, inv_l = [], [], []
    for u in range(len(units)):
        m = jnp.maximum(jnp.max(sc[u], axis=-1, keepdims=True), jnp.max(sn[u], axis=-1, keepdims=True))
        pc.append(jnp.exp2(sc[u] - m))
        pn.append(jnp.exp2(sn[u] - m))
        inv_l.append(1.0 / (jnp.sum(pc[u], axis=-1, keepdims=True) + jnp.sum(pn[u], axis=-1, keepdims=True)))
    for u, (e, h, ln) in enumerate(units):
        o = (_nt_dot(pc[u].astype(BF16), vc_ref[e, h].astype(BF16))
             + _dot(pn[u].astype(BF16), _pad_rows(vn_ref[e, :, ln].astype(BF16), n_pad)))
        o_ref[e, :, ln] = (o * inv_l[u]).astype(o_ref.dtype)


def _band_sample_kernel(q_ref, kc_ref, vc_ref, kn_ref, vn_ref, bc_ref, bn_ref, o_ref):
    s_len = q_ref.shape[1]
    rows = lambda h: slice(h * s_len, (h + 1) * s_len)
    _cached_attention(q_ref, kc_ref, vc_ref, kn_ref, vn_ref, o_ref,
                      lambda e, h: bc_ref[rows(h), :], lambda e, h: bn_ref[rows(h), :], bn_ref.shape[1])


def _band_sample(q, kct, vct, kn, vn, bias_c, bias_n, *, bb):
    b, s_len, w = q.shape
    new = pl.BlockSpec((bb, s_len, w), lambda bi: (bi, 0, 0))
    cache = pl.BlockSpec((bb,) + kct.shape[1:], lambda bi: (bi, 0, 0, 0))
    return pl.pallas_call(
        _band_sample_kernel, out_shape=jax.ShapeDtypeStruct((b, s_len, w), BF16), grid=(b // bb,),
        in_specs=[new, cache, cache, new, new, _const_spec(bias_c.shape), _const_spec(bias_n.shape)],
        out_specs=new, compiler_params=_cparams("arbitrary"), name="band_sample")(
            q, kct, vct, kn, vn, bias_c, bias_n)


def _forget_sample_kernel(q_ref, kc_ref, vc_ref, kn_ref, vn_ref, c_ref, o_ref, *, n_pad):
    s_len = q_ref.shape[1]
    p_len = kc_ref.shape[3]
    causal = (lax.broadcasted_iota(jnp.int32, (s_len, n_pad), 1)
              <= lax.broadcasted_iota(jnp.int32, (s_len, n_pad), 0))
    ch = [(c_ref[e, :, p_len - 1:p_len] - c_ref[e]) * LOG2E for e in range(c_ref.shape[0])]
    _cached_attention(q_ref, kc_ref, vc_ref, kn_ref, vn_ref, o_ref,
                      lambda e, h: ch[e][h:h + 1, 0:p_len],
                      lambda e, h: jnp.where(causal, ch[e][h:h + 1, p_len:p_len + n_pad], NEG), n_pad)


def _forget_sample(q, kct, vct, kn, vn, c, *, n_pad):
    b, s_len, w = q.shape
    new = pl.BlockSpec((1, s_len, w), lambda bi: (bi, 0, 0))
    cache = pl.BlockSpec((1,) + kct.shape[1:], lambda bi: (bi, 0, 0, 0))
    cs = pl.BlockSpec((1,) + c.shape[1:], lambda bi: (bi, 0, 0))
    return pl.pallas_call(
        functools.partial(_forget_sample_kernel, n_pad=n_pad),
        out_shape=jax.ShapeDtypeStruct((b, s_len, w), BF16), grid=(b,),
        in_specs=[new, cache, cache, new, new, cs], out_specs=new,
        compiler_params=_cparams("arbitrary"), name="forget_sample")(q, kct, vct, kn, vn, c)


def _qk_bound(g_q, g_k, dh):
    return 1.02 * LOG2E * dh ** 0.5 * jnp.max(jnp.abs(g_q)) * jnp.max(jnp.abs(g_k))


def _block_diag_mean(head, size=256):
    r = jnp.arange(size) // head
    return jnp.where(r[:, None] == r[None, :], 1.0 / head, 0.0).astype(BF16)


def _rel_bias_rows(rel_bias, d0, n_q, n_k):
    n = n_q + n_k - 1
    dist = d0 + (n_q - 1) - jnp.arange(n)
    e = rel_bias[jnp.clip(dist, -REL_CLIP, REL_CLIP) + REL_CLIP].T.astype(F32)
    h = e.shape[0]
    skew = jnp.tile(jnp.pad(e, ((0, 0), (0, 1))), (1, n_q))[:, :n_q * n].reshape(h, n_q, n)
    return skew[:, :, n_q - 1:].reshape(h * n_q, n_k)


def kernel(x_prompt, x_sample, mem_prompt, cache_a_k, cache_a_v, cache_b_k, cache_b_v, cache_b_logf,
           cache_mem_k, cache_mem_v, g_mix, w_in, b_f, g_qa, g_ka, g_qb, g_kb, g_qm, g_km, rel_bias,
           g_mem, w_mkv, w_pa, w_pb, w_pm, w_o, g_ffn, w_up, w_down):
    depth = w_in.shape[0]
    assert depth == 1
    batch, seq, d = x_prompt.shape
    dec_b, dec_s, _ = x_sample.shape
    _, _, n_cache, h_a, dh_a = cache_a_k.shape
    _, _, past, h_b, dh_b = cache_b_k.shape
    _, _, n_mem, h_m, dh_m = cache_mem_k.shape
    w_a, w_b, w_m = h_a * dh_a, h_b * dh_b, h_m * dh_m
    assert w_a == w_b == w_m == 512 and dh_a == dh_b == 64 and dh_m == 128 and h_b == 8
    keep_p = min(n_cache, seq)
    n_pad = 128
    l = 0

    qkv_end = 3 * w_a + 3 * w_b
    f_end = qkv_end + h_b
    qm_end = f_end + w_m
    w = w_in[l]
    wqkv = w[:, :qkv_end].astype(BF16)
    wft = jnp.zeros((16, d), BF16).at[:h_b].set(w[:, qkv_end:f_end].T.astype(BF16))
    wqm = w[:, f_end:qm_end].astype(BF16)
    wg = w[:, qm_end:].astype(BF16)
    gains = jnp.stack([jnp.tile(g[l], h_a) for g in (g_qa, g_ka, g_qb, g_kb)]).astype(F32)
    gqm = jnp.tile(g_qm[l], h_m)[None, :]
    gkm = jnp.tile(g_km[l], h_m)[None, :]
    bf = b_f[l][:, None].astype(F32)
    bd64 = _block_diag_mean(dh_a)
    bd128 = _block_diag_mean(dh_m)
    tri = (jnp.arange(128)[:, None] <= jnp.arange(128)[None, :]).astype(BF16)
    gmix = g_mix[l][None, :]
    proj_w = (gmix, wqkv, wft, wqm, gains, gqm, bf, bd64, bd128)
    wpa, wpb, wpm, wo = (a[l].astype(BF16) for a in (w_pa, w_pb, w_pm, w_o))
    wup, wdn = w_up[l].astype(BF16), w_down[l].astype(BF16)
    gffn = g_ffn[l][None, :]

    xp = x_prompt.reshape(batch * seq, d)
    pp = _proj(xp, *proj_w, tm=512, group_rows=seq, keep_rows=keep_p)
    mk_f, mv_f = _memkv(mem_prompt.reshape(batch * n_mem, d), g_mem[l][None, :], w_mkv[l].astype(BF16),
                        gkm, bd128, tm=256, heads=h_m)
    tq_b = 512
    lf_p = pp["lft"].reshape(h_b, batch, seq).transpose(1, 0, 2)
    c_p = _cumsum_lanes(lf_p.reshape(batch * h_b, seq), tri, rb=batch * h_b)
    c_p = c_p.reshape(batch, h_b // 4, 4, seq).transpose(0, 1, 3, 2)
    vt = pp["vbt"].reshape(batch, seq // tq_b, h_b, V_ROWS, tq_b)
    three = lambda a: a.reshape(batch, seq, -1)
    bias_p = _rel_bias_rows(rel_bias[l], n_cache, CHUNK, n_cache + CHUNK) * LOG2E
    bound_a = _qk_bound(g_qa[l], g_ka[l], dh_a) + LOG2E * jnp.max(jnp.abs(rel_bias[l]))
    o_a = _band_prompt(bound_a.reshape(1).astype(F32), three(pp["qa"]), three(pp["ka"]), three(pp["va"]),
                       bias_p, reach=n_cache, tq=256)
    o_b = _forget_prompt(_qk_bound(g_qb[l], g_kb[l], dh_b).reshape(1).astype(F32), three(pp["qb"]),
                         three(pp["kb"]), vt, c_p, head=dh_b, tq=tq_b)
    post_w = (gmix, wg, wpa, wpb, wpm, wo, gffn, wup, wdn)
    mem_p = (pp["qm"], mk_f.reshape(batch, n_mem * h_m, dh_m), mv_f.reshape(batch, n_mem * h_m, dh_m), h_m)
    y_prompt = _post(xp, o_a.reshape(-1, w_a), o_b.reshape(-1, w_b), None, *post_w, tm=512,
                     mem=mem_p).reshape(batch, seq, d)

    xs = x_sample.reshape(dec_b * dec_s, d)
    ps = _proj(xs, *proj_w, tm=512)
    sthree = lambda a: a.reshape(dec_b, dec_s, -1)
    heads_t = lambda a: jnp.transpose(a[l], (0, 2, 3, 1))
    bias_c = _rel_bias_rows(rel_bias[l], n_cache, dec_s, n_cache) * LOG2E
    bias_n = jnp.full((h_a * dec_s, n_pad), NEG, F32).at[:, :dec_s].set(
        _rel_bias_rows(rel_bias[l], 0, dec_s, dec_s) * LOG2E)
    o_a_s = _band_sample(sthree(ps["qa"]), heads_t(cache_a_k), heads_t(cache_a_v), sthree(ps["ka"]),
                         sthree(ps["va"]), bias_c, bias_n, bb=4)
    lf_new = ps["lft"].reshape(h_b, dec_b, dec_s).transpose(1, 0, 2)
    lf_all = jnp.concatenate([cache_b_logf[l].astype(F32).transpose(0, 2, 1), lf_new,
                              jnp.zeros((dec_b, h_b, n_pad - dec_s), F32)], axis=-1)
    c_s = _cumsum_lanes(lf_all.reshape(dec_b * h_b, past + n_pad), tri, rb=32)
    o_b_s = _forget_sample(sthree(ps["qb"]), heads_t(cache_b_k), heads_t(cache_b_v), sthree(ps["kb"]),
                           sthree(ps["vb_f"]), c_s.reshape(dec_b, h_b, past + n_pad), n_pad=n_pad)
    o_m_s = _mem_attn(sthree(ps["qm"]), cache_mem_k[l].reshape(dec_b, n_mem * h_m, dh_m),
                      cache_mem_v[l].reshape(dec_b, n_mem * h_m, dh_m), heads=h_m, tq=dec_s, bb=8)
    y_sample = _post(xs, o_a_s.reshape(-1, w_a), o_b_s.reshape(-1, w_b), o_m_s.reshape(-1, w_m), *post_w,
                     tm=512).reshape(dec_b, dec_s, d)

    lead = lambda a, *shape: a.reshape((depth,) + shape)
    frames_major = lambda a: jnp.transpose(a, (0, 3, 1, 2))[None]
    return (y_prompt, y_sample,
            frames_major(pp["ka_f"]), frames_major(pp["va_f"]),
            frames_major(pp["kb_f"]), frames_major(pp["vb_f"]),
            lf_p.transpose(0, 2, 1)[None],
            lead(mk_f, batch, n_mem, h_m, dh_m), lead(mv_f, batch, n_mem, h_m, dh_m),
            lead(ps["ka_f"], dec_b, dec_s, h_a, dh_a), lead(ps["va_f"], dec_b, dec_s, h_a, dh_a),
            lead(ps["kb_f"], dec_b, dec_s, h_b, dh_b), lead(ps["vb_f"], dec_b, dec_s, h_b, dh_b),
            lead(ps["lft"].T, dec_b, dec_s, h_b))
```

```python
import functools

import jax
import jax.numpy as jnp
from jax import lax
from jax.experimental import pallas as pl
from jax.experimental.pallas import tpu as pltpu

BF16 = jnp.bfloat16
F32 = jnp.float32

EPS = 1e-6
CHUNK = 64
REL_CLIP = 128
NEG = -1e30
LANE_GROUP = 256
LOG2E = 1.4426950408889634
V_ROWS = 64 + 16
MAX_ONE_PASS_BOUND = 40.0
VMEM_LIMIT = 56 * 1024 * 1024


def _cparams(*sem):
    return pltpu.CompilerParams(dimension_semantics=sem, vmem_limit_bytes=VMEM_LIMIT)


def _const_spec(shape):
    nd = len(shape)
    return pl.BlockSpec(shape, lambda *_: (0,) * nd, pipeline_mode=pl.Buffered(1))


def _nt_dot(a, b):
    return lax.dot_general(a, b, (((1,), (1,)), ((), ())), preferred_element_type=F32)


def _dot(a, b):
    return jnp.dot(a, b, preferred_element_type=F32)


def _rms_rows(x, gain):
    ms = jnp.mean(x * x, axis=-1, keepdims=True)
    return x * lax.rsqrt(ms + EPS) * gain


def _head_norm(z, bd_ref, gain):
    z2 = (z * z).astype(BF16)
    ms = jnp.concatenate(
        [_dot(z2[:, h * 256:(h + 1) * 256], bd_ref[...]) for h in range(z.shape[1] // 256)], axis=-1)
    return z * lax.rsqrt(ms + EPS) * gain


def _log_sigmoid(x):
    return jnp.minimum(x, 0.0) - jnp.log1p(jnp.exp(-jnp.abs(x)))


def _lane_group(width, head):
    return lax.broadcasted_iota(jnp.int32, (1, width), 1) // head


def _block_diag_rows(q4, head):
    grp = _lane_group(q4.shape[1], head)
    zero = jnp.zeros_like(q4)
    return jnp.concatenate([jnp.where(grp == g, q4, zero) for g in range(q4.shape[1] // head)], axis=0)


def _pick_diag(o, rows, head):
    grp = _lane_group(o.shape[1], head)
    out = jnp.zeros((rows, o.shape[1]), o.dtype)
    for g in range(o.shape[1] // head):
        out = jnp.where(grp == g, o[g * rows:(g + 1) * rows], out)
    return out


def _store_heads_t(ref, z_t):
    dh = ref.shape[1]
    for h in range(ref.shape[0]):
        ref[h] = z_t[h * dh:(h + 1) * dh, :]


def _proj_kernel(x_ref, gmix_ref, wqkv_ref, wft_ref, wqm_ref, gains_ref, gqm_ref, bf_ref,
                 bd64_ref, bd128_ref, *refs, out_names, scale_a, scale_b, scale_m, keep_from,
                 tiles_per_group, rider_pad):
    rider_in, out_refs = refs[:len(refs) - len(out_names)], refs[len(refs) - len(out_names):]
    o = dict(zip(out_names, out_refs))
    head_major = "vbt" in o
    xn = _rms_rows(x_ref[...], gmix_ref[...]).astype(BF16)

    def seg(j):
        return _dot(xn, wqkv_ref[:, j * 512:(j + 1) * 512])

    kb = _head_norm(seg(4), bd64_ref, gains_ref[3:4, :])
    o["kb"][...] = kb.astype(BF16)
    vb = seg(5)
    if head_major:
        _store_heads_t(o["kb_f"], kb.T)
        vb_t = vb.T
        _store_heads_t(o["vb_f"], vb_t)
        vbt = o["vbt"]
        head = vb_t.shape[0] // vbt.shape[0]
        pad = vbt.shape[1] - head
        ones_row = (lax.broadcasted_iota(jnp.int32, (pad, vb_t.shape[1]), 0) == 0).astype(BF16)
        for h in range(vbt.shape[0]):
            vbt[h, 0:head, :] = vb_t[h * head:(h + 1) * head, :].astype(BF16)
            vbt[h, head:head + pad, :] = ones_row
    else:
        o["kb_f"][...] = kb
        o["vb_f"][...] = vb
    qb = _head_norm(seg(3), bd64_ref, gains_ref[2:3, :])
    o["qb"][...] = (qb * scale_b).astype(BF16)
    qa = _head_norm(seg(0), bd64_ref, gains_ref[0:1, :])
    o["qa"][...] = (qa * scale_a).astype(BF16)
    ka = _head_norm(seg(1), bd64_ref, gains_ref[1:2, :])
    o["ka"][...] = ka.astype(BF16)
    qm = _head_norm(_dot(xn, wqm_ref[...]), bd128_ref, gqm_ref[...])
    o["qm"][...] = (qm * scale_m).astype(BF16)
    flt = _nt_dot(wft_ref[...], xn)
    o["lft"][...] = _log_sigmoid(flt[0:8, :] + bf_ref[...])
    va = seg(2)
    o["va"][...] = va.astype(BF16)
    if head_major:
        @pl.when(pl.program_id(0) % tiles_per_group >= keep_from)
        def _():
            _store_heads_t(o["ka_f"], ka.T)
            _store_heads_t(o["va_f"], va.T)
    else:
        o["ka_f"][...] = ka
        o["va_f"][...] = va
    if rider_in:
        _forget_sample_kernel(*rider_in, o["rider"], n_pad=rider_pad)


def _proj(x, gmix, wqkv, wft, wqm, gains, gqm, bf, bd64, bd128, *, tm, group_rows=None, keep_rows=None,
          rider=None, rider_pad=0):
    rows, d = x.shape
    n_tiles = rows // tm
    row = lambda i: (i, 0)
    wide = lambda dt: (jax.ShapeDtypeStruct((rows, 512), dt), pl.BlockSpec((tm, 512), row))
    outs = {n: wide(BF16) for n in ("qa", "ka", "va", "qb", "kb", "qm")}
    outs["lft"] = (jax.ShapeDtypeStruct((8, rows), F32), pl.BlockSpec((8, tm), lambda i: (0, i)))
    tpg = keep_from = 1
    if group_rows is None:
        for n in ("ka_f", "va_f", "kb_f", "vb_f"):
            outs[n] = wide(F32)
    else:
        assert keep_rows == tm
        tpg = group_rows // tm
        keep_from = tpg - 1
        groups = rows // group_rows
        kept = (jax.ShapeDtypeStruct((groups, 8, 64, keep_rows), F32),
                pl.BlockSpec((None, 8, 64, tm), lambda i: (i // tpg, 0, 0, 0)))
        full = (jax.ShapeDtypeStruct((groups, 8, 64, group_rows), F32),
                pl.BlockSpec((None, 8, 64, tm), lambda i: (i // tpg, 0, 0, i % tpg)))
        outs.update(ka_f=kept, va_f=kept, kb_f=full, vb_f=full)
        outs["vbt"] = (jax.ShapeDtypeStruct((n_tiles, 8, V_ROWS, tm), BF16),
                       pl.BlockSpec((None, 8, V_ROWS, tm), lambda i: (i, 0, 0, 0)))
    in_specs = [pl.BlockSpec((tm, d), row)] + [_const_spec(a.shape) for a in
                                               (gmix, wqkv, wft, wqm, gains, gqm, bf, bd64, bd128)]
    rider = tuple(rider or ())
    if rider:
        assert rider[0].shape[0] == n_tiles
        per_tile = lambda a: pl.BlockSpec((1,) + a.shape[1:], lambda i: (i,) + (0,) * (a.ndim - 1))
        in_specs += [per_tile(a) for a in rider]
        outs["rider"] = (jax.ShapeDtypeStruct(rider[0].shape, BF16), per_tile(rider[0]))
    names = tuple(outs)
    kern = functools.partial(_proj_kernel, out_names=names, scale_a=LOG2E * 64 ** -0.5,
                             scale_b=LOG2E * 64 ** -0.5, scale_m=LOG2E * 128 ** -0.5,
                             keep_from=keep_from, tiles_per_group=tpg, rider_pad=rider_pad)
    res = pl.pallas_call(kern, out_shape=[outs[n][0] for n in names], grid=(n_tiles,), in_specs=in_specs,
                         out_specs=[outs[n][1] for n in names], compiler_params=_cparams("arbitrary"),
                         name="proj")(x, gmix, wqkv, wft, wqm, gains, gqm, bf, bd64, bd128, *rider)
    return dict(zip(names, res))


def _memkv_kernel(m_ref, gmem_ref, w_ref, gkm_ref, bd128_ref, mk_o, mv_o, *, heads):
    tm = m_ref.shape[0]
    xn = _rms_rows(m_ref[...], gmem_ref[...]).astype(BF16)
    half = w_ref.shape[1] // 2
    dh = half // heads
    mk = _head_norm(_dot(xn, w_ref[:, :half]), bd128_ref, gkm_ref[...])
    mv = _dot(xn, w_ref[:, half:])
    for h in range(heads):
        mk_o[pl.ds(h, tm, stride=heads), :] = mk[:, h * dh:(h + 1) * dh]
        mv_o[pl.ds(h, tm, stride=heads), :] = mv[:, h * dh:(h + 1) * dh]


def _memkv(mem, gmem, w, gkm, bd128, *, tm, heads):
    rows, d = mem.shape
    dh = w.shape[1] // 2 // heads
    row = lambda i: (i, 0)
    return pl.pallas_call(
        functools.partial(_memkv_kernel, heads=heads),
        out_shape=[jax.ShapeDtypeStruct((rows * heads, dh), F32)] * 2, grid=(rows // tm,),
        in_specs=[pl.BlockSpec((tm, d), row)] + [_const_spec(a.shape) for a in (gmem, w, gkm, bd128)],
        out_specs=[pl.BlockSpec((tm * heads, dh), row)] * 2, compiler_params=_cparams("arbitrary"),
        name="memkv")(mem, gmem, w, gkm, bd128)


def _cumsum_kernel(x_ref, tri_ref, o_ref):
    rows, length = x_ref.shape
    tri = tri_ref[...]

    def local(c):
        x = x_ref[:, c * 128:(c + 1) * 128]
        hi = x.astype(BF16)
        r1 = x - hi.astype(F32)
        mid = r1.astype(BF16)
        lo = (r1 - mid.astype(F32)).astype(BF16)
        return _dot(hi, tri) + _dot(mid, tri) + _dot(lo, tri)

    chunks = [local(c) for c in range(length // 128)]
    carry = jnp.zeros((rows, 1), F32)
    for c, cc in enumerate(chunks):
        o_ref[:, c * 128:(c + 1) * 128] = cc + carry
        carry = carry + cc[:, 127:128]


def _cumsum_lanes(x, tri, *, rb):
    rows, length = x.shape
    return pl.pallas_call(
        _cumsum_kernel, out_shape=jax.ShapeDtypeStruct((rows, length), F32), grid=(rows // rb,),
        in_specs=[pl.BlockSpec((rb, length), lambda i: (i, 0)), _const_spec(tri.shape)],
        out_specs=pl.BlockSpec((rb, length), lambda i: (i, 0)), compiler_params=_cparams("arbitrary"),
        name="cumsum")(x, tri)


def _band_kernel(bound_ref, q_ref, k0_ref, k1_ref, k2_ref, v0_ref, v1_ref, v2_ref, bias_ref, o_ref,
                 k_sc, v_sc, *, reach, lookahead):
    tq = q_ref.shape[0]
    band = reach + CHUNK
    i = pl.program_id(1)
    for n, (kr, vr) in enumerate(((k0_ref, v0_ref), (k1_ref, v1_ref), (k2_ref, v2_ref))):
        k_sc[n * tq:(n + 1) * tq, :] = kr[...]
        v_sc[n * tq:(n + 1) * tq, :] = vr[...]
    col = lax.broadcasted_iota(jnp.int32, (1, band), 1)
    units = [(j, hg) for j in range(tq // CHUNK) for hg in range(q_ref.shape[1] // LANE_GROUP)]

    def scores(j, hg):
        r0 = 2 * tq + j * CHUNK - reach
        lanes = slice(hg * LANE_GROUP, (hg + 1) * LANE_GROUP)
        qbd = _block_diag_rows(q_ref[j * CHUNK:(j + 1) * CHUNK, lanes], CHUNK)
        s = _nt_dot(qbd, k_sc[r0:r0 + band, lanes]) + bias_ref[hg * 4 * CHUNK:(hg + 1) * 4 * CHUNK, :]
        return jnp.where((i - 2) * tq + r0 + col >= 0, s, NEG)

    def attend(one_pass):
        pending = [scores(*u) for u in units[:lookahead]]
        for n, (j, hg) in enumerate(units):
            s = pending.pop(0)
            if n + lookahead < len(units):
                pending.append(scores(*units[n + lookahead]))
            r0 = 2 * tq + j * CHUNK - reach
            lanes = slice(hg * LANE_GROUP, (hg + 1) * LANE_GROUP)
            p = jnp.exp2(s if one_pass else s - jnp.max(s, axis=-1, keepdims=True))
            l = jnp.sum(p, axis=-1, keepdims=True)
            o = _dot(p.astype(BF16), v_sc[r0:r0 + band, lanes]) / l
            o_ref[j * CHUNK:(j + 1) * CHUNK, lanes] = _pick_diag(o, CHUNK, CHUNK).astype(o_ref.dtype)

    lax.cond(bound_ref[0] <= MAX_ONE_PASS_BOUND, lambda: attend(True), lambda: attend(False))


def _band_prompt(logit_bound, q, k, v, bias, *, reach, tq):
    b, t, w = q.shape
    assert reach == 2 * tq and tq % CHUNK == 0
    blk = lambda off: pl.BlockSpec((None, tq, w), lambda bi, i: (bi, jnp.maximum(i - off, 0), 0))
    return pl.pallas_call(
        functools.partial(_band_kernel, reach=reach, lookahead=2),
        out_shape=jax.ShapeDtypeStruct((b, t, w), BF16), grid=(b, t // tq),
        in_specs=[pl.BlockSpec(memory_space=pltpu.SMEM), blk(0), blk(2), blk(1), blk(0), blk(2), blk(1), blk(0),
                  _const_spec(bias.shape)],
        out_specs=blk(0),
        scratch_shapes=[pltpu.VMEM((3 * tq, w), BF16), pltpu.VMEM((3 * tq, w), BF16)],
        compiler_params=_cparams("arbitrary", "arbitrary"), name="band_prompt")(
            logit_bound, q, k, k, k, v, v, v, bias)


def _forget_kernel(bound_ref, q_ref, k_ref, vt_ref, c_ref, o_ref, qt_sc, m_sc, acc_sc, *, head, lookahead):
    tq, width = q_ref.shape
    n_heads = width // head
    i = pl.program_id(2)
    logit_bound = bound_ref[0]
    q_t = q_ref[...].astype(F32).T
    row_grp = lax.broadcasted_iota(jnp.int32, (width, 1), 0) // head
    for g in range(n_heads):
        qt_sc[g] = jnp.where(row_grp == g, q_t, 0.0).astype(BF16)
    m_sc[...] = jnp.full(m_sc.shape, NEG, F32)
    acc_sc[...] = jnp.zeros(acc_sc.shape, F32)
    c_first = c_ref[pl.ds(pl.multiple_of(i * tq, tq), 8), :][0:1, :]

    def step(j, masked):
        k0 = pl.multiple_of(j * tq, tq)
        kt = k_ref[pl.ds(k0, tq), :]
        bias = (c_first - c_ref[pl.ds(k0, tq), :]) * LOG2E
        if masked:
            keep = (lax.broadcasted_iota(jnp.int32, (tq, tq), 0)
                    <= lax.broadcasted_iota(jnp.int32, (tq, tq), 1))

        def scores(g):
            s = _dot(kt, qt_sc[g]) + bias[:, g:g + 1]
            return jnp.where(keep, s, NEG) if masked else s

        pending = [scores(g) for g in range(min(lookahead, n_heads))]
        for g in range(n_heads):
            s = pending.pop(0)
            if g + lookahead < n_heads:
                pending.append(scores(g + lookahead))
            m_prev = m_sc[g]
            m_new = jnp.maximum(m_prev, jnp.max(s, axis=0, keepdims=True))
            p = jnp.exp2(s - m_new).astype(BF16)
            acc_sc[g] = jnp.exp2(m_prev - m_new) * acc_sc[g] + _dot(vt_ref[j, g], p)
            m_sc[g] = m_new

    def bounded_tiles(tiles):
        m_tile, logits = [], []
        for j in tiles:
            k0 = pl.multiple_of(j * tq, tq)
            c_tile = c_ref[pl.ds(k0, tq), :]
            c_last = c_tile[tq - 1:tq, :]
            m_tile.append((c_first - c_last) * LOG2E)
            shift = (c_last - c_tile) * LOG2E
            logits.append(functools.partial(
                lambda g, k0, shift: _dot(k_ref[pl.ds(k0, tq), :], qt_sc[g]) + shift[:, g:g + 1],
                k0=k0, shift=shift))
        units = [(t, g) for t in range(len(tiles)) for g in range(n_heads)]
        pending = [logits[t](g) for t, g in units[:lookahead]]
        for n, (t, g) in enumerate(units):
            s = pending.pop(0)
            if n + lookahead < len(units):
                tn, gn = units[n + lookahead]
                pending.append(logits[tn](gn))
            m_new = jnp.broadcast_to(m_tile[t][:, g:g + 1], (1, tq))
            p = jnp.exp2(s).astype(BF16)
            acc_sc[g] = jnp.exp2(m_sc[g] - m_new) * acc_sc[g] + _dot(vt_ref[tiles[t], g], p)
            m_sc[g] = m_new

    def bounded_loop():
        def pair(jj, carry):
            bounded_tiles([2 * jj, 2 * jj + 1])
            return carry
        lax.fori_loop(0, i // 2, pair, 0)

        @pl.when(i % 2 == 1)
        def _():
            bounded_tiles([i - 1])

    def two_pass_loop():
        def body(j, carry):
            step(j, False)
            return carry
        lax.fori_loop(0, i, body, 0)

    lax.cond(logit_bound <= MAX_ONE_PASS_BOUND, bounded_loop, two_pass_loop)
    step(i, True)
    o_t = jnp.concatenate([acc_sc[g, 0:head, :] / acc_sc[g, head:head + 1, :] for g in range(n_heads)],
                          axis=0)
    o_ref[...] = o_t.T.astype(o_ref.dtype)


def _forget_prompt(logit_bound, q, k, vt, c, *, head, tq):
    b, t, w = q.shape
    n_heads = LANE_GROUP // head
    v_rows = vt.shape[3]
    lookahead = 3
    assert lookahead <= n_heads
    return pl.pallas_call(
        functools.partial(_forget_kernel, head=head, lookahead=lookahead),
        out_shape=jax.ShapeDtypeStruct((b, t, w), BF16), grid=(b, w // LANE_GROUP, t // tq),
        in_specs=[pl.BlockSpec(memory_space=pltpu.SMEM),
                  pl.BlockSpec((None, tq, LANE_GROUP), lambda bi, hg, i: (bi, i, hg)),
                  pl.BlockSpec((None, t, LANE_GROUP), lambda bi, hg, i: (bi, 0, hg)),
                  pl.BlockSpec((None, t // tq, n_heads, v_rows, tq), lambda bi, hg, i: (bi, 0, hg, 0, 0)),
                  pl.BlockSpec((None, None, t, n_heads), lambda bi, hg, i: (bi, hg, 0, 0))],
        out_specs=pl.BlockSpec((None, tq, LANE_GROUP), lambda bi, hg, i: (bi, i, hg)),
        scratch_shapes=[pltpu.VMEM((n_heads, LANE_GROUP, tq), BF16), pltpu.VMEM((n_heads, 1, tq), F32),
                        pltpu.VMEM((n_heads, v_rows, tq), F32)],
        compiler_params=_cparams("arbitrary", "arbitrary", "arbitrary"), name="forget_prompt")(
            logit_bound, q, k, vt, c)


def _mem_attention(qs, mk_refs, mv_refs, heads):
    n_mem = mk_refs[0].shape[0] // heads
    dh = mk_refs[0].shape[1]
    units = [(e, h) for e in range(len(qs)) for h in range(heads)]
    rows = lambda ref, h: ref[pl.ds(h, n_mem, stride=heads), :].astype(BF16)
    s = [_nt_dot(qs[e][:, h * dh:(h + 1) * dh], rows(mk_refs[e], h)) for e, h in units]
    p, inv_l = [], []
    for u in range(len(units)):
        ex = jnp.exp2(s[u] - jnp.max(s[u], axis=-1, keepdims=True))
        inv_l.append(1.0 / jnp.sum(ex, axis=-1, keepdims=True))
        p.append(ex.astype(BF16))
    o = [(_dot(p[u], rows(mv_refs[e], h)) * inv_l[u]).astype(BF16) for u, (e, h) in enumerate(units)]
    return [jnp.concatenate(o[e * heads:(e + 1) * heads], axis=-1) for e in range(len(qs))]


def _mem_kernel(q_ref, mk_ref, mv_ref, o_ref, *, heads):
    bb = q_ref.shape[0]
    outs = _mem_attention([q_ref[e] for e in range(bb)], [mk_ref.at[e] for e in range(bb)],
                          [mv_ref.at[e] for e in range(bb)], heads)
    for e in range(bb):
        o_ref[e] = outs[e]


def _mem_attn(q, mk, mv, *, heads, tq, bb):
    b, t, w = q.shape
    kv = pl.BlockSpec((bb,) + mk.shape[1:], lambda bi, i: (bi, 0, 0))
    qs = pl.BlockSpec((bb, tq, w), lambda bi, i: (bi, i, 0))
    return pl.pallas_call(
        functools.partial(_mem_kernel, heads=heads), out_shape=jax.ShapeDtypeStruct((b, t, w), BF16),
        grid=(b // bb, t // tq), in_specs=[qs, kv, kv], out_specs=qs,
        compiler_params=_cparams("arbitrary", "arbitrary"), name="mem_attn")(q, mk, mv)


def _post_kernel(x_ref, oa_ref, ob_ref, *refs, mem_heads, col_chunk, ff_chunk):
    if mem_heads:
        qm_ref, mk_ref, mv_ref, *refs = refs
        o_m, = _mem_attention([qm_ref[...]], [mk_ref], [mv_ref], mem_heads)
    else:
        om_ref, *refs = refs
        o_m = om_ref[...]
    gmix_ref, wg_ref, wpa_ref, wpb_ref, wpm_ref, wo_ref, gffn_ref, wup_ref, wdn_ref, y_ref = refs
    x = x_ref[...]
    d = x.shape[1]
    xn = _rms_rows(x, gmix_ref[...]).astype(BF16)
    branches = ((oa_ref[...], wpa_ref), (ob_ref[...], wpb_ref), (o_m, wpm_ref))
    h = []
    for c in range(d // col_chunk):
        cols = slice(c * col_chunk, (c + 1) * col_chunk)
        hc = 0.0
        for b, (o_b, wp_ref) in enumerate(branches):
            gate = jax.nn.sigmoid(_dot(xn, wg_ref[:, b * d + c * col_chunk:b * d + (c + 1) * col_chunk]))
            hc = hc + gate * _dot(o_b, wp_ref[:, cols])
        h.append(hc.astype(BF16))
    x1 = x + _dot(jnp.concatenate(h, axis=-1), wo_ref[...])
    xn1 = _rms_rows(x1, gffn_ref[...]).astype(BF16)
    y = x1
    for c in range(wup_ref.shape[1] // ff_chunk):
        cols = slice(c * ff_chunk, (c + 1) * ff_chunk)
        u = jnp.maximum(_dot(xn1, wup_ref[:, cols]), 0.0)
        y = y + _dot((u * u).astype(BF16), wdn_ref[cols, :])
    y_ref[...] = y


def _post(x, oa, ob, om, gmix, wg, wpa, wpb, wpm, wo, gffn, wup, wdn, *, tm, mem=None):
    rows, d = x.shape
    row = lambda i: (i, 0)
    acts = [x, oa, ob]
    specs = [pl.BlockSpec((tm, a.shape[1]), row) for a in acts]
    heads = 0
    if om is None:
        qm, mk, mv, heads = mem
        tiles_per_batch = rows // mk.shape[0] // tm
        kv = pl.BlockSpec((None,) + mk.shape[1:], lambda i: (i // tiles_per_batch, 0, 0))
        acts += [qm, mk, mv]
        specs += [pl.BlockSpec((tm, qm.shape[1]), row), kv, kv]
    else:
        acts.append(om)
        specs.append(pl.BlockSpec((tm, om.shape[1]), row))
    weights = (gmix, wg, wpa, wpb, wpm, wo, gffn, wup, wdn)
    return pl.pallas_call(
        functools.partial(_post_kernel, mem_heads=heads, col_chunk=512, ff_chunk=1024),
        out_shape=jax.ShapeDtypeStruct((rows, d), F32), grid=(rows // tm,),
        in_specs=specs + [_const_spec(a.shape) for a in weights],
        out_specs=pl.BlockSpec((tm, d), row), compiler_params=_cparams("arbitrary"),
        name="post")(*acts, *weights)


def _pad_rows(x, rows):
    return jnp.concatenate([x, jnp.zeros((rows - x.shape[0], x.shape[1]), x.dtype)], axis=0)


def _cached_attention(q_ref, kc_ref, vc_ref, kn_ref, vn_ref, o_ref, bias_c, bias_n, n_pad):
    bb, heads, head, _ = kc_ref.shape
    units = [(e, h, slice(h * head, (h + 1) * head)) for e in range(bb) for h in range(heads)]
    sc = [_dot(q_ref[e, :, ln], kc_ref[e, h].astype(BF16)) + bias_c(e, h) for e, h, ln in units]
    sn = [_nt_dot(q_ref[e, :, ln], _pad_rows(kn_ref[e, :, ln].astype(BF16), n_pad)) + bias_n(e, h)
          for e, h, ln in units]
    pc, pn, inv_l = [], [], []
    for u in range(len(units)):
        m = jnp.maximum(jnp.max(sc[u], axis=-1, keepdims=True), jnp.max(sn[u], axis=-1, keepdims=True))
        pc.append(jnp.exp2(sc[u] - m))
        pn.append(jnp.exp2(sn[u] - m))
        inv_l.append(1.0 / (jnp.sum(pc[u], axis=-1, keepdims=True) + jnp.sum(pn[u], axis=-1, keepdims=True)))
    for u, (e, h, ln) in enumerate(units):
        o = (_nt_dot(pc[u].astype(BF16), vc_ref[e, h].astype(BF16))
             + _dot(pn[u].astype(BF16), _pad_rows(vn_ref[e, :, ln].astype(BF16), n_pad)))
        o_ref[e, :, ln] = (o * inv_l[u]).astype(o_ref.dtype)


def _band_sample_kernel(q_ref, kc_ref, vc_ref, kn_ref, vn_ref, bc_ref, bn_ref, o_ref):
    s_len = q_ref.shape[1]
    rows = lambda h: slice(h * s_len, (h + 1) * s_len)
    _cached_attention(q_ref, kc_ref, vc_ref, kn_ref, vn_ref, o_ref,
                      lambda e, h: bc_ref[rows(h), :], lambda e, h: bn_ref[rows(h), :], bn_ref.shape[1])


def _band_sample(q, kct, vct, kn, vn, bias_c, bias_n, *, bb):
    b, s_len, w = q.shape
    new = pl.BlockSpec((bb, s_len, w), lambda bi: (bi, 0, 0))
    cache = pl.BlockSpec((bb,) + kct.shape[1:], lambda bi: (bi, 0, 0, 0))
    return pl.pallas_call(
        _band_sample_kernel, out_shape=jax.ShapeDtypeStruct((b, s_len, w), BF16), grid=(b // bb,),
        in_specs=[new, cache, cache, new, new, _const_spec(bias_c.shape), _const_spec(bias_n.shape)],
        out_specs=new, compiler_params=_cparams("arbitrary"), name="band_sample")(
            q, kct, vct, kn, vn, bias_c, bias_n)


def _forget_sample_kernel(q_ref, kc_ref, vc_ref, kn_ref, vn_ref, c_ref, o_ref, *, n_pad):
    s_len = q_ref.shape[1]
    p_len = kc_ref.shape[3]
    causal = (lax.broadcasted_iota(jnp.int32, (s_len, n_pad), 1)
              <= lax.broadcasted_iota(jnp.int32, (s_len, n_pad), 0))
    ch = [(c_ref[e, :, p_len - 1:p_len] - c_ref[e]) * LOG2E for e in range(c_ref.shape[0])]
    _cached_attention(q_ref, kc_ref, vc_ref, kn_ref, vn_ref, o_ref,
                      lambda e, h: ch[e][h:h + 1, 0:p_len],
                      lambda e, h: jnp.where(causal, ch[e][h:h + 1, p_len:p_len + n_pad], NEG), n_pad)


def _forget_sample(q, kct, vct, kn, vn, c, *, n_pad):
    b, s_len, w = q.shape
    new = pl.BlockSpec((1, s_len, w), lambda bi: (bi, 0, 0))
    cache = pl.BlockSpec((1,) + kct.shape[1:], lambda bi: (bi, 0, 0, 0))
    cs = pl.BlockSpec((1,) + c.shape[1:], lambda bi: (bi, 0, 0))
    return pl.pallas_call(
        functools.partial(_forget_sample_kernel, n_pad=n_pad),
        out_shape=jax.ShapeDtypeStruct((b, s_len, w), BF16), grid=(b,),
        in_specs=[new, cache, cache, new, new, cs], out_specs=new,
        compiler_params=_cparams("arbitrary"), name="forget_sample")(q, kct, vct, kn, vn, c)


def _qk_bound(g_q, g_k, dh):
    return 1.02 * LOG2E * dh ** 0.5 * jnp.max(jnp.abs(g_q)) * jnp.max(jnp.abs(g_k))


def _block_diag_mean(head, size=256):
    r = jnp.arange(size) // head
    return jnp.where(r[:, None] == r[None, :], 1.0 / head, 0.0).astype(BF16)


def _rel_bias_rows(rel_bias, d0, n_q, n_k):
    n = n_q + n_k - 1
    dist = d0 + (n_q - 1) - jnp.arange(n)
    e = rel_bias[jnp.clip(dist, -REL_CLIP, REL_CLIP) + REL_CLIP].T.astype(F32)
    h = e.shape[0]
    skew = jnp.tile(jnp.pad(e, ((0, 0), (0, 1))), (1, n_q))[:, :n_q * n].reshape(h, n_q, n)
    return skew[:, :, n_q - 1:].reshape(h * n_q, n_k)


def kernel(x_prompt, x_sample, mem_prompt, cache_a_k, cache_a_v, cache_b_k, cache_b_v, cache_b_logf,
           cache_mem_k, cache_mem_v, g_mix, w_in, b_f, g_qa, g_ka, g_qb, g_kb, g_qm, g_km, rel_bias,
           g_mem, w_mkv, w_pa, w_pb, w_pm, w_o, g_ffn, w_up, w_down):
    depth = w_in.shape[0]
    assert depth == 1
    batch, seq, d = x_prompt.shape
    dec_b, dec_s, _ = x_sample.shape
    _, _, n_cache, h_a, dh_a = cache_a_k.shape
    _, _, past, h_b, dh_b = cache_b_k.shape
    _, _, n_mem, h_m, dh_m = cache_mem_k.shape
    w_a, w_b, w_m = h_a * dh_a, h_b * dh_b, h_m * dh_m
    assert w_a == w_b == w_m == 512 and dh_a == dh_b == 64 and dh_m == 128 and h_b == 8
    keep_p = min(n_cache, seq)
    n_pad = 128
    l = 0

    qkv_end = 3 * w_a + 3 * w_b
    f_end = qkv_end + h_b
    qm_end = f_end + w_m
    w = w_in[l]
    wqkv = w[:, :qkv_end].astype(BF16)
    wft = jnp.zeros((16, d), BF16).at[:h_b].set(w[:, qkv_end:f_end].T.astype(BF16))
    wqm = w[:, f_end:qm_end].astype(BF16)
    wg = w[:, qm_end:].astype(BF16)
    gains = jnp.stack([jnp.tile(g[l], h_a) for g in (g_qa, g_ka, g_qb, g_kb)]).astype(F32)
    gqm = jnp.tile(g_qm[l], h_m)[None, :]
    gkm = jnp.tile(g_km[l], h_m)[None, :]
    bf = b_f[l][:, None].astype(F32)
    bd64 = _block_diag_mean(dh_a)
    bd128 = _block_diag_mean(dh_m)
    tri = (jnp.arange(128)[:, None] <= jnp.arange(128)[None, :]).astype(BF16)
    gmix = g_mix[l][None, :]
    proj_w = (gmix, wqkv, wft, wqm, gains, gqm, bf, bd64, bd128)
    wpa, wpb, wpm, wo = (a[l].astype(BF16) for a in (w_pa, w_pb, w_pm, w_o))
    wup, wdn = w_up[l].astype(BF16), w_down[l].astype(BF16)
    gffn = g_ffn[l][None, :]

    xs = x_sample.reshape(dec_b * dec_s, d)
    ps = _proj(xs, *proj_w, tm=512)
    sthree = lambda a: a.reshape(dec_b, dec_s, -1)
    heads_t = lambda a: jnp.transpose(a[l], (0, 2, 3, 1))
    lf_new = ps["lft"].reshape(h_b, dec_b, dec_s).transpose(1, 0, 2)
    lf_all = jnp.concatenate([cache_b_logf[l].astype(F32).transpose(0, 2, 1), lf_new,
                              jnp.zeros((dec_b, h_b, n_pad - dec_s), F32)], axis=-1)
    c_s = _cumsum_lanes(lf_all.reshape(dec_b * h_b, past + n_pad), tri, rb=32)
    forget_s = (sthree(ps["qb"]), heads_t(cache_b_k), heads_t(cache_b_v), sthree(ps["kb"]),
                sthree(ps["vb_f"]), c_s.reshape(dec_b, h_b, past + n_pad))

    xp = x_prompt.reshape(batch * seq, d)
    tm_p = 512
    ride = batch * seq // tm_p == dec_b
    pp = _proj(xp, *proj_w, tm=tm_p, group_rows=seq, keep_rows=keep_p,
               rider=forget_s if ride else None, rider_pad=n_pad)
    o_b_s = pp["rider"] if ride else _forget_sample(*forget_s, n_pad=n_pad)
    mk_f, mv_f = _memkv(mem_prompt.reshape(batch * n_mem, d), g_mem[l][None, :], w_mkv[l].astype(BF16),
                        gkm, bd128, tm=256, heads=h_m)
    tq_b = 512
    lf_p = pp["lft"].reshape(h_b, batch, seq).transpose(1, 0, 2)
    c_p = _cumsum_lanes(lf_p.reshape(batch * h_b, seq), tri, rb=batch * h_b)
    c_p = c_p.reshape(batch, h_b // 4, 4, seq).transpose(0, 1, 3, 2)
    vt = pp["vbt"].reshape(batch, seq // tq_b, h_b, V_ROWS, tq_b)
    three = lambda a: a.reshape(batch, seq, -1)
    bias_p = _rel_bias_rows(rel_bias[l], n_cache, CHUNK, n_cache + CHUNK) * LOG2E
    bound_a = _qk_bound(g_qa[l], g_ka[l], dh_a) + LOG2E * jnp.max(jnp.abs(rel_bias[l]))
    o_a = _band_prompt(bound_a.reshape(1).astype(F32), three(pp["qa"]), three(pp["ka"]), three(pp["va"]),
                       bias_p, reach=n_cache, tq=256)
    o_b = _forget_prompt(_qk_bound(g_qb[l], g_kb[l], dh_b).reshape(1).astype(F32), three(pp["qb"]),
                         three(pp["kb"]), vt, c_p, head=dh_b, tq=tq_b)
    post_w = (gmix, wg, wpa, wpb, wpm, wo, gffn, wup, wdn)
    mem_p = (pp["qm"], mk_f.reshape(batch, n_mem * h_m, dh_m), mv_f.reshape(batch, n_mem * h_m, dh_m), h_m)
    y_prompt = _post(xp, o_a.reshape(-1, w_a), o_b.reshape(-1, w_b), None, *post_w, tm=512,
                     mem=mem_p).reshape(batch, seq, d)

    bias_c = _rel_bias_rows(rel_bias[l], n_cache, dec_s, n_cache) * LOG2E
    bias_n = jnp.full((h_a * dec_s, n_pad), NEG, F32).at[:, :dec_s].set(
        _rel_bias_rows(rel_bias[l], 0, dec_s, dec_s) * LOG2E)
    o_a_s = _band_sample(sthree(ps["qa"]), heads_t(cache_a_k), heads_t(cache_a_v), sthree(ps["ka"]),
                         sthree(ps["va"]), bias_c, bias_n, bb=4)
    o_m_s = _mem_attn(sthree(ps["qm"]), cache_mem_k[l].reshape(dec_b, n_mem * h_m, dh_m),
                      cache_mem_v[l].reshape(dec_b, n_mem * h_m, dh_m), heads=h_m, tq=dec_s, bb=8)
    y_sample = _post(xs, o_a_s.reshape(-1, w_a), o_b_s.reshape(-1, w_b), o_m_s.reshape(-1, w_m), *post_w,
                     tm=512).reshape(dec_b, dec_s, d)

    lead = lambda a, *shape: a.reshape((depth,) + shape)
    frames_major = lambda a: jnp.transpose(a, (0, 3, 1, 2))[None]
    return (y_prompt, y_sample,
            frames_major(pp["ka_f"]), frames_major(pp["va_f"]),
            frames_major(pp["kb_f"]), frames_major(pp["vb_f"]),
            lf_p.transpose(0, 2, 1)[None],
            lead(mk_f, batch, n_mem, h_m, dh_m), lead(mv_f, batch, n_mem, h_m, dh_m),
            lead(ps["ka_f"], dec_b, dec_s, h_a, dh_a), lead(ps["va_f"], dec_b, dec_s, h_a, dh_a),
            lead(ps["kb_f"], dec_b, dec_s, h_b, dh_b), lead(ps["vb_f"], dec_b, dec_s, h_b, dh_b),
            lead(ps["lft"].T, dec_b, dec_s, h_b))
```

```python
import functools

import jax
import jax.numpy as jnp
from jax import lax
from jax.experimental import pallas as pl
from jax.experimental.pallas import tpu as pltpu

BF16 = jnp.bfloat16
F32 = jnp.float32

EPS = 1e-6
CHUNK = 64
REL_CLIP = 128
NEG = -1e30
LANE_GROUP = 256
LOG2E = 1.4426950408889634
V_ROWS = 64 + 16
MAX_ONE_PASS_BOUND = 40.0
VMEM_LIMIT = 56 * 1024 * 1024


def _cparams(*sem):
    return pltpu.CompilerParams(dimension_semantics=sem, vmem_limit_bytes=VMEM_LIMIT)


def _const_spec(shape):
    nd = len(shape)
    return pl.BlockSpec(shape, lambda *_: (0,) * nd, pipeline_mode=pl.Buffered(1))


def _nt_dot(a, b):
    return lax.dot_general(a, b, (((1,), (1,)), ((), ())), preferred_element_type=F32)


def _dot(a, b):
    return jnp.dot(a, b, preferred_element_type=F32)


def _rms_rows(x, gain):
    ms = jnp.mean(x * x, axis=-1, keepdims=True)
    return x * lax.rsqrt(ms + EPS) * gain


def _head_norm(z, bd_ref, gain):
    z2 = (z * z).astype(BF16)
    ms = jnp.concatenate(
        [_dot(z2[:, h * 256:(h + 1) * 256], bd_ref[...]) for h in range(z.shape[1] // 256)], axis=-1)
    return z * lax.rsqrt(ms + EPS) * gain


def _log_sigmoid(x):
    return jnp.minimum(x, 0.0) - jnp.log1p(jnp.exp(-jnp.abs(x)))


def _lane_group(width, head):
    return lax.broadcasted_iota(jnp.int32, (1, width), 1) // head


def _block_diag_rows(q4, head):
    grp = _lane_group(q4.shape[1], head)
    zero = jnp.zeros_like(q4)
    return jnp.concatenate([jnp.where(grp == g, q4, zero) for g in range(q4.shape[1] // head)], axis=0)


def _pick_diag(o, rows, head):
    grp = _lane_group(o.shape[1], head)
    out = jnp.zeros((rows, o.shape[1]), o.dtype)
    for g in range(o.shape[1] // head):
        out = jnp.where(grp == g, o[g * rows:(g + 1) * rows], out)
    return out


def _store_heads_t(ref, z_t):
    dh = ref.shape[1]
    for h in range(ref.shape[0]):
        ref[h] = z_t[h * dh:(h + 1) * dh, :]


def _proj_kernel(x_ref, gmix_ref, wqkv_ref, wft_ref, wqm_ref, gains_ref, gqm_ref, bf_ref,
                 bd64_ref, bd128_ref, *refs, out_names, scale_a, scale_b, scale_m, keep_from,
                 tiles_per_group, riders):
    rider_in, out_refs = refs[:len(refs) - len(out_names)], refs[len(refs) - len(out_names):]
    o = dict(zip(out_names, out_refs))
    head_major = "vbt" in o
    xn = _rms_rows(x_ref[...], gmix_ref[...]).astype(BF16)

    def seg(j):
        return _dot(xn, wqkv_ref[:, j * 512:(j + 1) * 512])

    kb = _head_norm(seg(4), bd64_ref, gains_ref[3:4, :])
    o["kb"][...] = kb.astype(BF16)
    vb = seg(5)
    if head_major:
        _store_heads_t(o["kb_f"], kb.T)
        vb_t = vb.T
        _store_heads_t(o["vb_f"], vb_t)
        vbt = o["vbt"]
        head = vb_t.shape[0] // vbt.shape[0]
        pad = vbt.shape[1] - head
        ones_row = (lax.broadcasted_iota(jnp.int32, (pad, vb_t.shape[1]), 0) == 0).astype(BF16)
        for h in range(vbt.shape[0]):
            vbt[h, 0:head, :] = vb_t[h * head:(h + 1) * head, :].astype(BF16)
            vbt[h, head:head + pad, :] = ones_row
    else:
        o["kb_f"][...] = kb
        o["vb_f"][...] = vb
    qb = _head_norm(seg(3), bd64_ref, gains_ref[2:3, :])
    o["qb"][...] = (qb * scale_b).astype(BF16)
    qa = _head_norm(seg(0), bd64_ref, gains_ref[0:1, :])
    o["qa"][...] = (qa * scale_a).astype(BF16)
    ka = _head_norm(seg(1), bd64_ref, gains_ref[1:2, :])
    o["ka"][...] = ka.astype(BF16)
    qm = _head_norm(_dot(xn, wqm_ref[...]), bd128_ref, gqm_ref[...])
    o["qm"][...] = (qm * scale_m).astype(BF16)
    flt = _nt_dot(wft_ref[...], xn)
    o["lft"][...] = _log_sigmoid(flt[0:8, :] + bf_ref[...])
    va = seg(2)
    o["va"][...] = va.astype(BF16)
    if head_major:
        @pl.when(pl.program_id(0) % tiles_per_group >= keep_from)
        def _():
            _store_heads_t(o["ka_f"], ka.T)
            _store_heads_t(o["va_f"], va.T)
    else:
        o["ka_f"][...] = ka
        o["va_f"][...] = va
    for k, (rider_kernel, n_in) in enumerate(riders):
        rider_kernel(*rider_in[:n_in], o[f"rider{k}"])
        rider_in = rider_in[n_in:]


def _proj(x, gmix, wqkv, wft, wqm, gains, gqm, bf, bd64, bd128, *, tm, group_rows=None, keep_rows=None,
          riders=()):
    rows, d = x.shape
    n_tiles = rows // tm
    row = lambda i: (i, 0)
    wide = lambda dt: (jax.ShapeDtypeStruct((rows, 512), dt), pl.BlockSpec((tm, 512), row))
    outs = {n: wide(BF16) for n in ("qa", "ka", "va", "qb", "kb", "qm")}
    outs["lft"] = (jax.ShapeDtypeStruct((8, rows), F32), pl.BlockSpec((8, tm), lambda i: (0, i)))
    tpg = keep_from = 1
    if group_rows is None:
        for n in ("ka_f", "va_f", "kb_f", "vb_f"):
            outs[n] = wide(F32)
    else:
        assert keep_rows == tm
        tpg = group_rows // tm
        keep_from = tpg - 1
        groups = rows // group_rows
        kept = (jax.ShapeDtypeStruct((groups, 8, 64, keep_rows), F32),
                pl.BlockSpec((None, 8, 64, tm), lambda i: (i // tpg, 0, 0, 0)))
        full = (jax.ShapeDtypeStruct((groups, 8, 64, group_rows), F32),
                pl.BlockSpec((None, 8, 64, tm), lambda i: (i // tpg, 0, 0, i % tpg)))
        outs.update(ka_f=kept, va_f=kept, kb_f=full, vb_f=full)
        outs["vbt"] = (jax.ShapeDtypeStruct((n_tiles, 8, V_ROWS, tm), BF16),
                       pl.BlockSpec((None, 8, V_ROWS, tm), lambda i: (i, 0, 0, 0)))
    in_specs = [pl.BlockSpec((tm, d), row)] + [_const_spec(a.shape) for a in
                                               (gmix, wqkv, wft, wqm, gains, gqm, bf, bd64, bd128)]
    rider_args, rider_kernels = [], []
    for k, (rider_kernel, per_tile, const) in enumerate(riders):
        assert all(a.shape[0] % n_tiles == 0 for a in per_tile)
        tile_spec = lambda a: pl.BlockSpec((a.shape[0] // n_tiles,) + a.shape[1:],
                                           lambda i, nd=a.ndim: (i,) + (0,) * (nd - 1))
        in_specs += [tile_spec(a) for a in per_tile] + [_const_spec(a.shape) for a in const]
        outs[f"rider{k}"] = (jax.ShapeDtypeStruct(per_tile[0].shape, BF16), tile_spec(per_tile[0]))
        rider_args += [*per_tile, *const]
        rider_kernels.append((rider_kernel, len(per_tile) + len(const)))
    names = tuple(outs)
    kern = functools.partial(_proj_kernel, out_names=names, scale_a=LOG2E * 64 ** -0.5,
                             scale_b=LOG2E * 64 ** -0.5, scale_m=LOG2E * 128 ** -0.5,
                             keep_from=keep_from, tiles_per_group=tpg, riders=tuple(rider_kernels))
    res = pl.pallas_call(kern, out_shape=[outs[n][0] for n in names], grid=(n_tiles,), in_specs=in_specs,
                         out_specs=[outs[n][1] for n in names], compiler_params=_cparams("arbitrary"),
                         name="proj")(x, gmix, wqkv, wft, wqm, gains, gqm, bf, bd64, bd128, *rider_args)
    return dict(zip(names, res))


def _memkv_kernel(m_ref, gmem_ref, w_ref, gkm_ref, bd128_ref, mk_o, mv_o, *, heads):
    tm = m_ref.shape[0]
    xn = _rms_rows(m_ref[...], gmem_ref[...]).astype(BF16)
    half = w_ref.shape[1] // 2
    dh = half // heads
    mk = _head_norm(_dot(xn, w_ref[:, :half]), bd128_ref, gkm_ref[...])
    mv = _dot(xn, w_ref[:, half:])
    for h in range(heads):
        mk_o[pl.ds(h, tm, stride=heads), :] = mk[:, h * dh:(h + 1) * dh]
        mv_o[pl.ds(h, tm, stride=heads), :] = mv[:, h * dh:(h + 1) * dh]


def _memkv(mem, gmem, w, gkm, bd128, *, tm, heads):
    rows, d = mem.shape
    dh = w.shape[1] // 2 // heads
    row = lambda i: (i, 0)
    return pl.pallas_call(
        functools.partial(_memkv_kernel, heads=heads),
        out_shape=[jax.ShapeDtypeStruct((rows * heads, dh), F32)] * 2, grid=(rows // tm,),
        in_specs=[pl.BlockSpec((tm, d), row)] + [_const_spec(a.shape) for a in (gmem, w, gkm, bd128)],
        out_specs=[pl.BlockSpec((tm * heads, dh), row)] * 2, compiler_params=_cparams("arbitrary"),
        name="memkv")(mem, gmem, w, gkm, bd128)


def _cumsum_kernel(x_ref, tri_ref, o_ref):
    rows, length = x_ref.shape
    tri = tri_ref[...]

    def local(c):
        x = x_ref[:, c * 128:(c + 1) * 128]
        hi = x.astype(BF16)
        r1 = x - hi.astype(F32)
        mid = r1.astype(BF16)
        lo = (r1 - mid.astype(F32)).astype(BF16)
        return _dot(hi, tri) + _dot(mid, tri) + _dot(lo, tri)

    chunks = [local(c) for c in range(length // 128)]
    carry = jnp.zeros((rows, 1), F32)
    for c, cc in enumerate(chunks):
        o_ref[:, c * 128:(c + 1) * 128] = cc + carry
        carry = carry + cc[:, 127:128]


def _cumsum_lanes(x, tri, *, rb):
    rows, length = x.shape
    return pl.pallas_call(
        _cumsum_kernel, out_shape=jax.ShapeDtypeStruct((rows, length), F32), grid=(rows // rb,),
        in_specs=[pl.BlockSpec((rb, length), lambda i: (i, 0)), _const_spec(tri.shape)],
        out_specs=pl.BlockSpec((rb, length), lambda i: (i, 0)), compiler_params=_cparams("arbitrary"),
        name="cumsum")(x, tri)


def _band_kernel(bound_ref, q_ref, k0_ref, k1_ref, k2_ref, v0_ref, v1_ref, v2_ref, bias_ref, o_ref,
                 k_sc, v_sc, *, reach, lookahead):
    tq = q_ref.shape[0]
    band = reach + CHUNK
    i = pl.program_id(1)
    for n, (kr, vr) in enumerate(((k0_ref, v0_ref), (k1_ref, v1_ref), (k2_ref, v2_ref))):
        k_sc[n * tq:(n + 1) * tq, :] = kr[...]
        v_sc[n * tq:(n + 1) * tq, :] = vr[...]
    col = lax.broadcasted_iota(jnp.int32, (1, band), 1)
    units = [(j, hg) for j in range(tq // CHUNK) for hg in range(q_ref.shape[1] // LANE_GROUP)]

    def scores(j, hg):
        r0 = 2 * tq + j * CHUNK - reach
        lanes = slice(hg * LANE_GROUP, (hg + 1) * LANE_GROUP)
        qbd = _block_diag_rows(q_ref[j * CHUNK:(j + 1) * CHUNK, lanes], CHUNK)
        s = _nt_dot(qbd, k_sc[r0:r0 + band, lanes]) + bias_ref[hg * 4 * CHUNK:(hg + 1) * 4 * CHUNK, :]
        return jnp.where((i - 2) * tq + r0 + col >= 0, s, NEG)

    def attend(one_pass):
        pending = [scores(*u) for u in units[:lookahead]]
        for n, (j, hg) in enumerate(units):
            s = pending.pop(0)
            if n + lookahead < len(units):
                pending.append(scores(*units[n + lookahead]))
            r0 = 2 * tq + j * CHUNK - reach
            lanes = slice(hg * LANE_GROUP, (hg + 1) * LANE_GROUP)
            p = jnp.exp2(s if one_pass else s - jnp.max(s, axis=-1, keepdims=True))
            l = jnp.sum(p, axis=-1, keepdims=True)
            o = _dot(p.astype(BF16), v_sc[r0:r0 + band, lanes]) / l
            o_ref[j * CHUNK:(j + 1) * CHUNK, lanes] = _pick_diag(o, CHUNK, CHUNK).astype(o_ref.dtype)

    lax.cond(bound_ref[0] <= MAX_ONE_PASS_BOUND, lambda: attend(True), lambda: attend(False))


def _band_prompt(logit_bound, q, k, v, bias, *, reach, tq):
    b, t, w = q.shape
    assert reach == 2 * tq and tq % CHUNK == 0
    blk = lambda off: pl.BlockSpec((None, tq, w), lambda bi, i: (bi, jnp.maximum(i - off, 0), 0))
    return pl.pallas_call(
        functools.partial(_band_kernel, reach=reach, lookahead=2),
        out_shape=jax.ShapeDtypeStruct((b, t, w), BF16), grid=(b, t // tq),
        in_specs=[pl.BlockSpec(memory_space=pltpu.SMEM), blk(0), blk(2), blk(1), blk(0), blk(2), blk(1), blk(0),
                  _const_spec(bias.shape)],
        out_specs=blk(0),
        scratch_shapes=[pltpu.VMEM((3 * tq, w), BF16), pltpu.VMEM((3 * tq, w), BF16)],
        compiler_params=_cparams("arbitrary", "arbitrary"), name="band_prompt")(
            logit_bound, q, k, k, k, v, v, v, bias)


def _forget_kernel(bound_ref, q_ref, k_ref, vt_ref, c_ref, o_ref, qt_sc, m_sc, acc_sc, *, head, lookahead):
    tq, width = q_ref.shape
    n_heads = width // head
    i = pl.program_id(2)
    logit_bound = bound_ref[0]
    q_t = q_ref[...].astype(F32).T
    row_grp = lax.broadcasted_iota(jnp.int32, (width, 1), 0) // head
    for g in range(n_heads):
        qt_sc[g] = jnp.where(row_grp == g, q_t, 0.0).astype(BF16)
    m_sc[...] = jnp.full(m_sc.shape, NEG, F32)
    acc_sc[...] = jnp.zeros(acc_sc.shape, F32)
    c_first = c_ref[pl.ds(pl.multiple_of(i * tq, tq), 8), :][0:1, :]

    def step(j, masked):
        k0 = pl.multiple_of(j * tq, tq)
        kt = k_ref[pl.ds(k0, tq), :]
        bias = (c_first - c_ref[pl.ds(k0, tq), :]) * LOG2E
        if masked:
            keep = (lax.broadcasted_iota(jnp.int32, (tq, tq), 0)
                    <= lax.broadcasted_iota(jnp.int32, (tq, tq), 1))

        def scores(g):
            s = _dot(kt, qt_sc[g]) + bias[:, g:g + 1]
            return jnp.where(keep, s, NEG) if masked else s

        pending = [scores(g) for g in range(min(lookahead, n_heads))]
        for g in range(n_heads):
            s = pending.pop(0)
            if g + lookahead < n_heads:
                pending.append(scores(g + lookahead))
            m_prev = m_sc[g]
            m_new = jnp.maximum(m_prev, jnp.max(s, axis=0, keepdims=True))
            p = jnp.exp2(s - m_new).astype(BF16)
            acc_sc[g] = jnp.exp2(m_prev - m_new) * acc_sc[g] + _dot(vt_ref[j, g], p)
            m_sc[g] = m_new

    def bounded_tiles(tiles):
        m_tile, logits = [], []
        for j in tiles:
            k0 = pl.multiple_of(j * tq, tq)
            c_tile = c_ref[pl.ds(k0, tq), :]
            c_last = c_tile[tq - 1:tq, :]
            m_tile.append((c_first - c_last) * LOG2E)
            shift = (c_last - c_tile) * LOG2E
            logits.append(functools.partial(
                lambda g, k0, shift: _dot(k_ref[pl.ds(k0, tq), :], qt_sc[g]) + shift[:, g:g + 1],
                k0=k0, shift=shift))
        units = [(t, g) for t in range(len(tiles)) for g in range(n_heads)]
        pending = [logits[t](g) for t, g in units[:lookahead]]
        for n, (t, g) in enumerate(units):
            s = pending.pop(0)
            if n + lookahead < len(units):
                tn, gn = units[n + lookahead]
                pending.append(logits[tn](gn))
            m_new = jnp.broadcast_to(m_tile[t][:, g:g + 1], (1, tq))
            p = jnp.exp2(s).astype(BF16)
            acc_sc[g] = jnp.exp2(m_sc[g] - m_new) * acc_sc[g] + _dot(vt_ref[tiles[t], g], p)
            m_sc[g] = m_new

    def bounded_loop():
        def pair(jj, carry):
            bounded_tiles([2 * jj, 2 * jj + 1])
            return carry
        lax.fori_loop(0, i // 2, pair, 0)

        @pl.when(i % 2 == 1)
        def _():
            bounded_tiles([i - 1])

    def two_pass_loop():
        def body(j, carry):
            step(j, False)
            return carry
        lax.fori_loop(0, i, body, 0)

    lax.cond(logit_bound <= MAX_ONE_PASS_BOUND, bounded_loop, two_pass_loop)
    step(i, True)
    o_t = jnp.concatenate([acc_sc[g, 0:head, :] / acc_sc[g, head:head + 1, :] for g in range(n_heads)],
                          axis=0)
    o_ref[...] = o_t.T.astype(o_ref.dtype)


def _forget_prompt(logit_bound, q, k, vt, c, *, head, tq):
    b, t, w = q.shape
    n_heads = LANE_GROUP // head
    v_rows = vt.shape[3]
    lookahead = 3
    assert lookahead <= n_heads
    return pl.pallas_call(
        functools.partial(_forget_kernel, head=head, lookahead=lookahead),
        out_shape=jax.ShapeDtypeStruct((b, t, w), BF16), grid=(b, w // LANE_GROUP, t // tq),
        in_specs=[pl.BlockSpec(memory_space=pltpu.SMEM),
                  pl.BlockSpec((None, tq, LANE_GROUP), lambda bi, hg, i: (bi, i, hg)),
                  pl.BlockSpec((None, t, LANE_GROUP), lambda bi, hg, i: (bi, 0, hg)),
                  pl.BlockSpec((None, t // tq, n_heads, v_rows, tq), lambda bi, hg, i: (bi, 0, hg, 0, 0)),
                  pl.BlockSpec((None, None, t, n_heads), lambda bi, hg, i: (bi, hg, 0, 0))],
        out_specs=pl.BlockSpec((None, tq, LANE_GROUP), lambda bi, hg, i: (bi, i, hg)),
        scratch_shapes=[pltpu.VMEM((n_heads, LANE_GROUP, tq), BF16), pltpu.VMEM((n_heads, 1, tq), F32),
                        pltpu.VMEM((n_heads, v_rows, tq), F32)],
        compiler_params=_cparams("arbitrary", "arbitrary", "arbitrary"), name="forget_prompt")(
            logit_bound, q, k, vt, c)


def _mem_attention(qs, mk_refs, mv_refs, heads):
    n_mem = mk_refs[0].shape[0] // heads
    dh = mk_refs[0].shape[1]
    units = [(e, h) for e in range(len(qs)) for h in range(heads)]
    rows = lambda ref, h: ref[pl.ds(h, n_mem, stride=heads), :].astype(BF16)
    s = [_nt_dot(qs[e][:, h * dh:(h + 1) * dh], rows(mk_refs[e], h)) for e, h in units]
    p, inv_l = [], []
    for u in range(len(units)):
        ex = jnp.exp2(s[u] - jnp.max(s[u], axis=-1, keepdims=True))
        inv_l.append(1.0 / jnp.sum(ex, axis=-1, keepdims=True))
        p.append(ex.astype(BF16))
    o = [(_dot(p[u], rows(mv_refs[e], h)) * inv_l[u]).astype(BF16) for u, (e, h) in enumerate(units)]
    return [jnp.concatenate(o[e * heads:(e + 1) * heads], axis=-1) for e in range(len(qs))]


def _mem_kernel(q_ref, mk_ref, mv_ref, o_ref, *, heads):
    bb = q_ref.shape[0]
    outs = _mem_attention([q_ref[e] for e in range(bb)], [mk_ref.at[e] for e in range(bb)],
                          [mv_ref.at[e] for e in range(bb)], heads)
    for e in range(bb):
        o_ref[e] = outs[e]


def _mem_attn(q, mk, mv, *, heads, tq, bb):
    b, t, w = q.shape
    kv = pl.BlockSpec((bb,) + mk.shape[1:], lambda bi, i: (bi, 0, 0))
    qs = pl.BlockSpec((bb, tq, w), lambda bi, i: (bi, i, 0))
    return pl.pallas_call(
        functools.partial(_mem_kernel, heads=heads), out_shape=jax.ShapeDtypeStruct((b, t, w), BF16),
        grid=(b // bb, t // tq), in_specs=[qs, kv, kv], out_specs=qs,
        compiler_params=_cparams("arbitrary", "arbitrary"), name="mem_attn")(q, mk, mv)


def _post_kernel(x_ref, oa_ref, ob_ref, *refs, mem_heads, col_chunk, ff_chunk):
    if mem_heads:
        qm_ref, mk_ref, mv_ref, *refs = refs
        o_m, = _mem_attention([qm_ref[...]], [mk_ref], [mv_ref], mem_heads)
    else:
        om_ref, *refs = refs
        o_m = om_ref[...]
    gmix_ref, wg_ref, wpa_ref, wpb_ref, wpm_ref, wo_ref, gffn_ref, wup_ref, wdn_ref, y_ref = refs
    x = x_ref[...]
    d = x.shape[1]
    xn = _rms_rows(x, gmix_ref[...]).astype(BF16)
    branches = ((oa_ref[...], wpa_ref), (ob_ref[...], wpb_ref), (o_m, wpm_ref))
    h = []
    for c in range(d // col_chunk):
        cols = slice(c * col_chunk, (c + 1) * col_chunk)
        hc = 0.0
        for b, (o_b, wp_ref) in enumerate(branches):
            gate = jax.nn.sigmoid(_dot(xn, wg_ref[:, b * d + c * col_chunk:b * d + (c + 1) * col_chunk]))
            hc = hc + gate * _dot(o_b, wp_ref[:, cols])
        h.append(hc.astype(BF16))
    x1 = x + _dot(jnp.concatenate(h, axis=-1), wo_ref[...])
    xn1 = _rms_rows(x1, gffn_ref[...]).astype(BF16)
    y = x1
    for c in range(wup_ref.shape[1] // ff_chunk):
        cols = slice(c * ff_chunk, (c + 1) * ff_chunk)
        u = jnp.maximum(_dot(xn1, wup_ref[:, cols]), 0.0)
        y = y + _dot((u * u).astype(BF16), wdn_ref[cols, :])
    y_ref[...] = y


def _post(x, oa, ob, om, gmix, wg, wpa, wpb, wpm, wo, gffn, wup, wdn, *, tm, mem=None):
    rows, d = x.shape
    row = lambda i: (i, 0)
    acts = [x, oa, ob]
    specs = [pl.BlockSpec((tm, a.shape[1]), row) for a in acts]
    heads = 0
    if om is None:
        qm, mk, mv, heads = mem
        tiles_per_batch = rows // mk.shape[0] // tm
        kv = pl.BlockSpec((None,) + mk.shape[1:], lambda i: (i // tiles_per_batch, 0, 0))
        acts += [qm, mk, mv]
        specs += [pl.BlockSpec((tm, qm.shape[1]), row), kv, kv]
    else:
        acts.append(om)
        specs.append(pl.BlockSpec((tm, om.shape[1]), row))
    weights = (gmix, wg, wpa, wpb, wpm, wo, gffn, wup, wdn)
    return pl.pallas_call(
        functools.partial(_post_kernel, mem_heads=heads, col_chunk=512, ff_chunk=1024),
        out_shape=jax.ShapeDtypeStruct((rows, d), F32), grid=(rows // tm,),
        in_specs=specs + [_const_spec(a.shape) for a in weights],
        out_specs=pl.BlockSpec((tm, d), row), compiler_params=_cparams("arbitrary"),
        name="post")(*acts, *weights)


def _pad_rows(x, rows):
    return jnp.concatenate([x, jnp.zeros((rows - x.shape[0], x.shape[1]), x.dtype)], axis=0)


def _cached_attention(q_ref, kc_ref, vc_ref, kn_ref, vn_ref, o_ref, bias_c, bias_n, n_pad):
    bb, heads, head, _ = kc_ref.shape
    units = [(e, h, slice(h * head, (h + 1) * head)) for e in range(bb) for h in range(heads)]
    sc = [_dot(q_ref[e, :, ln], kc_ref[e, h].astype(BF16)) + bias_c(e, h) for e, h, ln in units]
    sn = [_nt_dot(q_ref[e, :, ln], _pad_rows(kn_ref[e, :, ln].astype(BF16), n_pad)) + bias_n(e, h)
          for e, h, ln in units]
    pc, pn, inv_l = [], [], []
    for u in range(len(units)):
        m = jnp.maximum(jnp.max(sc[u], axis=-1, keepdims=True), jnp.max(sn[u], axis=-1, keepdims=True))
        pc.append(jnp.exp2(sc[u] - m))
        pn.append(jnp.exp2(sn[u] - m))
        inv_l.append(1.0 / (jnp.sum(pc[u], axis=-1, keepdims=True) + jnp.sum(pn[u], axis=-1, keepdims=True)))
    for u, (e, h, ln) in enumerate(units):
        o = (_nt_dot(pc[u].astype(BF16), vc_ref[e, h].astype(BF16))
             + _dot(pn[u].astype(BF16), _pad_rows(vn_ref[e, :, ln].astype(BF16), n_pad)))
        o_ref[e, :, ln] = (o * inv_l[u]).astype(o_ref.dtype)


def _band_sample_kernel(q_ref, kc_ref, vc_ref, kn_ref, vn_ref, bc_ref, bn_ref, o_ref):
    s_len = q_ref.shape[1]
    rows = lambda h: slice(h * s_len, (h + 1) * s_len)
    _cached_attention(q_ref, kc_ref, vc_ref, kn_ref, vn_ref, o_ref,
                      lambda e, h: bc_ref[rows(h), :], lambda e, h: bn_ref[rows(h), :], bn_ref.shape[1])


def _band_sample(q, kct, vct, kn, vn, bias_c, bias_n, *, bb):
    b, s_len, w = q.shape
    new = pl.BlockSpec((bb, s_len, w), lambda bi: (bi, 0, 0))
    cache = pl.BlockSpec((bb,) + kct.shape[1:], lambda bi: (bi, 0, 0, 0))
    return pl.pallas_call(
        _band_sample_kernel, out_shape=jax.ShapeDtypeStruct((b, s_len, w), BF16), grid=(b // bb,),
        in_specs=[new, cache, cache, new, new, _const_spec(bias_c.shape), _const_spec(bias_n.shape)],
        out_specs=new, compiler_params=_cparams("arbitrary"), name="band_sample")(
            q, kct, vct, kn, vn, bias_c, bias_n)


def _forget_sample_kernel(q_ref, kc_ref, vc_ref, kn_ref, vn_ref, c_ref, o_ref, *, n_pad):
    s_len = q_ref.shape[1]
    p_len = kc_ref.shape[3]
    causal = (lax.broadcasted_iota(jnp.int32, (s_len, n_pad), 1)
              <= lax.broadcasted_iota(jnp.int32, (s_len, n_pad), 0))
    ch = [(c_ref[e, :, p_len - 1:p_len] - c_ref[e]) * LOG2E for e in range(c_ref.shape[0])]
    _cached_attention(q_ref, kc_ref, vc_ref, kn_ref, vn_ref, o_ref,
                      lambda e, h: ch[e][h:h + 1, 0:p_len],
                      lambda e, h: jnp.where(causal, ch[e][h:h + 1, p_len:p_len + n_pad], NEG), n_pad)


def _forget_sample(q, kct, vct, kn, vn, c, *, n_pad):
    b, s_len, w = q.shape
    new = pl.BlockSpec((1, s_len, w), lambda bi: (bi, 0, 0))
    cache = pl.BlockSpec((1,) + kct.shape[1:], lambda bi: (bi, 0, 0, 0))
    cs = pl.BlockSpec((1,) + c.shape[1:], lambda bi: (bi, 0, 0))
    return pl.pallas_call(
        functools.partial(_forget_sample_kernel, n_pad=n_pad),
        out_shape=jax.ShapeDtypeStruct((b, s_len, w), BF16), grid=(b,),
        in_specs=[new, cache, cache, new, new, cs], out_specs=new,
        compiler_params=_cparams("arbitrary"), name="forget_sample")(q, kct, vct, kn, vn, c)


def _qk_bound(g_q, g_k, dh):
    return 1.02 * LOG2E * dh ** 0.5 * jnp.max(jnp.abs(g_q)) * jnp.max(jnp.abs(g_k))


def _block_diag_mean(head, size=256):
    r = jnp.arange(size) // head
    return jnp.where(r[:, None] == r[None, :], 1.0 / head, 0.0).astype(BF16)


def _rel_bias_rows(rel_bias, d0, n_q, n_k):
    n = n_q + n_k - 1
    dist = d0 + (n_q - 1) - jnp.arange(n)
    e = rel_bias[jnp.clip(dist, -REL_CLIP, REL_CLIP) + REL_CLIP].T.astype(F32)
    h = e.shape[0]
    skew = jnp.tile(jnp.pad(e, ((0, 0), (0, 1))), (1, n_q))[:, :n_q * n].reshape(h, n_q, n)
    return skew[:, :, n_q - 1:].reshape(h * n_q, n_k)


def kernel(x_prompt, x_sample, mem_prompt, cache_a_k, cache_a_v, cache_b_k, cache_b_v, cache_b_logf,
           cache_mem_k, cache_mem_v, g_mix, w_in, b_f, g_qa, g_ka, g_qb, g_kb, g_qm, g_km, rel_bias,
           g_mem, w_mkv, w_pa, w_pb, w_pm, w_o, g_ffn, w_up, w_down):
    depth = w_in.shape[0]
    assert depth == 1
    batch, seq, d = x_prompt.shape
    dec_b, dec_s, _ = x_sample.shape
    _, _, n_cache, h_a, dh_a = cache_a_k.shape
    _, _, past, h_b, dh_b = cache_b_k.shape
    _, _, n_mem, h_m, dh_m = cache_mem_k.shape
    w_a, w_b, w_m = h_a * dh_a, h_b * dh_b, h_m * dh_m
    assert w_a == w_b == w_m == 512 and dh_a == dh_b == 64 and dh_m == 128 and h_b == 8
    keep_p = min(n_cache, seq)
    n_pad = 128
    l = 0

    qkv_end = 3 * w_a + 3 * w_b
    f_end = qkv_end + h_b
    qm_end = f_end + w_m
    w = w_in[l]
    wqkv = w[:, :qkv_end].astype(BF16)
    wft = jnp.zeros((16, d), BF16).at[:h_b].set(w[:, qkv_end:f_end].T.astype(BF16))
    wqm = w[:, f_end:qm_end].astype(BF16)
    wg = w[:, qm_end:].astype(BF16)
    gains = jnp.stack([jnp.tile(g[l], h_a) for g in (g_qa, g_ka, g_qb, g_kb)]).astype(F32)
    gqm = jnp.tile(g_qm[l], h_m)[None, :]
    gkm = jnp.tile(g_km[l], h_m)[None, :]
    bf = b_f[l][:, None].astype(F32)
    bd64 = _block_diag_mean(dh_a)
    bd128 = _block_diag_mean(dh_m)
    tri = (jnp.arange(128)[:, None] <= jnp.arange(128)[None, :]).astype(BF16)
    gmix = g_mix[l][None, :]
    proj_w = (gmix, wqkv, wft, wqm, gains, gqm, bf, bd64, bd128)
    wpa, wpb, wpm, wo = (a[l].astype(BF16) for a in (w_pa, w_pb, w_pm, w_o))
    wup, wdn = w_up[l].astype(BF16), w_down[l].astype(BF16)
    gffn = g_ffn[l][None, :]

    xs = x_sample.reshape(dec_b * dec_s, d)
    ps = _proj(xs, *proj_w, tm=512)
    sthree = lambda a: a.reshape(dec_b, dec_s, -1)
    heads_t = lambda a: jnp.transpose(a[l], (0, 2, 3, 1))
    lf_new = ps["lft"].reshape(h_b, dec_b, dec_s).transpose(1, 0, 2)
    lf_all = jnp.concatenate([cache_b_logf[l].astype(F32).transpose(0, 2, 1), lf_new,
                              jnp.zeros((dec_b, h_b, n_pad - dec_s), F32)], axis=-1)
    c_s = _cumsum_lanes(lf_all.reshape(dec_b * h_b, past + n_pad), tri, rb=32)
    forget_s = (sthree(ps["qb"]), heads_t(cache_b_k), heads_t(cache_b_v), sthree(ps["kb"]),
                sthree(ps["vb_f"]), c_s.reshape(dec_b, h_b, past + n_pad))
    bias_c = _rel_bias_rows(rel_bias[l], n_cache, dec_s, n_cache) * LOG2E
    bias_n = jnp.full((h_a * dec_s, n_pad), NEG, F32).at[:, :dec_s].set(
        _rel_bias_rows(rel_bias[l], 0, dec_s, dec_s) * LOG2E)
    band_s = (sthree(ps["qa"]), heads_t(cache_a_k), heads_t(cache_a_v), sthree(ps["ka"]), sthree(ps["va"]))
    mem_s = (sthree(ps["qm"]), cache_mem_k[l].reshape(dec_b, n_mem * h_m, dh_m),
             cache_mem_v[l].reshape(dec_b, n_mem * h_m, dh_m))

    xp = x_prompt.reshape(batch * seq, d)
    tm_p = 512
    ride = dec_b % (batch * seq // tm_p) == 0
    riders = ((functools.partial(_forget_sample_kernel, n_pad=n_pad), forget_s, ()),
              (_band_sample_kernel, band_s, (bias_c, bias_n)),
              (functools.partial(_mem_kernel, heads=h_m), mem_s, ()))
    pp = _proj(xp, *proj_w, tm=tm_p, group_rows=seq, keep_rows=keep_p, riders=riders if ride else ())
    if ride:
        o_b_s, o_a_s, o_m_s = pp["rider0"], pp["rider1"], pp["rider2"]
    else:
        o_b_s = _forget_sample(*forget_s, n_pad=n_pad)
        o_a_s = _band_sample(*band_s, bias_c, bias_n, bb=4)
        o_m_s = _mem_attn(*mem_s, heads=h_m, tq=dec_s, bb=8)
    mk_f, mv_f = _memkv(mem_prompt.reshape(batch * n_mem, d), g_mem[l][None, :], w_mkv[l].astype(BF16),
                        gkm, bd128, tm=256, heads=h_m)
    tq_b = 512
    lf_p = pp["lft"].reshape(h_b, batch, seq).transpose(1, 0, 2)
    c_p = _cumsum_lanes(lf_p.reshape(batch * h_b, seq), tri, rb=batch * h_b)
    c_p = c_p.reshape(batch, h_b // 4, 4, seq).transpose(0, 1, 3, 2)
    vt = pp["vbt"].reshape(batch, seq // tq_b, h_b, V_ROWS, tq_b)
    three = lambda a: a.reshape(batch, seq, -1)
    bias_p = _rel_bias_rows(rel_bias[l], n_cache, CHUNK, n_cache + CHUNK) * LOG2E
    bound_a = _qk_bound(g_qa[l], g_ka[l], dh_a) + LOG2E * jnp.max(jnp.abs(rel_bias[l]))
    o_a = _band_prompt(bound_a.reshape(1).astype(F32), three(pp["qa"]), three(pp["ka"]), three(pp["va"]),
                       bias_p, reach=n_cache, tq=256)
    o_b = _forget_prompt(_qk_bound(g_qb[l], g_kb[l], dh_b).reshape(1).astype(F32), three(pp["qb"]),
                         three(pp["kb"]), vt, c_p, head=dh_b, tq=tq_b)
    post_w = (gmix, wg, wpa, wpb, wpm, wo, gffn, wup, wdn)
    mem_p = (pp["qm"], mk_f.reshape(batch, n_mem * h_m, dh_m), mv_f.reshape(batch, n_mem * h_m, dh_m), h_m)
    y_prompt = _post(xp, o_a.reshape(-1, w_a), o_b.reshape(-1, w_b), None, *post_w, tm=512,
                     mem=mem_p).reshape(batch, seq, d)

    y_sample = _post(xs, o_a_s.reshape(-1, w_a), o_b_s.reshape(-1, w_b), o_m_s.reshape(-1, w_m), *post_w,
                     tm=512).reshape(dec_b, dec_s, d)

    lead = lambda a, *shape: a.reshape((depth,) + shape)
    frames_major = lambda a: jnp.transpose(a, (0, 3, 1, 2))[None]
    return (y_prompt, y_sample,
            frames_major(pp["ka_f"]), frames_major(pp["va_f"]),
            frames_major(pp["kb_f"]), frames_major(pp["vb_f"]),
            lf_p.transpose(0, 2, 1)[None],
            lead(mk_f, batch, n_mem, h_m, dh_m), lead(mv_f, batch, n_mem, h_m, dh_m),
            lead(ps["ka_f"], dec_b, dec_s, h_a, dh_a), lead(ps["va_f"], dec_b, dec_s, h_a, dh_a),
            lead(ps["kb_f"], dec_b, dec_s, h_b, dh_b), lead(ps["vb_f"], dec_b, dec_s, h_b, dh_b),
            lead(ps["lft"].T, dec_b, dec_s, h_b))
```

```python
import functools

import jax
import jax.numpy as jnp
from jax import lax
from jax.experimental import pallas as pl
from jax.experimental.pallas import tpu as pltpu

BF16 = jnp.bfloat16
F32 = jnp.float32

EPS = 1e-6
CHUNK = 64
REL_CLIP = 128
NEG = -1e30
LANE_GROUP = 256
LOG2E = 1.4426950408889634
V_ROWS = 64 + 16
MAX_ONE_PASS_BOUND = 40.0
VMEM_LIMIT = 58 * 1024 * 1024


def _cparams(*sem):
    return pltpu.CompilerParams(dimension_semantics=sem, vmem_limit_bytes=VMEM_LIMIT)


def _const_spec(shape):
    nd = len(shape)
    return pl.BlockSpec(shape, lambda *_: (0,) * nd, pipeline_mode=pl.Buffered(1))


def _nt_dot(a, b):
    return lax.dot_general(a, b, (((1,), (1,)), ((), ())), preferred_element_type=F32)


def _dot(a, b):
    return jnp.dot(a, b, preferred_element_type=F32)


def _rms_rows(x, gain):
    ms = jnp.mean(x * x, axis=-1, keepdims=True)
    return x * lax.rsqrt(ms + EPS) * gain


def _head_norm(z, bd_ref, gain):
    z2 = (z * z).astype(BF16)
    ms = jnp.concatenate(
        [_dot(z2[:, h * 256:(h + 1) * 256], bd_ref[...]) for h in range(z.shape[1] // 256)], axis=-1)
    return z * lax.rsqrt(ms + EPS) * gain


def _log_sigmoid(x):
    return jnp.minimum(x, 0.0) - jnp.log1p(jnp.exp(-jnp.abs(x)))


def _lane_group(width, head):
    return lax.broadcasted_iota(jnp.int32, (1, width), 1) // head


def _block_diag_rows(q4, head):
    grp = _lane_group(q4.shape[1], head)
    zero = jnp.zeros_like(q4)
    return jnp.concatenate([jnp.where(grp == g, q4, zero) for g in range(q4.shape[1] // head)], axis=0)


def _pick_diag(o, rows, head):
    grp = _lane_group(o.shape[1], head)
    out = jnp.zeros((rows, o.shape[1]), o.dtype)
    for g in range(o.shape[1] // head):
        out = jnp.where(grp == g, o[g * rows:(g + 1) * rows], out)
    return out


def _cast_kernel(x_ref, o_ref):
    o_ref[...] = x_ref[...].astype(o_ref.dtype)


def _store_heads_t(ref, z_t):
    dh = ref.shape[1]
    for h in range(ref.shape[0]):
        ref[h] = z_t[h * dh:(h + 1) * dh, :]


def _proj_kernel(x_ref, gmix_ref, wqkv_ref, wft_ref, wqm_ref, gains_ref, gqm_ref, bf_ref,
                 bd64_ref, bd128_ref, *refs, out_names, scale_a, scale_b, scale_m, keep_from,
                 tiles_per_group, riders):
    rider_in, out_refs = refs[:len(refs) - len(out_names)], refs[len(refs) - len(out_names):]
    o = dict(zip(out_names, out_refs))
    head_major = "vbt" in o
    xn = _rms_rows(x_ref[...], gmix_ref[...]).astype(BF16)

    def seg(j):
        return _dot(xn, wqkv_ref[:, j * 512:(j + 1) * 512])

    kb = _head_norm(seg(4), bd64_ref, gains_ref[3:4, :])
    o["kb"][...] = kb.astype(BF16)
    vb = seg(5)
    if head_major:
        _store_heads_t(o["kb_f"], kb.T)
        vb_t = vb.T
        _store_heads_t(o["vb_f"], vb_t)
        vbt = o["vbt"]
        head = vb_t.shape[0] // vbt.shape[0]
        pad = vbt.shape[1] - head
        ones_row = (lax.broadcasted_iota(jnp.int32, (pad, vb_t.shape[1]), 0) == 0).astype(BF16)
        for h in range(vbt.shape[0]):
            vbt[h, 0:head, :] = vb_t[h * head:(h + 1) * head, :].astype(BF16)
            vbt[h, head:head + pad, :] = ones_row
    else:
        o["kb_f"][...] = kb
        o["vb_f"][...] = vb
    qb = _head_norm(seg(3), bd64_ref, gains_ref[2:3, :])
    o["qb"][...] = (qb * scale_b).astype(BF16)
    qa = _head_norm(seg(0), bd64_ref, gains_ref[0:1, :])
    o["qa"][...] = (qa * scale_a).astype(BF16)
    ka = _head_norm(seg(1), bd64_ref, gains_ref[1:2, :])
    o["ka"][...] = ka.astype(BF16)
    qm = _head_norm(_dot(xn, wqm_ref[...]), bd128_ref, gqm_ref[...])
    o["qm"][...] = (qm * scale_m).astype(BF16)
    flt = _nt_dot(wft_ref[...], xn)
    o["lft"][...] = _log_sigmoid(flt[0:8, :] + bf_ref[...])
    va = seg(2)
    o["va"][...] = va.astype(BF16)
    if head_major:
        @pl.when(pl.program_id(0) % tiles_per_group >= keep_from)
        def _():
            _store_heads_t(o["ka_f"], ka.T)
            _store_heads_t(o["va_f"], va.T)
    else:
        o["ka_f"][...] = ka
        o["va_f"][...] = va
    for k, (rider_kernel, n_in) in enumerate(riders):
        rider_kernel(*rider_in[:n_in], o[f"rider{k}"])
        rider_in = rider_in[n_in:]


def _proj(x, gmix, wqkv, wft, wqm, gains, gqm, bf, bd64, bd128, *, tm, group_rows=None, keep_rows=None,
          riders=()):
    rows, d = x.shape
    n_tiles = rows // tm
    row = lambda i: (i, 0)
    wide = lambda dt: (jax.ShapeDtypeStruct((rows, 512), dt), pl.BlockSpec((tm, 512), row))
    outs = {n: wide(BF16) for n in ("qa", "ka", "va", "qb", "kb", "qm")}
    outs["lft"] = (jax.ShapeDtypeStruct((8, rows), F32), pl.BlockSpec((8, tm), lambda i: (0, i)))
    tpg = keep_from = 1
    if group_rows is None:
        for n in ("ka_f", "va_f", "kb_f", "vb_f"):
            outs[n] = wide(F32)
    else:
        assert keep_rows == tm
        tpg = group_rows // tm
        keep_from = tpg - 1
        groups = rows // group_rows
        kept = (jax.ShapeDtypeStruct((groups, 8, 64, keep_rows), F32),
                pl.BlockSpec((None, 8, 64, tm), lambda i: (i // tpg, 0, 0, 0)))
        full = (jax.ShapeDtypeStruct((groups, 8, 64, group_rows), F32),
                pl.BlockSpec((None, 8, 64, tm), lambda i: (i // tpg, 0, 0, i % tpg)))
        outs.update(ka_f=kept, va_f=kept, kb_f=full, vb_f=full)
        outs["vbt"] = (jax.ShapeDtypeStruct((n_tiles, 8, V_ROWS, tm), BF16),
                       pl.BlockSpec((None, 8, V_ROWS, tm), lambda i: (i, 0, 0, 0)))
    in_specs = [pl.BlockSpec((tm, d), row)] + [_const_spec(a.shape) for a in
                                               (gmix, wqkv, wft, wqm, gains, gqm, bf, bd64, bd128)]
    rider_args, rider_kernels = [], []
    for k, (rider_kernel, per_tile, const) in enumerate(riders):
        assert all(a.shape[0] % n_tiles == 0 for a in per_tile)
        tile_spec = lambda a: pl.BlockSpec((a.shape[0] // n_tiles,) + a.shape[1:],
                                           lambda i, nd=a.ndim: (i,) + (0,) * (nd - 1))
        in_specs += [tile_spec(a) for a in per_tile] + [_const_spec(a.shape) for a in const]
        outs[f"rider{k}"] = (jax.ShapeDtypeStruct(per_tile[0].shape, BF16), tile_spec(per_tile[0]))
        rider_args += [*per_tile, *const]
        rider_kernels.append((rider_kernel, len(per_tile) + len(const)))
    names = tuple(outs)
    kern = functools.partial(_proj_kernel, out_names=names, scale_a=LOG2E * 64 ** -0.5,
                             scale_b=LOG2E * 64 ** -0.5, scale_m=LOG2E * 128 ** -0.5,
                             keep_from=keep_from, tiles_per_group=tpg, riders=tuple(rider_kernels))
    res = pl.pallas_call(kern, out_shape=[outs[n][0] for n in names], grid=(n_tiles,), in_specs=in_specs,
                         out_specs=[outs[n][1] for n in names], compiler_params=_cparams("arbitrary"),
                         name="proj")(x, gmix, wqkv, wft, wqm, gains, gqm, bf, bd64, bd128, *rider_args)
    return dict(zip(names, res))


def _memkv_kernel(m_ref, gmem_ref, w_ref, gkm_ref, bd128_ref, mk_o, mv_o, *, heads):
    tm = m_ref.shape[0]
    xn = _rms_rows(m_ref[...], gmem_ref[...]).astype(BF16)
    half = w_ref.shape[1] // 2
    dh = half // heads
    mk = _head_norm(_dot(xn, w_ref[:, :half]), bd128_ref, gkm_ref[...])
    mv = _dot(xn, w_ref[:, half:])
    for h in range(heads):
        mk_o[pl.ds(h, tm, stride=heads), :] = mk[:, h * dh:(h + 1) * dh]
        mv_o[pl.ds(h, tm, stride=heads), :] = mv[:, h * dh:(h + 1) * dh]


def _memkv(mem, gmem, w, gkm, bd128, *, tm, heads):
    rows, d = mem.shape
    dh = w.shape[1] // 2 // heads
    row = lambda i: (i, 0)
    return pl.pallas_call(
        functools.partial(_memkv_kernel, heads=heads),
        out_shape=[jax.ShapeDtypeStruct((rows * heads, dh), F32)] * 2, grid=(rows // tm,),
        in_specs=[pl.BlockSpec((tm, d), row)] + [_const_spec(a.shape) for a in (gmem, w, gkm, bd128)],
        out_specs=[pl.BlockSpec((tm * heads, dh), row)] * 2, compiler_params=_cparams("arbitrary"),
        name="memkv")(mem, gmem, w, gkm, bd128)


def _cumsum_kernel(x_ref, tri_ref, o_ref):
    rows, length = x_ref.shape
    tri = tri_ref[...]

    def local(c):
        x = x_ref[:, c * 128:(c + 1) * 128]
        hi = x.astype(BF16)
        r1 = x - hi.astype(F32)
        mid = r1.astype(BF16)
        lo = (r1 - mid.astype(F32)).astype(BF16)
        return _dot(hi, tri) + _dot(mid, tri) + _dot(lo, tri)

    chunks = [local(c) for c in range(length // 128)]
    carry = jnp.zeros((rows, 1), F32)
    for c, cc in enumerate(chunks):
        o_ref[:, c * 128:(c + 1) * 128] = cc + carry
        carry = carry + cc[:, 127:128]


def _cumsum_lanes(x, tri, *, rb):
    rows, length = x.shape
    return pl.pallas_call(
        _cumsum_kernel, out_shape=jax.ShapeDtypeStruct((rows, length), F32), grid=(rows // rb,),
        in_specs=[pl.BlockSpec((rb, length), lambda i: (i, 0)), _const_spec(tri.shape)],
        out_specs=pl.BlockSpec((rb, length), lambda i: (i, 0)), compiler_params=_cparams("arbitrary"),
        name="cumsum")(x, tri)


def _band_kernel(bound_ref, q_ref, k0_ref, k1_ref, k2_ref, v0_ref, v1_ref, v2_ref, bias_ref, o_ref,
                 k_sc, v_sc, *, reach, lookahead):
    tq = q_ref.shape[0]
    band = reach + CHUNK
    i = pl.program_id(1)
    for n, (kr, vr) in enumerate(((k0_ref, v0_ref), (k1_ref, v1_ref), (k2_ref, v2_ref))):
        k_sc[n * tq:(n + 1) * tq, :] = kr[...]
        v_sc[n * tq:(n + 1) * tq, :] = vr[...]
    col = lax.broadcasted_iota(jnp.int32, (1, band), 1)
    units = [(j, hg) for j in range(tq // CHUNK) for hg in range(q_ref.shape[1] // LANE_GROUP)]

    def scores(j, hg):
        r0 = 2 * tq + j * CHUNK - reach
        lanes = slice(hg * LANE_GROUP, (hg + 1) * LANE_GROUP)
        qbd = _block_diag_rows(q_ref[j * CHUNK:(j + 1) * CHUNK, lanes], CHUNK)
        s = _nt_dot(qbd, k_sc[r0:r0 + band, lanes]) + bias_ref[hg * 4 * CHUNK:(hg + 1) * 4 * CHUNK, :]
        return jnp.where((i - 2) * tq + r0 + col >= 0, s, NEG)

    def attend(one_pass):
        pending = [scores(*u) for u in units[:lookahead]]
        for n, (j, hg) in enumerate(units):
            s = pending.pop(0)
            if n + lookahead < len(units):
                pending.append(scores(*units[n + lookahead]))
            r0 = 2 * tq + j * CHUNK - reach
            lanes = slice(hg * LANE_GROUP, (hg + 1) * LANE_GROUP)
            p = jnp.exp2(s if one_pass else s - jnp.max(s, axis=-1, keepdims=True))
            l = jnp.sum(p, axis=-1, keepdims=True)
            o = _dot(p.astype(BF16), v_sc[r0:r0 + band, lanes]) / l
            o_ref[j * CHUNK:(j + 1) * CHUNK, lanes] = _pick_diag(o, CHUNK, CHUNK).astype(o_ref.dtype)

    lax.cond(bound_ref[0] <= MAX_ONE_PASS_BOUND, lambda: attend(True), lambda: attend(False))


def _band_prompt(logit_bound, q, k, v, bias, *, reach, tq):
    b, t, w = q.shape
    assert reach == 2 * tq and tq % CHUNK == 0
    blk = lambda off: pl.BlockSpec((None, tq, w), lambda bi, i: (bi, jnp.maximum(i - off, 0), 0))
    return pl.pallas_call(
        functools.partial(_band_kernel, reach=reach, lookahead=2),
        out_shape=jax.ShapeDtypeStruct((b, t, w), BF16), grid=(b, t // tq),
        in_specs=[pl.BlockSpec(memory_space=pltpu.SMEM), blk(0), blk(2), blk(1), blk(0), blk(2), blk(1), blk(0),
                  _const_spec(bias.shape)],
        out_specs=blk(0),
        scratch_shapes=[pltpu.VMEM((3 * tq, w), BF16), pltpu.VMEM((3 * tq, w), BF16)],
        compiler_params=_cparams("arbitrary", "arbitrary"), name="band_prompt")(
            logit_bound, q, k, k, k, v, v, v, bias)


def _forget_kernel(bound_ref, q_ref, k_ref, vt_ref, c_ref, o_ref, qt_sc, m_sc, acc_sc, *, head, lookahead):
    tq, width = q_ref.shape
    n_heads = width // head
    i = pl.program_id(2)
    logit_bound = bound_ref[0]
    q_t = q_ref[...].astype(F32).T
    row_grp = lax.broadcasted_iota(jnp.int32, (width, 1), 0) // head
    for g in range(n_heads):
        qt_sc[g] = jnp.where(row_grp == g, q_t, 0.0).astype(BF16)
    m_sc[...] = jnp.full(m_sc.shape, NEG, F32)
    acc_sc[...] = jnp.zeros(acc_sc.shape, F32)
    c_first = c_ref[pl.ds(pl.multiple_of(i * tq, tq), 8), :][0:1, :]

    def step(j, masked):
        k0 = pl.multiple_of(j * tq, tq)
        kt = k_ref[pl.ds(k0, tq), :]
        bias = (c_first - c_ref[pl.ds(k0, tq), :]) * LOG2E
        if masked:
            keep = (lax.broadcasted_iota(jnp.int32, (tq, tq), 0)
                    <= lax.broadcasted_iota(jnp.int32, (tq, tq), 1))

        def scores(g):
            s = _dot(kt, qt_sc[g]) + bias[:, g:g + 1]
            return jnp.where(keep, s, NEG) if masked else s

        pending = [scores(g) for g in range(min(lookahead, n_heads))]
        for g in range(n_heads):
            s = pending.pop(0)
            if g + lookahead < n_heads:
                pending.append(scores(g + lookahead))
            m_prev = m_sc[g]
            m_new = jnp.maximum(m_prev, jnp.max(s, axis=0, keepdims=True))
            p = jnp.exp2(s - m_new).astype(BF16)
            acc_sc[g] = jnp.exp2(m_prev - m_new) * acc_sc[g] + _dot(vt_ref[j, g], p)
            m_sc[g] = m_new

    def bounded_tiles(tiles):
        m_tile, logits = [], []
        for j in tiles:
            k0 = pl.multiple_of(j * tq, tq)
            c_tile = c_ref[pl.ds(k0, tq), :]
            c_last = c_tile[tq - 1:tq, :]
            m_tile.append((c_first - c_last) * LOG2E)
            shift = (c_last - c_tile) * LOG2E
            logits.append(functools.partial(
                lambda g, k0, shift: _dot(k_ref[pl.ds(k0, tq), :], qt_sc[g]) + shift[:, g:g + 1],
                k0=k0, shift=shift))
        units = [(t, g) for t in range(len(tiles)) for g in range(n_heads)]
        pending = [logits[t](g) for t, g in units[:lookahead]]
        for n, (t, g) in enumerate(units):
            s = pending.pop(0)
            if n + lookahead < len(units):
                tn, gn = units[n + lookahead]
                pending.append(logits[tn](gn))
            m_new = jnp.broadcast_to(m_tile[t][:, g:g + 1], (1, tq))
            p = jnp.exp2(s).astype(BF16)
            acc_sc[g] = jnp.exp2(m_sc[g] - m_new) * acc_sc[g] + _dot(vt_ref[tiles[t], g], p)
            m_sc[g] = m_new

    def bounded_loop():
        def pair(jj, carry):
            bounded_tiles([2 * jj, 2 * jj + 1])
            return carry
        lax.fori_loop(0, i // 2, pair, 0)

        @pl.when(i % 2 == 1)
        def _():
            bounded_tiles([i - 1])

    def two_pass_loop():
        def body(j, carry):
            step(j, False)
            return carry
        lax.fori_loop(0, i, body, 0)

    lax.cond(logit_bound <= MAX_ONE_PASS_BOUND, bounded_loop, two_pass_loop)
    step(i, True)
    o_t = jnp.concatenate([acc_sc[g, 0:head, :] / acc_sc[g, head:head + 1, :] for g in range(n_heads)],
                          axis=0)
    o_ref[...] = o_t.T.astype(o_ref.dtype)


def _forget_prompt(logit_bound, q, k, vt, c, *, head, tq):
    b, t, w = q.shape
    n_heads = LANE_GROUP // head
    v_rows = vt.shape[3]
    lookahead = 3
    assert lookahead <= n_heads
    return pl.pallas_call(
        functools.partial(_forget_kernel, head=head, lookahead=lookahead),
        out_shape=jax.ShapeDtypeStruct((b, t, w), BF16), grid=(b, w // LANE_GROUP, t // tq),
        in_specs=[pl.BlockSpec(memory_space=pltpu.SMEM),
                  pl.BlockSpec((None, tq, LANE_GROUP), lambda bi, hg, i: (bi, i, hg)),
                  pl.BlockSpec((None, t, LANE_GROUP), lambda bi, hg, i: (bi, 0, hg)),
                  pl.BlockSpec((None, t // tq, n_heads, v_rows, tq), lambda bi, hg, i: (bi, 0, hg, 0, 0)),
                  pl.BlockSpec((None, None, t, n_heads), lambda bi, hg, i: (bi, hg, 0, 0))],
        out_specs=pl.BlockSpec((None, tq, LANE_GROUP), lambda bi, hg, i: (bi, i, hg)),
        scratch_shapes=[pltpu.VMEM((n_heads, LANE_GROUP, tq), BF16), pltpu.VMEM((n_heads, 1, tq), F32),
                        pltpu.VMEM((n_heads, v_rows, tq), F32)],
        compiler_params=_cparams("arbitrary", "arbitrary", "arbitrary"), name="forget_prompt")(
            logit_bound, q, k, vt, c)


def _mem_attention(qs, mk_refs, mv_refs, heads):
    n_mem = mk_refs[0].shape[0] // heads
    dh = mk_refs[0].shape[1]
    units = [(e, h) for e in range(len(qs)) for h in range(heads)]
    rows = lambda ref, h: ref[pl.ds(h, n_mem, stride=heads), :].astype(BF16)
    s = [_nt_dot(qs[e][:, h * dh:(h + 1) * dh], rows(mk_refs[e], h)) for e, h in units]
    p, inv_l = [], []
    for u in range(len(units)):
        ex = jnp.exp2(s[u] - jnp.max(s[u], axis=-1, keepdims=True))
        inv_l.append(1.0 / jnp.sum(ex, axis=-1, keepdims=True))
        p.append(ex.astype(BF16))
    o = [(_dot(p[u], rows(mv_refs[e], h)) * inv_l[u]).astype(BF16) for u, (e, h) in enumerate(units)]
    return [jnp.concatenate(o[e * heads:(e + 1) * heads], axis=-1) for e in range(len(qs))]


def _mem_kernel(q_ref, mk_ref, mv_ref, o_ref, *, heads):
    bb = q_ref.shape[0]
    outs = _mem_attention([q_ref[e] for e in range(bb)], [mk_ref.at[e] for e in range(bb)],
                          [mv_ref.at[e] for e in range(bb)], heads)
    for e in range(bb):
        o_ref[e] = outs[e]


def _mem_attn(q, mk, mv, *, heads, tq, bb):
    b, t, w = q.shape
    kv = pl.BlockSpec((bb,) + mk.shape[1:], lambda bi, i: (bi, 0, 0))
    qs = pl.BlockSpec((bb, tq, w), lambda bi, i: (bi, i, 0))
    return pl.pallas_call(
        functools.partial(_mem_kernel, heads=heads), out_shape=jax.ShapeDtypeStruct((b, t, w), BF16),
        grid=(b // bb, t // tq), in_specs=[qs, kv, kv], out_specs=qs,
        compiler_params=_cparams("arbitrary", "arbitrary"), name="mem_attn")(q, mk, mv)


def _post_kernel(x_ref, oa_ref, ob_ref, *refs, mem_heads, col_chunk, ff_chunk):
    if mem_heads:
        qm_ref, mk_ref, mv_ref, *refs = refs
        o_m, = _mem_attention([qm_ref[...]], [mk_ref], [mv_ref], mem_heads)
    else:
        om_ref, *refs = refs
        o_m = om_ref[...]
    gmix_ref, wg_ref, wpa_ref, wpb_ref, wpm_ref, wo_ref, gffn_ref, wup_ref, wdn_ref, y_ref = refs
    x = x_ref[...]
    d = x.shape[1]
    xn = _rms_rows(x, gmix_ref[...]).astype(BF16)
    branches = ((oa_ref[...], wpa_ref), (ob_ref[...], wpb_ref), (o_m, wpm_ref))
    h = []
    for c in range(d // col_chunk):
        cols = slice(c * col_chunk, (c + 1) * col_chunk)
        hc = 0.0
        for b, (o_b, wp_ref) in enumerate(branches):
            gate = jax.nn.sigmoid(_dot(xn, wg_ref[:, b * d + c * col_chunk:b * d + (c + 1) * col_chunk]))
            hc = hc + gate * _dot(o_b, wp_ref[:, cols])
        h.append(hc.astype(BF16))
    x1 = x + _dot(jnp.concatenate(h, axis=-1), wo_ref[...])
    xn1 = _rms_rows(x1, gffn_ref[...]).astype(BF16)
    y = x1
    for c in range(wup_ref.shape[1] // ff_chunk):
        cols = slice(c * ff_chunk, (c + 1) * ff_chunk)
        u = jnp.maximum(_dot(xn1, wup_ref[:, cols]), 0.0)
        y = y + _dot((u * u).astype(BF16), wdn_ref[cols, :])
    y_ref[...] = y


def _post(x, oa, ob, om, gmix, wg, wpa, wpb, wpm, wo, gffn, wup, wdn, *, tm, mem=None):
    rows, d = x.shape
    row = lambda i: (i, 0)
    acts = [x, oa, ob]
    specs = [pl.BlockSpec((tm, a.shape[1]), row) for a in acts]
    heads = 0
    if om is None:
        qm, mk, mv, heads = mem
        tiles_per_batch = rows // mk.shape[0] // tm
        kv = pl.BlockSpec((None,) + mk.shape[1:], lambda i: (i // tiles_per_batch, 0, 0))
        acts += [qm, mk, mv]
        specs += [pl.BlockSpec((tm, qm.shape[1]), row), kv, kv]
    else:
        acts.append(om)
        specs.append(pl.BlockSpec((tm, om.shape[1]), row))
    weights = (gmix, wg, wpa, wpb, wpm, wo, gffn, wup, wdn)
    return pl.pallas_call(
        functools.partial(_post_kernel, mem_heads=heads, col_chunk=512, ff_chunk=1024),
        out_shape=jax.ShapeDtypeStruct((rows, d), F32), grid=(rows // tm,),
        in_specs=specs + [_const_spec(a.shape) for a in weights],
        out_specs=pl.BlockSpec((tm, d), row), compiler_params=_cparams("arbitrary"),
        name="post")(*acts, *weights)


def _pad_rows(x, rows):
    return jnp.concatenate([x, jnp.zeros((rows - x.shape[0], x.shape[1]), x.dtype)], axis=0)


def _cached_attention(q_ref, kc_ref, vc_ref, kn_ref, vn_ref, o_ref, bias_c, bias_n, n_pad):
    bb, heads, head, _ = kc_ref.shape
    units = [(e, h, slice(h * head, (h + 1) * head)) for e in range(bb) for h in range(heads)]
    sc = [_dot(q_ref[e, :, ln], kc_ref[e, h].astype(BF16)) + bias_c(e, h) for e, h, ln in units]
    sn = [_nt_dot(q_ref[e, :, ln], _pad_rows(kn_ref[e, :, ln].astype(BF16), n_pad)) + bias_n(e, h)
          for e, h, ln in units]
    pc, pn, inv_l = [], [], []
    for u in range(len(units)):
        m = jnp.maximum(jnp.max(sc[u], axis=-1, keepdims=True), jnp.max(sn[u], axis=-1, keepdims=True))
        pc.append(jnp.exp2(sc[u] - m))
        pn.append(jnp.exp2(sn[u] - m))
        inv_l.append(1.0 / (jnp.sum(pc[u], axis=-1, keepdims=True) + jnp.sum(pn[u], axis=-1, keepdims=True)))
    for u, (e, h, ln) in enumerate(units):
        o = (_nt_dot(pc[u].astype(BF16), vc_ref[e, h].astype(BF16))
             + _dot(pn[u].astype(BF16), _pad_rows(vn_ref[e, :, ln].astype(BF16), n_pad)))
        o_ref[e, :, ln] = (o * inv_l[u]).astype(o_ref.dtype)


def _band_sample_kernel(q_ref, kc_ref, vc_ref, kn_ref, vn_ref, bc_ref, bn_ref, o_ref):
    s_len = q_ref.shape[1]
    rows = lambda h: slice(h * s_len, (h + 1) * s_len)
    _cached_attention(q_ref, kc_ref, vc_ref, kn_ref, vn_ref, o_ref,
                      lambda e, h: bc_ref[rows(h), :], lambda e, h: bn_ref[rows(h), :], bn_ref.shape[1])


def _band_sample(q, kct, vct, kn, vn, bias_c, bias_n, *, bb):
    b, s_len, w = q.shape
    new = pl.BlockSpec((bb, s_len, w), lambda bi: (bi, 0, 0))
    cache = pl.BlockSpec((bb,) + kct.shape[1:], lambda bi: (bi, 0, 0, 0))
    return pl.pallas_call(
        _band_sample_kernel, out_shape=jax.ShapeDtypeStruct((b, s_len, w), BF16), grid=(b // bb,),
        in_specs=[new, cache, cache, new, new, _const_spec(bias_c.shape), _const_spec(bias_n.shape)],
        out_specs=new, compiler_params=_cparams("arbitrary"), name="band_sample")(
            q, kct, vct, kn, vn, bias_c, bias_n)


def _forget_sample_kernel(q_ref, kc_ref, vc_ref, kn_ref, vn_ref, c_ref, o_ref, *, n_pad):
    s_len = q_ref.shape[1]
    p_len = kc_ref.shape[3]
    causal = (lax.broadcasted_iota(jnp.int32, (s_len, n_pad), 1)
              <= lax.broadcasted_iota(jnp.int32, (s_len, n_pad), 0))
    ch = [(c_ref[e, :, p_len - 1:p_len] - c_ref[e]) * LOG2E for e in range(c_ref.shape[0])]
    _cached_attention(q_ref, kc_ref, vc_ref, kn_ref, vn_ref, o_ref,
                      lambda e, h: ch[e][h:h + 1, 0:p_len],
                      lambda e, h: jnp.where(causal, ch[e][h:h + 1, p_len:p_len + n_pad], NEG), n_pad)


def _forget_sample(q, kct, vct, kn, vn, c, *, n_pad):
    b, s_len, w = q.shape
    new = pl.BlockSpec((1, s_len, w), lambda bi: (bi, 0, 0))
    cache = pl.BlockSpec((1,) + kct.shape[1:], lambda bi: (bi, 0, 0, 0))
    cs = pl.BlockSpec((1,) + c.shape[1:], lambda bi: (bi, 0, 0))
    return pl.pallas_call(
        functools.partial(_forget_sample_kernel, n_pad=n_pad),
        out_shape=jax.ShapeDtypeStruct((b, s_len, w), BF16), grid=(b,),
        in_specs=[new, cache, cache, new, new, cs], out_specs=new,
        compiler_params=_cparams("arbitrary"), name="forget_sample")(q, kct, vct, kn, vn, c)


def _qk_bound(g_q, g_k, dh):
    return 1.02 * LOG2E * dh ** 0.5 * jnp.max(jnp.abs(g_q)) * jnp.max(jnp.abs(g_k))


def _block_diag_mean(head, size=256):
    r = jnp.arange(size) // head
    return jnp.where(r[:, None] == r[None, :], 1.0 / head, 0.0).astype(BF16)


def _rel_bias_rows(rel_bias, d0, n_q, n_k):
    n = n_q + n_k - 1
    dist = d0 + (n_q - 1) - jnp.arange(n)
    e = rel_bias[jnp.clip(dist, -REL_CLIP, REL_CLIP) + REL_CLIP].T.astype(F32)
    h = e.shape[0]
    skew = jnp.tile(jnp.pad(e, ((0, 0), (0, 1))), (1, n_q))[:, :n_q * n].reshape(h, n_q, n)
    return skew[:, :, n_q - 1:].reshape(h * n_q, n_k)


def kernel(x_prompt, x_sample, mem_prompt, cache_a_k, cache_a_v, cache_b_k, cache_b_v, cache_b_logf,
           cache_mem_k, cache_mem_v, g_mix, w_in, b_f, g_qa, g_ka, g_qb, g_kb, g_qm, g_km, rel_bias,
           g_mem, w_mkv, w_pa, w_pb, w_pm, w_o, g_ffn, w_up, w_down):
    depth = w_in.shape[0]
    assert depth == 1
    batch, seq, d = x_prompt.shape
    dec_b, dec_s, _ = x_sample.shape
    _, _, n_cache, h_a, dh_a = cache_a_k.shape
    _, _, past, h_b, dh_b = cache_b_k.shape
    _, _, n_mem, h_m, dh_m = cache_mem_k.shape
    w_a, w_b, w_m = h_a * dh_a, h_b * dh_b, h_m * dh_m
    assert w_a == w_b == w_m == 512 and dh_a == dh_b == 64 and dh_m == 128 and h_b == 8
    keep_p = min(n_cache, seq)
    n_pad = 128
    l = 0

    qkv_end = 3 * w_a + 3 * w_b
    f_end = qkv_end + h_b
    qm_end = f_end + w_m
    w = w_in[l]
    wqkv = w[:, :qkv_end].astype(BF16)
    wft = jnp.zeros((16, d), BF16).at[:h_b].set(w[:, qkv_end:f_end].T.astype(BF16))
    wqm = w[:, f_end:qm_end].astype(BF16)
    wg = w[:, qm_end:].astype(BF16)
    gains = jnp.stack([jnp.tile(g[l], h_a) for g in (g_qa, g_ka, g_qb, g_kb)]).astype(F32)
    gqm = jnp.tile(g_qm[l], h_m)[None, :]
    gkm = jnp.tile(g_km[l], h_m)[None, :]
    bf = b_f[l][:, None].astype(F32)
    bd64 = _block_diag_mean(dh_a)
    bd128 = _block_diag_mean(dh_m)
    tri = (jnp.arange(128)[:, None] <= jnp.arange(128)[None, :]).astype(BF16)
    gmix = g_mix[l][None, :]
    proj_w = (gmix, wqkv, wft, wqm, gains, gqm, bf, bd64, bd128)
    gffn = g_ffn[l][None, :]

    xs = x_sample.reshape(dec_b * dec_s, d)
    ps = _proj(xs, *proj_w, tm=512)
    sthree = lambda a: a.reshape(dec_b, dec_s, -1)
    heads_t = lambda a: jnp.transpose(a[l], (0, 2, 3, 1))
    lf_new = ps["lft"].reshape(h_b, dec_b, dec_s).transpose(1, 0, 2)
    lf_all = jnp.concatenate([cache_b_logf[l].astype(F32).transpose(0, 2, 1), lf_new,
                              jnp.zeros((dec_b, h_b, n_pad - dec_s), F32)], axis=-1)
    c_s = _cumsum_lanes(lf_all.reshape(dec_b * h_b, past + n_pad), tri, rb=32)
    forget_s = (sthree(ps["qb"]), heads_t(cache_b_k), heads_t(cache_b_v), sthree(ps["kb"]),
                sthree(ps["vb_f"]), c_s.reshape(dec_b, h_b, past + n_pad))
    bias_c = _rel_bias_rows(rel_bias[l], n_cache, dec_s, n_cache) * LOG2E
    bias_n = jnp.full((h_a * dec_s, n_pad), NEG, F32).at[:, :dec_s].set(
        _rel_bias_rows(rel_bias[l], 0, dec_s, dec_s) * LOG2E)
    band_s = (sthree(ps["qa"]), heads_t(cache_a_k), heads_t(cache_a_v), sthree(ps["ka"]), sthree(ps["va"]))
    mem_s = (sthree(ps["qm"]), cache_mem_k[l].reshape(dec_b, n_mem * h_m, dh_m),
             cache_mem_v[l].reshape(dec_b, n_mem * h_m, dh_m))

    xp = x_prompt.reshape(batch * seq, d)
    tm_p = 512
    ride = dec_b % (batch * seq // tm_p) == 0
    riders = tuple((_cast_kernel, (a[l],), ()) for a in (w_pa, w_pb, w_pm, w_o, w_up, w_down))
    if ride:
        riders += ((functools.partial(_forget_sample_kernel, n_pad=n_pad), forget_s, ()),
                   (_band_sample_kernel, band_s, (bias_c, bias_n)),
                   (functools.partial(_mem_kernel, heads=h_m), mem_s, ()))
    pp = _proj(xp, *proj_w, tm=tm_p, group_rows=seq, keep_rows=keep_p, riders=riders)
    wpa, wpb, wpm, wo, wup, wdn = (pp[f"rider{k}"] for k in range(6))
    if ride:
        o_b_s, o_a_s, o_m_s = pp["rider6"], pp["rider7"], pp["rider8"]
    else:
        o_b_s = _forget_sample(*forget_s, n_pad=n_pad)
        o_a_s = _band_sample(*band_s, bias_c, bias_n, bb=4)
        o_m_s = _mem_attn(*mem_s, heads=h_m, tq=dec_s, bb=8)
    mk_f, mv_f = _memkv(mem_prompt.reshape(batch * n_mem, d), g_mem[l][None, :], w_mkv[l].astype(BF16),
                        gkm, bd128, tm=256, heads=h_m)
    tq_b = 512
    lf_p = pp["lft"].reshape(h_b, batch, seq).transpose(1, 0, 2)
    c_p = _cumsum_lanes(lf_p.reshape(batch * h_b, seq), tri, rb=batch * h_b)
    c_p = c_p.reshape(batch, h_b // 4, 4, seq).transpose(0, 1, 3, 2)
    vt = pp["vbt"].reshape(batch, seq // tq_b, h_b, V_ROWS, tq_b)
    three = lambda a: a.reshape(batch, seq, -1)
    bias_p = _rel_bias_rows(rel_bias[l], n_cache, CHUNK, n_cache + CHUNK) * LOG2E
    bound_a = _qk_bound(g_qa[l], g_ka[l], dh_a) + LOG2E * jnp.max(jnp.abs(rel_bias[l]))
    o_a = _band_prompt(bound_a.reshape(1).astype(F32), three(pp["qa"]), three(pp["ka"]), three(pp["va"]),
                       bias_p, reach=n_cache, tq=256)
    o_b = _forget_prompt(_qk_bound(g_qb[l], g_kb[l], dh_b).reshape(1).astype(F32), three(pp["qb"]),
                         three(pp["kb"]), vt, c_p, head=dh_b, tq=tq_b)
    post_w = (gmix, wg, wpa, wpb, wpm, wo, gffn, wup, wdn)
    mem_p = (pp["qm"], mk_f.reshape(batch, n_mem * h_m, dh_m), mv_f.reshape(batch, n_mem * h_m, dh_m), h_m)
    y_prompt = _post(xp, o_a.reshape(-1, w_a), o_b.reshape(-1, w_b), None, *post_w, tm=512,
                     mem=mem_p).reshape(batch, seq, d)

    y_sample = _post(xs, o_a_s.reshape(-1, w_a), o_b_s.reshape(-1, w_b), o_m_s.reshape(-1, w_m), *post_w,
                     tm=512).reshape(dec_b, dec_s, d)

    lead = lambda a, *shape: a.reshape((depth,) + shape)
    frames_major = lambda a: jnp.transpose(a, (0, 3, 1, 2))[None]
    return (y_prompt, y_sample,
            frames_major(pp["ka_f"]), frames_major(pp["va_f"]),
            frames_major(pp["kb_f"]), frames_major(pp["vb_f"]),
            lf_p.transpose(0, 2, 1)[None],
            lead(mk_f, batch, n_mem, h_m, dh_m), lead(mv_f, batch, n_mem, h_m, dh_m),
            lead(ps["ka_f"], dec_b, dec_s, h_a, dh_a), lead(ps["va_f"], dec_b, dec_s, h_a, dh_a),
            lead(ps["kb_f"], dec_b, dec_s, h_b, dh_b), lead(ps["vb_f"], dec_b, dec_s, h_b, dh_b),
            lead(ps["lft"].T, dec_b, dec_s, h_b))
```

```python
import functools

import jax
import jax.numpy as jnp
from jax import lax
from jax.experimental import pallas as pl
from jax.experimental.pallas import tpu as pltpu

BF16 = jnp.bfloat16
F32 = jnp.float32

EPS = 1e-6
CHUNK = 64
REL_CLIP = 128
NEG = -1e30
LANE_GROUP = 256
LOG2E = 1.4426950408889634
V_ROWS = 64 + 16
MAX_ONE_PASS_BOUND = 40.0
VMEM_LIMIT = 58 * 1024 * 1024


def _cparams(*sem):
    return pltpu.CompilerParams(dimension_semantics=sem, vmem_limit_bytes=VMEM_LIMIT)


def _const_spec(shape):
    nd = len(shape)
    return pl.BlockSpec(shape, lambda *_: (0,) * nd, pipeline_mode=pl.Buffered(1))


def _nt_dot(a, b):
    return lax.dot_general(a, b, (((1,), (1,)), ((), ())), preferred_element_type=F32)


def _dot(a, b):
    return jnp.dot(a, b, preferred_element_type=F32)


def _rms_rows(x, gain):
    ms = jnp.mean(x * x, axis=-1, keepdims=True)
    return x * lax.rsqrt(ms + EPS) * gain


def _head_norm(z, bd_ref, gain):
    z2 = (z * z).astype(BF16)
    ms = jnp.concatenate(
        [_dot(z2[:, h * 256:(h + 1) * 256], bd_ref[...]) for h in range(z.shape[1] // 256)], axis=-1)
    return z * lax.rsqrt(ms + EPS) * gain


def _log_sigmoid(x):
    return jnp.minimum(x, 0.0) - jnp.log1p(jnp.exp(-jnp.abs(x)))


def _lane_group(width, head):
    return lax.broadcasted_iota(jnp.int32, (1, width), 1) // head


def _block_diag_rows(q4, head):
    grp = _lane_group(q4.shape[1], head)
    zero = jnp.zeros_like(q4)
    return jnp.concatenate([jnp.where(grp == g, q4, zero) for g in range(q4.shape[1] // head)], axis=0)


def _pick_diag(o, rows, head):
    grp = _lane_group(o.shape[1], head)
    out = jnp.zeros((rows, o.shape[1]), o.dtype)
    for g in range(o.shape[1] // head):
        out = jnp.where(grp == g, o[g * rows:(g + 1) * rows], out)
    return out


def _cast_kernel(x_ref, o_ref):
    o_ref[...] = x_ref[...].astype(o_ref.dtype)


def _store_heads_t(ref, z_t):
    dh = ref.shape[1]
    for h in range(ref.shape[0]):
        ref[h] = z_t[h * dh:(h + 1) * dh, :]


def _proj_kernel(x_ref, gmix_ref, wqkv_ref, wft_ref, wqm_ref, gains_ref, gqm_ref, bf_ref,
                 bd64_ref, bd128_ref, *refs, out_names, scale_a, scale_b, scale_m, keep_from,
                 tiles_per_group, riders):
    rider_in, out_refs = refs[:len(refs) - len(out_names)], refs[len(refs) - len(out_names):]
    o = dict(zip(out_names, out_refs))
    head_major = "vbt" in o
    xn = _rms_rows(x_ref[...], gmix_ref[...]).astype(BF16)

    def seg(j):
        return _dot(xn, wqkv_ref[:, j * 512:(j + 1) * 512])

    kb = _head_norm(seg(4), bd64_ref, gains_ref[3:4, :])
    o["kb"][...] = kb.astype(BF16)
    vb = seg(5)
    if head_major:
        _store_heads_t(o["kb_f"], kb.T)
        vb_t = vb.T
        _store_heads_t(o["vb_f"], vb_t)
        vbt = o["vbt"]
        head = vb_t.shape[0] // vbt.shape[0]
        pad = vbt.shape[1] - head
        ones_row = (lax.broadcasted_iota(jnp.int32, (pad, vb_t.shape[1]), 0) == 0).astype(BF16)
        for h in range(vbt.shape[0]):
            vbt[h, 0:head, :] = vb_t[h * head:(h + 1) * head, :].astype(BF16)
            vbt[h, head:head + pad, :] = ones_row
    else:
        o["kb_f"][...] = kb
        o["vb_f"][...] = vb
    qb = _head_norm(seg(3), bd64_ref, gains_ref[2:3, :])
    o["qb"][...] = (qb * scale_b).astype(BF16)
    qa = _head_norm(seg(0), bd64_ref, gains_ref[0:1, :])
    o["qa"][...] = (qa * scale_a).astype(BF16)
    ka = _head_norm(seg(1), bd64_ref, gains_ref[1:2, :])
    o["ka"][...] = ka.astype(BF16)
    qm = _head_norm(_dot(xn, wqm_ref[...]), bd128_ref, gqm_ref[...])
    o["qm"][...] = (qm * scale_m).astype(BF16)
    flt = _nt_dot(wft_ref[...], xn)
    o["lft"][...] = _log_sigmoid(flt[0:8, :] + bf_ref[...])
    va = seg(2)
    o["va"][...] = va.astype(BF16)
    if head_major:
        @pl.when(pl.program_id(0) % tiles_per_group >= keep_from)
        def _():
            _store_heads_t(o["ka_f"], ka.T)
            _store_heads_t(o["va_f"], va.T)
    else:
        o["ka_f"][...] = ka
        o["va_f"][...] = va
    for k, (rider_kernel, n_in) in enumerate(riders):
        rider_kernel(*rider_in[:n_in], o[f"rider{k}"])
        rider_in = rider_in[n_in:]


def _proj(x, gmix, wqkv, wft, wqm, gains, gqm, bf, bd64, bd128, *, tm, group_rows=None, keep_rows=None,
          riders=()):
    rows, d = x.shape
    n_tiles = rows // tm
    row = lambda i: (i, 0)
    wide = lambda dt: (jax.ShapeDtypeStruct((rows, 512), dt), pl.BlockSpec((tm, 512), row))
    outs = {n: wide(BF16) for n in ("qa", "ka", "va", "qb", "kb", "qm")}
    outs["lft"] = (jax.ShapeDtypeStruct((8, rows), F32), pl.BlockSpec((8, tm), lambda i: (0, i)))
    tpg = keep_from = 1
    if group_rows is None:
        for n in ("ka_f", "va_f", "kb_f", "vb_f"):
            outs[n] = wide(F32)
    else:
        assert keep_rows == tm
        tpg = group_rows // tm
        keep_from = tpg - 1
        groups = rows // group_rows
        kept = (jax.ShapeDtypeStruct((groups, 8, 64, keep_rows), F32),
                pl.BlockSpec((None, 8, 64, tm), lambda i: (i // tpg, 0, 0, 0)))
        full = (jax.ShapeDtypeStruct((groups, 8, 64, group_rows), F32),
                pl.BlockSpec((None, 8, 64, tm), lambda i: (i // tpg, 0, 0, i % tpg)))
        outs.update(ka_f=kept, va_f=kept, kb_f=full, vb_f=full)
        outs["vbt"] = (jax.ShapeDtypeStruct((n_tiles, 8, V_ROWS, tm), BF16),
                       pl.BlockSpec((None, 8, V_ROWS, tm), lambda i: (i, 0, 0, 0)))
    in_specs = [pl.BlockSpec((tm, d), row)] + [_const_spec(a.shape) for a in
                                               (gmix, wqkv, wft, wqm, gains, gqm, bf, bd64, bd128)]
    rider_args, rider_kernels = [], []
    for k, (rider_kernel, per_tile, const) in enumerate(riders):
        assert all(a.shape[0] % n_tiles == 0 for a in per_tile)
        tile_spec = lambda a: pl.BlockSpec((a.shape[0] // n_tiles,) + a.shape[1:],
                                           lambda i, nd=a.ndim: (i,) + (0,) * (nd - 1))
        in_specs += [tile_spec(a) for a in per_tile] + [_const_spec(a.shape) for a in const]
        outs[f"rider{k}"] = (jax.ShapeDtypeStruct(per_tile[0].shape, BF16), tile_spec(per_tile[0]))
        rider_args += [*per_tile, *const]
        rider_kernels.append((rider_kernel, len(per_tile) + len(const)))
    names = tuple(outs)
    kern = functools.partial(_proj_kernel, out_names=names, scale_a=LOG2E * 64 ** -0.5,
                             scale_b=LOG2E * 64 ** -0.5, scale_m=LOG2E * 128 ** -0.5,
                             keep_from=keep_from, tiles_per_group=tpg, riders=tuple(rider_kernels))
    res = pl.pallas_call(kern, out_shape=[outs[n][0] for n in names], grid=(n_tiles,), in_specs=in_specs,
                         out_specs=[outs[n][1] for n in names], compiler_params=_cparams("arbitrary"),
                         name="proj")(x, gmix, wqkv, wft, wqm, gains, gqm, bf, bd64, bd128, *rider_args)
    return dict(zip(names, res))


def _memkv_kernel(m_ref, gmem_ref, w_ref, gkm_ref, bd128_ref, mk_o, mv_o, *, heads):
    tm = m_ref.shape[0]
    xn = _rms_rows(m_ref[...], gmem_ref[...]).astype(BF16)
    half = w_ref.shape[1] // 2
    dh = half // heads
    mk = _head_norm(_dot(xn, w_ref[:, :half]), bd128_ref, gkm_ref[...])
    mv = _dot(xn, w_ref[:, half:])
    for h in range(heads):
        mk_o[pl.ds(h, tm, stride=heads), :] = mk[:, h * dh:(h + 1) * dh]
        mv_o[pl.ds(h, tm, stride=heads), :] = mv[:, h * dh:(h + 1) * dh]


def _memkv(mem, gmem, w, gkm, bd128, *, tm, heads):
    rows, d = mem.shape
    dh = w.shape[1] // 2 // heads
    row = lambda i: (i, 0)
    return pl.pallas_call(
        functools.partial(_memkv_kernel, heads=heads),
        out_shape=[jax.ShapeDtypeStruct((rows * heads, dh), F32)] * 2, grid=(rows // tm,),
        in_specs=[pl.BlockSpec((tm, d), row)] + [_const_spec(a.shape) for a in (gmem, w, gkm, bd128)],
        out_specs=[pl.BlockSpec((tm * heads, dh), row)] * 2, compiler_params=_cparams("arbitrary"),
        name="memkv")(mem, gmem, w, gkm, bd128)


def _cumsum_kernel(x_ref, tri_ref, o_ref):
    rows, length = x_ref.shape
    tri = tri_ref[...]

    def local(c):
        x = x_ref[:, c * 128:(c + 1) * 128]
        hi = x.astype(BF16)
        r1 = x - hi.astype(F32)
        mid = r1.astype(BF16)
        lo = (r1 - mid.astype(F32)).astype(BF16)
        return _dot(hi, tri) + _dot(mid, tri) + _dot(lo, tri)

    chunks = [local(c) for c in range(length // 128)]
    carry = jnp.zeros((rows, 1), F32)
    for c, cc in enumerate(chunks):
        o_ref[:, c * 128:(c + 1) * 128] = cc + carry
        carry = carry + cc[:, 127:128]


def _cumsum_lanes(x, tri, *, rb):
    rows, length = x.shape
    return pl.pallas_call(
        _cumsum_kernel, out_shape=jax.ShapeDtypeStruct((rows, length), F32), grid=(rows // rb,),
        in_specs=[pl.BlockSpec((rb, length), lambda i: (i, 0)), _const_spec(tri.shape)],
        out_specs=pl.BlockSpec((rb, length), lambda i: (i, 0)), compiler_params=_cparams("arbitrary"),
        name="cumsum")(x, tri)


def _band_kernel(bound_ref, q_ref, k0_ref, k1_ref, k2_ref, v0_ref, v1_ref, v2_ref, bias_ref, o_ref,
                 k_sc, v_sc, *, reach, lookahead):
    tq = q_ref.shape[0]
    band = reach + CHUNK
    i = pl.program_id(1)
    for n, (kr, vr) in enumerate(((k0_ref, v0_ref), (k1_ref, v1_ref), (k2_ref, v2_ref))):
        k_sc[n * tq:(n + 1) * tq, :] = kr[...]
        v_sc[n * tq:(n + 1) * tq, :] = vr[...]
    col = lax.broadcasted_iota(jnp.int32, (1, band), 1)
    units = [(j, hg) for j in range(tq // CHUNK) for hg in range(q_ref.shape[1] // LANE_GROUP)]

    def scores(j, hg):
        r0 = 2 * tq + j * CHUNK - reach
        lanes = slice(hg * LANE_GROUP, (hg + 1) * LANE_GROUP)
        qbd = _block_diag_rows(q_ref[j * CHUNK:(j + 1) * CHUNK, lanes], CHUNK)
        s = _nt_dot(qbd, k_sc[r0:r0 + band, lanes]) + bias_ref[hg * 4 * CHUNK:(hg + 1) * 4 * CHUNK, :]
        return jnp.where((i - 2) * tq + r0 + col >= 0, s, NEG)

    def attend(one_pass):
        pending = [scores(*u) for u in units[:lookahead]]
        for n, (j, hg) in enumerate(units):
            s = pending.pop(0)
            if n + lookahead < len(units):
                pending.append(scores(*units[n + lookahead]))
            r0 = 2 * tq + j * CHUNK - reach
            lanes = slice(hg * LANE_GROUP, (hg + 1) * LANE_GROUP)
            p = jnp.exp2(s if one_pass else s - jnp.max(s, axis=-1, keepdims=True))
            l = jnp.sum(p, axis=-1, keepdims=True)
            o = _dot(p.astype(BF16), v_sc[r0:r0 + band, lanes]) / l
            o_ref[j * CHUNK:(j + 1) * CHUNK, lanes] = _pick_diag(o, CHUNK, CHUNK).astype(o_ref.dtype)

    lax.cond(bound_ref[0] <= MAX_ONE_PASS_BOUND, lambda: attend(True), lambda: attend(False))


def _band_prompt(logit_bound, q, k, v, bias, *, reach, tq):
    b, t, w = q.shape
    assert reach == 2 * tq and tq % CHUNK == 0
    blk = lambda off: pl.BlockSpec((None, tq, w), lambda bi, i: (bi, jnp.maximum(i - off, 0), 0))
    return pl.pallas_call(
        functools.partial(_band_kernel, reach=reach, lookahead=2),
        out_shape=jax.ShapeDtypeStruct((b, t, w), BF16), grid=(b, t // tq),
        in_specs=[pl.BlockSpec(memory_space=pltpu.SMEM), blk(0), blk(2), blk(1), blk(0), blk(2), blk(1), blk(0),
                  _const_spec(bias.shape)],
        out_specs=blk(0),
        scratch_shapes=[pltpu.VMEM((3 * tq, w), BF16), pltpu.VMEM((3 * tq, w), BF16)],
        compiler_params=_cparams("arbitrary", "arbitrary"), name="band_prompt")(
            logit_bound, q, k, k, k, v, v, v, bias)


def _forget_kernel(bound_ref, q_ref, k_ref, vt_ref, c_ref, o_ref, qt_sc, m_sc, acc_sc, *, head, lookahead):
    tq, width = q_ref.shape
    n_heads = width // head
    i = pl.program_id(2)
    logit_bound = bound_ref[0]
    q_t = q_ref[...].astype(F32).T
    row_grp = lax.broadcasted_iota(jnp.int32, (width, 1), 0) // head
    for g in range(n_heads):
        qt_sc[g] = jnp.where(row_grp == g, q_t, 0.0).astype(BF16)
    m_sc[...] = jnp.full(m_sc.shape, NEG, F32)
    acc_sc[...] = jnp.zeros(acc_sc.shape, F32)
    c_first = c_ref[pl.ds(pl.multiple_of(i * tq, tq), 8), :][0:1, :]

    def step(j, masked):
        k0 = pl.multiple_of(j * tq, tq)
        kt = k_ref[pl.ds(k0, tq), :]
        bias = (c_first - c_ref[pl.ds(k0, tq), :]) * LOG2E
        if masked:
            keep = (lax.broadcasted_iota(jnp.int32, (tq, tq), 0)
                    <= lax.broadcasted_iota(jnp.int32, (tq, tq), 1))

        def scores(g):
            s = _dot(kt, qt_sc[g]) + bias[:, g:g + 1]
            return jnp.where(keep, s, NEG) if masked else s

        pending = [scores(g) for g in range(min(lookahead, n_heads))]
        for g in range(n_heads):
            s = pending.pop(0)
            if g + lookahead < n_heads:
                pending.append(scores(g + lookahead))
            m_prev = m_sc[g]
            m_new = jnp.maximum(m_prev, jnp.max(s, axis=0, keepdims=True))
            p = jnp.exp2(s - m_new).astype(BF16)
            acc_sc[g] = jnp.exp2(m_prev - m_new) * acc_sc[g] + _dot(vt_ref[j, g], p)
            m_sc[g] = m_new

    def bounded_tiles(tiles):
        m_tile, logits = [], []
        for j in tiles:
            k0 = pl.multiple_of(j * tq, tq)
            c_tile = c_ref[pl.ds(k0, tq), :]
            c_last = c_tile[tq - 1:tq, :]
            m_tile.append((c_first - c_last) * LOG2E)
            shift = (c_last - c_tile) * LOG2E
            logits.append(functools.partial(
                lambda g, k0, shift: _dot(k_ref[pl.ds(k0, tq), :], qt_sc[g]) + shift[:, g:g + 1],
                k0=k0, shift=shift))
        units = [(t, g) for t in range(len(tiles)) for g in range(n_heads)]
        pending = [logits[t](g) for t, g in units[:lookahead]]
        for n, (t, g) in enumerate(units):
            s = pending.pop(0)
            if n + lookahead < len(units):
                tn, gn = units[n + lookahead]
                pending.append(logits[tn](gn))
            m_new = jnp.broadcast_to(m_tile[t][:, g:g + 1], (1, tq))
            p = jnp.exp2(s).astype(BF16)
            acc_sc[g] = jnp.exp2(m_sc[g] - m_new) * acc_sc[g] + _dot(vt_ref[tiles[t], g], p)
            m_sc[g] = m_new

    def bounded_loop():
        def pair(jj, carry):
            bounded_tiles([2 * jj, 2 * jj + 1])
            return carry
        lax.fori_loop(0, i // 2, pair, 0)

        @pl.when(i % 2 == 1)
        def _():
            bounded_tiles([i - 1])

    def two_pass_loop():
        def body(j, carry):
            step(j, False)
            return carry
        lax.fori_loop(0, i, body, 0)

    lax.cond(logit_bound <= MAX_ONE_PASS_BOUND, bounded_loop, two_pass_loop)
    step(i, True)
    o_t = jnp.concatenate([acc_sc[g, 0:head, :] / acc_sc[g, head:head + 1, :] for g in range(n_heads)],
                          axis=0)
    o_ref[...] = o_t.T.astype(o_ref.dtype)


def _forget_prompt(logit_bound, q, k, vt, c, *, head, tq):
    b, t, w = q.shape
    n_heads = LANE_GROUP // head
    v_rows = vt.shape[3]
    lookahead = 3
    assert lookahead <= n_heads
    return pl.pallas_call(
        functools.partial(_forget_kernel, head=head, lookahead=lookahead),
        out_shape=jax.ShapeDtypeStruct((b, t, w), BF16), grid=(b, w // LANE_GROUP, t // tq),
        in_specs=[pl.BlockSpec(memory_space=pltpu.SMEM),
                  pl.BlockSpec((None, tq, LANE_GROUP), lambda bi, hg, i: (bi, i, hg)),
                  pl.BlockSpec((None, t, LANE_GROUP), lambda bi, hg, i: (bi, 0, hg)),
                  pl.BlockSpec((None, t // tq, n_heads, v_rows, tq), lambda bi, hg, i: (bi, 0, hg, 0, 0)),
                  pl.BlockSpec((None, None, t, n_heads), lambda bi, hg, i: (bi, hg, 0, 0))],
        out_specs=pl.BlockSpec((None, tq, LANE_GROUP), lambda bi, hg, i: (bi, i, hg)),
        scratch_shapes=[pltpu.VMEM((n_heads, LANE_GROUP, tq), BF16), pltpu.VMEM((n_heads, 1, tq), F32),
                        pltpu.VMEM((n_heads, v_rows, tq), F32)],
        compiler_params=_cparams("arbitrary", "arbitrary", "arbitrary"), name="forget_prompt")(
            logit_bound, q, k, vt, c)


def _mem_attention(qs, mk_refs, mv_refs, heads):
    n_mem = mk_refs[0].shape[0] // heads
    dh = mk_refs[0].shape[1]
    units = [(e, h) for e in range(len(qs)) for h in range(heads)]
    rows = lambda ref, h: ref[pl.ds(h, n_mem, stride=heads), :].astype(BF16)
    s = [_nt_dot(qs[e][:, h * dh:(h + 1) * dh], rows(mk_refs[e], h)) for e, h in units]
    p, inv_l = [], []
    for u in range(len(units)):
        ex = jnp.exp2(s[u] - jnp.max(s[u], axis=-1, keepdims=True))
        inv_l.append(1.0 / jnp.sum(ex, axis=-1, keepdims=True))
        p.append(ex.astype(BF16))
    o = [(_dot(p[u], rows(mv_refs[e], h)) * inv_l[u]).astype(BF16) for u, (e, h) in enumerate(units)]
    return [jnp.concatenate(o[e * heads:(e + 1) * heads], axis=-1) for e in range(len(qs))]


def _mem_kernel(q_ref, mk_ref, mv_ref, o_ref, *, heads):
    bb = q_ref.shape[0]
    outs = _mem_attention([q_ref[e] for e in range(bb)], [mk_ref.at[e] for e in range(bb)],
                          [mv_ref.at[e] for e in range(bb)], heads)
    for e in range(bb):
        o_ref[e] = outs[e]


def _mem_attn(q, mk, mv, *, heads, tq, bb):
    b, t, w = q.shape
    kv = pl.BlockSpec((bb,) + mk.shape[1:], lambda bi, i: (bi, 0, 0))
    qs = pl.BlockSpec((bb, tq, w), lambda bi, i: (bi, i, 0))
    return pl.pallas_call(
        functools.partial(_mem_kernel, heads=heads), out_shape=jax.ShapeDtypeStruct((b, t, w), BF16),
        grid=(b // bb, t // tq), in_specs=[qs, kv, kv], out_specs=qs,
        compiler_params=_cparams("arbitrary", "arbitrary"), name="mem_attn")(q, mk, mv)


def _post_kernel(x_ref, oa_ref, ob_ref, qm_ref, mk_ref, mv_ref, xt_ref, oat_ref, obt_ref, omt_ref,
                 gmix_ref, wg_ref, wpa_ref, wpb_ref, wpm_ref, wo_ref, gffn_ref, wup_ref, wdn_ref,
                 y_ref, yt_ref, *, mem_heads, col_chunk, ff_chunk):
    tail = pl.program_id(0) == pl.num_programs(0) - 1
    o_m, = _mem_attention([qm_ref[...]], [mk_ref], [mv_ref], mem_heads)
    x = jnp.where(tail, xt_ref[...], x_ref[...])
    o_a = jnp.where(tail, oat_ref[...], oa_ref[...])
    o_b = jnp.where(tail, obt_ref[...], ob_ref[...])
    o_m = jnp.where(tail, omt_ref[...], o_m)
    d = x.shape[1]
    xn = _rms_rows(x, gmix_ref[...]).astype(BF16)
    branches = ((o_a, wpa_ref), (o_b, wpb_ref), (o_m, wpm_ref))
    h = []
    for c in range(d // col_chunk):
        cols = slice(c * col_chunk, (c + 1) * col_chunk)
        hc = 0.0
        for b, (o_b, wp_ref) in enumerate(branches):
            gate = jax.nn.sigmoid(_dot(xn, wg_ref[:, b * d + c * col_chunk:b * d + (c + 1) * col_chunk]))
            hc = hc + gate * _dot(o_b, wp_ref[:, cols])
        h.append(hc.astype(BF16))
    x1 = x + _dot(jnp.concatenate(h, axis=-1), wo_ref[...])
    xn1 = _rms_rows(x1, gffn_ref[...]).astype(BF16)
    y = x1
    for c in range(wup_ref.shape[1] // ff_chunk):
        cols = slice(c * ff_chunk, (c + 1) * ff_chunk)
        u = jnp.maximum(_dot(xn1, wup_ref[:, cols]), 0.0)
        y = y + _dot((u * u).astype(BF16), wdn_ref[cols, :])

    @pl.when(jnp.logical_not(tail))
    def _():
        y_ref[...] = y

    @pl.when(tail)
    def _():
        yt_ref[...] = y


def _post(x, oa, ob, qm, mk, mv, tail, gmix, wg, wpa, wpb, wpm, wo, gffn, wup, wdn, *, tm, mem_heads):
    rows, d = x.shape
    n_main = rows // tm
    assert all(a.shape[0] == tm for a in tail)
    row = lambda i: (jnp.minimum(i, n_main - 1), 0)
    tiles_per_batch = n_main // mk.shape[0]
    kv = pl.BlockSpec((None,) + mk.shape[1:], lambda i: (jnp.minimum(i, n_main - 1) // tiles_per_batch, 0, 0))
    acts = (x, oa, ob, qm)
    weights = (gmix, wg, wpa, wpb, wpm, wo, gffn, wup, wdn)
    return pl.pallas_call(
        functools.partial(_post_kernel, mem_heads=mem_heads, col_chunk=512, ff_chunk=1024),
        out_shape=[jax.ShapeDtypeStruct((rows, d), F32), jax.ShapeDtypeStruct((tm, d), F32)],
        grid=(n_main + 1,),
        in_specs=[pl.BlockSpec((tm, a.shape[1]), row) for a in acts] + [kv, kv]
        + [_const_spec(a.shape) for a in (*tail, *weights)],
        out_specs=[pl.BlockSpec((tm, d), row), pl.BlockSpec((tm, d), lambda i: (0, 0))],
        compiler_params=_cparams("arbitrary"), name="post")(*acts, mk, mv, *tail, *weights)


def _pad_rows(x, rows):
    return jnp.concatenate([x, jnp.zeros((rows - x.shape[0], x.shape[1]), x.dtype)], axis=0)


def _cached_attention(q_ref, kc_ref, vc_ref, kn_ref, vn_ref, o_ref, bias_c, bias_n, n_pad):
    bb, heads, head, _ = kc_ref.shape
    units = [(e, h, slice(h * head, (h + 1) * head)) for e in range(bb) for h in range(heads)]
    sc = [_dot(q_ref[e, :, ln], kc_ref[e, h].astype(BF16)) + bias_c(e, h) for e, h, ln in units]
    sn = [_nt_dot(q_ref[e, :, ln], _pad_rows(kn_ref[e, :, ln].astype(BF16), n_pad)) + bias_n(e, h)
          for e, h, ln in units]
    pc, pn, inv_l = [], [], []
    for u in range(len(units)):
        m = jnp.maximum(jnp.max(sc[u], axis=-1, keepdims=True), jnp.max(sn[u], axis=-1, keepdims=True))
        pc.append(jnp.exp2(sc[u] - m))
        pn.append(jnp.exp2(sn[u] - m))
        inv_l.append(1.0 / (jnp.sum(pc[u], axis=-1, keepdims=True) + jnp.sum(pn[u], axis=-1, keepdims=True)))
    for u, (e, h, ln) in enumerate(units):
        o = (_nt_dot(pc[u].astype(BF16), vc_ref[e, h].astype(BF16))
             + _dot(pn[u].astype(BF16), _pad_rows(vn_ref[e, :, ln].astype(BF16), n_pad)))
        o_ref[e, :, ln] = (o * inv_l[u]).astype(o_ref.dtype)


def _band_sample_kernel(q_ref, kc_ref, vc_ref, kn_ref, vn_ref, bc_ref, bn_ref, o_ref):
    s_len = q_ref.shape[1]
    rows = lambda h: slice(h * s_len, (h + 1) * s_len)
    _cached_attention(q_ref, kc_ref, vc_ref, kn_ref, vn_ref, o_ref,
                      lambda e, h: bc_ref[rows(h), :], lambda e, h: bn_ref[rows(h), :], bn_ref.shape[1])


def _band_sample(q, kct, vct, kn, vn, bias_c, bias_n, *, bb):
    b, s_len, w = q.shape
    new = pl.BlockSpec((bb, s_len, w), lambda bi: (bi, 0, 0))
    cache = pl.BlockSpec((bb,) + kct.shape[1:], lambda bi: (bi, 0, 0, 0))
    return pl.pallas_call(
        _band_sample_kernel, out_shape=jax.ShapeDtypeStruct((b, s_len, w), BF16), grid=(b // bb,),
        in_specs=[new, cache, cache, new, new, _const_spec(bias_c.shape), _const_spec(bias_n.shape)],
        out_specs=new, compiler_params=_cparams("arbitrary"), name="band_sample")(
            q, kct, vct, kn, vn, bias_c, bias_n)


def _forget_sample_kernel(q_ref, kc_ref, vc_ref, kn_ref, vn_ref, c_ref, o_ref, *, n_pad):
    s_len = q_ref.shape[1]
    p_len = kc_ref.shape[3]
    causal = (lax.broadcasted_iota(jnp.int32, (s_len, n_pad), 1)
              <= lax.broadcasted_iota(jnp.int32, (s_len, n_pad), 0))
    ch = [(c_ref[e, :, p_len - 1:p_len] - c_ref[e]) * LOG2E for e in range(c_ref.shape[0])]
    _cached_attention(q_ref, kc_ref, vc_ref, kn_ref, vn_ref, o_ref,
                      lambda e, h: ch[e][h:h + 1, 0:p_len],
                      lambda e, h: jnp.where(causal, ch[e][h:h + 1, p_len:p_len + n_pad], NEG), n_pad)


def _forget_sample(q, kct, vct, kn, vn, c, *, n_pad):
    b, s_len, w = q.shape
    new = pl.BlockSpec((1, s_len, w), lambda bi: (bi, 0, 0))
    cache = pl.BlockSpec((1,) + kct.shape[1:], lambda bi: (bi, 0, 0, 0))
    cs = pl.BlockSpec((1,) + c.shape[1:], lambda bi: (bi, 0, 0))
    return pl.pallas_call(
        functools.partial(_forget_sample_kernel, n_pad=n_pad),
        out_shape=jax.ShapeDtypeStruct((b, s_len, w), BF16), grid=(b,),
        in_specs=[new, cache, cache, new, new, cs], out_specs=new,
        compiler_params=_cparams("arbitrary"), name="forget_sample")(q, kct, vct, kn, vn, c)


def _qk_bound(g_q, g_k, dh):
    return 1.02 * LOG2E * dh ** 0.5 * jnp.max(jnp.abs(g_q)) * jnp.max(jnp.abs(g_k))


def _block_diag_mean(head, size=256):
    r = jnp.arange(size) // head
    return jnp.where(r[:, None] == r[None, :], 1.0 / head, 0.0).astype(BF16)


def _rel_bias_rows(rel_bias, d0, n_q, n_k):
    n = n_q + n_k - 1
    dist = d0 + (n_q - 1) - jnp.arange(n)
    e = rel_bias[jnp.clip(dist, -REL_CLIP, REL_CLIP) + REL_CLIP].T.astype(F32)
    h = e.shape[0]
    skew = jnp.tile(jnp.pad(e, ((0, 0), (0, 1))), (1, n_q))[:, :n_q * n].reshape(h, n_q, n)
    return skew[:, :, n_q - 1:].reshape(h * n_q, n_k)


def kernel(x_prompt, x_sample, mem_prompt, cache_a_k, cache_a_v, cache_b_k, cache_b_v, cache_b_logf,
           cache_mem_k, cache_mem_v, g_mix, w_in, b_f, g_qa, g_ka, g_qb, g_kb, g_qm, g_km, rel_bias,
           g_mem, w_mkv, w_pa, w_pb, w_pm, w_o, g_ffn, w_up, w_down):
    depth = w_in.shape[0]
    assert depth == 1
    batch, seq, d = x_prompt.shape
    dec_b, dec_s, _ = x_sample.shape
    _, _, n_cache, h_a, dh_a = cache_a_k.shape
    _, _, past, h_b, dh_b = cache_b_k.shape
    _, _, n_mem, h_m, dh_m = cache_mem_k.shape
    w_a, w_b, w_m = h_a * dh_a, h_b * dh_b, h_m * dh_m
    assert w_a == w_b == w_m == 512 and dh_a == dh_b == 64 and dh_m == 128 and h_b == 8
    keep_p = min(n_cache, seq)
    n_pad = 128
    l = 0

    qkv_end = 3 * w_a + 3 * w_b
    f_end = qkv_end + h_b
    qm_end = f_end + w_m
    w = w_in[l]
    wqkv = w[:, :qkv_end].astype(BF16)
    wft = jnp.zeros((16, d), BF16).at[:h_b].set(w[:, qkv_end:f_end].T.astype(BF16))
    wqm = w[:, f_end:qm_end].astype(BF16)
    wg = w[:, qm_end:].astype(BF16)
    gains = jnp.stack([jnp.tile(g[l], h_a) for g in (g_qa, g_ka, g_qb, g_kb)]).astype(F32)
    gqm = jnp.tile(g_qm[l], h_m)[None, :]
    gkm = jnp.tile(g_km[l], h_m)[None, :]
    bf = b_f[l][:, None].astype(F32)
    bd64 = _block_diag_mean(dh_a)
    bd128 = _block_diag_mean(dh_m)
    tri = (jnp.arange(128)[:, None] <= jnp.arange(128)[None, :]).astype(BF16)
    gmix = g_mix[l][None, :]
    proj_w = (gmix, wqkv, wft, wqm, gains, gqm, bf, bd64, bd128)
    gffn = g_ffn[l][None, :]

    xs = x_sample.reshape(dec_b * dec_s, d)
    ps = _proj(xs, *proj_w, tm=512)
    sthree = lambda a: a.reshape(dec_b, dec_s, -1)
    heads_t = lambda a: jnp.transpose(a[l], (0, 2, 3, 1))
    lf_new = ps["lft"].reshape(h_b, dec_b, dec_s).transpose(1, 0, 2)
    lf_all = jnp.concatenate([cache_b_logf[l].astype(F32).transpose(0, 2, 1), lf_new,
                              jnp.zeros((dec_b, h_b, n_pad - dec_s), F32)], axis=-1)
    c_s = _cumsum_lanes(lf_all.reshape(dec_b * h_b, past + n_pad), tri, rb=dec_b * h_b)
    forget_s = (sthree(ps["qb"]), heads_t(cache_b_k), heads_t(cache_b_v), sthree(ps["kb"]),
                sthree(ps["vb_f"]), c_s.reshape(dec_b, h_b, past + n_pad))
    bias_c = _rel_bias_rows(rel_bias[l], n_cache, dec_s, n_cache) * LOG2E
    bias_n = jnp.full((h_a * dec_s, n_pad), NEG, F32).at[:, :dec_s].set(
        _rel_bias_rows(rel_bias[l], 0, dec_s, dec_s) * LOG2E)
    band_s = (sthree(ps["qa"]), heads_t(cache_a_k), heads_t(cache_a_v), sthree(ps["ka"]), sthree(ps["va"]))
    mem_s = (sthree(ps["qm"]), cache_mem_k[l].reshape(dec_b, n_mem * h_m, dh_m),
             cache_mem_v[l].reshape(dec_b, n_mem * h_m, dh_m))

    xp = x_prompt.reshape(batch * seq, d)
    tm_p = 512
    ride = dec_b % (batch * seq // tm_p) == 0
    riders = tuple((_cast_kernel, (a[l],), ()) for a in (w_pa, w_pb, w_pm, w_o, w_up, w_down))
    if ride:
        riders += ((functools.partial(_forget_sample_kernel, n_pad=n_pad), forget_s, ()),
                   (_band_sample_kernel, band_s, (bias_c, bias_n)),
                   (functools.partial(_mem_kernel, heads=h_m), mem_s, ()))
    pp = _proj(xp, *proj_w, tm=tm_p, group_rows=seq, keep_rows=keep_p, riders=riders)
    wpa, wpb, wpm, wo, wup, wdn = (pp[f"rider{k}"] for k in range(6))
    if ride:
        o_b_s, o_a_s, o_m_s = pp["rider6"], pp["rider7"], pp["rider8"]
    else:
        o_b_s = _forget_sample(*forget_s, n_pad=n_pad)
        o_a_s = _band_sample(*band_s, bias_c, bias_n, bb=4)
        o_m_s = _mem_attn(*mem_s, heads=h_m, tq=dec_s, bb=8)
    mk_f, mv_f = _memkv(mem_prompt.reshape(batch * n_mem, d), g_mem[l][None, :], w_mkv[l].astype(BF16),
                        gkm, bd128, tm=512, heads=h_m)
    tq_b = 512
    lf_p = pp["lft"].reshape(h_b, batch, seq).transpose(1, 0, 2)
    c_p = _cumsum_lanes(lf_p.reshape(batch * h_b, seq), tri, rb=batch * h_b)
    c_p = c_p.reshape(batch, h_b // 4, 4, seq).transpose(0, 1, 3, 2)
    vt = pp["vbt"].reshape(batch, seq // tq_b, h_b, V_ROWS, tq_b)
    three = lambda a: a.reshape(batch, seq, -1)
    bias_p = _rel_bias_rows(rel_bias[l], n_cache, CHUNK, n_cache + CHUNK) * LOG2E
    bound_a = _qk_bound(g_qa[l], g_ka[l], dh_a) + LOG2E * jnp.max(jnp.abs(rel_bias[l]))
    o_a = _band_prompt(bound_a.reshape(1).astype(F32), three(pp["qa"]), three(pp["ka"]), three(pp["va"]),
                       bias_p, reach=n_cache, tq=256)
    o_b = _forget_prompt(_qk_bound(g_qb[l], g_kb[l], dh_b).reshape(1).astype(F32), three(pp["qb"]),
                         three(pp["kb"]), vt, c_p, head=dh_b, tq=tq_b)
    sample_tail = (xs, o_a_s.reshape(-1, w_a), o_b_s.reshape(-1, w_b), o_m_s.reshape(-1, w_m))
    y_prompt, y_sample = _post(
        xp, o_a.reshape(-1, w_a), o_b.reshape(-1, w_b), pp["qm"], mk_f.reshape(batch, n_mem * h_m, dh_m),
        mv_f.reshape(batch, n_mem * h_m, dh_m), sample_tail, gmix, wg, wpa, wpb, wpm, wo, gffn, wup, wdn,
        tm=dec_b * dec_s, mem_heads=h_m)
    y_prompt = y_prompt.reshape(batch, seq, d)
    y_sample = y_sample.reshape(dec_b, dec_s, d)

    lead = lambda a, *shape: a.reshape((depth,) + shape)
    frames_major = lambda a: jnp.transpose(a, (0, 3, 1, 2))[None]
    return (y_prompt, y_sample,
            frames_major(pp["ka_f"]), frames_major(pp["va_f"]),
            frames_major(pp["kb_f"]), frames_major(pp["vb_f"]),
            lf_p.transpose(0, 2, 1)[None],
            lead(mk_f, batch, n_mem, h_m, dh_m), lead(mv_f, batch, n_mem, h_m, dh_m),
            lead(ps["ka_f"], dec_b, dec_s, h_a, dh_a), lead(ps["va_f"], dec_b, dec_s, h_a, dh_a),
            lead(ps["kb_f"], dec_b, dec_s, h_b, dh_b), lead(ps["vb_f"], dec_b, dec_s, h_b, dh_b),
            lead(ps["lft"].T, dec_b, dec_s, h_b))
```

```python
import functools

import jax
import jax.numpy as jnp
from jax import lax
from jax.experimental import pallas as pl
from jax.experimental.pallas import tpu as pltpu

BF16 = jnp.bfloat16
F32 = jnp.float32

EPS = 1e-6
CHUNK = 64
REL_CLIP = 128
NEG = -1e30
LANE_GROUP = 256
LOG2E = 1.4426950408889634
V_ROWS = 64 + 16
MAX_ONE_PASS_BOUND = 40.0
VMEM_LIMIT = 58 * 1024 * 1024


def _cparams(*sem):
    return pltpu.CompilerParams(dimension_semantics=sem, vmem_limit_bytes=VMEM_LIMIT)


def _const_spec(shape):
    nd = len(shape)
    return pl.BlockSpec(shape, lambda *_: (0,) * nd, pipeline_mode=pl.Buffered(1))


def _nt_dot(a, b):
    return lax.dot_general(a, b, (((1,), (1,)), ((), ())), preferred_element_type=F32)


def _dot(a, b):
    return jnp.dot(a, b, preferred_element_type=F32)


def _rms_rows(x, gain):
    ms = jnp.mean(x * x, axis=-1, keepdims=True)
    return x * lax.rsqrt(ms + EPS) * gain


def _head_norm(z, bd_ref, gain):
    z2 = (z * z).astype(BF16)
    ms = jnp.concatenate(
        [_dot(z2[:, h * 256:(h + 1) * 256], bd_ref[...]) for h in range(z.shape[1] // 256)], axis=-1)
    return z * lax.rsqrt(ms + EPS) * gain


def _log_sigmoid(x):
    return jnp.minimum(x, 0.0) - jnp.log1p(jnp.exp(-jnp.abs(x)))


def _lane_group(width, head):
    return lax.broadcasted_iota(jnp.int32, (1, width), 1) // head


def _block_diag_rows(q4, head):
    grp = _lane_group(q4.shape[1], head)
    zero = jnp.zeros_like(q4)
    return jnp.concatenate([jnp.where(grp == g, q4, zero) for g in range(q4.shape[1] // head)], axis=0)


def _pick_diag(o, rows, head):
    grp = _lane_group(o.shape[1], head)
    out = jnp.zeros((rows, o.shape[1]), o.dtype)
    for g in range(o.shape[1] // head):
        out = jnp.where(grp == g, o[g * rows:(g + 1) * rows], out)
    return out


def _cast_kernel(x_ref, o_ref):
    o_ref[...] = x_ref[...].astype(o_ref.dtype)


def _store_heads_t(ref, z_t):
    dh = ref.shape[1]
    for h in range(ref.shape[0]):
        ref[h] = z_t[h * dh:(h + 1) * dh, :]


def _proj_kernel(x_ref, gmix_ref, wqkv_ref, wft_ref, gains_ref, gqm_ref, bf_ref,
                 bd64_ref, bd128_ref, *refs, out_names, scale_a, scale_b, scale_m, keep_from,
                 tiles_per_group, riders):
    rider_in, out_refs = refs[:len(refs) - len(out_names)], refs[len(refs) - len(out_names):]
    o = dict(zip(out_names, out_refs))
    head_major = "vbt" in o
    xn = _rms_rows(x_ref[...], gmix_ref[...]).astype(BF16)

    def seg(j):
        return _dot(xn, wqkv_ref[:, j * 512:(j + 1) * 512])

    kb = _head_norm(seg(4), bd64_ref, gains_ref[3:4, :])
    o["kb"][...] = kb.astype(BF16)
    vb = seg(5)
    if head_major:
        _store_heads_t(o["kb_f"], kb.T)
        vb_t = vb.T
        _store_heads_t(o["vb_f"], vb_t)
        vbt = o["vbt"]
        head = vb_t.shape[0] // vbt.shape[0]
        pad = vbt.shape[1] - head
        ones_row = (lax.broadcasted_iota(jnp.int32, (pad, vb_t.shape[1]), 0) == 0).astype(BF16)
        for h in range(vbt.shape[0]):
            vbt[h, 0:head, :] = vb_t[h * head:(h + 1) * head, :].astype(BF16)
            vbt[h, head:head + pad, :] = ones_row
    else:
        o["kb_f"][...] = kb
        o["vb_f"][...] = vb
    qb = _head_norm(seg(3), bd64_ref, gains_ref[2:3, :])
    o["qb"][...] = (qb * scale_b).astype(BF16)
    qa = _head_norm(seg(0), bd64_ref, gains_ref[0:1, :])
    o["qa"][...] = (qa * scale_a).astype(BF16)
    ka = _head_norm(seg(1), bd64_ref, gains_ref[1:2, :])
    o["ka"][...] = ka.astype(BF16)
    qm = _head_norm(seg(6), bd128_ref, gqm_ref[...])
    o["qm"][...] = (qm * scale_m).astype(BF16)
    flt = _nt_dot(wft_ref[...], xn)
    o["lft"][...] = _log_sigmoid(flt[0:8, :] + bf_ref[...])
    va = seg(2)
    o["va"][...] = va.astype(BF16)
    if head_major:
        @pl.when(pl.program_id(0) % tiles_per_group >= keep_from)
        def _():
            _store_heads_t(o["ka_f"], ka.T)
            _store_heads_t(o["va_f"], va.T)
    else:
        o["ka_f"][...] = ka
        o["va_f"][...] = va
    for k, (rider_kernel, n_in) in enumerate(riders):
        rider_kernel(*rider_in[:n_in], o[f"rider{k}"])
        rider_in = rider_in[n_in:]


def _proj(x, gmix, wqkv, wft, gains, gqm, bf, bd64, bd128, *, tm, group_rows=None, keep_rows=None,
          riders=()):
    rows, d = x.shape
    n_tiles = rows // tm
    row = lambda i: (i, 0)
    wide = lambda dt: (jax.ShapeDtypeStruct((rows, 512), dt), pl.BlockSpec((tm, 512), row))
    outs = {n: wide(BF16) for n in ("qa", "ka", "va", "qb", "kb", "qm")}
    outs["lft"] = (jax.ShapeDtypeStruct((8, rows), F32), pl.BlockSpec((8, tm), lambda i: (0, i)))
    tpg = keep_from = 1
    if group_rows is None:
        for n in ("ka_f", "va_f", "kb_f", "vb_f"):
            outs[n] = wide(F32)
    else:
        assert keep_rows == tm
        tpg = group_rows // tm
        keep_from = tpg - 1
        groups = rows // group_rows
        kept = (jax.ShapeDtypeStruct((groups, 8, 64, keep_rows), F32),
                pl.BlockSpec((None, 8, 64, tm), lambda i: (i // tpg, 0, 0, 0)))
        full = (jax.ShapeDtypeStruct((groups, 8, 64, group_rows), F32),
                pl.BlockSpec((None, 8, 64, tm), lambda i: (i // tpg, 0, 0, i % tpg)))
        outs.update(ka_f=kept, va_f=kept, kb_f=full, vb_f=full)
        outs["vbt"] = (jax.ShapeDtypeStruct((n_tiles, 8, V_ROWS, tm), BF16),
                       pl.BlockSpec((None, 8, V_ROWS, tm), lambda i: (i, 0, 0, 0)))
    in_specs = [pl.BlockSpec((tm, d), row)] + [_const_spec(a.shape) for a in
                                               (gmix, wqkv, wft, gains, gqm, bf, bd64, bd128)]
    rider_args, rider_kernels = [], []
    for k, (rider_kernel, per_tile, const) in enumerate(riders):
        assert all(a.shape[0] % n_tiles == 0 for a in per_tile)
        tile_spec = lambda a: pl.BlockSpec((a.shape[0] // n_tiles,) + a.shape[1:],
                                           lambda i, nd=a.ndim: (i,) + (0,) * (nd - 1))
        in_specs += [tile_spec(a) for a in per_tile] + [_const_spec(a.shape) for a in const]
        outs[f"rider{k}"] = (jax.ShapeDtypeStruct(per_tile[0].shape, BF16), tile_spec(per_tile[0]))
        rider_args += [*per_tile, *const]
        rider_kernels.append((rider_kernel, len(per_tile) + len(const)))
    names = tuple(outs)
    kern = functools.partial(_proj_kernel, out_names=names, scale_a=LOG2E * 64 ** -0.5,
                             scale_b=LOG2E * 64 ** -0.5, scale_m=LOG2E * 128 ** -0.5,
                             keep_from=keep_from, tiles_per_group=tpg, riders=tuple(rider_kernels))
    res = pl.pallas_call(kern, out_shape=[outs[n][0] for n in names], grid=(n_tiles,), in_specs=in_specs,
                         out_specs=[outs[n][1] for n in names], compiler_params=_cparams("arbitrary"),
                         name="proj")(x, gmix, wqkv, wft, gains, gqm, bf, bd64, bd128, *rider_args)
    return dict(zip(names, res))


def _split_w_in_kernel(wt_ref, wp_ref, wg_ref, *, n_proj):
    i = pl.program_id(0)
    blk = wt_ref[...].T.astype(BF16)

    @pl.when(i < n_proj)
    def _():
        wp_ref[...] = blk

    @pl.when(i >= n_proj)
    def _():
        wg_ref[...] = blk


def _split_w_in(w_t, *, proj_cols, qm_cols, gate_cols, block=512):
    d = w_t.shape[1]
    spans = [proj_cols, qm_cols, gate_cols]
    assert all((b - a) % block == 0 and a % 8 == 0 for a, b in spans) and proj_cols[0] == 0
    n_a = (proj_cols[1] - proj_cols[0]) // block
    n_proj = n_a + (qm_cols[1] - qm_cols[0]) // block
    n_gate = (gate_cols[1] - gate_cols[0]) // block
    assert qm_cols[1] == gate_cols[0]

    def row_start(i):
        return pl.multiple_of(jnp.where(i < n_a, block * i, qm_cols[0] + block * (i - n_a)), 8)

    return pl.pallas_call(
        functools.partial(_split_w_in_kernel, n_proj=n_proj),
        out_shape=[jax.ShapeDtypeStruct((d, block * n_proj), BF16), jax.ShapeDtypeStruct((d, block * n_gate), BF16)],
        grid=(n_proj + n_gate,),
        in_specs=[pl.BlockSpec((pl.Element(block), pl.Element(d)), lambda i: (row_start(i), 0))],
        out_specs=[pl.BlockSpec((d, block), lambda i: (0, jnp.minimum(i, n_proj - 1))),
                   pl.BlockSpec((d, block), lambda i: (0, jnp.maximum(i - n_proj, 0)))],
        compiler_params=_cparams("arbitrary"), name="split_w_in")(w_t)


def _memkv_kernel(m_ref, gmem_ref, w_ref, gkm_ref, bd128_ref, mk_o, mv_o, *, heads):
    tm = m_ref.shape[0]
    xn = _rms_rows(m_ref[...], gmem_ref[...]).astype(BF16)
    half = w_ref.shape[1] // 2
    dh = half // heads
    mk = _head_norm(_dot(xn, w_ref[:, :half]), bd128_ref, gkm_ref[...])
    mv = _dot(xn, w_ref[:, half:])
    for h in range(heads):
        mk_o[pl.ds(h, tm, stride=heads), :] = mk[:, h * dh:(h + 1) * dh]
        mv_o[pl.ds(h, tm, stride=heads), :] = mv[:, h * dh:(h + 1) * dh]


def _memkv(mem, gmem, w, gkm, bd128, *, tm, heads):
    rows, d = mem.shape
    dh = w.shape[1] // 2 // heads
    row = lambda i: (i, 0)
    return pl.pallas_call(
        functools.partial(_memkv_kernel, heads=heads),
        out_shape=[jax.ShapeDtypeStruct((rows * heads, dh), F32)] * 2, grid=(rows // tm,),
        in_specs=[pl.BlockSpec((tm, d), row)] + [_const_spec(a.shape) for a in (gmem, w, gkm, bd128)],
        out_specs=[pl.BlockSpec((tm * heads, dh), row)] * 2, compiler_params=_cparams("arbitrary"),
        name="memkv")(mem, gmem, w, gkm, bd128)


def _cumsum_kernel(x_ref, tri_ref, o_ref):
    rows, length = x_ref.shape
    tri = tri_ref[...]

    def local(c):
        x = x_ref[:, c * 128:(c + 1) * 128]
        hi = x.astype(BF16)
        r1 = x - hi.astype(F32)
        mid = r1.astype(BF16)
        lo = (r1 - mid.astype(F32)).astype(BF16)
        return _dot(hi, tri) + _dot(mid, tri) + _dot(lo, tri)

    chunks = [local(c) for c in range(length // 128)]
    carry = jnp.zeros((rows, 1), F32)
    for c, cc in enumerate(chunks):
        o_ref[:, c * 128:(c + 1) * 128] = cc + carry
        carry = carry + cc[:, 127:128]


def _cumsum_lanes(x, tri, *, rb):
    rows, length = x.shape
    return pl.pallas_call(
        _cumsum_kernel, out_shape=jax.ShapeDtypeStruct((rows, length), F32), grid=(rows // rb,),
        in_specs=[pl.BlockSpec((rb, length), lambda i: (i, 0)), _const_spec(tri.shape)],
        out_specs=pl.BlockSpec((rb, length), lambda i: (i, 0)), compiler_params=_cparams("arbitrary"),
        name="cumsum")(x, tri)


def _band_kernel(bound_ref, q_ref, k0_ref, k1_ref, k2_ref, v0_ref, v1_ref, v2_ref, bias_ref, o_ref,
                 k_sc, v_sc, *, reach, lookahead):
    tq = q_ref.shape[0]
    band = reach + CHUNK
    i = pl.program_id(1)
    for n, (kr, vr) in enumerate(((k0_ref, v0_ref), (k1_ref, v1_ref), (k2_ref, v2_ref))):
        k_sc[n * tq:(n + 1) * tq, :] = kr[...]
        v_sc[n * tq:(n + 1) * tq, :] = vr[...]
    col = lax.broadcasted_iota(jnp.int32, (1, band), 1)
    units = [(j, hg) for j in range(tq // CHUNK) for hg in range(q_ref.shape[1] // LANE_GROUP)]

    def scores(j, hg):
        r0 = 2 * tq + j * CHUNK - reach
        lanes = slice(hg * LANE_GROUP, (hg + 1) * LANE_GROUP)
        qbd = _block_diag_rows(q_ref[j * CHUNK:(j + 1) * CHUNK, lanes], CHUNK)
        s = _nt_dot(qbd, k_sc[r0:r0 + band, lanes]) + bias_ref[hg * 4 * CHUNK:(hg + 1) * 4 * CHUNK, :]
        return jnp.where((i - 2) * tq + r0 + col >= 0, s, NEG)

    def attend(one_pass):
        pending = [scores(*u) for u in units[:lookahead]]
        for n, (j, hg) in enumerate(units):
            s = pending.pop(0)
            if n + lookahead < len(units):
                pending.append(scores(*units[n + lookahead]))
            r0 = 2 * tq + j * CHUNK - reach
            lanes = slice(hg * LANE_GROUP, (hg + 1) * LANE_GROUP)
            p = jnp.exp2(s if one_pass else s - jnp.max(s, axis=-1, keepdims=True))
            l = jnp.sum(p, axis=-1, keepdims=True)
            o = _dot(p.astype(BF16), v_sc[r0:r0 + band, lanes]) / l
            o_ref[j * CHUNK:(j + 1) * CHUNK, lanes] = _pick_diag(o, CHUNK, CHUNK).astype(o_ref.dtype)

    lax.cond(bound_ref[0] <= MAX_ONE_PASS_BOUND, lambda: attend(True), lambda: attend(False))


def _band_prompt(logit_bound, q, k, v, bias, *, reach, tq):
    b, t, w = q.shape
    assert reach == 2 * tq and tq % CHUNK == 0
    blk = lambda off: pl.BlockSpec((None, tq, w), lambda bi, i: (bi, jnp.maximum(i - off, 0), 0))
    return pl.pallas_call(
        functools.partial(_band_kernel, reach=reach, lookahead=2),
        out_shape=jax.ShapeDtypeStruct((b, t, w), BF16), grid=(b, t // tq),
        in_specs=[pl.BlockSpec(memory_space=pltpu.SMEM), blk(0), blk(2), blk(1), blk(0), blk(2), blk(1), blk(0),
                  _const_spec(bias.shape)],
        out_specs=blk(0),
        scratch_shapes=[pltpu.VMEM((3 * tq, w), BF16), pltpu.VMEM((3 * tq, w), BF16)],
        compiler_params=_cparams("arbitrary", "arbitrary"), name="band_prompt")(
            logit_bound, q, k, k, k, v, v, v, bias)


def _forget_kernel(bound_ref, q_ref, k_ref, vt_ref, c_ref, o_ref, qt_sc, m_sc, acc_sc, *, head, lookahead):
    tq, width = q_ref.shape
    n_heads = width // head
    i = pl.program_id(2)
    logit_bound = bound_ref[0]
    q_t = q_ref[...].astype(F32).T
    row_grp = lax.broadcasted_iota(jnp.int32, (width, 1), 0) // head
    for g in range(n_heads):
        qt_sc[g] = jnp.where(row_grp == g, q_t, 0.0).astype(BF16)
    m_sc[...] = jnp.full(m_sc.shape, NEG, F32)
    acc_sc[...] = jnp.zeros(acc_sc.shape, F32)
    c_first = c_ref[pl.ds(pl.multiple_of(i * tq, tq), 8), :][0:1, :]

    def step(j, masked):
        k0 = pl.multiple_of(j * tq, tq)
        kt = k_ref[pl.ds(k0, tq), :]
        bias = (c_first - c_ref[pl.ds(k0, tq), :]) * LOG2E
        if masked:
            keep = (lax.broadcasted_iota(jnp.int32, (tq, tq), 0)
                    <= lax.broadcasted_iota(jnp.int32, (tq, tq), 1))

        def scores(g):
            s = _dot(kt, qt_sc[g]) + bias[:, g:g + 1]
            return jnp.where(keep, s, NEG) if masked else s

        pending = [scores(g) for g in range(min(lookahead, n_heads))]
        for g in range(n_heads):
            s = pending.pop(0)
            if g + lookahead < n_heads:
                pending.append(scores(g + lookahead))
            m_prev = m_sc[g]
            m_new = jnp.maximum(m_prev, jnp.max(s, axis=0, keepdims=True))
            p = jnp.exp2(s - m_new).astype(BF16)
            acc_sc[g] = jnp.exp2(m_prev - m_new) * acc_sc[g] + _dot(vt_ref[j, g], p)
            m_sc[g] = m_new

    def bounded_tiles(tiles):
        m_tile, logits = [], []
        for j in tiles:
            k0 = pl.multiple_of(j * tq, tq)
            c_tile = c_ref[pl.ds(k0, tq), :]
            c_last = c_tile[tq - 1:tq, :]
            m_tile.append((c_first - c_last) * LOG2E)
            shift = (c_last - c_tile) * LOG2E
            logits.append(functools.partial(
                lambda g, k0, shift: _dot(k_ref[pl.ds(k0, tq), :], qt_sc[g]) + shift[:, g:g + 1],
                k0=k0, shift=shift))
        units = [(t, g) for t in range(len(tiles)) for g in range(n_heads)]
        pending = [logits[t](g) for t, g in units[:lookahead]]
        for n, (t, g) in enumerate(units):
            s = pending.pop(0)
            if n + lookahead < len(units):
                tn, gn = units[n + lookahead]
                pending.append(logits[tn](gn))
            m_new = jnp.broadcast_to(m_tile[t][:, g:g + 1], (1, tq))
            p = jnp.exp2(s).astype(BF16)
            acc_sc[g] = jnp.exp2(m_sc[g] - m_new) * acc_sc[g] + _dot(vt_ref[tiles[t], g], p)
            m_sc[g] = m_new

    def bounded_loop():
        def pair(jj, carry):
            bounded_tiles([2 * jj, 2 * jj + 1])
            return carry
        lax.fori_loop(0, i // 2, pair, 0)

        @pl.when(i % 2 == 1)
        def _():
            bounded_tiles([i - 1])

    def two_pass_loop():
        def body(j, carry):
            step(j, False)
            return carry
        lax.fori_loop(0, i, body, 0)

    lax.cond(logit_bound <= MAX_ONE_PASS_BOUND, bounded_loop, two_pass_loop)
    step(i, True)
    o_t = jnp.concatenate([acc_sc[g, 0:head, :] / acc_sc[g, head:head + 1, :] for g in range(n_heads)],
                          axis=0)
    o_ref[...] = o_t.T.astype(o_ref.dtype)


def _forget_prompt(logit_bound, q, k, vt, c, *, head, tq):
    b, t, w = q.shape
    n_heads = LANE_GROUP // head
    v_rows = vt.shape[3]
    lookahead = 3
    assert lookahead <= n_heads
    return pl.pallas_call(
        functools.partial(_forget_kernel, head=head, lookahead=lookahead),
        out_shape=jax.ShapeDtypeStruct((b, t, w), BF16), grid=(b, w // LANE_GROUP, t // tq),
        in_specs=[pl.BlockSpec(memory_space=pltpu.SMEM),
                  pl.BlockSpec((None, tq, LANE_GROUP), lambda bi, hg, i: (bi, i, hg)),
                  pl.BlockSpec((None, t, LANE_GROUP), lambda bi, hg, i: (bi, 0, hg)),
                  pl.BlockSpec((None, t // tq, n_heads, v_rows, tq), lambda bi, hg, i: (bi, 0, hg, 0, 0)),
                  pl.BlockSpec((None, None, t, n_heads), lambda bi, hg, i: (bi, hg, 0, 0))],
        out_specs=pl.BlockSpec((None, tq, LANE_GROUP), lambda bi, hg, i: (bi, i, hg)),
        scratch_shapes=[pltpu.VMEM((n_heads, LANE_GROUP, tq), BF16), pltpu.VMEM((n_heads, 1, tq), F32),
                        pltpu.VMEM((n_heads, v_rows, tq), F32)],
        compiler_params=_cparams("arbitrary", "arbitrary", "arbitrary"), name="forget_prompt")(
            logit_bound, q, k, vt, c)


def _mem_attention(qs, mk_refs, mv_refs, heads):
    n_mem = mk_refs[0].shape[0] // heads
    dh = mk_refs[0].shape[1]
    units = [(e, h) for e in range(len(qs)) for h in range(heads)]
    rows = lambda ref, h: ref[pl.ds(h, n_mem, stride=heads), :].astype(BF16)
    s = [_nt_dot(qs[e][:, h * dh:(h + 1) * dh], rows(mk_refs[e], h)) for e, h in units]
    p, inv_l = [], []
    for u in range(len(units)):
        ex = jnp.exp2(s[u] - jnp.max(s[u], axis=-1, keepdims=True))
        inv_l.append(1.0 / jnp.sum(ex, axis=-1, keepdims=True))
        p.append(ex.astype(BF16))
    o = [(_dot(p[u], rows(mv_refs[e], h)) * inv_l[u]).astype(BF16) for u, (e, h) in enumerate(units)]
    return [jnp.concatenate(o[e * heads:(e + 1) * heads], axis=-1) for e in range(len(qs))]


def _mem_kernel(q_ref, mk_ref, mv_ref, o_ref, *, heads):
    bb = q_ref.shape[0]
    outs = _mem_attention([q_ref[e] for e in range(bb)], [mk_ref.at[e] for e in range(bb)],
                          [mv_ref.at[e] for e in range(bb)], heads)
    for e in range(bb):
        o_ref[e] = outs[e]


def _mem_attn(q, mk, mv, *, heads, tq, bb):
    b, t, w = q.shape
    kv = pl.BlockSpec((bb,) + mk.shape[1:], lambda bi, i: (bi, 0, 0))
    qs = pl.BlockSpec((bb, tq, w), lambda bi, i: (bi, i, 0))
    return pl.pallas_call(
        functools.partial(_mem_kernel, heads=heads), out_shape=jax.ShapeDtypeStruct((b, t, w), BF16),
        grid=(b // bb, t // tq), in_specs=[qs, kv, kv], out_specs=qs,
        compiler_params=_cparams("arbitrary", "arbitrary"), name="mem_attn")(q, mk, mv)


def _post_kernel(x_ref, oa_ref, ob_ref, qm_ref, mk_ref, mv_ref, xt_ref, oat_ref, obt_ref, omt_ref,
                 gmix_ref, wg_ref, wpa_ref, wpb_ref, wpm_ref, wo_ref, gffn_ref, wup_ref, wdn_ref,
                 y_ref, yt_ref, *, mem_heads, col_chunk, ff_chunk):
    tail = pl.program_id(0) == pl.num_programs(0) - 1
    o_m, = _mem_attention([qm_ref[...]], [mk_ref], [mv_ref], mem_heads)
    x = jnp.where(tail, xt_ref[...], x_ref[...])
    o_a = jnp.where(tail, oat_ref[...], oa_ref[...])
    o_b = jnp.where(tail, obt_ref[...], ob_ref[...])
    o_m = jnp.where(tail, omt_ref[...], o_m)
    d = x.shape[1]
    xn = _rms_rows(x, gmix_ref[...]).astype(BF16)
    branches = ((o_a, wpa_ref), (o_b, wpb_ref), (o_m, wpm_ref))
    h = []
    for c in range(d // col_chunk):
        cols = slice(c * col_chunk, (c + 1) * col_chunk)
        hc = 0.0
        for b, (o_b, wp_ref) in enumerate(branches):
            gate = jax.nn.sigmoid(_dot(xn, wg_ref[:, b * d + c * col_chunk:b * d + (c + 1) * col_chunk]))
            hc = hc + gate * _dot(o_b, wp_ref[:, cols])
        h.append(hc.astype(BF16))
    x1 = x + _dot(jnp.concatenate(h, axis=-1), wo_ref[...])
    xn1 = _rms_rows(x1, gffn_ref[...]).astype(BF16)
    y = x1
    for c in range(wup_ref.shape[1] // ff_chunk):
        cols = slice(c * ff_chunk, (c + 1) * ff_chunk)
        u = jnp.maximum(_dot(xn1, wup_ref[:, cols]), 0.0)
        y = y + _dot((u * u).astype(BF16), wdn_ref[cols, :])

    @pl.when(jnp.logical_not(tail))
    def _():
        y_ref[...] = y

    @pl.when(tail)
    def _():
        yt_ref[...] = y


def _post(x, oa, ob, qm, mk, mv, tail, gmix, wg, wpa, wpb, wpm, wo, gffn, wup, wdn, *, tm, mem_heads):
    rows, d = x.shape
    n_main = rows // tm
    assert all(a.shape[0] == tm for a in tail)
    row = lambda i: (jnp.minimum(i, n_main - 1), 0)
    tiles_per_batch = n_main // mk.shape[0]
    kv = pl.BlockSpec((None,) + mk.shape[1:], lambda i: (jnp.minimum(i, n_main - 1) // tiles_per_batch, 0, 0))
    acts = (x, oa, ob, qm)
    weights = (gmix, wg, wpa, wpb, wpm, wo, gffn, wup, wdn)
    return pl.pallas_call(
        functools.partial(_post_kernel, mem_heads=mem_heads, col_chunk=512, ff_chunk=1024),
        out_shape=[jax.ShapeDtypeStruct((rows, d), F32), jax.ShapeDtypeStruct((tm, d), F32)],
        grid=(n_main + 1,),
        in_specs=[pl.BlockSpec((tm, a.shape[1]), row) for a in acts] + [kv, kv]
        + [_const_spec(a.shape) for a in (*tail, *weights)],
        out_specs=[pl.BlockSpec((tm, d), row), pl.BlockSpec((tm, d), lambda i: (0, 0))],
        compiler_params=_cparams("arbitrary"), name="post")(*acts, mk, mv, *tail, *weights)


def _pad_rows(x, rows):
    return jnp.concatenate([x, jnp.zeros((rows - x.shape[0], x.shape[1]), x.dtype)], axis=0)


def _cached_attention(q_ref, kc_ref, vc_ref, kn_ref, vn_ref, o_ref, bias_c, bias_n, n_pad):
    bb, heads, head, _ = kc_ref.shape
    units = [(e, h, slice(h * head, (h + 1) * head)) for e in range(bb) for h in range(heads)]
    sc = [_dot(q_ref[e, :, ln], kc_ref[e, h].astype(BF16)) + bias_c(e, h) for e, h, ln in units]
    sn = [_nt_dot(q_ref[e, :, ln], _pad_rows(kn_ref[e, :, ln].astype(BF16), n_pad)) + bias_n(e, h)
          for e, h, ln in units]
    pc, pn, inv_l = [], [], []
    for u in range(len(units)):
        m = jnp.maximum(jnp.max(sc[u], axis=-1, keepdims=True), jnp.max(sn[u], axis=-1, keepdims=True))
        pc.append(jnp.exp2(sc[u] - m))
        pn.append(jnp.exp2(sn[u] - m))
        inv_l.append(1.0 / (jnp.sum(pc[u], axis=-1, keepdims=True) + jnp.sum(pn[u], axis=-1, keepdims=True)))
    for u, (e, h, ln) in enumerate(units):
        o = (_nt_dot(pc[u].astype(BF16), vc_ref[e, h].astype(BF16))
             + _dot(pn[u].astype(BF16), _pad_rows(vn_ref[e, :, ln].astype(BF16), n_pad)))
        o_ref[e, :, ln] = (o * inv_l[u]).astype(o_ref.dtype)


def _band_sample_kernel(q_ref, kc_ref, vc_ref, kn_ref, vn_ref, bc_ref, bn_ref, o_ref):
    s_len = q_ref.shape[1]
    rows = lambda h: slice(h * s_len, (h + 1) * s_len)
    _cached_attention(q_ref, kc_ref, vc_ref, kn_ref, vn_ref, o_ref,
                      lambda e, h: bc_ref[rows(h), :], lambda e, h: bn_ref[rows(h), :], bn_ref.shape[1])


def _band_sample(q, kct, vct, kn, vn, bias_c, bias_n, *, bb):
    b, s_len, w = q.shape
    new = pl.BlockSpec((bb, s_len, w), lambda bi: (bi, 0, 0))
    cache = pl.BlockSpec((bb,) + kct.shape[1:], lambda bi: (bi, 0, 0, 0))
    return pl.pallas_call(
        _band_sample_kernel, out_shape=jax.ShapeDtypeStruct((b, s_len, w), BF16), grid=(b // bb,),
        in_specs=[new, cache, cache, new, new, _const_spec(bias_c.shape), _const_spec(bias_n.shape)],
        out_specs=new, compiler_params=_cparams("arbitrary"), name="band_sample")(
            q, kct, vct, kn, vn, bias_c, bias_n)


def _forget_sample_kernel(q_ref, kc_ref, vc_ref, kn_ref, vn_ref, c_ref, o_ref, *, n_pad):
    s_len = q_ref.shape[1]
    p_len = kc_ref.shape[3]
    causal = (lax.broadcasted_iota(jnp.int32, (s_len, n_pad), 1)
              <= lax.broadcasted_iota(jnp.int32, (s_len, n_pad), 0))
    ch = [(c_ref[e, :, p_len - 1:p_len] - c_ref[e]) * LOG2E for e in range(c_ref.shape[0])]
    _cached_attention(q_ref, kc_ref, vc_ref, kn_ref, vn_ref, o_ref,
                      lambda e, h: ch[e][h:h + 1, 0:p_len],
                      lambda e, h: jnp.where(causal, ch[e][h:h + 1, p_len:p_len + n_pad], NEG), n_pad)


def _forget_sample(q, kct, vct, kn, vn, c, *, n_pad):
    b, s_len, w = q.shape
    new = pl.BlockSpec((1, s_len, w), lambda bi: (bi, 0, 0))
    cache = pl.BlockSpec((1,) + kct.shape[1:], lambda bi: (bi, 0, 0, 0))
    cs = pl.BlockSpec((1,) + c.shape[1:], lambda bi: (bi, 0, 0))
    return pl.pallas_call(
        functools.partial(_forget_sample_kernel, n_pad=n_pad),
        out_shape=jax.ShapeDtypeStruct((b, s_len, w), BF16), grid=(b,),
        in_specs=[new, cache, cache, new, new, cs], out_specs=new,
        compiler_params=_cparams("arbitrary"), name="forget_sample")(q, kct, vct, kn, vn, c)


def _qk_bound(g_q, g_k, dh):
    return 1.02 * LOG2E * dh ** 0.5 * jnp.max(jnp.abs(g_q)) * jnp.max(jnp.abs(g_k))


def _block_diag_mean(head, size=256):
    r = jnp.arange(size) // head
    return jnp.where(r[:, None] == r[None, :], 1.0 / head, 0.0).astype(BF16)


def _rel_bias_rows(rel_bias, d0, n_q, n_k):
    n = n_q + n_k - 1
    dist = d0 + (n_q - 1) - jnp.arange(n)
    e = rel_bias[jnp.clip(dist, -REL_CLIP, REL_CLIP) + REL_CLIP].T.astype(F32)
    h = e.shape[0]
    skew = jnp.tile(jnp.pad(e, ((0, 0), (0, 1))), (1, n_q))[:, :n_q * n].reshape(h, n_q, n)
    return skew[:, :, n_q - 1:].reshape(h * n_q, n_k)


def kernel(x_prompt, x_sample, mem_prompt, cache_a_k, cache_a_v, cache_b_k, cache_b_v, cache_b_logf,
           cache_mem_k, cache_mem_v, g_mix, w_in, b_f, g_qa, g_ka, g_qb, g_kb, g_qm, g_km, rel_bias,
           g_mem, w_mkv, w_pa, w_pb, w_pm, w_o, g_ffn, w_up, w_down):
    depth = w_in.shape[0]
    assert depth == 1
    batch, seq, d = x_prompt.shape
    dec_b, dec_s, _ = x_sample.shape
    _, _, n_cache, h_a, dh_a = cache_a_k.shape
    _, _, past, h_b, dh_b = cache_b_k.shape
    _, _, n_mem, h_m, dh_m = cache_mem_k.shape
    w_a, w_b, w_m = h_a * dh_a, h_b * dh_b, h_m * dh_m
    assert w_a == w_b == w_m == 512 and dh_a == dh_b == 64 and dh_m == 128 and h_b == 8
    keep_p = min(n_cache, seq)
    n_pad = 128
    l = 0

    qkv_end = 3 * w_a + 3 * w_b
    f_end = qkv_end + h_b
    qm_end = f_end + w_m
    w_t = jnp.transpose(w_in[l])
    wft = jnp.zeros((16, d), BF16).at[:h_b].set(w_t[qkv_end:f_end].astype(BF16))
    wqkvm, wg = _split_w_in(w_t, proj_cols=(0, qkv_end), qm_cols=(f_end, qm_end), gate_cols=(qm_end, w_t.shape[0]))
    gains = jnp.stack([jnp.tile(g[l], h_a) for g in (g_qa, g_ka, g_qb, g_kb)]).astype(F32)
    gqm = jnp.tile(g_qm[l], h_m)[None, :]
    gkm = jnp.tile(g_km[l], h_m)[None, :]
    bf = b_f[l][:, None].astype(F32)
    bd64 = _block_diag_mean(dh_a)
    bd128 = _block_diag_mean(dh_m)
    tri = (jnp.arange(128)[:, None] <= jnp.arange(128)[None, :]).astype(BF16)
    gmix = g_mix[l][None, :]
    proj_w = (gmix, wqkvm, wft, gains, gqm, bf, bd64, bd128)
    gffn = g_ffn[l][None, :]

    xs = x_sample.reshape(dec_b * dec_s, d)
    ps = _proj(xs, *proj_w, tm=512)
    sthree = lambda a: a.reshape(dec_b, dec_s, -1)
    heads_t = lambda a: jnp.transpose(a[l], (0, 2, 3, 1))
    lf_new = ps["lft"].reshape(h_b, dec_b, dec_s).transpose(1, 0, 2)
    lf_all = jnp.concatenate([cache_b_logf[l].astype(F32).transpose(0, 2, 1), lf_new,
                              jnp.zeros((dec_b, h_b, n_pad - dec_s), F32)], axis=-1)
    c_s = _cumsum_lanes(lf_all.reshape(dec_b * h_b, past + n_pad), tri, rb=dec_b * h_b)
    forget_s = (sthree(ps["qb"]), heads_t(cache_b_k), heads_t(cache_b_v), sthree(ps["kb"]),
                sthree(ps["vb_f"]), c_s.reshape(dec_b, h_b, past + n_pad))
    bias_c = _rel_bias_rows(rel_bias[l], n_cache, dec_s, n_cache) * LOG2E
    bias_n = jnp.full((h_a * dec_s, n_pad), NEG, F32).at[:, :dec_s].set(
        _rel_bias_rows(rel_bias[l], 0, dec_s, dec_s) * LOG2E)
    band_s = (sthree(ps["qa"]), heads_t(cache_a_k), heads_t(cache_a_v), sthree(ps["ka"]), sthree(ps["va"]))
    mem_s = (sthree(ps["qm"]), cache_mem_k[l].reshape(dec_b, n_mem * h_m, dh_m),
             cache_mem_v[l].reshape(dec_b, n_mem * h_m, dh_m))

    xp = x_prompt.reshape(batch * seq, d)
    tm_p = 512
    ride = dec_b % (batch * seq // tm_p) == 0
    riders = tuple((_cast_kernel, (a[l],), ()) for a in (w_pa, w_pb, w_pm, w_o, w_up, w_down))
    if ride:
        riders += ((functools.partial(_forget_sample_kernel, n_pad=n_pad), forget_s, ()),
                   (_band_sample_kernel, band_s, (bias_c, bias_n)),
                   (functools.partial(_mem_kernel, heads=h_m), mem_s, ()))
    pp = _proj(xp, *proj_w, tm=tm_p, group_rows=seq, keep_rows=keep_p, riders=riders)
    wpa, wpb, wpm, wo, wup, wdn = (pp[f"rider{k}"] for k in range(6))
    if ride:
        o_b_s, o_a_s, o_m_s = pp["rider6"], pp["rider7"], pp["rider8"]
    else:
        o_b_s = _forget_sample(*forget_s, n_pad=n_pad)
        o_a_s = _band_sample(*band_s, bias_c, bias_n, bb=4)
        o_m_s = _mem_attn(*mem_s, heads=h_m, tq=dec_s, bb=8)
    mk_f, mv_f = _memkv(mem_prompt.reshape(batch * n_mem, d), g_mem[l][None, :], w_mkv[l].astype(BF16),
                        gkm, bd128, tm=512, heads=h_m)
    tq_b = 512
    lf_p = pp["lft"].reshape(h_b, batch, seq).transpose(1, 0, 2)
    c_p = _cumsum_lanes(lf_p.reshape(batch * h_b, seq), tri, rb=batch * h_b)
    c_p = c_p.reshape(batch, h_b // 4, 4, seq).transpose(0, 1, 3, 2)
    vt = pp["vbt"].reshape(batch, seq // tq_b, h_b, V_ROWS, tq_b)
    three = lambda a: a.reshape(batch, seq, -1)
    bias_p = _rel_bias_rows(rel_bias[l], n_cache, CHUNK, n_cache + CHUNK) * LOG2E
    bound_a = _qk_bound(g_qa[l], g_ka[l], dh_a) + LOG2E * jnp.max(jnp.abs(rel_bias[l]))
    o_a = _band_prompt(bound_a.reshape(1).astype(F32), three(pp["qa"]), three(pp["ka"]), three(pp["va"]),
                       bias_p, reach=n_cache, tq=256)
    o_b = _forget_prompt(_qk_bound(g_qb[l], g_kb[l], dh_b).reshape(1).astype(F32), three(pp["qb"]),
                         three(pp["kb"]), vt, c_p, head=dh_b, tq=tq_b)
    sample_tail = (xs, o_a_s.reshape(-1, w_a), o_b_s.reshape(-1, w_b), o_m_s.reshape(-1, w_m))
    y_prompt, y_sample = _post(
        xp, o_a.reshape(-1, w_a), o_b.reshape(-1, w_b), pp["qm"], mk_f.reshape(batch, n_mem * h_m, dh_m),
        mv_f.reshape(batch, n_mem * h_m, dh_m), sample_tail, gmix, wg, wpa, wpb, wpm, wo, gffn, wup, wdn,
        tm=dec_b * dec_s, mem_heads=h_m)
    y_prompt = y_prompt.reshape(batch, seq, d)
    y_sample = y_sample.reshape(dec_b, dec_s, d)

    lead = lambda a, *shape: a.reshape((depth,) + shape)
    frames_major = lambda a: jnp.transpose(a, (0, 3, 1, 2))[None]
    return (y_prompt, y_sample,
            frames_major(pp["ka_f"]), frames_major(pp["va_f"]),
            frames_major(pp["kb_f"]), frames_major(pp["vb_f"]),
            lf_p.transpose(0, 2, 1)[None],
            lead(mk_f, batch, n_mem, h_m, dh_m), lead(mv_f, batch, n_mem, h_m, dh_m),
            lead(ps["ka_f"], dec_b, dec_s, h_a, dh_a), lead(ps["va_f"], dec_b, dec_s, h_a, dh_a),
            lead(ps["kb_f"], dec_b, dec_s, h_b, dh_b), lead(ps["vb_f"], dec_b, dec_s, h_b, dh_b),
            lead(ps["lft"].T, dec_b, dec_s, h_b))
```

```python
import functools

import jax
import jax.numpy as jnp
from jax import lax
from jax.experimental import pallas as pl
from jax.experimental.pallas import tpu as pltpu

BF16 = jnp.bfloat16
F32 = jnp.float32

EPS = 1e-6
CHUNK = 64
REL_CLIP = 128
NEG = -1e30
LANE_GROUP = 256
LOG2E = 1.4426950408889634
V_ROWS = 64 + 16
MAX_ONE_PASS_BOUND = 40.0
VMEM_LIMIT = 58 * 1024 * 1024


def _cparams(*sem):
    return pltpu.CompilerParams(dimension_semantics=sem, vmem_limit_bytes=VMEM_LIMIT)


def _const_spec(shape):
    nd = len(shape)
    return pl.BlockSpec(shape, lambda *_: (0,) * nd, pipeline_mode=pl.Buffered(1))


def _nt_dot(a, b):
    return lax.dot_general(a, b, (((1,), (1,)), ((), ())), preferred_element_type=F32)


def _dot(a, b):
    return jnp.dot(a, b, preferred_element_type=F32)


def _rms_rows(x, gain):
    ms = jnp.mean(x * x, axis=-1, keepdims=True)
    return x * lax.rsqrt(ms + EPS) * gain


def _head_norm(z, bd_ref, gain):
    z2 = (z * z).astype(BF16)
    ms = jnp.concatenate(
        [_dot(z2[:, h * 256:(h + 1) * 256], bd_ref[...]) for h in range(z.shape[1] // 256)], axis=-1)
    return z * lax.rsqrt(ms + EPS) * gain


def _log_sigmoid(x):
    return jnp.minimum(x, 0.0) - jnp.log1p(jnp.exp(-jnp.abs(x)))


def _lane_group(width, head):
    return lax.broadcasted_iota(jnp.int32, (1, width), 1) // head


def _block_diag_rows(q4, head):
    grp = _lane_group(q4.shape[1], head)
    zero = jnp.zeros_like(q4)
    return jnp.concatenate([jnp.where(grp == g, q4, zero) for g in range(q4.shape[1] // head)], axis=0)


def _pick_diag(o, rows, head):
    grp = _lane_group(o.shape[1], head)
    out = jnp.zeros((rows, o.shape[1]), o.dtype)
    for g in range(o.shape[1] // head):
        out = jnp.where(grp == g, o[g * rows:(g + 1) * rows], out)
    return out


def _cast_kernel(x_ref, o_ref):
    o_ref[...] = x_ref[...].astype(o_ref.dtype)


def _store_heads(ref, z):
    dh = ref.shape[1]
    heads = ref.shape[0] // z.shape[0]
    for h in range(heads):
        ref[pl.ds(h, z.shape[0], stride=heads), :] = z[:, h * dh:(h + 1) * dh]


def _store_heads_t(ref, z_t):
    dh = ref.shape[1]
    for h in range(ref.shape[0]):
        ref[h] = z_t[h * dh:(h + 1) * dh, :]


def _proj_kernel(x_ref, gmix_ref, wqkv_ref, wft_ref, gains_ref, gqm_ref, bf_ref,
                 bd64_ref, bd128_ref, *refs, out_names, scale_a, scale_b, scale_m, keep_from,
                 tiles_per_group, riders):
    rider_in, out_refs = refs[:len(refs) - len(out_names)], refs[len(refs) - len(out_names):]
    o = dict(zip(out_names, out_refs))
    head_major = "vbt" in o
    for k, (rider_kernel, n_in) in enumerate(riders):
        rider_kernel(*rider_in[:n_in], o[f"rider{k}"])
        rider_in = rider_in[n_in:]
    xn = _rms_rows(x_ref[...], gmix_ref[...]).astype(BF16)

    def seg(j):
        return _dot(xn, wqkv_ref[:, j * 512:(j + 1) * 512])

    kb = _head_norm(seg(4), bd64_ref, gains_ref[3:4, :])
    o["kb"][...] = kb.astype(BF16)
    vb = seg(5)
    if head_major:
        _store_heads_t(o["kb_f"], kb.T)
        vb_t = vb.T
        _store_heads_t(o["vb_f"], vb_t)
        vbt = o["vbt"]
        head = vb_t.shape[0] // vbt.shape[0]
        pad = vbt.shape[1] - head
        ones_row = (lax.broadcasted_iota(jnp.int32, (pad, vb_t.shape[1]), 0) == 0).astype(BF16)
        for h in range(vbt.shape[0]):
            vbt[h, 0:head, :] = vb_t[h * head:(h + 1) * head, :].astype(BF16)
            vbt[h, head:head + pad, :] = ones_row
    else:
        _store_heads(o["kb_f"], kb)
        _store_heads(o["vb_f"], vb)
        o["vb"][...] = vb.astype(BF16)
    qb = _head_norm(seg(3), bd64_ref, gains_ref[2:3, :])
    o["qb"][...] = (qb * scale_b).astype(BF16)
    qa = _head_norm(seg(0), bd64_ref, gains_ref[0:1, :])
    o["qa"][...] = (qa * scale_a).astype(BF16)
    ka = _head_norm(seg(1), bd64_ref, gains_ref[1:2, :])
    o["ka"][...] = ka.astype(BF16)
    qm = _head_norm(seg(6), bd128_ref, gqm_ref[...])
    o["qm"][...] = (qm * scale_m).astype(BF16)
    flt = _nt_dot(wft_ref[...], xn)
    o["lft"][...] = _log_sigmoid(flt[0:8, :] + bf_ref[...])
    va = seg(2)
    o["va"][...] = va.astype(BF16)
    if head_major:
        @pl.when(pl.program_id(0) % tiles_per_group >= keep_from)
        def _():
            _store_heads_t(o["ka_f"], ka.T)
            _store_heads_t(o["va_f"], va.T)
    else:
        _store_heads(o["ka_f"], ka)
        _store_heads(o["va_f"], va)


def _proj(x, gmix, wqkv, wft, gains, gqm, bf, bd64, bd128, *, tm, group_rows=None, keep_rows=None,
          riders=()):
    rows, d = x.shape
    n_tiles = rows // tm
    row = lambda i: (i, 0)
    wide = lambda dt: (jax.ShapeDtypeStruct((rows, 512), dt), pl.BlockSpec((tm, 512), row))
    outs = {n: wide(BF16) for n in ("qa", "ka", "va", "qb", "kb", "qm")}
    outs["lft"] = (jax.ShapeDtypeStruct((8, rows), F32), pl.BlockSpec((8, tm), lambda i: (0, i)))
    tpg = keep_from = 1
    if group_rows is None:
        outs["vb"] = wide(BF16)
        for n in ("ka_f", "va_f", "kb_f", "vb_f"):
            outs[n] = (jax.ShapeDtypeStruct((rows * 8, 64), F32), pl.BlockSpec((tm * 8, 64), row))
    else:
        assert keep_rows == tm
        tpg = group_rows // tm
        keep_from = tpg - 1
        groups = rows // group_rows
        kept = (jax.ShapeDtypeStruct((groups, 8, 64, keep_rows), F32),
                pl.BlockSpec((None, 8, 64, tm), lambda i: (i // tpg, 0, 0, 0)))
        full = (jax.ShapeDtypeStruct((groups, 8, 64, group_rows), F32),
                pl.BlockSpec((None, 8, 64, tm), lambda i: (i // tpg, 0, 0, i % tpg)))
        outs.update(ka_f=kept, va_f=kept, kb_f=full, vb_f=full)
        outs["vbt"] = (jax.ShapeDtypeStruct((n_tiles, 8, V_ROWS, tm), BF16),
                       pl.BlockSpec((None, 8, V_ROWS, tm), lambda i: (i, 0, 0, 0)))
    in_specs = [pl.BlockSpec((tm, d), row)] + [_const_spec(a.shape) for a in
                                               (gmix, wqkv, wft, gains, gqm, bf, bd64, bd128)]
    rider_args, rider_kernels = [], []
    for k, (rider_kernel, per_tile, const) in enumerate(riders):
        assert all(a.shape[0] % n_tiles == 0 for a in per_tile)
        tile_spec = lambda a: pl.BlockSpec((a.shape[0] // n_tiles,) + a.shape[1:],
                                           lambda i, nd=a.ndim: (i,) + (0,) * (nd - 1))
        in_specs += [tile_spec(a) for a in per_tile] + [_const_spec(a.shape) for a in const]
        outs[f"rider{k}"] = (jax.ShapeDtypeStruct(per_tile[0].shape, BF16), tile_spec(per_tile[0]))
        rider_args += [*per_tile, *const]
        rider_kernels.append((rider_kernel, len(per_tile) + len(const)))
    names = tuple(outs)
    kern = functools.partial(_proj_kernel, out_names=names, scale_a=LOG2E * 64 ** -0.5,
                             scale_b=LOG2E * 64 ** -0.5, scale_m=LOG2E * 128 ** -0.5,
                             keep_from=keep_from, tiles_per_group=tpg, riders=tuple(rider_kernels))
    res = pl.pallas_call(kern, out_shape=[outs[n][0] for n in names], grid=(n_tiles,), in_specs=in_specs,
                         out_specs=[outs[n][1] for n in names], compiler_params=_cparams("arbitrary"),
                         name="proj")(x, gmix, wqkv, wft, gains, gqm, bf, bd64, bd128, *rider_args)
    return dict(zip(names, res))


def _split_w_in_kernel(wt_ref, wp_ref, wg_ref, *, n_proj):
    i = pl.program_id(0)
    blk = wt_ref[...].T.astype(BF16)

    @pl.when(i < n_proj)
    def _():
        wp_ref[...] = blk

    @pl.when(i >= n_proj)
    def _():
        wg_ref[...] = blk


def _split_w_in(w_t, *, proj_cols, qm_cols, gate_cols, block=512):
    d = w_t.shape[1]
    spans = [proj_cols, qm_cols, gate_cols]
    assert all((b - a) % block == 0 and a % 8 == 0 for a, b in spans) and proj_cols[0] == 0
    n_a = (proj_cols[1] - proj_cols[0]) // block
    n_proj = n_a + (qm_cols[1] - qm_cols[0]) // block
    n_gate = (gate_cols[1] - gate_cols[0]) // block
    assert qm_cols[1] == gate_cols[0]

    def row_start(i):
        return pl.multiple_of(jnp.where(i < n_a, block * i, qm_cols[0] + block * (i - n_a)), 8)

    return pl.pallas_call(
        functools.partial(_split_w_in_kernel, n_proj=n_proj),
        out_shape=[jax.ShapeDtypeStruct((d, block * n_proj), BF16), jax.ShapeDtypeStruct((d, block * n_gate), BF16)],
        grid=(n_proj + n_gate,),
        in_specs=[pl.BlockSpec((pl.Element(block), pl.Element(d)), lambda i: (row_start(i), 0))],
        out_specs=[pl.BlockSpec((d, block), lambda i: (0, jnp.minimum(i, n_proj - 1))),
                   pl.BlockSpec((d, block), lambda i: (0, jnp.maximum(i - n_proj, 0)))],
        compiler_params=_cparams("arbitrary"), name="split_w_in")(w_t)


def _memkv_kernel(m_ref, gmem_ref, w_ref, gkm_ref, bd128_ref, mk_o, mv_o, *, heads):
    tm = m_ref.shape[0]
    xn = _rms_rows(m_ref[...], gmem_ref[...]).astype(BF16)
    half = w_ref.shape[1] // 2
    dh = half // heads
    mk = _head_norm(_dot(xn, w_ref[:, :half]), bd128_ref, gkm_ref[...])
    mv = _dot(xn, w_ref[:, half:])
    for h in range(heads):
        mk_o[pl.ds(h, tm, stride=heads), :] = mk[:, h * dh:(h + 1) * dh]
        mv_o[pl.ds(h, tm, stride=heads), :] = mv[:, h * dh:(h + 1) * dh]


def _memkv(mem, gmem, w, gkm, bd128, *, tm, heads):
    rows, d = mem.shape
    dh = w.shape[1] // 2 // heads
    row = lambda i: (i, 0)
    return pl.pallas_call(
        functools.partial(_memkv_kernel, heads=heads),
        out_shape=[jax.ShapeDtypeStruct((rows * heads, dh), F32)] * 2, grid=(rows // tm,),
        in_specs=[pl.BlockSpec((tm, d), row)] + [_const_spec(a.shape) for a in (gmem, w, gkm, bd128)],
        out_specs=[pl.BlockSpec((tm * heads, dh), row)] * 2, compiler_params=_cparams("arbitrary"),
        name="memkv")(mem, gmem, w, gkm, bd128)


def _cumsum_kernel(x_ref, tri_ref, o_ref):
    rows, length = x_ref.shape
    tri = tri_ref[...]

    def local(c):
        x = x_ref[:, c * 128:(c + 1) * 128]
        hi = x.astype(BF16)
        r1 = x - hi.astype(F32)
        mid = r1.astype(BF16)
        lo = (r1 - mid.astype(F32)).astype(BF16)
        return _dot(hi, tri) + _dot(mid, tri) + _dot(lo, tri)

    chunks = [local(c) for c in range(length // 128)]
    carry = jnp.zeros((rows, 1), F32)
    for c, cc in enumerate(chunks):
        o_ref[:, c * 128:(c + 1) * 128] = cc + carry
        carry = carry + cc[:, 127:128]


def _cumsum_lanes(x, tri, *, rb):
    rows, length = x.shape
    return pl.pallas_call(
        _cumsum_kernel, out_shape=jax.ShapeDtypeStruct((rows, length), F32), grid=(rows // rb,),
        in_specs=[pl.BlockSpec((rb, length), lambda i: (i, 0)), _const_spec(tri.shape)],
        out_specs=pl.BlockSpec((rb, length), lambda i: (i, 0)), compiler_params=_cparams("arbitrary"),
        name="cumsum")(x, tri)


def _band_kernel(bound_ref, q_ref, k0_ref, k1_ref, k2_ref, v0_ref, v1_ref, v2_ref, bias_ref, o_ref,
                 k_sc, v_sc, *, reach, lookahead):
    tq = q_ref.shape[0]
    band = reach + CHUNK
    i = pl.program_id(1)
    for n, (kr, vr) in enumerate(((k0_ref, v0_ref), (k1_ref, v1_ref), (k2_ref, v2_ref))):
        k_sc[n * tq:(n + 1) * tq, :] = kr[...]
        v_sc[n * tq:(n + 1) * tq, :] = vr[...]
    col = lax.broadcasted_iota(jnp.int32, (1, band), 1)
    units = [(j, hg) for j in range(tq // CHUNK) for hg in range(q_ref.shape[1] // LANE_GROUP)]

    def scores(j, hg):
        r0 = 2 * tq + j * CHUNK - reach
        lanes = slice(hg * LANE_GROUP, (hg + 1) * LANE_GROUP)
        qbd = _block_diag_rows(q_ref[j * CHUNK:(j + 1) * CHUNK, lanes], CHUNK)
        s = _nt_dot(qbd, k_sc[r0:r0 + band, lanes]) + bias_ref[hg * 4 * CHUNK:(hg + 1) * 4 * CHUNK, :]
        return jnp.where((i - 2) * tq + r0 + col >= 0, s, NEG)

    def attend(one_pass):
        pending = [scores(*u) for u in units[:lookahead]]
        for n, (j, hg) in enumerate(units):
            s = pending.pop(0)
            if n + lookahead < len(units):
                pending.append(scores(*units[n + lookahead]))
            r0 = 2 * tq + j * CHUNK - reach
            lanes = slice(hg * LANE_GROUP, (hg + 1) * LANE_GROUP)
            p = jnp.exp2(s if one_pass else s - jnp.max(s, axis=-1, keepdims=True))
            l = jnp.sum(p, axis=-1, keepdims=True)
            o = _dot(p.astype(BF16), v_sc[r0:r0 + band, lanes]) / l
            o_ref[j * CHUNK:(j + 1) * CHUNK, lanes] = _pick_diag(o, CHUNK, CHUNK).astype(o_ref.dtype)

    lax.cond(bound_ref[0] <= MAX_ONE_PASS_BOUND, lambda: attend(True), lambda: attend(False))


def _band_prompt(logit_bound, q, k, v, bias, *, reach, tq):
    b, t, w = q.shape
    assert reach == 2 * tq and tq % CHUNK == 0
    blk = lambda off: pl.BlockSpec((None, tq, w), lambda bi, i: (bi, jnp.maximum(i - off, 0), 0))
    return pl.pallas_call(
        functools.partial(_band_kernel, reach=reach, lookahead=2),
        out_shape=jax.ShapeDtypeStruct((b, t, w), BF16), grid=(b, t // tq),
        in_specs=[pl.BlockSpec(memory_space=pltpu.SMEM), blk(0), blk(2), blk(1), blk(0), blk(2), blk(1), blk(0),
                  _const_spec(bias.shape)],
        out_specs=blk(0),
        scratch_shapes=[pltpu.VMEM((3 * tq, w), BF16), pltpu.VMEM((3 * tq, w), BF16)],
        compiler_params=_cparams("arbitrary", "arbitrary"), name="band_prompt")(
            logit_bound, q, k, k, k, v, v, v, bias)


def _forget_kernel(bound_ref, q_ref, k_ref, vt_ref, c_ref, o_ref, qt_sc, m_sc, acc_sc, *, head, lookahead):
    tq, width = q_ref.shape
    n_heads = width // head
    i = pl.program_id(2)
    logit_bound = bound_ref[0]
    q_t = q_ref[...].astype(F32).T
    row_grp = lax.broadcasted_iota(jnp.int32, (width, 1), 0) // head
    for g in range(n_heads):
        qt_sc[g] = jnp.where(row_grp == g, q_t, 0.0).astype(BF16)
    m_sc[...] = jnp.full(m_sc.shape, NEG, F32)
    acc_sc[...] = jnp.zeros(acc_sc.shape, F32)
    c_first = c_ref[pl.ds(pl.multiple_of(i * tq, tq), 8), :][0:1, :]

    def step(j, masked):
        k0 = pl.multiple_of(j * tq, tq)
        kt = k_ref[pl.ds(k0, tq), :]
        bias = (c_first - c_ref[pl.ds(k0, tq), :]) * LOG2E
        if masked:
            keep = (lax.broadcasted_iota(jnp.int32, (tq, tq), 0)
                    <= lax.broadcasted_iota(jnp.int32, (tq, tq), 1))

        def scores(g):
            s = _dot(kt, qt_sc[g]) + bias[:, g:g + 1]
            return jnp.where(keep, s, NEG) if masked else s

        pending = [scores(g) for g in range(min(lookahead, n_heads))]
        for g in range(n_heads):
            s = pending.pop(0)
            if g + lookahead < n_heads:
                pending.append(scores(g + lookahead))
            m_prev = m_sc[g]
            m_new = jnp.maximum(m_prev, jnp.max(s, axis=0, keepdims=True))
            p = jnp.exp2(s - m_new).astype(BF16)
            acc_sc[g] = jnp.exp2(m_prev - m_new) * acc_sc[g] + _dot(vt_ref[j, g], p)
            m_sc[g] = m_new

    def bounded_tiles(tiles):
        m_tile, logits = [], []
        for j in tiles:
            k0 = pl.multiple_of(j * tq, tq)
            c_tile = c_ref[pl.ds(k0, tq), :]
            c_last = c_tile[tq - 1:tq, :]
            m_tile.append((c_first - c_last) * LOG2E)
            shift = (c_last - c_tile) * LOG2E
            logits.append(functools.partial(
                lambda g, k0, shift: _dot(k_ref[pl.ds(k0, tq), :], qt_sc[g]) + shift[:, g:g + 1],
                k0=k0, shift=shift))
        units = [(t, g) for t in range(len(tiles)) for g in range(n_heads)]
        pending = [logits[t](g) for t, g in units[:lookahead]]
        for n, (t, g) in enumerate(units):
            s = pending.pop(0)
            if n + lookahead < len(units):
                tn, gn = units[n + lookahead]
                pending.append(logits[tn](gn))
            m_new = jnp.broadcast_to(m_tile[t][:, g:g + 1], (1, tq))
            p = jnp.exp2(s).astype(BF16)
            acc_sc[g] = jnp.exp2(m_sc[g] - m_new) * acc_sc[g] + _dot(vt_ref[tiles[t], g], p)
            m_sc[g] = m_new

    def bounded_loop():
        def pair(jj, carry):
            bounded_tiles([2 * jj, 2 * jj + 1])
            return carry
        lax.fori_loop(0, i // 2, pair, 0)

        @pl.when(i % 2 == 1)
        def _():
            bounded_tiles([i - 1])

    def two_pass_loop():
        def body(j, carry):
            step(j, False)
            return carry
        lax.fori_loop(0, i, body, 0)

    lax.cond(logit_bound <= MAX_ONE_PASS_BOUND, bounded_loop, two_pass_loop)
    step(i, True)
    o_t = jnp.concatenate([acc_sc[g, 0:head, :] / acc_sc[g, head:head + 1, :] for g in range(n_heads)],
                          axis=0)
    o_ref[...] = o_t.T.astype(o_ref.dtype)


def _forget_prompt(logit_bound, q, k, vt, c, *, head, tq):
    b, t, w = q.shape
    n_heads = LANE_GROUP // head
    v_rows = vt.shape[3]
    lookahead = 3
    assert lookahead <= n_heads
    return pl.pallas_call(
        functools.partial(_forget_kernel, head=head, lookahead=lookahead),
        out_shape=jax.ShapeDtypeStruct((b, t, w), BF16), grid=(b, w // LANE_GROUP, t // tq),
        in_specs=[pl.BlockSpec(memory_space=pltpu.SMEM),
                  pl.BlockSpec((None, tq, LANE_GROUP), lambda bi, hg, i: (bi, i, hg)),
                  pl.BlockSpec((None, t, LANE_GROUP), lambda bi, hg, i: (bi, 0, hg)),
                  pl.BlockSpec((None, t // tq, n_heads, v_rows, tq), lambda bi, hg, i: (bi, 0, hg, 0, 0)),
                  pl.BlockSpec((None, None, t, n_heads), lambda bi, hg, i: (bi, hg, 0, 0))],
        out_specs=pl.BlockSpec((None, tq, LANE_GROUP), lambda bi, hg, i: (bi, i, hg)),
        scratch_shapes=[pltpu.VMEM((n_heads, LANE_GROUP, tq), BF16), pltpu.VMEM((n_heads, 1, tq), F32),
                        pltpu.VMEM((n_heads, v_rows, tq), F32)],
        compiler_params=_cparams("arbitrary", "arbitrary", "arbitrary"), name="forget_prompt")(
            logit_bound, q, k, vt, c)


def _mem_attention(qs, mk_refs, mv_refs, heads):
    n_mem = mk_refs[0].shape[0] // heads
    dh = mk_refs[0].shape[1]
    units = [(e, h) for e in range(len(qs)) for h in range(heads)]
    rows = lambda ref, h: ref[pl.ds(h, n_mem, stride=heads), :].astype(BF16)
    s = [_nt_dot(qs[e][:, h * dh:(h + 1) * dh], rows(mk_refs[e], h)) for e, h in units]
    p, inv_l = [], []
    for u in range(len(units)):
        ex = jnp.exp2(s[u] - jnp.max(s[u], axis=-1, keepdims=True))
        inv_l.append(1.0 / jnp.sum(ex, axis=-1, keepdims=True))
        p.append(ex.astype(BF16))
    o = [(_dot(p[u], rows(mv_refs[e], h)) * inv_l[u]).astype(BF16) for u, (e, h) in enumerate(units)]
    return [jnp.concatenate(o[e * heads:(e + 1) * heads], axis=-1) for e in range(len(qs))]


def _mem_kernel(q_ref, mk_ref, mv_ref, o_ref, *, heads):
    bb = q_ref.shape[0]
    outs = _mem_attention([q_ref[e] for e in range(bb)], [mk_ref.at[e] for e in range(bb)],
                          [mv_ref.at[e] for e in range(bb)], heads)
    for e in range(bb):
        o_ref[e] = outs[e]


def _mem_attn(q, mk, mv, *, heads, tq, bb):
    b, t, w = q.shape
    kv = pl.BlockSpec((bb,) + mk.shape[1:], lambda bi, i: (bi, 0, 0))
    qs = pl.BlockSpec((bb, tq, w), lambda bi, i: (bi, i, 0))
    return pl.pallas_call(
        functools.partial(_mem_kernel, heads=heads), out_shape=jax.ShapeDtypeStruct((b, t, w), BF16),
        grid=(b // bb, t // tq), in_specs=[qs, kv, kv], out_specs=qs,
        compiler_params=_cparams("arbitrary", "arbitrary"), name="mem_attn")(q, mk, mv)


def _post_kernel(x_ref, oa_ref, ob_ref, qm_ref, mk_ref, mv_ref, xt_ref, oat_ref, obt_ref, omt_ref,
                 gmix_ref, wg_ref, wpa_ref, wpb_ref, wpm_ref, wo_ref, gffn_ref, wup_ref, wdn_ref,
                 y_ref, yt_ref, *, mem_heads, col_chunk, ff_chunk):
    tail = pl.program_id(0) == pl.num_programs(0) - 1
    o_m, = _mem_attention([qm_ref[...]], [mk_ref], [mv_ref], mem_heads)
    x = jnp.where(tail, xt_ref[...], x_ref[...])
    o_a = jnp.where(tail, oat_ref[...], oa_ref[...])
    o_b = jnp.where(tail, obt_ref[...], ob_ref[...])
    o_m = jnp.where(tail, omt_ref[...], o_m)
    d = x.shape[1]
    xn = _rms_rows(x, gmix_ref[...]).astype(BF16)
    branches = ((o_a, wpa_ref), (o_b, wpb_ref), (o_m, wpm_ref))
    h = []
    for c in range(d // col_chunk):
        cols = slice(c * col_chunk, (c + 1) * col_chunk)
        hc = 0.0
        for b, (o_b, wp_ref) in enumerate(branches):
            gate = jax.nn.sigmoid(_dot(xn, wg_ref[:, b * d + c * col_chunk:b * d + (c + 1) * col_chunk]))
            hc = hc + gate * _dot(o_b, wp_ref[:, cols])
        h.append(hc.astype(BF16))
    x1 = x + _dot(jnp.concatenate(h, axis=-1), wo_ref[...])
    xn1 = _rms_rows(x1, gffn_ref[...]).astype(BF16)
    y = x1
    for c in range(wup_ref.shape[1] // ff_chunk):
        cols = slice(c * ff_chunk, (c + 1) * ff_chunk)
        u = jnp.maximum(_dot(xn1, wup_ref[:, cols]), 0.0)
        y = y + _dot((u * u).astype(BF16), wdn_ref[cols, :])

    @pl.when(jnp.logical_not(tail))
    def _():
        y_ref[...] = y

    @pl.when(tail)
    def _():
        yt_ref[...] = y


def _post(x, oa, ob, qm, mk, mv, tail, gmix, wg, wpa, wpb, wpm, wo, gffn, wup, wdn, *, tm, mem_heads):
    rows, d = x.shape
    n_main = rows // tm
    assert all(a.shape[0] == tm for a in tail)
    row = lambda i: (jnp.minimum(i, n_main - 1), 0)
    tiles_per_batch = n_main // mk.shape[0]
    kv = pl.BlockSpec((None,) + mk.shape[1:], lambda i: (jnp.minimum(i, n_main - 1) // tiles_per_batch, 0, 0))
    acts = (x, oa, ob, qm)
    weights = (gmix, wg, wpa, wpb, wpm, wo, gffn, wup, wdn)
    return pl.pallas_call(
        functools.partial(_post_kernel, mem_heads=mem_heads, col_chunk=512, ff_chunk=1024),
        out_shape=[jax.ShapeDtypeStruct((rows, d), F32), jax.ShapeDtypeStruct((tm, d), F32)],
        grid=(n_main + 1,),
        in_specs=[pl.BlockSpec((tm, a.shape[1]), row) for a in acts] + [kv, kv]
        + [_const_spec(a.shape) for a in (*tail, *weights)],
        out_specs=[pl.BlockSpec((tm, d), row), pl.BlockSpec((tm, d), lambda i: (0, 0))],
        compiler_params=_cparams("arbitrary"), name="post")(*acts, mk, mv, *tail, *weights)


def _pad_rows(x, rows):
    return jnp.concatenate([x, jnp.zeros((rows - x.shape[0], x.shape[1]), x.dtype)], axis=0)


def _cached_attention(q_ref, kc_ref, vc_ref, kn_ref, vn_ref, o_ref, bias_c, bias_n, n_pad):
    bb, heads, head, _ = kc_ref.shape
    units = [(e, h, slice(h * head, (h + 1) * head)) for e in range(bb) for h in range(heads)]
    sc = [_dot(q_ref[e, :, ln], kc_ref[e, h].astype(BF16)) + bias_c(e, h) for e, h, ln in units]
    sn = [_nt_dot(q_ref[e, :, ln], _pad_rows(kn_ref[e, :, ln].astype(BF16), n_pad)) + bias_n(e, h)
          for e, h, ln in units]
    pc, pn, inv_l = [], [], []
    for u in range(len(units)):
        m = jnp.maximum(jnp.max(sc[u], axis=-1, keepdims=True), jnp.max(sn[u], axis=-1, keepdims=True))
        pc.append(jnp.exp2(sc[u] - m))
        pn.append(jnp.exp2(sn[u] - m))
        inv_l.append(1.0 / (jnp.sum(pc[u], axis=-1, keepdims=True) + jnp.sum(pn[u], axis=-1, keepdims=True)))
    for u, (e, h, ln) in enumerate(units):
        o = (_nt_dot(pc[u].astype(BF16), vc_ref[e, h].astype(BF16))
             + _dot(pn[u].astype(BF16), _pad_rows(vn_ref[e, :, ln].astype(BF16), n_pad)))
        o_ref[e, :, ln] = (o * inv_l[u]).astype(o_ref.dtype)


def _band_sample_kernel(q_ref, kc_ref, vc_ref, kn_ref, vn_ref, bc_ref, bn_ref, o_ref):
    s_len = q_ref.shape[1]
    rows = lambda h: slice(h * s_len, (h + 1) * s_len)
    _cached_attention(q_ref, kc_ref, vc_ref, kn_ref, vn_ref, o_ref,
                      lambda e, h: bc_ref[rows(h), :], lambda e, h: bn_ref[rows(h), :], bn_ref.shape[1])


def _band_sample(q, kct, vct, kn, vn, bias_c, bias_n, *, bb):
    b, s_len, w = q.shape
    new = pl.BlockSpec((bb, s_len, w), lambda bi: (bi, 0, 0))
    cache = pl.BlockSpec((bb,) + kct.shape[1:], lambda bi: (bi, 0, 0, 0))
    return pl.pallas_call(
        _band_sample_kernel, out_shape=jax.ShapeDtypeStruct((b, s_len, w), BF16), grid=(b // bb,),
        in_specs=[new, cache, cache, new, new, _const_spec(bias_c.shape), _const_spec(bias_n.shape)],
        out_specs=new, compiler_params=_cparams("arbitrary"), name="band_sample")(
            q, kct, vct, kn, vn, bias_c, bias_n)


def _forget_sample_kernel(q_ref, kc_ref, vc_ref, kn_ref, vn_ref, c_ref, o_ref, *, n_pad):
    s_len = q_ref.shape[1]
    p_len = kc_ref.shape[3]
    causal = (lax.broadcasted_iota(jnp.int32, (s_len, n_pad), 1)
              <= lax.broadcasted_iota(jnp.int32, (s_len, n_pad), 0))
    ch = [(c_ref[e, :, p_len - 1:p_len] - c_ref[e]) * LOG2E for e in range(c_ref.shape[0])]
    _cached_attention(q_ref, kc_ref, vc_ref, kn_ref, vn_ref, o_ref,
                      lambda e, h: ch[e][h:h + 1, 0:p_len],
                      lambda e, h: jnp.where(causal, ch[e][h:h + 1, p_len:p_len + n_pad], NEG), n_pad)


def _forget_sample(q, kct, vct, kn, vn, c, *, n_pad):
    b, s_len, w = q.shape
    new = pl.BlockSpec((1, s_len, w), lambda bi: (bi, 0, 0))
    cache = pl.BlockSpec((1,) + kct.shape[1:], lambda bi: (bi, 0, 0, 0))
    cs = pl.BlockSpec((1,) + c.shape[1:], lambda bi: (bi, 0, 0))
    return pl.pallas_call(
        functools.partial(_forget_sample_kernel, n_pad=n_pad),
        out_shape=jax.ShapeDtypeStruct((b, s_len, w), BF16), grid=(b,),
        in_specs=[new, cache, cache, new, new, cs], out_specs=new,
        compiler_params=_cparams("arbitrary"), name="forget_sample")(q, kct, vct, kn, vn, c)


def _qk_bound(g_q, g_k, dh):
    return 1.02 * LOG2E * dh ** 0.5 * jnp.max(jnp.abs(g_q)) * jnp.max(jnp.abs(g_k))


def _block_diag_mean(head, size=256):
    r = jnp.arange(size) // head
    return jnp.where(r[:, None] == r[None, :], 1.0 / head, 0.0).astype(BF16)


def _rel_bias_rows(rel_bias, d0, n_q, n_k):
    n = n_q + n_k - 1
    dist = d0 + (n_q - 1) - jnp.arange(n)
    e = rel_bias[jnp.clip(dist, -REL_CLIP, REL_CLIP) + REL_CLIP].T.astype(F32)
    h = e.shape[0]
    skew = jnp.tile(jnp.pad(e, ((0, 0), (0, 1))), (1, n_q))[:, :n_q * n].reshape(h, n_q, n)
    return skew[:, :, n_q - 1:].reshape(h * n_q, n_k)


def kernel(x_prompt, x_sample, mem_prompt, cache_a_k, cache_a_v, cache_b_k, cache_b_v, cache_b_logf,
           cache_mem_k, cache_mem_v, g_mix, w_in, b_f, g_qa, g_ka, g_qb, g_kb, g_qm, g_km, rel_bias,
           g_mem, w_mkv, w_pa, w_pb, w_pm, w_o, g_ffn, w_up, w_down):
    depth = w_in.shape[0]
    assert depth == 1
    batch, seq, d = x_prompt.shape
    dec_b, dec_s, _ = x_sample.shape
    _, _, n_cache, h_a, dh_a = cache_a_k.shape
    _, _, past, h_b, dh_b = cache_b_k.shape
    _, _, n_mem, h_m, dh_m = cache_mem_k.shape
    w_a, w_b, w_m = h_a * dh_a, h_b * dh_b, h_m * dh_m
    assert w_a == w_b == w_m == 512 and dh_a == dh_b == 64 and dh_m == 128 and h_b == 8
    keep_p = min(n_cache, seq)
    n_pad = 128
    l = 0

    qkv_end = 3 * w_a + 3 * w_b
    f_end = qkv_end + h_b
    qm_end = f_end + w_m
    w_t = jnp.transpose(w_in[l])
    wft = jnp.zeros((16, d), BF16).at[:h_b].set(w_t[qkv_end:f_end].astype(BF16))
    wqkvm, wg = _split_w_in(w_t, proj_cols=(0, qkv_end), qm_cols=(f_end, qm_end), gate_cols=(qm_end, w_t.shape[0]))
    gains = jnp.stack([jnp.tile(g[l], h_a) for g in (g_qa, g_ka, g_qb, g_kb)]).astype(F32)
    gqm = jnp.tile(g_qm[l], h_m)[None, :]
    gkm = jnp.tile(g_km[l], h_m)[None, :]
    bf = b_f[l][:, None].astype(F32)
    bd64 = _block_diag_mean(dh_a)
    bd128 = _block_diag_mean(dh_m)
    tri = (jnp.arange(128)[:, None] <= jnp.arange(128)[None, :]).astype(BF16)
    gmix = g_mix[l][None, :]
    proj_w = (gmix, wqkvm, wft, gains, gqm, bf, bd64, bd128)
    gffn = g_ffn[l][None, :]

    xs = x_sample.reshape(dec_b * dec_s, d)
    ps = _proj(xs, *proj_w, tm=512)
    sthree = lambda a: a.reshape(dec_b, dec_s, -1)
    heads_t = lambda a: jnp.transpose(a[l], (0, 2, 3, 1))
    lf_new = ps["lft"].reshape(h_b, dec_b, dec_s).transpose(1, 0, 2)
    lf_all = jnp.concatenate([cache_b_logf[l].astype(F32).transpose(0, 2, 1), lf_new,
                              jnp.zeros((dec_b, h_b, n_pad - dec_s), F32)], axis=-1)
    c_s = _cumsum_lanes(lf_all.reshape(dec_b * h_b, past + n_pad), tri, rb=dec_b * h_b)
    forget_s = (sthree(ps["qb"]), heads_t(cache_b_k), heads_t(cache_b_v), sthree(ps["kb"]),
                sthree(ps["vb"]), c_s.reshape(dec_b, h_b, past + n_pad))
    bias_c = _rel_bias_rows(rel_bias[l], n_cache, dec_s, n_cache) * LOG2E
    bias_n = jnp.full((h_a * dec_s, n_pad), NEG, F32).at[:, :dec_s].set(
        _rel_bias_rows(rel_bias[l], 0, dec_s, dec_s) * LOG2E)
    band_s = (sthree(ps["qa"]), heads_t(cache_a_k), heads_t(cache_a_v), sthree(ps["ka"]), sthree(ps["va"]))
    mem_s = (sthree(ps["qm"]), cache_mem_k[l].reshape(dec_b, n_mem * h_m, dh_m),
             cache_mem_v[l].reshape(dec_b, n_mem * h_m, dh_m))

    xp = x_prompt.reshape(batch * seq, d)
    tm_p = 512
    ride = dec_b % (batch * seq // tm_p) == 0
    riders = tuple((_cast_kernel, (a[l],), ()) for a in (w_pa, w_pb, w_pm, w_o, w_up, w_down))
    if ride:
        riders += ((functools.partial(_forget_sample_kernel, n_pad=n_pad), forget_s, ()),
                   (_band_sample_kernel, band_s, (bias_c, bias_n)),
                   (functools.partial(_mem_kernel, heads=h_m), mem_s, ()))
    pp = _proj(xp, *proj_w, tm=tm_p, group_rows=seq, keep_rows=keep_p, riders=riders)
    wpa, wpb, wpm, wo, wup, wdn = (pp[f"rider{k}"] for k in range(6))
    if ride:
        o_b_s, o_a_s, o_m_s = pp["rider6"], pp["rider7"], pp["rider8"]
    else:
        o_b_s = _forget_sample(*forget_s, n_pad=n_pad)
        o_a_s = _band_sample(*band_s, bias_c, bias_n, bb=4)
        o_m_s = _mem_attn(*mem_s, heads=h_m, tq=dec_s, bb=8)
    mk_f, mv_f = _memkv(mem_prompt.reshape(batch * n_mem, d), g_mem[l][None, :], w_mkv[l].astype(BF16),
                        gkm, bd128, tm=512, heads=h_m)
    tq_b = 512
    lf_p = pp["lft"].reshape(h_b, batch, seq).transpose(1, 0, 2)
    c_p = _cumsum_lanes(lf_p.reshape(batch * h_b, seq), tri, rb=batch * h_b)
    c_p = c_p.reshape(batch, h_b // 4, 4, seq).transpose(0, 1, 3, 2)
    vt = pp["vbt"].reshape(batch, seq // tq_b, h_b, V_ROWS, tq_b)
    three = lambda a: a.reshape(batch, seq, -1)
    bias_p = _rel_bias_rows(rel_bias[l], n_cache, CHUNK, n_cache + CHUNK) * LOG2E
    bound_a = _qk_bound(g_qa[l], g_ka[l], dh_a) + LOG2E * jnp.max(jnp.abs(rel_bias[l]))
    o_a = _band_prompt(bound_a.reshape(1).astype(F32), three(pp["qa"]), three(pp["ka"]), three(pp["va"]),
                       bias_p, reach=n_cache, tq=256)
    o_b = _forget_prompt(_qk_bound(g_qb[l], g_kb[l], dh_b).reshape(1).astype(F32), three(pp["qb"]),
                         three(pp["kb"]), vt, c_p, head=dh_b, tq=tq_b)
    sample_tail = (xs, o_a_s.reshape(-1, w_a), o_b_s.reshape(-1, w_b), o_m_s.reshape(-1, w_m))
    y_prompt, y_sample = _post(
        xp, o_a.reshape(-1, w_a), o_b.reshape(-1, w_b), pp["qm"], mk_f.reshape(batch, n_mem * h_m, dh_m),
        mv_f.reshape(batch, n_mem * h_m, dh_m), sample_tail, gmix, wg, wpa, wpb, wpm, wo, gffn, wup, wdn,
        tm=dec_b * dec_s, mem_heads=h_m)
    y_prompt = y_prompt.reshape(batch, seq, d)
    y_sample = y_sample.reshape(dec_b, dec_s, d)

    lead = lambda a, *shape: a.reshape((depth,) + shape)
    frames_major = lambda a: jnp.transpose(a, (0, 3, 1, 2))[None]
    return (y_prompt, y_sample,
            frames_major(pp["ka_f"]), frames_major(pp["va_f"]),
            frames_major(pp["kb_f"]), frames_major(pp["vb_f"]),
            lf_p.transpose(0, 2, 1)[None],
            lead(mk_f, batch, n_mem, h_m, dh_m), lead(mv_f, batch, n_mem, h_m, dh_m),
            lead(ps["ka_f"], dec_b, dec_s, h_a, dh_a), lead(ps["va_f"], dec_b, dec_s, h_a, dh_a),
            lead(ps["kb_f"], dec_b, dec_s, h_b, dh_b), lead(ps["vb_f"], dec_b, dec_s, h_b, dh_b),
            lead(ps["lft"].T, dec_b, dec_s, h_b))
```

```python
import functools

import jax
import jax.numpy as jnp
from jax import lax
from jax.experimental import pallas as pl
from jax.experimental.pallas import tpu as pltpu

BF16 = jnp.bfloat16
F32 = jnp.float32

EPS = 1e-6
CHUNK = 64
REL_CLIP = 128
NEG = -1e30
LANE_GROUP = 256
LOG2E = 1.4426950408889634
V_ROWS = 64 + 16
MAX_ONE_PASS_BOUND = 40.0
VMEM_LIMIT = 58 * 1024 * 1024


def _cparams(*sem):
    return pltpu.CompilerParams(dimension_semantics=sem, vmem_limit_bytes=VMEM_LIMIT)


def _const_spec(shape):
    nd = len(shape)
    return pl.BlockSpec(shape, lambda *_: (0,) * nd, pipeline_mode=pl.Buffered(1))


def _nt_dot(a, b):
    return lax.dot_general(a, b, (((1,), (1,)), ((), ())), preferred_element_type=F32)


def _dot(a, b):
    return jnp.dot(a, b, preferred_element_type=F32)


def _rms_rows(x, gain):
    ms = jnp.mean(x * x, axis=-1, keepdims=True)
    return x * lax.rsqrt(ms + EPS) * gain


def _head_norm(z, bd_ref, gain):
    z2 = (z * z).astype(BF16)
    ms = jnp.concatenate(
        [_dot(z2[:, h * 256:(h + 1) * 256], bd_ref[...]) for h in range(z.shape[1] // 256)], axis=-1)
    return z * lax.rsqrt(ms + EPS) * gain


def _log_sigmoid(x):
    return jnp.minimum(x, 0.0) - jnp.log1p(jnp.exp(-jnp.abs(x)))


def _lane_group(width, head):
    return lax.broadcasted_iota(jnp.int32, (1, width), 1) // head


def _block_diag_rows(q4, head):
    grp = _lane_group(q4.shape[1], head)
    zero = jnp.zeros_like(q4)
    return jnp.concatenate([jnp.where(grp == g, q4, zero) for g in range(q4.shape[1] // head)], axis=0)


def _pick_diag(o, rows, head):
    grp = _lane_group(o.shape[1], head)
    out = jnp.zeros((rows, o.shape[1]), o.dtype)
    for g in range(o.shape[1] // head):
        out = jnp.where(grp == g, o[g * rows:(g + 1) * rows], out)
    return out


def _cast_kernel(x_ref, o_ref):
    o_ref[...] = x_ref[...].astype(o_ref.dtype)


def _store_heads(ref, z):
    dh = ref.shape[1]
    heads = ref.shape[0] // z.shape[0]
    for h in range(heads):
        ref[pl.ds(h, z.shape[0], stride=heads), :] = z[:, h * dh:(h + 1) * dh]


def _store_heads_t(ref, z_t):
    dh = ref.shape[1]
    for h in range(ref.shape[0]):
        ref[h] = z_t[h * dh:(h + 1) * dh, :]


def _proj_kernel(x_ref, gmix_ref, wqkv_ref, wft_ref, gains_ref, gqm_ref, bf_ref,
                 bd64_ref, bd128_ref, *refs, out_names, scale_a, scale_b, scale_m, keep_from,
                 tiles_per_group, riders):
    rider_in, out_refs = refs[:len(refs) - len(out_names)], refs[len(refs) - len(out_names):]
    o = dict(zip(out_names, out_refs))
    head_major = "vbt" in o
    for k, (rider_kernel, n_in) in enumerate(riders):
        rider_kernel(*rider_in[:n_in], o[f"rider{k}"])
        rider_in = rider_in[n_in:]
    xn = _rms_rows(x_ref[...], gmix_ref[...]).astype(BF16)

    def seg(j):
        return _dot(xn, wqkv_ref[:, j * 512:(j + 1) * 512])

    kb = _head_norm(seg(4), bd64_ref, gains_ref[3:4, :])
    o["kb"][...] = kb.astype(BF16)
    vb = seg(5)
    if head_major:
        _store_heads_t(o["kb_f"], kb.T)
        vb_t = vb.T
        _store_heads_t(o["vb_f"], vb_t)
        vbt = o["vbt"]
        head = vb_t.shape[0] // vbt.shape[0]
        pad = vbt.shape[1] - head
        ones_row = (lax.broadcasted_iota(jnp.int32, (pad, vb_t.shape[1]), 0) == 0).astype(BF16)
        for h in range(vbt.shape[0]):
            vbt[h, 0:head, :] = vb_t[h * head:(h + 1) * head, :].astype(BF16)
            vbt[h, head:head + pad, :] = ones_row
    else:
        _store_heads(o["kb_f"], kb)
        _store_heads(o["vb_f"], vb)
        o["vb"][...] = vb.astype(BF16)
    qb = _head_norm(seg(3), bd64_ref, gains_ref[2:3, :])
    o["qb"][...] = (qb * scale_b).astype(BF16)
    qa = _head_norm(seg(0), bd64_ref, gains_ref[0:1, :])
    o["qa"][...] = (qa * scale_a).astype(BF16)
    ka = _head_norm(seg(1), bd64_ref, gains_ref[1:2, :])
    o["ka"][...] = ka.astype(BF16)
    qm = _head_norm(seg(6), bd128_ref, gqm_ref[...])
    o["qm"][...] = (qm * scale_m).astype(BF16)
    flt = _nt_dot(wft_ref[...], xn)
    o["lft"][...] = _log_sigmoid(flt[0:8, :] + bf_ref[...])
    va = seg(2)
    o["va"][...] = va.astype(BF16)
    if head_major:
        @pl.when(pl.program_id(0) % tiles_per_group >= keep_from)
        def _():
            _store_heads_t(o["ka_f"], ka.T)
            _store_heads_t(o["va_f"], va.T)
    else:
        _store_heads(o["ka_f"], ka)
        _store_heads(o["va_f"], va)


def _proj(x, gmix, wqkv, wft, gains, gqm, bf, bd64, bd128, *, tm, group_rows=None, keep_rows=None,
          riders=()):
    rows, d = x.shape
    n_tiles = rows // tm
    row = lambda i: (i, 0)
    wide = lambda dt: (jax.ShapeDtypeStruct((rows, 512), dt), pl.BlockSpec((tm, 512), row))
    outs = {n: wide(BF16) for n in ("qa", "ka", "va", "qb", "kb", "qm")}
    outs["lft"] = (jax.ShapeDtypeStruct((8, rows), F32), pl.BlockSpec((8, tm), lambda i: (0, i)))
    tpg = keep_from = 1
    if group_rows is None:
        outs["vb"] = wide(BF16)
        for n in ("ka_f", "va_f", "kb_f", "vb_f"):
            outs[n] = (jax.ShapeDtypeStruct((rows * 8, 64), F32), pl.BlockSpec((tm * 8, 64), row))
    else:
        assert keep_rows == tm
        tpg = group_rows // tm
        keep_from = tpg - 1
        groups = rows // group_rows
        kept = (jax.ShapeDtypeStruct((groups, 8, 64, keep_rows), F32),
                pl.BlockSpec((None, 8, 64, tm), lambda i: (i // tpg, 0, 0, 0)))
        full = (jax.ShapeDtypeStruct((groups, 8, 64, group_rows), F32),
                pl.BlockSpec((None, 8, 64, tm), lambda i: (i // tpg, 0, 0, i % tpg)))
        outs.update(ka_f=kept, va_f=kept, kb_f=full, vb_f=full)
        outs["vbt"] = (jax.ShapeDtypeStruct((n_tiles, 8, V_ROWS, tm), BF16),
                       pl.BlockSpec((None, 8, V_ROWS, tm), lambda i: (i, 0, 0, 0)))
    in_specs = [pl.BlockSpec((tm, d), row)] + [_const_spec(a.shape) for a in
                                               (gmix, wqkv, wft, gains, gqm, bf, bd64, bd128)]
    rider_args, rider_kernels = [], []
    for k, (rider_kernel, per_tile, const) in enumerate(riders):
        assert all(a.shape[0] % n_tiles == 0 for a in per_tile)
        tile_spec = lambda a: pl.BlockSpec((a.shape[0] // n_tiles,) + a.shape[1:],
                                           lambda i, nd=a.ndim: (i,) + (0,) * (nd - 1))
        in_specs += [tile_spec(a) for a in per_tile] + [_const_spec(a.shape) for a in const]
        outs[f"rider{k}"] = (jax.ShapeDtypeStruct(per_tile[0].shape, BF16), tile_spec(per_tile[0]))
        rider_args += [*per_tile, *const]
        rider_kernels.append((rider_kernel, len(per_tile) + len(const)))
    names = tuple(outs)
    kern = functools.partial(_proj_kernel, out_names=names, scale_a=LOG2E * 64 ** -0.5,
                             scale_b=LOG2E * 64 ** -0.5, scale_m=LOG2E * 128 ** -0.5,
                             keep_from=keep_from, tiles_per_group=tpg, riders=tuple(rider_kernels))
    res = pl.pallas_call(kern, out_shape=[outs[n][0] for n in names], grid=(n_tiles,), in_specs=in_specs,
                         out_specs=[outs[n][1] for n in names], compiler_params=_cparams("arbitrary"),
                         name="proj")(x, gmix, wqkv, wft, gains, gqm, bf, bd64, bd128, *rider_args)
    return dict(zip(names, res))


def _split_w_in_kernel(wt_ref, wp_ref, wg_ref, *, n_proj):
    i = pl.program_id(0)
    blk = wt_ref[...].T.astype(BF16)

    @pl.when(i < n_proj)
    def _():
        wp_ref[...] = blk

    @pl.when(i >= n_proj)
    def _():
        wg_ref[...] = blk


def _split_w_in(w_t, *, proj_cols, qm_cols, gate_cols, block=512):
    d = w_t.shape[1]
    spans = [proj_cols, qm_cols, gate_cols]
    assert all((b - a) % block == 0 and a % 8 == 0 for a, b in spans) and proj_cols[0] == 0
    n_a = (proj_cols[1] - proj_cols[0]) // block
    n_proj = n_a + (qm_cols[1] - qm_cols[0]) // block
    n_gate = (gate_cols[1] - gate_cols[0]) // block
    assert qm_cols[1] == gate_cols[0]

    def row_start(i):
        return pl.multiple_of(jnp.where(i < n_a, block * i, qm_cols[0] + block * (i - n_a)), 8)

    return pl.pallas_call(
        functools.partial(_split_w_in_kernel, n_proj=n_proj),
        out_shape=[jax.ShapeDtypeStruct((d, block * n_proj), BF16), jax.ShapeDtypeStruct((d, block * n_gate), BF16)],
        grid=(n_proj + n_gate,),
        in_specs=[pl.BlockSpec((pl.Element(block), pl.Element(d)), lambda i: (row_start(i), 0))],
        out_specs=[pl.BlockSpec((d, block), lambda i: (0, jnp.minimum(i, n_proj - 1))),
                   pl.BlockSpec((d, block), lambda i: (0, jnp.maximum(i - n_proj, 0)))],
        compiler_params=_cparams("arbitrary"), name="split_w_in")(w_t)


def _memkv_kernel(m_ref, gmem_ref, w_ref, gkm_ref, bd128_ref, mk_o, mv_o, *, heads):
    tm = m_ref.shape[0]
    xn = _rms_rows(m_ref[...], gmem_ref[...]).astype(BF16)
    half = w_ref.shape[1] // 2
    dh = half // heads
    mk = _head_norm(_dot(xn, w_ref[:, :half]), bd128_ref, gkm_ref[...])
    mv = _dot(xn, w_ref[:, half:])
    for h in range(heads):
        mk_o[pl.ds(h, tm, stride=heads), :] = mk[:, h * dh:(h + 1) * dh]
        mv_o[pl.ds(h, tm, stride=heads), :] = mv[:, h * dh:(h + 1) * dh]


def _memkv(mem, gmem, w, gkm, bd128, *, tm, heads):
    rows, d = mem.shape
    dh = w.shape[1] // 2 // heads
    row = lambda i: (i, 0)
    return pl.pallas_call(
        functools.partial(_memkv_kernel, heads=heads),
        out_shape=[jax.ShapeDtypeStruct((rows * heads, dh), F32)] * 2, grid=(rows // tm,),
        in_specs=[pl.BlockSpec((tm, d), row)] + [_const_spec(a.shape) for a in (gmem, w, gkm, bd128)],
        out_specs=[pl.BlockSpec((tm * heads, dh), row)] * 2, compiler_params=_cparams("arbitrary"),
        name="memkv")(mem, gmem, w, gkm, bd128)


def _cumsum_kernel(x_ref, tri_ref, o_ref):
    rows, length = x_ref.shape
    tri = tri_ref[...]

    def local(c):
        x = x_ref[:, c * 128:(c + 1) * 128]
        hi = x.astype(BF16)
        r1 = x - hi.astype(F32)
        mid = r1.astype(BF16)
        lo = (r1 - mid.astype(F32)).astype(BF16)
        return _dot(hi, tri) + _dot(mid, tri) + _dot(lo, tri)

    chunks = [local(c) for c in range(length // 128)]
    carry = jnp.zeros((rows, 1), F32)
    for c, cc in enumerate(chunks):
        o_ref[:, c * 128:(c + 1) * 128] = cc + carry
        carry = carry + cc[:, 127:128]


def _cumsum_lanes(x, tri, *, rb):
    rows, length = x.shape
    return pl.pallas_call(
        _cumsum_kernel, out_shape=jax.ShapeDtypeStruct((rows, length), F32), grid=(rows // rb,),
        in_specs=[pl.BlockSpec((rb, length), lambda i: (i, 0)), _const_spec(tri.shape)],
        out_specs=pl.BlockSpec((rb, length), lambda i: (i, 0)), compiler_params=_cparams("arbitrary"),
        name="cumsum")(x, tri)


def _band_kernel(bound_ref, q_ref, k0_ref, k1_ref, k2_ref, v0_ref, v1_ref, v2_ref, bias_ref, o_ref,
                 k_sc, v_sc, *, reach, lookahead):
    tq = q_ref.shape[0]
    band = reach + CHUNK
    i = pl.program_id(1)
    for n, (kr, vr) in enumerate(((k0_ref, v0_ref), (k1_ref, v1_ref), (k2_ref, v2_ref))):
        k_sc[n * tq:(n + 1) * tq, :] = kr[...]
        v_sc[n * tq:(n + 1) * tq, :] = vr[...]
    col = lax.broadcasted_iota(jnp.int32, (1, band), 1)
    units = [(j, hg) for j in range(tq // CHUNK) for hg in range(q_ref.shape[1] // LANE_GROUP)]

    def scores(j, hg):
        r0 = 2 * tq + j * CHUNK - reach
        lanes = slice(hg * LANE_GROUP, (hg + 1) * LANE_GROUP)
        qbd = _block_diag_rows(q_ref[j * CHUNK:(j + 1) * CHUNK, lanes], CHUNK)
        s = _nt_dot(qbd, k_sc[r0:r0 + band, lanes]) + bias_ref[hg * 4 * CHUNK:(hg + 1) * 4 * CHUNK, :]
        return jnp.where((i - 2) * tq + r0 + col >= 0, s, NEG)

    def attend(one_pass):
        pending = [scores(*u) for u in units[:lookahead]]
        for n, (j, hg) in enumerate(units):
            s = pending.pop(0)
            if n + lookahead < len(units):
                pending.append(scores(*units[n + lookahead]))
            r0 = 2 * tq + j * CHUNK - reach
            lanes = slice(hg * LANE_GROUP, (hg + 1) * LANE_GROUP)
            p = jnp.exp2(s if one_pass else s - jnp.max(s, axis=-1, keepdims=True))
            l = jnp.sum(p, axis=-1, keepdims=True)
            o = _dot(p.astype(BF16), v_sc[r0:r0 + band, lanes]) / l
            o_ref[j * CHUNK:(j + 1) * CHUNK, lanes] = _pick_diag(o, CHUNK, CHUNK).astype(o_ref.dtype)

    lax.cond(bound_ref[0] <= MAX_ONE_PASS_BOUND, lambda: attend(True), lambda: attend(False))


def _band_prompt(logit_bound, q, k, v, bias, *, reach, tq):
    b, t, w = q.shape
    assert reach == 2 * tq and tq % CHUNK == 0
    blk = lambda off: pl.BlockSpec((None, tq, w), lambda bi, i: (bi, jnp.maximum(i - off, 0), 0))
    return pl.pallas_call(
        functools.partial(_band_kernel, reach=reach, lookahead=2),
        out_shape=jax.ShapeDtypeStruct((b, t, w), BF16), grid=(b, t // tq),
        in_specs=[pl.BlockSpec(memory_space=pltpu.SMEM), blk(0), blk(2), blk(1), blk(0), blk(2), blk(1), blk(0),
                  _const_spec(bias.shape)],
        out_specs=blk(0),
        scratch_shapes=[pltpu.VMEM((3 * tq, w), BF16), pltpu.VMEM((3 * tq, w), BF16)],
        compiler_params=_cparams("arbitrary", "arbitrary"), name="band_prompt")(
            logit_bound, q, k, k, k, v, v, v, bias)


def _forget_kernel(bound_ref, q_ref, k_ref, vt_ref, c_ref, cq_ref, o_ref, qt_sc, m_sc, acc_sc,
                   *, head, lookahead):
    tq, width = q_ref.shape
    n_heads = width // head
    i = pl.program_id(2)
    logit_bound = bound_ref[0]
    q_t = q_ref[...].astype(F32).T
    row_grp = lax.broadcasted_iota(jnp.int32, (width, 1), 0) // head
    for g in range(n_heads):
        qt_sc[g] = jnp.where(row_grp == g, q_t, 0.0).astype(BF16)
    m_sc[...] = jnp.full(m_sc.shape, NEG, F32)
    acc_sc[...] = jnp.zeros(acc_sc.shape, F32)
    c_first = c_ref[pl.ds(pl.multiple_of(i * tq, tq), 8), :][0:1, :]

    def step(j, masked):
        k0 = pl.multiple_of(j * tq, tq)
        kt = k_ref[pl.ds(k0, tq), :]
        bias = (c_first - c_ref[pl.ds(k0, tq), :]) * LOG2E
        if masked:
            keep = (lax.broadcasted_iota(jnp.int32, (tq, tq), 0)
                    <= lax.broadcasted_iota(jnp.int32, (tq, tq), 1))

        def scores(g):
            s = _dot(kt, qt_sc[g]) + bias[:, g:g + 1]
            return jnp.where(keep, s, NEG) if masked else s

        pending = [scores(g) for g in range(min(lookahead, n_heads))]
        for g in range(n_heads):
            s = pending.pop(0)
            if g + lookahead < n_heads:
                pending.append(scores(g + lookahead))
            m_prev = m_sc[g]
            m_new = jnp.maximum(m_prev, jnp.max(s, axis=0, keepdims=True))
            p = jnp.exp2(s - m_new).astype(BF16)
            acc_sc[g] = jnp.exp2(m_prev - m_new) * acc_sc[g] + _dot(vt_ref[j, g], p)
            m_sc[g] = m_new

    def bounded_tiles(tiles):
        m_tile, logits = [], []
        for j in tiles:
            k0 = pl.multiple_of(j * tq, tq)
            c_tile = c_ref[pl.ds(k0, tq), :]
            c_last = c_tile[tq - 1:tq, :]
            m_tile.append((c_first - c_last) * LOG2E)
            shift = (c_last - c_tile) * LOG2E
            logits.append(functools.partial(
                lambda g, k0, shift: _dot(k_ref[pl.ds(k0, tq), :], qt_sc[g]) + shift[:, g:g + 1],
                k0=k0, shift=shift))
        units = [(t, g) for t in range(len(tiles)) for g in range(n_heads)]
        pending = [logits[t](g) for t, g in units[:lookahead]]
        for n, (t, g) in enumerate(units):
            s = pending.pop(0)
            if n + lookahead < len(units):
                tn, gn = units[n + lookahead]
                pending.append(logits[tn](gn))
            m_new = jnp.broadcast_to(m_tile[t][:, g:g + 1], (1, tq))
            p = jnp.exp2(s).astype(BF16)
            acc_sc[g] = jnp.exp2(m_sc[g] - m_new) * acc_sc[g] + _dot(vt_ref[tiles[t], g], p)
            m_sc[g] = m_new

    def bounded_loop():
        def pair(jj, carry):
            bounded_tiles([2 * jj, 2 * jj + 1])
            return carry
        lax.fori_loop(0, i // 2, pair, 0)

        @pl.when(i % 2 == 1)
        def _():
            bounded_tiles([i - 1])

    def two_pass_loop():
        def body(j, carry):
            step(j, False)
            return carry
        lax.fori_loop(0, i, body, 0)

    def bounded_diag():
        k0 = pl.multiple_of(i * tq, tq)
        kt = k_ref[pl.ds(k0, tq), :]
        col = (c_first - c_ref[pl.ds(k0, tq), :]) * LOG2E
        keep = (lax.broadcasted_iota(jnp.int32, (tq, tq), 0)
                <= lax.broadcasted_iota(jnp.int32, (tq, tq), 1))
        m_row = [(c_first[:, g:g + 1] - cq_ref[g:g + 1, :]) * LOG2E for g in range(n_heads)]

        def logits(g):
            return jnp.where(keep, _dot(kt, qt_sc[g]) + col[:, g:g + 1] - m_row[g], NEG)

        pending = [logits(g) for g in range(min(lookahead, n_heads))]
        for g in range(n_heads):
            s = pending.pop(0)
            if g + lookahead < n_heads:
                pending.append(logits(g + lookahead))
            p = jnp.exp2(s).astype(BF16)
            acc_sc[g] = jnp.exp2(m_sc[g] - m_row[g]) * acc_sc[g] + _dot(vt_ref[i, g], p)

    one_pass = logit_bound <= MAX_ONE_PASS_BOUND
    lax.cond(one_pass, bounded_loop, two_pass_loop)
    lax.cond(one_pass, bounded_diag, lambda: step(i, True))
    o_t = jnp.concatenate([acc_sc[g, 0:head, :] / acc_sc[g, head:head + 1, :] for g in range(n_heads)],
                          axis=0)
    o_ref[...] = o_t.T.astype(o_ref.dtype)


def _forget_prompt(logit_bound, q, k, vt, c, c_rows, *, head, tq):
    b, t, w = q.shape
    n_heads = LANE_GROUP // head
    v_rows = vt.shape[3]
    lookahead = 3
    assert lookahead <= n_heads
    return pl.pallas_call(
        functools.partial(_forget_kernel, head=head, lookahead=lookahead),
        out_shape=jax.ShapeDtypeStruct((b, t, w), BF16), grid=(b, w // LANE_GROUP, t // tq),
        in_specs=[pl.BlockSpec(memory_space=pltpu.SMEM),
                  pl.BlockSpec((None, tq, LANE_GROUP), lambda bi, hg, i: (bi, i, hg)),
                  pl.BlockSpec((None, t, LANE_GROUP), lambda bi, hg, i: (bi, 0, hg)),
                  pl.BlockSpec((None, t // tq, n_heads, v_rows, tq), lambda bi, hg, i: (bi, 0, hg, 0, 0)),
                  pl.BlockSpec((None, None, t, n_heads), lambda bi, hg, i: (bi, hg, 0, 0)),
                  pl.BlockSpec((None, None, n_heads, tq), lambda bi, hg, i: (bi, hg, 0, i))],
        out_specs=pl.BlockSpec((None, tq, LANE_GROUP), lambda bi, hg, i: (bi, i, hg)),
        scratch_shapes=[pltpu.VMEM((n_heads, LANE_GROUP, tq), BF16), pltpu.VMEM((n_heads, 1, tq), F32),
                        pltpu.VMEM((n_heads, v_rows, tq), F32)],
        compiler_params=_cparams("arbitrary", "arbitrary", "arbitrary"), name="forget_prompt")(
            logit_bound, q, k, vt, c, c_rows)


def _mem_attention(qs, mk_refs, mv_refs, heads):
    n_mem = mk_refs[0].shape[0] // heads
    dh = mk_refs[0].shape[1]
    units = [(e, h) for e in range(len(qs)) for h in range(heads)]
    rows = lambda ref, h: ref[pl.ds(h, n_mem, stride=heads), :].astype(BF16)
    s = [_nt_dot(qs[e][:, h * dh:(h + 1) * dh], rows(mk_refs[e], h)) for e, h in units]
    p, inv_l = [], []
    for u in range(len(units)):
        ex = jnp.exp2(s[u] - jnp.max(s[u], axis=-1, keepdims=True))
        inv_l.append(1.0 / jnp.sum(ex, axis=-1, keepdims=True))
        p.append(ex.astype(BF16))
    o = [(_dot(p[u], rows(mv_refs[e], h)) * inv_l[u]).astype(BF16) for u, (e, h) in enumerate(units)]
    return [jnp.concatenate(o[e * heads:(e + 1) * heads], axis=-1) for e in range(len(qs))]


def _mem_kernel(q_ref, mk_ref, mv_ref, o_ref, *, heads):
    bb = q_ref.shape[0]
    outs = _mem_attention([q_ref[e] for e in range(bb)], [mk_ref.at[e] for e in range(bb)],
                          [mv_ref.at[e] for e in range(bb)], heads)
    for e in range(bb):
        o_ref[e] = outs[e]


def _mem_attn(q, mk, mv, *, heads, tq, bb):
    b, t, w = q.shape
    kv = pl.BlockSpec((bb,) + mk.shape[1:], lambda bi, i: (bi, 0, 0))
    qs = pl.BlockSpec((bb, tq, w), lambda bi, i: (bi, i, 0))
    return pl.pallas_call(
        functools.partial(_mem_kernel, heads=heads), out_shape=jax.ShapeDtypeStruct((b, t, w), BF16),
        grid=(b // bb, t // tq), in_specs=[qs, kv, kv], out_specs=qs,
        compiler_params=_cparams("arbitrary", "arbitrary"), name="mem_attn")(q, mk, mv)


def _post_kernel(x_ref, oa_ref, ob_ref, qm_ref, mk_ref, mv_ref, xt_ref, oat_ref, obt_ref, omt_ref,
                 gmix_ref, wg_ref, wpa_ref, wpb_ref, wpm_ref, wo_ref, gffn_ref, wup_ref, wdn_ref,
                 y_ref, yt_ref, *, mem_heads, col_chunk, ff_chunk):
    tail = pl.program_id(0) == pl.num_programs(0) - 1
    o_m, = _mem_attention([qm_ref[...]], [mk_ref], [mv_ref], mem_heads)
    x = jnp.where(tail, xt_ref[...], x_ref[...])
    o_a = jnp.where(tail, oat_ref[...], oa_ref[...])
    o_b = jnp.where(tail, obt_ref[...], ob_ref[...])
    o_m = jnp.where(tail, omt_ref[...], o_m)
    d = x.shape[1]
    xn = _rms_rows(x, gmix_ref[...]).astype(BF16)
    branches = ((o_a, wpa_ref), (o_b, wpb_ref), (o_m, wpm_ref))
    h = []
    for c in range(d // col_chunk):
        cols = slice(c * col_chunk, (c + 1) * col_chunk)
        hc = 0.0
        for b, (o_b, wp_ref) in enumerate(branches):
            gate = jax.nn.sigmoid(_dot(xn, wg_ref[:, b * d + c * col_chunk:b * d + (c + 1) * col_chunk]))
            hc = hc + gate * _dot(o_b, wp_ref[:, cols])
        h.append(hc.astype(BF16))
    x1 = x + _dot(jnp.concatenate(h, axis=-1), wo_ref[...])
    xn1 = _rms_rows(x1, gffn_ref[...]).astype(BF16)
    y = x1
    for c in range(wup_ref.shape[1] // ff_chunk):
        cols = slice(c * ff_chunk, (c + 1) * ff_chunk)
        u = jnp.maximum(_dot(xn1, wup_ref[:, cols]), 0.0)
        y = y + _dot((u * u).astype(BF16), wdn_ref[cols, :])

    @pl.when(jnp.logical_not(tail))
    def _():
        y_ref[...] = y

    @pl.when(tail)
    def _():
        yt_ref[...] = y


def _post(x, oa, ob, qm, mk, mv, tail, gmix, wg, wpa, wpb, wpm, wo, gffn, wup, wdn, *, tm, mem_heads):
    rows, d = x.shape
    n_main = rows // tm
    assert all(a.shape[0] == tm for a in tail)
    row = lambda i: (jnp.minimum(i, n_main - 1), 0)
    tiles_per_batch = n_main // mk.shape[0]
    kv = pl.BlockSpec((None,) + mk.shape[1:], lambda i: (jnp.minimum(i, n_main - 1) // tiles_per_batch, 0, 0))
    acts = (x, oa, ob, qm)
    weights = (gmix, wg, wpa, wpb, wpm, wo, gffn, wup, wdn)
    return pl.pallas_call(
        functools.partial(_post_kernel, mem_heads=mem_heads, col_chunk=512, ff_chunk=1024),
        out_shape=[jax.ShapeDtypeStruct((rows, d), F32), jax.ShapeDtypeStruct((tm, d), F32)],
        grid=(n_main + 1,),
        in_specs=[pl.BlockSpec((tm, a.shape[1]), row) for a in acts] + [kv, kv]
        + [_const_spec(a.shape) for a in (*tail, *weights)],
        out_specs=[pl.BlockSpec((tm, d), row), pl.BlockSpec((tm, d), lambda i: (0, 0))],
        compiler_params=_cparams("arbitrary"), name="post")(*acts, mk, mv, *tail, *weights)


def _pad_rows(x, rows):
    return jnp.concatenate([x, jnp.zeros((rows - x.shape[0], x.shape[1]), x.dtype)], axis=0)


def _cached_attention(q_ref, kc_ref, vc_ref, kn_ref, vn_ref, o_ref, bias_c, bias_n, n_pad):
    bb, heads, head, _ = kc_ref.shape
    units = [(e, h, slice(h * head, (h + 1) * head)) for e in range(bb) for h in range(heads)]
    sc = [_dot(q_ref[e, :, ln], kc_ref[e, h].astype(BF16)) + bias_c(e, h) for e, h, ln in units]
    sn = [_nt_dot(q_ref[e, :, ln], _pad_rows(kn_ref[e, :, ln].astype(BF16), n_pad)) + bias_n(e, h)
          for e, h, ln in units]
    pc, pn, inv_l = [], [], []
    for u in range(len(units)):
        m = jnp.maximum(jnp.max(sc[u], axis=-1, keepdims=True), jnp.max(sn[u], axis=-1, keepdims=True))
        pc.append(jnp.exp2(sc[u] - m))
        pn.append(jnp.exp2(sn[u] - m))
        inv_l.append(1.0 / (jnp.sum(pc[u], axis=-1, keepdims=True) + jnp.sum(pn[u], axis=-1, keepdims=True)))
    for u, (e, h, ln) in enumerate(units):
        o = (_nt_dot(pc[u].astype(BF16), vc_ref[e, h].astype(BF16))
             + _dot(pn[u].astype(BF16), _pad_rows(vn_ref[e, :, ln].astype(BF16), n_pad)))
        o_ref[e, :, ln] = (o * inv_l[u]).astype(o_ref.dtype)


def _band_sample_kernel(q_ref, kc_ref, vc_ref, kn_ref, vn_ref, bc_ref, bn_ref, o_ref):
    s_len = q_ref.shape[1]
    rows = lambda h: slice(h * s_len, (h + 1) * s_len)
    _cached_attention(q_ref, kc_ref, vc_ref, kn_ref, vn_ref, o_ref,
                      lambda e, h: bc_ref[rows(h), :], lambda e, h: bn_ref[rows(h), :], bn_ref.shape[1])


def _band_sample(q, kct, vct, kn, vn, bias_c, bias_n, *, bb):
    b, s_len, w = q.shape
    new = pl.BlockSpec((bb, s_len, w), lambda bi: (bi, 0, 0))
    cache = pl.BlockSpec((bb,) + kct.shape[1:], lambda bi: (bi, 0, 0, 0))
    return pl.pallas_call(
        _band_sample_kernel, out_shape=jax.ShapeDtypeStruct((b, s_len, w), BF16), grid=(b // bb,),
        in_specs=[new, cache, cache, new, new, _const_spec(bias_c.shape), _const_spec(bias_n.shape)],
        out_specs=new, compiler_params=_cparams("arbitrary"), name="band_sample")(
            q, kct, vct, kn, vn, bias_c, bias_n)


def _forget_sample_kernel(q_ref, kc_ref, vc_ref, kn_ref, vn_ref, c_ref, o_ref, *, n_pad):
    s_len = q_ref.shape[1]
    p_len = kc_ref.shape[3]
    causal = (lax.broadcasted_iota(jnp.int32, (s_len, n_pad), 1)
              <= lax.broadcasted_iota(jnp.int32, (s_len, n_pad), 0))
    ch = [(c_ref[e, :, p_len - 1:p_len] - c_ref[e]) * LOG2E for e in range(c_ref.shape[0])]
    _cached_attention(q_ref, kc_ref, vc_ref, kn_ref, vn_ref, o_ref,
                      lambda e, h: ch[e][h:h + 1, 0:p_len],
                      lambda e, h: jnp.where(causal, ch[e][h:h + 1, p_len:p_len + n_pad], NEG), n_pad)


def _forget_sample(q, kct, vct, kn, vn, c, *, n_pad):
    b, s_len, w = q.shape
    new = pl.BlockSpec((1, s_len, w), lambda bi: (bi, 0, 0))
    cache = pl.BlockSpec((1,) + kct.shape[1:], lambda bi: (bi, 0, 0, 0))
    cs = pl.BlockSpec((1,) + c.shape[1:], lambda bi: (bi, 0, 0))
    return pl.pallas_call(
        functools.partial(_forget_sample_kernel, n_pad=n_pad),
        out_shape=jax.ShapeDtypeStruct((b, s_len, w), BF16), grid=(b,),
        in_specs=[new, cache, cache, new, new, cs], out_specs=new,
        compiler_params=_cparams("arbitrary"), name="forget_sample")(q, kct, vct, kn, vn, c)


def _qk_bound(g_q, g_k, dh):
    return 1.02 * LOG2E * dh ** 0.5 * jnp.max(jnp.abs(g_q)) * jnp.max(jnp.abs(g_k))


def _block_diag_mean(head, size=256):
    r = jnp.arange(size) // head
    return jnp.where(r[:, None] == r[None, :], 1.0 / head, 0.0).astype(BF16)


def _rel_bias_rows(rel_bias, d0, n_q, n_k):
    n = n_q + n_k - 1
    dist = d0 + (n_q - 1) - jnp.arange(n)
    e = rel_bias[jnp.clip(dist, -REL_CLIP, REL_CLIP) + REL_CLIP].T.astype(F32)
    h = e.shape[0]
    skew = jnp.tile(jnp.pad(e, ((0, 0), (0, 1))), (1, n_q))[:, :n_q * n].reshape(h, n_q, n)
    return skew[:, :, n_q - 1:].reshape(h * n_q, n_k)


def kernel(x_prompt, x_sample, mem_prompt, cache_a_k, cache_a_v, cache_b_k, cache_b_v, cache_b_logf,
           cache_mem_k, cache_mem_v, g_mix, w_in, b_f, g_qa, g_ka, g_qb, g_kb, g_qm, g_km, rel_bias,
           g_mem, w_mkv, w_pa, w_pb, w_pm, w_o, g_ffn, w_up, w_down):
    depth = w_in.shape[0]
    assert depth == 1
    batch, seq, d = x_prompt.shape
    dec_b, dec_s, _ = x_sample.shape
    _, _, n_cache, h_a, dh_a = cache_a_k.shape
    _, _, past, h_b, dh_b = cache_b_k.shape
    _, _, n_mem, h_m, dh_m = cache_mem_k.shape
    w_a, w_b, w_m = h_a * dh_a, h_b * dh_b, h_m * dh_m
    assert w_a == w_b == w_m == 512 and dh_a == dh_b == 64 and dh_m == 128 and h_b == 8
    keep_p = min(n_cache, seq)
    n_pad = 128
    l = 0

    qkv_end = 3 * w_a + 3 * w_b
    f_end = qkv_end + h_b
    qm_end = f_end + w_m
    w_t = jnp.transpose(w_in[l])
    wft = jnp.zeros((16, d), BF16).at[:h_b].set(w_t[qkv_end:f_end].astype(BF16))
    wqkvm, wg = _split_w_in(w_t, proj_cols=(0, qkv_end), qm_cols=(f_end, qm_end), gate_cols=(qm_end, w_t.shape[0]))
    gains = jnp.stack([jnp.tile(g[l], h_a) for g in (g_qa, g_ka, g_qb, g_kb)]).astype(F32)
    gqm = jnp.tile(g_qm[l], h_m)[None, :]
    gkm = jnp.tile(g_km[l], h_m)[None, :]
    bf = b_f[l][:, None].astype(F32)
    bd64 = _block_diag_mean(dh_a)
    bd128 = _block_diag_mean(dh_m)
    tri = (jnp.arange(128)[:, None] <= jnp.arange(128)[None, :]).astype(BF16)
    gmix = g_mix[l][None, :]
    proj_w = (gmix, wqkvm, wft, gains, gqm, bf, bd64, bd128)
    gffn = g_ffn[l][None, :]

    xs = x_sample.reshape(dec_b * dec_s, d)
    ps = _proj(xs, *proj_w, tm=512)
    sthree = lambda a: a.reshape(dec_b, dec_s, -1)
    heads_t = lambda a: jnp.transpose(a[l], (0, 2, 3, 1))
    lf_new = ps["lft"].reshape(h_b, dec_b, dec_s).transpose(1, 0, 2)
    lf_all = jnp.concatenate([cache_b_logf[l].astype(F32).transpose(0, 2, 1), lf_new,
                              jnp.zeros((dec_b, h_b, n_pad - dec_s), F32)], axis=-1)
    c_s = _cumsum_lanes(lf_all.reshape(dec_b * h_b, past + n_pad), tri, rb=dec_b * h_b)
    forget_s = (sthree(ps["qb"]), heads_t(cache_b_k), heads_t(cache_b_v), sthree(ps["kb"]),
                sthree(ps["vb"]), c_s.reshape(dec_b, h_b, past + n_pad))
    bias_c = _rel_bias_rows(rel_bias[l], n_cache, dec_s, n_cache) * LOG2E
    bias_n = jnp.full((h_a * dec_s, n_pad), NEG, F32).at[:, :dec_s].set(
        _rel_bias_rows(rel_bias[l], 0, dec_s, dec_s) * LOG2E)
    band_s = (sthree(ps["qa"]), heads_t(cache_a_k), heads_t(cache_a_v), sthree(ps["ka"]), sthree(ps["va"]))
    mem_s = (sthree(ps["qm"]), cache_mem_k[l].reshape(dec_b, n_mem * h_m, dh_m),
             cache_mem_v[l].reshape(dec_b, n_mem * h_m, dh_m))

    xp = x_prompt.reshape(batch * seq, d)
    tm_p = 512
    ride = dec_b % (batch * seq // tm_p) == 0
    riders = tuple((_cast_kernel, (a[l],), ()) for a in (w_pa, w_pb, w_pm, w_o, w_up, w_down))
    if ride:
        riders += ((functools.partial(_forget_sample_kernel, n_pad=n_pad), forget_s, ()),
                   (_band_sample_kernel, band_s, (bias_c, bias_n)),
                   (functools.partial(_mem_kernel, heads=h_m), mem_s, ()))
    pp = _proj(xp, *proj_w, tm=tm_p, group_rows=seq, keep_rows=keep_p, riders=riders)
    wpa, wpb, wpm, wo, wup, wdn = (pp[f"rider{k}"] for k in range(6))
    if ride:
        o_b_s, o_a_s, o_m_s = pp["rider6"], pp["rider7"], pp["rider8"]
    else:
        o_b_s = _forget_sample(*forget_s, n_pad=n_pad)
        o_a_s = _band_sample(*band_s, bias_c, bias_n, bb=4)
        o_m_s = _mem_attn(*mem_s, heads=h_m, tq=dec_s, bb=8)
    mk_f, mv_f = _memkv(mem_prompt.reshape(batch * n_mem, d), g_mem[l][None, :], w_mkv[l].astype(BF16),
                        gkm, bd128, tm=512, heads=h_m)
    tq_b = 512
    lf_p = pp["lft"].reshape(h_b, batch, seq).transpose(1, 0, 2)
    c_rows_p = _cumsum_lanes(lf_p.reshape(batch * h_b, seq), tri, rb=batch * h_b).reshape(
        batch, h_b // 4, 4, seq)
    c_p = c_rows_p.transpose(0, 1, 3, 2)
    vt = pp["vbt"].reshape(batch, seq // tq_b, h_b, V_ROWS, tq_b)
    three = lambda a: a.reshape(batch, seq, -1)
    bias_p = _rel_bias_rows(rel_bias[l], n_cache, CHUNK, n_cache + CHUNK) * LOG2E
    bound_a = _qk_bound(g_qa[l], g_ka[l], dh_a) + LOG2E * jnp.max(jnp.abs(rel_bias[l]))
    o_a = _band_prompt(bound_a.reshape(1).astype(F32), three(pp["qa"]), three(pp["ka"]), three(pp["va"]),
                       bias_p, reach=n_cache, tq=256)
    o_b = _forget_prompt(_qk_bound(g_qb[l], g_kb[l], dh_b).reshape(1).astype(F32), three(pp["qb"]),
                         three(pp["kb"]), vt, c_p, c_rows_p, head=dh_b, tq=tq_b)
    sample_tail = (xs, o_a_s.reshape(-1, w_a), o_b_s.reshape(-1, w_b), o_m_s.reshape(-1, w_m))
    y_prompt, y_sample = _post(
        xp, o_a.reshape(-1, w_a), o_b.reshape(-1, w_b), pp["qm"], mk_f.reshape(batch, n_mem * h_m, dh_m),
        mv_f.reshape(batch, n_mem * h_m, dh_m), sample_tail, gmix, wg, wpa, wpb, wpm, wo, gffn, wup, wdn,
        tm=dec_b * dec_s, mem_heads=h_m)
    y_prompt = y_prompt.reshape(batch, seq, d)
    y_sample = y_sample.reshape(dec_b, dec_s, d)

    lead = lambda a, *shape: a.reshape((depth,) + shape)
    frames_major = lambda a: jnp.transpose(a, (0, 3, 1, 2))[None]
    return (y_prompt, y_sample,
            frames_major(pp["ka_f"]), frames_major(pp["va_f"]),
            frames_major(pp["kb_f"]), frames_major(pp["vb_f"]),
            lf_p.transpose(0, 2, 1)[None],
            lead(mk_f, batch, n_mem, h_m, dh_m), lead(mv_f, batch, n_mem, h_m, dh_m),
            lead(ps["ka_f"], dec_b, dec_s, h_a, dh_a), lead(ps["va_f"], dec_b, dec_s, h_a, dh_a),
            lead(ps["kb_f"], dec_b, dec_s, h_b, dh_b), lead(ps["vb_f"], dec_b, dec_s, h_b, dh_b),
            lead(ps["lft"].T, dec_b, dec_s, h_b))
```
